```python
import math
import jax, jax.numpy as jnp
from jax import lax
import numpy as np

D_MODEL = 1024
BATCH = 8
SEQ = 2048
DEPTH = 4
DEC_BATCH = 128
DEC_SEQ = 4
PAST_LEN = 16384
PAGE_SIZE = 128

N_EVEN = (DEPTH + 1) // 2
N_ODD = DEPTH // 2
GLA_HEADS = 4
GLA_V_WIDTH = D_MODEL // 2
GLA_DV = GLA_V_WIDTH // GLA_HEADS
GLA_DK = GLA_DV // 2
GLA_K_WIDTH = GLA_HEADS * GLA_DK
GLA_RANK = 16
GLA_TAU = 16.0
GLA_CHUNK = 16
S5_WIDTH = D_MODEL - GLA_V_WIDTH
S5_GROUP = 16
S5_GROUPS = S5_WIDTH // S5_GROUP
S5_STATE = 64
LRU_WIDTH = D_MODEL
LRU_BLOCKS = 8
LRU_BLOCK = LRU_WIDTH // LRU_BLOCKS
LRU_C = 8.0
CONV_WIDTH = 4
D_FF = 4 * D_MODEL
NORM_EPS = 1e-6
EVEN_IN_WIDTH = 2 * GLA_K_WIDTH + 2 * GLA_V_WIDTH + GLA_RANK + S5_WIDTH

kernel_name = 'hybrid_gla_s5_rglru_step'


def rmsnorm(x, g):
    xf = x.astype(jnp.float32)
    y = xf * lax.rsqrt(jnp.mean(xf * xf, axis=-1, keepdims=True) + NORM_EPS) * g.astype(jnp.float32)
    return y.astype(x.dtype)


def gla_chunked(q, k, v, log_a, s0):
    bsz, slen, nh, dk = q.shape
    dv = v.shape[-1]
    c = math.gcd(slen, GLA_CHUNK)
    n = slen // c

    def blocks(t):
        return t.astype(jnp.float32).reshape(bsz, n, c, nh, -1).transpose(1, 0, 3, 2, 4)

    qb, kb, vb, gb = blocks(q), blocks(k), blocks(v), blocks(log_a)
    cum = jnp.cumsum(gb, axis=3)
    last = cum[:, :, :, -1:, :]
    q_t = qb * jnp.exp(cum) * (dk ** -0.5)
    k_t = kb * jnp.exp(-cum)
    k_end = kb * jnp.exp(last - cum)
    mask = jnp.tril(jnp.ones((c, c), dtype=bool))
    scores = jnp.where(mask, jnp.einsum('nbhid,nbhjd->nbhij', q_t, k_t), 0.0)
    o_intra = jnp.einsum('nbhij,nbhje->nbhie', scores, vb)

    def step(state, inp):
        q_c, k_c, v_c, dec = inp
        o = jnp.einsum('bhid,bhde->bhie', q_c, state)
        state = jnp.exp(dec)[..., 0, :, None] * state + jnp.einsum('bhjd,bhje->bhde', k_c, v_c)
        return state, o

    s_fin, o_inter = lax.scan(step, s0.astype(jnp.float32), (q_t, k_end, vb, last))
    o = (o_intra + o_inter).transpose(1, 0, 3, 2, 4).reshape(bsz, slen, nh, dv)
    return o, s_fin


def s5_scan(u, h0_re, h0_im, a_re, a_im, log_dt, b_re, b_im, c_re, c_im, d_skip):
    f32 = jnp.float32
    a_re = a_re.astype(f32)
    a_im = a_im.astype(f32)
    dt = jnp.exp(log_dt.astype(f32))[:, None]
    mag = jnp.exp(dt * a_re)
    ab_re = mag * jnp.cos(dt * a_im)
    ab_im = mag * jnp.sin(dt * a_im)
    den = a_re * a_re + a_im * a_im
    coef_re = ((ab_re - 1.0) * a_re + ab_im * a_im) / den
    coef_im = (ab_im * a_re - (ab_re - 1.0) * a_im) / den
    b_re = b_re.astype(f32)
    b_im = b_im.astype(f32)
    bb_re = coef_re[..., None] * b_re - coef_im[..., None] * b_im
    bb_im = coef_re[..., None] * b_im + coef_im[..., None] * b_re
    bu_re = jnp.einsum('bsgc,gpc->bsgp', u, bb_re)
    bu_im = jnp.einsum('bsgc,gpc->bsgp', u, bb_im)
    h0_re = h0_re.astype(f32)
    h0_im = h0_im.astype(f32)
    bu_re = bu_re.at[:, 0].add(ab_re * h0_re - ab_im * h0_im)
    bu_im = bu_im.at[:, 0].add(ab_re * h0_im + ab_im * h0_re)
    ar = jnp.broadcast_to(ab_re, bu_re.shape)
    ai = jnp.broadcast_to(ab_im, bu_im.shape)

    def combine(e1, e2):
        ar1, ai1, br1, bi1 = e1
        ar2, ai2, br2, bi2 = e2
        return (ar2 * ar1 - ai2 * ai1,
                ar2 * ai1 + ai2 * ar1,
                ar2 * br1 - ai2 * bi1 + br2,
                ar2 * bi1 + ai2 * br1 + bi2)

    _, _, h_re, h_im = lax.associative_scan(combine, (ar, ai, bu_re, bu_im), axis=1)
    y = (jnp.einsum('bsgp,gcp->bsgc', h_re, c_re.astype(f32))
         - jnp.einsum('bsgp,gcp->bsgc', h_im, c_im.astype(f32))
         + d_skip.astype(f32) * u)
    return y, h_re[:, -1], h_im[:, -1]


def causal_conv(x, buf, w, bias):
    slen = x.shape[1]
    xp = jnp.concatenate([buf.astype(x.dtype), x], axis=1)
    out = bias.astype(jnp.float32)
    for j in range(CONV_WIDTH):
        out = out + xp[:, j:j + slen].astype(jnp.float32) * w[j].astype(jnp.float32)
    return out, xp[:, slen:].astype(jnp.float32)


def rg_lru(x, h0, w_r, b_r, w_i, b_i, lam):
    bsz, slen, _ = x.shape
    xb = x.reshape(bsz, slen, LRU_BLOCKS, LRU_BLOCK)
    r = jax.nn.sigmoid(jnp.einsum('bsnc,ncd->bsnd', xb, w_r).reshape(bsz, slen, LRU_WIDTH) + b_r)
    i = jax.nn.sigmoid(jnp.einsum('bsnc,ncd->bsnd', xb, w_i).reshape(bsz, slen, LRU_WIDTH) + b_i)
    log_a = -LRU_C * r.astype(jnp.float32) * jax.nn.softplus(-lam.astype(jnp.float32))
    a = jnp.exp(log_a)
    b = jnp.sqrt(-jnp.expm1(2.0 * log_a)) * (i * x)
    b = b.at[:, 0].add(a[:, 0] * h0.astype(jnp.float32))

    def combine(e1, e2):
        return (e1[0] * e2[0], e2[0] * e1[1] + e2[1])

    _, h = lax.associative_scan(combine, (a, b), axis=1)
    return h, h[:, -1]


def even_mixer(x, s_gla, s_re, s_im, g_norm, w_in, w_a2, b_a2, g_head, a_re, a_im, log_dt,
               b_re, b_im, c_re, c_im, d_skip, w_glu, b_glu, w_out):
    f32 = jnp.float32
    bsz, slen, _ = x.shape
    p = rmsnorm(x, g_norm) @ w_in
    o1 = GLA_K_WIDTH
    o2 = 2 * GLA_K_WIDTH
    o3 = o2 + GLA_V_WIDTH
    o4 = o3 + GLA_V_WIDTH
    o5 = o4 + GLA_RANK
    q = p[..., :o1].reshape(bsz, slen, GLA_HEADS, GLA_DK)
    k = p[..., o1:o2].reshape(bsz, slen, GLA_HEADS, GLA_DK)
    v = p[..., o2:o3].reshape(bsz, slen, GLA_HEADS, GLA_DV)
    g = p[..., o3:o4]
    a_lr = p[..., o4:o5]
    u = p[..., o5:]
    log_a = (jax.nn.log_sigmoid((a_lr @ w_a2 + b_a2).astype(f32)) / GLA_TAU).reshape(bsz, slen, GLA_HEADS, GLA_DK)
    o, s_gla_new = gla_chunked(q, k, v, log_a, s_gla)
    o = rmsnorm(o, g_head.reshape(GLA_HEADS, GLA_DV)).reshape(bsz, slen, GLA_V_WIDTH) * jax.nn.silu(g.astype(f32))
    y5, s_re_new, s_im_new = s5_scan(u.astype(f32).reshape(bsz, slen, S5_GROUPS, S5_GROUP), s_re, s_im,
                                     a_re, a_im, log_dt, b_re, b_im, c_re, c_im, d_skip)
    z = jax.nn.gelu(y5.reshape(bsz, slen, S5_WIDTH))
    z = z * jax.nn.sigmoid(z @ w_glu.astype(f32) + b_glu.astype(f32))
    mix = jnp.concatenate([o, z], axis=-1).astype(x.dtype) @ w_out
    return mix, s_gla_new, s_re_new, s_im_new


def odd_mixer(x, s_lru, s_conv, g_norm, w_in, conv_w, conv_b, w_r, b_r, w_i, b_i, lam, w_out):
    p = rmsnorm(x, g_norm) @ w_in
    gate = jax.nn.gelu(p[..., :LRU_WIDTH].astype(jnp.float32))
    xc, conv_new = causal_conv(p[..., LRU_WIDTH:], s_conv, conv_w, conv_b)
    h, lru_new = rg_lru(xc, s_lru, w_r, b_r, w_i, b_i, lam)
    mix = (gate * h).astype(x.dtype) @ w_out
    return mix, lru_new, conv_new


def channel_mixer(x, g_norm, w_up, w_down):
    h = jnp.square(jax.nn.relu(rmsnorm(x, g_norm) @ w_up))
    return h @ w_down


def setup_inputs(seed: int = 0) -> dict:
    key = jax.random.key(seed)
    ks = iter(jax.random.split(key, 64))
    f32 = jnp.float32

    def nrm(shape, scale):
        return scale * jax.random.normal(next(ks), shape, f32)

    def gain(shape):
        return 1.0 + nrm(shape, 0.02)

    out_scale = (2.0 * DEPTH) ** -0.5
    lam_u = jax.random.uniform(next(ks), (N_ODD, LRU_WIDTH), f32, 0.9, 0.999)
    return {
        'x_prompt': nrm((BATCH, SEQ, D_MODEL), 1.0),
        'x_sample': nrm((DEC_BATCH, DEC_SEQ, D_MODEL), 1.0),
        'state_gla': nrm((N_EVEN, DEC_BATCH, GLA_HEADS, GLA_DK, GLA_DV), 1.0),
        'state_s5_re': nrm((N_EVEN, DEC_BATCH, S5_GROUPS, S5_STATE), 0.3),
        'state_s5_im': nrm((N_EVEN, DEC_BATCH, S5_GROUPS, S5_STATE), 0.3),
        'state_lru': nrm((N_ODD, DEC_BATCH, LRU_WIDTH), 0.5),
        'state_conv': nrm((N_ODD, DEC_BATCH, CONV_WIDTH - 1, LRU_WIDTH), 1.0),
        'norm_mix_even': gain((N_EVEN, D_MODEL)),
        'w_in_even': nrm((N_EVEN, D_MODEL, EVEN_IN_WIDTH), D_MODEL ** -0.5),
        'gla_w_a2': nrm((N_EVEN, GLA_RANK, GLA_K_WIDTH), GLA_RANK ** -0.5),
        'gla_b_a2': nrm((N_EVEN, GLA_K_WIDTH), 0.1),
        'gla_norm': gain((N_EVEN, GLA_V_WIDTH)),
        's5_a_re': -0.5 + nrm((N_EVEN, S5_GROUPS, S5_STATE), 0.01),
        's5_a_im': math.pi * jnp.arange(S5_STATE, dtype=f32) + nrm((N_EVEN, S5_GROUPS, S5_STATE), 0.01),
        's5_log_dt': jax.random.uniform(next(ks), (N_EVEN, S5_GROUPS), f32, math.log(1e-3), math.log(1e-1)),
        's5_b_re': nrm((N_EVEN, S5_GROUPS, S5_STATE, S5_GROUP), (2.0 * S5_GROUP) ** -0.5),
        's5_b_im': nrm((N_EVEN, S5_GROUPS, S5_STATE, S5_GROUP), (2.0 * S5_GROUP) ** -0.5),
        's5_c_re': nrm((N_EVEN, S5_GROUPS, S5_GROUP, S5_STATE), S5_STATE ** -0.5),
        's5_c_im': nrm((N_EVEN, S5_GROUPS, S5_GROUP, S5_STATE), S5_STATE ** -0.5),
        's5_d': nrm((N_EVEN, S5_GROUPS, S5_GROUP), 1.0),
        's5_w_glu': nrm((N_EVEN, S5_WIDTH, S5_WIDTH), S5_WIDTH ** -0.5),
        's5_b_glu': nrm((N_EVEN, S5_WIDTH), 0.01),
        'w_out_even': nrm((N_EVEN, GLA_V_WIDTH + S5_WIDTH, D_MODEL), (GLA_V_WIDTH + S5_WIDTH) ** -0.5 * out_scale),
        'norm_mix_odd': gain((N_ODD, D_MODEL)),
        'w_in_odd': nrm((N_ODD, D_MODEL, 2 * LRU_WIDTH), D_MODEL ** -0.5),
        'conv_w': nrm((N_ODD, CONV_WIDTH, LRU_WIDTH), CONV_WIDTH ** -0.5),
        'conv_b': nrm((N_ODD, LRU_WIDTH), 0.01),
        'lru_w_r': nrm((N_ODD, LRU_BLOCKS, LRU_BLOCK, LRU_BLOCK), LRU_BLOCK ** -0.5),
        'lru_b_r': nrm((N_ODD, LRU_WIDTH), 0.01),
        'lru_w_i': nrm((N_ODD, LRU_BLOCKS, LRU_BLOCK, LRU_BLOCK), LRU_BLOCK ** -0.5),
        'lru_b_i': nrm((N_ODD, LRU_WIDTH), 0.01),
        'lru_lam': jnp.log(lam_u) - jnp.log1p(-lam_u),
        'w_out_odd': nrm((N_ODD, LRU_WIDTH, D_MODEL), LRU_WIDTH ** -0.5 * out_scale),
        'norm_mlp': gain((DEPTH, D_MODEL)),
        'w_up': nrm((DEPTH, D_MODEL, D_FF), D_MODEL ** -0.5),
        'w_down': nrm((DEPTH, D_FF, D_MODEL), D_FF ** -0.5 * out_scale),
        'norm_final': gain((D_MODEL,)),
    }


def reference(x_prompt, x_sample, state_gla, state_s5_re, state_s5_im, state_lru, state_conv,
              norm_mix_even, w_in_even, gla_w_a2, gla_b_a2, gla_norm, s5_a_re, s5_a_im, s5_log_dt,
              s5_b_re, s5_b_im, s5_c_re, s5_c_im, s5_d, s5_w_glu, s5_b_glu, w_out_even,
              norm_mix_odd, w_in_odd, conv_w, conv_b, lru_w_r, lru_b_r, lru_w_i, lru_b_i, lru_lam, w_out_odd,
              norm_mlp, w_up, w_down, norm_final):
    f32 = jnp.float32
    bp = x_prompt.shape[0]
    yp, ys = x_prompt, x_sample
    gla_p, re_p, im_p, lru_p, conv_p = [], [], [], [], []
    gla_s, re_s, im_s, lru_s, conv_s = [], [], [], [], []
    for layer in range(DEPTH):
        if layer % 2 == 0:
            e = layer // 2
            prm = (norm_mix_even[e], w_in_even[e], gla_w_a2[e], gla_b_a2[e], gla_norm[e], s5_a_re[e], s5_a_im[e],
                   s5_log_dt[e], s5_b_re[e], s5_b_im[e], s5_c_re[e], s5_c_im[e], s5_d[e], s5_w_glu[e], s5_b_glu[e],
                   w_out_even[e])
            mp, sg, sr, si = even_mixer(yp, jnp.zeros((bp, GLA_HEADS, GLA_DK, GLA_DV), f32),
                                        jnp.zeros((bp, S5_GROUPS, S5_STATE), f32),
                                        jnp.zeros((bp, S5_GROUPS, S5_STATE), f32), *prm)
            gla_p.append(sg); re_p.append(sr); im_p.append(si)
            ms, sg, sr, si = even_mixer(ys, state_gla[e], state_s5_re[e], state_s5_im[e], *prm)
            gla_s.append(sg); re_s.append(sr); im_s.append(si)
        else:
            o = layer // 2
            prm = (norm_mix_odd[o], w_in_odd[o], conv_w[o], conv_b[o], lru_w_r[o], lru_b_r[o], lru_w_i[o],
                   lru_b_i[o], lru_lam[o], w_out_odd[o])
            mp, sl, sc = odd_mixer(yp, jnp.zeros((bp, LRU_WIDTH), f32),
                                   jnp.zeros((bp, CONV_WIDTH - 1, LRU_WIDTH), f32), *prm)
            lru_p.append(sl); conv_p.append(sc)
            ms, sl, sc = odd_mixer(ys, state_lru[o], state_conv[o], *prm)
            lru_s.append(sl); conv_s.append(sc)
        yp = yp + mp
        ys = ys + ms
        yp = yp + channel_mixer(yp, norm_mlp[layer], w_up[layer], w_down[layer])
        ys = ys + channel_mixer(ys, norm_mlp[layer], w_up[layer], w_down[layer])
    y_prompt = rmsnorm(yp, norm_final)
    y_sample = rmsnorm(ys, norm_final)
    return (y_prompt, y_sample,
            jnp.stack(gla_p), jnp.stack(re_p), jnp.stack(im_p), jnp.stack(lru_p), jnp.stack(conv_p),
            jnp.stack(gla_s), jnp.stack(re_s), jnp.stack(im_s), jnp.stack(lru_s), jnp.stack(conv_s))
```

```python
import functools
import math

import jax
import jax.numpy as jnp
from jax import lax
from jax.experimental import pallas as pl
from jax.experimental.pallas import tpu as pltpu

F32 = jnp.float32
BF16 = jnp.bfloat16

NORM_EPS = 1e-6
GLA_TAU = 16.0
GLA_CHUNK = 16
LRU_C = 8.0

SUBLANES = 8
LANES = 128
ROW_TILE = 512
FF_CHUNK = 1024
VMEM_LIMIT = 48 * 1024 * 1024


def _params(n_axes):
    return pltpu.CompilerParams(dimension_semantics=("arbitrary",) * n_axes, vmem_limit_bytes=VMEM_LIMIT)


def _const_spec(shape):
    zeros = (0,) * len(shape)
    return pl.BlockSpec(shape, lambda *_: zeros, pipeline_mode=pl.Buffered(1))


def _dot(a, b):
    return jnp.dot(a, b, preferred_element_type=F32)


def _dot_nt(a, b):
    return lax.dot_general(a, b, (((1,), (1,)), ((), ())), preferred_element_type=F32)


def _dot_tn(a, b):
    return lax.dot_general(a, b, (((0,), (0,)), ((), ())), preferred_element_type=F32)


def _rmsnorm(x, g):
    return x * lax.rsqrt(jnp.mean(x * x, axis=-1, keepdims=True) + NORM_EPS) * g


def _log_sigmoid(z):
    return jnp.minimum(z, 0.0) - jnp.log1p(jnp.exp(-jnp.abs(z)))


def _softplus(z):
    return jnp.maximum(z, 0.0) + jnp.log1p(jnp.exp(-jnp.abs(z)))


def _gelu(x):
    return jax.nn.gelu(x, approximate=True)


def _rows(ref, start, size):
    return ref[pl.ds(pl.multiple_of(start, size), size), :]


def _inproj_even_body(x_ref, g_ref, w_ref, wa2_ref, ba2_ref, q_ref, k_ref, v_ref, gt_ref, u_ref, la_ref, *, kw, vw, sw):
    xn = _rmsnorm(x_ref[...], g_ref[...]).astype(BF16)
    o1, o2 = kw, 2 * kw
    o3 = o2 + vw
    o4 = o3 + vw
    o5 = o4 + sw
    q_ref[...] = _dot(xn, w_ref[:, 0:o1])
    k_ref[...] = _dot(xn, w_ref[:, o1:o2])
    v_ref[...] = _dot(xn, w_ref[:, o2:o3]).astype(BF16)
    gt_ref[...] = _dot(xn, w_ref[:, o3:o4])
    u_ref[...] = _dot(xn, w_ref[:, o4:o5])
    a_lr = _dot(xn, w_ref[:, o5:o5 + LANES]).astype(BF16)
    z = _dot(a_lr, wa2_ref[...]) + ba2_ref[...]
    la_ref[...] = _log_sigmoid(z) * (1.0 / GLA_TAU)


def _inproj_even(x2d, g, w, wa2, ba2, *, kw, vw, sw):
    rows, d = x2d.shape
    tm = min(ROW_TILE, rows)
    row_spec = lambda width: pl.BlockSpec((tm, width), lambda i: (i, 0))
    out_shapes = (
        jax.ShapeDtypeStruct((rows, kw), F32),
        jax.ShapeDtypeStruct((rows, kw), F32),
        jax.ShapeDtypeStruct((rows, vw), BF16),
        jax.ShapeDtypeStruct((rows, vw), F32),
        jax.ShapeDtypeStruct((rows, sw), F32),
        jax.ShapeDtypeStruct((rows, kw), F32),
    )
    return pl.pallas_call(
        functools.partial(_inproj_even_body, kw=kw, vw=vw, sw=sw),
        grid=(rows // tm,),
        in_specs=[row_spec(d), _const_spec(g.shape), _const_spec(w.shape), _const_spec(wa2.shape), _const_spec(ba2.shape)],
        out_specs=(row_spec(kw), row_spec(kw), row_spec(vw), row_spec(vw), row_spec(sw), row_spec(kw)),
        out_shape=out_shapes,
        compiler_params=_params(1),
        name="inproj_even",
    )(x2d, g, w, wa2, ba2)


def _inproj_odd_body(x_ref, g_ref, w_ref, gate_ref, xin_ref, *, width):
    xn = _rmsnorm(x_ref[...], g_ref[...]).astype(BF16)
    gate_ref[...] = _dot(xn, w_ref[:, 0:width])
    xin_ref[...] = _dot(xn, w_ref[:, width:2 * width])


def _inproj_odd(x2d, g, w):
    rows, d = x2d.shape
    width = w.shape[1] // 2
    tm = min(ROW_TILE, rows)
    row_spec = lambda wd: pl.BlockSpec((tm, wd), lambda i: (i, 0))
    return pl.pallas_call(
        functools.partial(_inproj_odd_body, width=width),
        grid=(rows // tm,),
        in_specs=[row_spec(d), _const_spec(g.shape), _const_spec(w.shape)],
        out_specs=(row_spec(width), row_spec(width)),
        out_shape=(jax.ShapeDtypeStruct((rows, width), F32), jax.ShapeDtypeStruct((rows, width), F32)),
        compiler_params=_params(1),
        name="inproj_odd",
    )(x2d, g, w)


def _out_mlp_body(x_ref, a1_ref, a2_ref, wo_ref, g_ref, wu_ref, wd_ref, gf_ref, o_ref, *, final):
    half = a1_ref.shape[1]
    mix = _dot(a1_ref[...], wo_ref[0:half, :]) + _dot(a2_ref[...], wo_ref[half:2 * half, :])
    x1 = x_ref[...] + mix
    xn = _rmsnorm(x1, g_ref[...]).astype(BF16)
    acc = x1
    for c in range(wu_ref.shape[1] // FF_CHUNK):
        h = _dot(xn, wu_ref[:, c * FF_CHUNK:(c + 1) * FF_CHUNK])
        h = jnp.square(jnp.maximum(h, 0.0)).astype(BF16)
        acc = acc + _dot(h, wd_ref[c * FF_CHUNK:(c + 1) * FF_CHUNK, :])
    if final:
        acc = _rmsnorm(acc, gf_ref[...])
    o_ref[...] = acc


def _out_mlp(x2d, a1, a2, col2, wo, g, wu, wd, gf, *, final):
    rows, d = x2d.shape
    half = d // 2
    tm = min(ROW_TILE, rows)
    return pl.pallas_call(
        functools.partial(_out_mlp_body, final=final),
        grid=(rows // tm,),
        in_specs=[
            pl.BlockSpec((tm, d), lambda i: (i, 0)),
            pl.BlockSpec((tm, half), lambda i: (i, 0)),
            pl.BlockSpec((tm, half), lambda i: (i, col2)),
            _const_spec(wo.shape), _const_spec(g.shape), _const_spec(wu.shape), _const_spec(wd.shape), _const_spec(gf.shape),
        ],
        out_specs=pl.BlockSpec((tm, d), lambda i: (i, 0)),
        out_shape=jax.ShapeDtypeStruct((rows, d), F32),
        compiler_params=_params(1),
        name="out_mlp",
    )(x2d, a1, a2, wo, g, wu, wd, gf)


def _time_grid(total_rows, n_bg, tblock):
    steps = total_rows // (n_bg * SUBLANES)
    tb = min(tblock, steps)
    n_t = steps // tb
    blk = lambda width: pl.BlockSpec((tb * SUBLANES, width), lambda bg, t: (bg * n_t + t, 0))
    per_group = lambda rows, width: pl.BlockSpec((rows, width), lambda bg, t: (bg, 0))
    return tb, (n_bg, n_t), blk, per_group


def _gla_body(q_ref, k_ref, la_ref, v_ref, gt_ref, gh_ref, s0_ref, o_ref, sfin_ref, st_ref, *, c, nh, dk, dv):
    tb = pl.program_id(1)
    rows = SUBLANES * c
    n_chunks = q_ref.shape[0] // rows
    dk_shift = int(math.log2(dk))
    heads_per_tile = LANES // dk

    @pl.when(tb == 0)
    def _():
        st_ref[...] = s0_ref[...]

    def iota(shape, axis):
        return lax.broadcasted_iota(jnp.int32, shape, axis)

    causal = ((iota((rows, rows), 0) & 7) == (iota((rows, rows), 1) & 7)) & (iota((rows, rows), 1) <= iota((rows, rows), 0))
    emask = (iota((rows, SUBLANES * dk), 0) & 7) == lax.shift_right_logical(iota((rows, SUBLANES * dk), 1), dk_shift)
    dmask = iota((SUBLANES, SUBLANES * dk), 0) == lax.shift_right_logical(iota((SUBLANES, SUBLANES * dk), 1), dk_shift)
    lo_rows = iota((rows, LANES), 1) < dk
    lo_8 = iota((SUBLANES, LANES), 1) < dk

    def expand(tile, h, lo, mask):
        rolled = pltpu.roll(tile, dk, axis=1)
        d = jnp.where(lo, tile, rolled) if h % heads_per_tile == 0 else jnp.where(lo, rolled, tile)
        return jnp.where(mask, jnp.concatenate([d] * (SUBLANES * dk // LANES), axis=1), 0.0)

    def chunk(n, carry):
        r0 = n * rows
        q = _rows(q_ref, r0, rows)
        k = _rows(k_ref, r0, rows)
        la = _rows(la_ref, r0, rows)
        v = _rows(v_ref, r0, rows)
        gt = _rows(gt_ref, r0, rows)
        acc = jnp.zeros((SUBLANES, nh * dk), F32)
        pieces = []
        for t in range(c):
            acc = acc + la[t * SUBLANES:(t + 1) * SUBLANES, :]
            pieces.append(acc)
        cum = jnp.concatenate(pieces, axis=0)
        last = jnp.concatenate([acc] * c, axis=0)
        q_t = q * jnp.exp(cum) * (dk ** -0.5)
        k_t = k * jnp.exp(-cum)
        k_end = k * jnp.exp(last - cum)
        dec = jnp.exp(acc)
        outs = []
        for h in range(nh):
            tile = (h // heads_per_tile) * LANES
            sel = lo_rows if h % heads_per_tile == 0 else jnp.logical_not(lo_rows)
            q_tile = q_t[:, tile:tile + LANES]
            scores = _dot_nt(jnp.where(sel, q_tile, 0.0).astype(BF16), k_t[:, tile:tile + LANES].astype(BF16))
            scores = jnp.where(causal, scores, 0.0).astype(BF16)
            v_h = v[:, h * dv:(h + 1) * dv]
            st = st_ref[h]
            q_exp = expand(q_tile, h, lo_rows, emask).astype(BF16)
            o = _dot(scores, v_h) + _dot_nt(q_exp, st.astype(BF16))
            k_exp = expand(k_end[:, tile:tile + LANES], h, lo_rows, emask).astype(BF16)
            dec_row = jnp.sum(expand(dec[:, tile:tile + LANES], h, lo_8, dmask), axis=0, keepdims=True)
            st_ref[h] = st * dec_row + _dot_tn(v_h, k_exp)
            o = _rmsnorm(o, gh_ref[:, h * dv:(h + 1) * dv])
            outs.append(o * jax.nn.silu(gt[:, h * dv:(h + 1) * dv]))
        o_ref[pl.ds(pl.multiple_of(r0, rows), rows), :] = jnp.concatenate(outs, axis=1).astype(BF16)
        return carry

    lax.fori_loop(0, n_chunks, chunk, 0)

    @pl.when(tb == pl.num_programs(1) - 1)
    def _():
        sfin_ref[...] = st_ref[...]


def _gla(q, k, la, v, gt, gh, s0t, *, c, nh, dk, dv, tblock):
    n_bg = s0t.shape[0]
    tb, grid, blk, _ = _time_grid(q.shape[0], n_bg, tblock)
    st_spec = pl.BlockSpec((None,) + s0t.shape[1:], lambda bg, t: (bg, 0, 0, 0))
    return pl.pallas_call(
        functools.partial(_gla_body, c=c, nh=nh, dk=dk, dv=dv),
        grid=grid,
        in_specs=[blk(nh * dk), blk(nh * dk), blk(nh * dk), blk(nh * dv), blk(nh * dv), _const_spec(gh.shape), st_spec],
        out_specs=(blk(nh * dv), st_spec),
        out_shape=(jax.ShapeDtypeStruct((q.shape[0], nh * dv), BF16), jax.ShapeDtypeStruct(s0t.shape, F32)),
        scratch_shapes=[pltpu.VMEM(s0t.shape[1:], F32)],
        compiler_params=_params(2),
        name="gla",
    )(q, k, la, v, gt, gh, s0t)


def _s5_disc_body(a_re_ref, a_im_ref, log_dt_ref, b_re_ref, b_im_ref, ab_re_ref, ab_im_ref, bb_re_ref, bb_im_ref):
    a_re = a_re_ref[...]
    a_im = a_im_ref[...]
    dt = jnp.exp(log_dt_ref[...])
    mag = jnp.exp(dt * a_re)
    ab_re = mag * jnp.cos(dt * a_im)
    ab_im = mag * jnp.sin(dt * a_im)
    den = a_re * a_re + a_im * a_im
    coef_re = ((ab_re - 1.0) * a_re + ab_im * a_im) / den
    coef_im = (ab_im * a_re - (ab_re - 1.0) * a_im) / den
    b_re = b_re_ref[...]
    b_im = b_im_ref[...]
    ab_re_ref[...] = ab_re
    ab_im_ref[...] = ab_im
    bb_re_ref[...] = coef_re * b_re - coef_im * b_im
    bb_im_ref[...] = coef_re * b_im + coef_im * b_re


def _s5_discretise(a_re, a_im, log_dt, b_re, b_im):
    g, p, ch = b_re.shape
    rep = lambda t: jnp.repeat(t, ch, axis=0)
    flat = lambda t: jnp.transpose(t, (0, 2, 1)).reshape(g * ch, p)
    args = (rep(a_re), rep(a_im), rep(jnp.broadcast_to(log_dt[:, None], (g, p))), flat(b_re), flat(b_im))
    spec = _const_spec((g * ch, p))
    shape = jax.ShapeDtypeStruct((g * ch, p), F32)
    ab_re, ab_im, bb_re, bb_im = pl.pallas_call(
        _s5_disc_body, grid=(1,), in_specs=[spec] * 5, out_specs=(spec,) * 4, out_shape=(shape,) * 4,
        compiler_params=_params(1), name="s5_discretise",
    )(*args)
    return ab_re[::ch], ab_im[::ch], bb_re.reshape(g, ch, p), bb_im.reshape(g, ch, p)


def _block_diag_in(w_gcp):
    g, ch, p = w_gcp.shape
    gpt = LANES // ch
    w = w_gcp.reshape(g // gpt, gpt, ch, p)
    eye = jnp.eye(gpt, dtype=w.dtype)
    return jnp.einsum("ab,jacp->jacbp", eye, w).reshape(g // gpt, gpt * ch, gpt * p)


def _block_diag_out(w_gcp):
    g, ch, p = w_gcp.shape
    gpt = LANES // ch
    w = w_gcp.reshape(g // gpt, gpt, ch, p)
    eye = jnp.eye(gpt, dtype=w.dtype)
    return jnp.einsum("ab,jacp->japbc", eye, w).reshape(g // gpt, gpt * p, gpt * ch)


def _s5_body(u_ref, ab_re_ref, ab_im_ref, wb_re_ref, wb_im_ref, wc_ref, d_ref, wglu_ref, bglu_ref, h0_re_ref, h0_im_ref,
             z_ref, hf_re_ref, hf_im_ref, h_re_ref, h_im_ref, bu_re_ref, bu_im_ref, y_ref):
    tb = pl.program_id(1)
    rows = u_ref.shape[0]
    steps = rows // SUBLANES
    n_tiles = wb_re_ref.shape[0]
    sw = wb_re_ref.shape[2]

    @pl.when(tb == 0)
    def _():
        h_re_ref[...] = h0_re_ref[...]
        h_im_ref[...] = h0_im_ref[...]

    for j in range(n_tiles):
        u_j = u_ref[:, j * LANES:(j + 1) * LANES]
        u_b = u_j.astype(BF16)
        bu_re_ref[...] = _dot(u_b, wb_re_ref[j])
        bu_im_ref[...] = _dot(u_b, wb_im_ref[j])
        a_re = jnp.broadcast_to(ab_re_ref[:, j * sw:(j + 1) * sw], (SUBLANES, sw))
        a_im = jnp.broadcast_to(ab_im_ref[:, j * sw:(j + 1) * sw], (SUBLANES, sw))

        def step(t, carry):
            h_re, h_im = carry
            sl = pl.ds(pl.multiple_of(t * SUBLANES, SUBLANES), SUBLANES)
            n_re = a_re * h_re - a_im * h_im + bu_re_ref[sl, :]
            n_im = a_re * h_im + a_im * h_re + bu_im_ref[sl, :]
            bu_re_ref[sl, :] = n_re
            bu_im_ref[sl, :] = n_im
            return n_re, n_im

        h_re, h_im = lax.fori_loop(0, steps, step, (h_re_ref[:, j * sw:(j + 1) * sw], h_im_ref[:, j * sw:(j + 1) * sw]),
                                   unroll=min(steps, 8))
        h_re_ref[:, j * sw:(j + 1) * sw] = h_re
        h_im_ref[:, j * sw:(j + 1) * sw] = h_im
        h_cat = jnp.concatenate([bu_re_ref[...].astype(BF16), bu_im_ref[...].astype(BF16)], axis=1)
        y_ref[:, j * LANES:(j + 1) * LANES] = _dot(h_cat, wc_ref[j]) + d_ref[:, j * LANES:(j + 1) * LANES] * u_j

    z = _gelu(y_ref[...])
    gate = jax.nn.sigmoid(_dot(z.astype(BF16), wglu_ref[...]) + bglu_ref[...])
    z_ref[...] = (z * gate).astype(BF16)

    @pl.when(tb == pl.num_programs(1) - 1)
    def _():
        hf_re_ref[...] = h_re_ref[...]
        hf_im_ref[...] = h_im_ref[...]


def _s5(u, ab_re, ab_im, wb_re, wb_im, wc, d_skip, wglu, bglu, h0_re, h0_im, *, tblock):
    total, width = u.shape
    n_state = ab_re.shape[1]
    n_bg = h0_re.shape[0] // SUBLANES
    sw = wb_re.shape[2]
    tb, grid, blk, per_group = _time_grid(total, n_bg, tblock)
    rows = tb * SUBLANES
    st_spec = per_group(SUBLANES, n_state)
    consts = (ab_re, ab_im, wb_re, wb_im, wc, d_skip, wglu, bglu)
    st_shape = jax.ShapeDtypeStruct((n_bg * SUBLANES, n_state), F32)
    return pl.pallas_call(
        _s5_body,
        grid=grid,
        in_specs=[blk(width)] + [_const_spec(t.shape) for t in consts] + [st_spec, st_spec],
        out_specs=(blk(width), st_spec, st_spec),
        out_shape=(jax.ShapeDtypeStruct((total, width), BF16), st_shape, st_shape),
        scratch_shapes=[pltpu.VMEM((SUBLANES, n_state), F32), pltpu.VMEM((SUBLANES, n_state), F32),
                        pltpu.VMEM((rows, sw), F32), pltpu.VMEM((rows, sw), F32), pltpu.VMEM((rows, width), F32)],
        compiler_params=_params(2),
        name="s5",
    )(u, *consts, h0_re, h0_im)


def _lru_body(gate_ref, xin_ref, conv0_ref, h0_ref, cw_ref, cb_ref, wr_ref, br_ref, wi_ref, bi_ref, lam_ref,
              o_ref, hf_ref, convf_ref, xbuf_ref, a_ref, b_ref, h_ref):
    tb = pl.program_id(1)
    rows, width = xin_ref.shape
    steps = rows // SUBLANES
    taps = cw_ref.shape[0]
    tail = (taps - 1) * SUBLANES
    n_blocks = wr_ref.shape[0]
    bw = wr_ref.shape[1]

    @pl.when(tb == 0)
    def _():
        xbuf_ref[0:tail, :] = conv0_ref[...]
        h_ref[...] = h0_ref[...]

    xbuf_ref[tail:tail + rows, :] = xin_ref[...]
    for n in range(n_blocks):
        cols = slice(n * bw, (n + 1) * bw)
        xc = cb_ref[:, cols] + xbuf_ref[0:rows, cols] * cw_ref[0:1, cols]
        for j in range(1, taps):
            xc = xc + xbuf_ref[j * SUBLANES:j * SUBLANES + rows, cols] * cw_ref[j:j + 1, cols]
        xb = xc.astype(BF16)
        r = jax.nn.sigmoid(_dot(xb, wr_ref[n]) + br_ref[:, cols])
        i = jax.nn.sigmoid(_dot(xb, wi_ref[n]) + bi_ref[:, cols])
        log_a = -LRU_C * r * _softplus(-lam_ref[:, cols])
        a_ref[:, cols] = jnp.exp(log_a)
        th = jnp.tanh(log_a)
        b_ref[:, cols] = jnp.sqrt(-2.0 * th / (1.0 - th)) * (i * xc)

    def step(t, h):
        sl = pl.ds(pl.multiple_of(t * SUBLANES, SUBLANES), SUBLANES)
        h = a_ref[sl, :] * h + b_ref[sl, :]
        b_ref[sl, :] = h
        return h

    h_ref[...] = lax.fori_loop(0, steps, step, h_ref[...], unroll=min(steps, 8))
    o_ref[...] = (_gelu(gate_ref[...]) * b_ref[...]).astype(BF16)
    xbuf_ref[0:tail, :] = xbuf_ref[rows:rows + tail, :]

    @pl.when(tb == pl.num_programs(1) - 1)
    def _():
        hf_ref[...] = h_ref[...]
        convf_ref[...] = xbuf_ref[0:tail, :]


def _lru(gate, xin, conv0, h0, cw, cb, wr, br, wi, bi, lam, *, tblock):
    total, width = xin.shape
    n_bg = h0.shape[0] // SUBLANES
    tail = (cw.shape[0] - 1) * SUBLANES
    tb, grid, blk, per_group = _time_grid(total, n_bg, tblock)
    rows = tb * SUBLANES
    h_spec = per_group(SUBLANES, width)
    c_spec = per_group(tail, width)
    consts = (cw, cb, wr, br, wi, bi, lam)
    return pl.pallas_call(
        _lru_body,
        grid=grid,
        in_specs=[blk(width), blk(width), c_spec, h_spec] + [_const_spec(t.shape) for t in consts],
        out_specs=(blk(width), h_spec, c_spec),
        out_shape=(jax.ShapeDtypeStruct((total, width), BF16), jax.ShapeDtypeStruct((n_bg * SUBLANES, width), F32),
                   jax.ShapeDtypeStruct((n_bg * tail, width), F32)),
        scratch_shapes=[pltpu.VMEM((rows + tail, width), F32), pltpu.VMEM((rows, width), F32),
                        pltpu.VMEM((rows, width), F32), pltpu.VMEM((SUBLANES, width), F32)],
        compiler_params=_params(2),
        name="rg_lru",
    )(gate, xin, conv0, h0, *consts)


def _row(t):
    return t.reshape(1, -1).astype(F32)


def _to_rows(x):
    b, s, d = x.shape
    return jnp.transpose(x.reshape(b // SUBLANES, SUBLANES, s, d), (0, 2, 1, 3)).reshape(b * s, d)


def _from_rows(y, b, s):
    d = y.shape[-1]
    return jnp.transpose(y.reshape(b // SUBLANES, s, SUBLANES, d), (0, 2, 1, 3)).reshape(b, s, d)


def _gla_state_in(s):
    b, h, dk, dv = s.shape
    return jnp.transpose(s.reshape(b // SUBLANES, SUBLANES, h, dk, dv), (0, 2, 4, 1, 3)).reshape(b // SUBLANES, h, dv, SUBLANES * dk)


def _gla_state_out(st, dk):
    n_bg, h, dv, _ = st.shape
    return jnp.transpose(st.reshape(n_bg, h, dv, SUBLANES, dk), (0, 3, 1, 4, 2)).reshape(n_bg * SUBLANES, h, dk, dv)


def _even_layer(x2d, nb, s_gla, s_re, s_im, prm, mlp, *, final):
    (g_norm, w_in, wa2, ba2, g_head, ab_re, ab_im, wb_re, wb_im, wc, d_skip, wglu, bglu, w_out, dims) = prm
    nh, dk, dv, sw = dims
    n_bg = nb // SUBLANES
    steps = x2d.shape[0] // nb
    q, k, v, gt, u, la = _inproj_even(x2d, g_norm, w_in, wa2, ba2, kw=nh * dk, vw=nh * dv, sw=sw)
    s0t = jnp.zeros((n_bg, nh, dv, SUBLANES * dk), F32) if s_gla is None else _gla_state_in(s_gla.astype(F32))
    og, sfin = _gla(q, k, la, v, gt, g_head, s0t, c=math.gcd(steps, GLA_CHUNK), nh=nh, dk=dk, dv=dv, tblock=128)
    n_state = ab_re.shape[1]
    h0_re = jnp.zeros((nb, n_state), F32) if s_re is None else s_re.astype(F32).reshape(nb, n_state)
    h0_im = jnp.zeros((nb, n_state), F32) if s_im is None else s_im.astype(F32).reshape(nb, n_state)
    zz, hf_re, hf_im = _s5(u, ab_re, ab_im, wb_re, wb_im, wc, d_skip, wglu, bglu, h0_re, h0_im, tblock=64)
    g_mlp, w_up, w_down, g_fin = mlp
    y = _out_mlp(x2d, og, zz, 0, w_out, g_mlp, w_up, w_down, g_fin, final=final)
    return y, _gla_state_out(sfin, dk), hf_re, hf_im


def _odd_layer(x2d, nb, s_lru, s_conv, prm, mlp, *, final):
    (g_norm, w_in, cw, cb, wr, br, wi, bi, lam, w_out) = prm
    n_bg = nb // SUBLANES
    taps = cw.shape[0]
    width = w_in.shape[1] // 2
    gate, xin = _inproj_odd(x2d, g_norm, w_in)
    h0 = jnp.zeros((nb, width), F32) if s_lru is None else s_lru.astype(F32)
    if s_conv is None:
        conv0 = jnp.zeros((n_bg * (taps - 1) * SUBLANES, width), F32)
    else:
        conv0 = jnp.transpose(s_conv.astype(F32).reshape(n_bg, SUBLANES, taps - 1, width), (0, 2, 1, 3)).reshape(-1, width)
    gh, hf, convf = _lru(gate, xin, conv0, h0, cw, cb, wr, br, wi, bi, lam, tblock=64)
    g_mlp, w_up, w_down, g_fin = mlp
    y = _out_mlp(x2d, gh, gh, 1, w_out, g_mlp, w_up, w_down, g_fin, final=final)
    conv_new = jnp.transpose(convf.reshape(n_bg, taps - 1, SUBLANES, width), (0, 2, 1, 3)).reshape(nb, taps - 1, width)
    return y, hf, conv_new


def kernel(x_prompt, x_sample, state_gla, state_s5_re, state_s5_im, state_lru, state_conv, norm_mix_even, w_in_even, gla_w_a2, gla_b_a2, gla_norm, s5_a_re, s5_a_im, s5_log_dt, s5_b_re, s5_b_im, s5_c_re, s5_c_im, s5_d, s5_w_glu, s5_b_glu, w_out_even, norm_mix_odd, w_in_odd, conv_w, conv_b, lru_w_r, lru_b_r, lru_w_i, lru_b_i, lru_lam, w_out_odd, norm_mlp, w_up, w_down, norm_final):
    depth = norm_mlp.shape[0]
    nh, dk, dv = state_gla.shape[2:]
    rank = gla_w_a2.shape[1]
    n_groups, n_p, ch = s5_b_re.shape[1:]
    sw = n_groups * ch
    kw, vw = nh * dk, nh * dv
    assert rank <= LANES and LANES % ch == 0 and LANES % dk == 0 and dv == LANES
    bp, sp, _ = x_prompt.shape
    bs, ss, _ = x_sample.shape

    yp = _to_rows(x_prompt.astype(F32))
    ys = _to_rows(x_sample.astype(F32))
    g_fin = _row(norm_final)
    keys = ("gla", "re", "im", "lru", "conv")
    outs_p = {key: [] for key in keys}
    outs_s = {key: [] for key in keys}

    def record(outs, sg, sr, si):
        outs["gla"].append(sg)
        outs["re"].append(sr.reshape(-1, n_groups, n_p))
        outs["im"].append(si.reshape(-1, n_groups, n_p))

    for layer in range(depth):
        final = layer == depth - 1
        mlp = (_row(norm_mlp[layer]), w_up[layer].astype(BF16), w_down[layer].astype(BF16), g_fin)
        if layer % 2 == 0:
            e = layer // 2
            w = w_in_even[e]
            o4 = 2 * kw + 2 * vw
            o5 = o4 + rank
            w_perm = jnp.concatenate([w[:, :o4], w[:, o5:], w[:, o4:o5], jnp.zeros((w.shape[0], LANES - rank), w.dtype)], axis=1)
            wa2 = jnp.concatenate([gla_w_a2[e], jnp.zeros((LANES - rank, kw), gla_w_a2.dtype)], axis=0).astype(BF16)
            ab_re, ab_im, bb_re, bb_im = _s5_discretise(s5_a_re[e].astype(F32), s5_a_im[e].astype(F32), s5_log_dt[e].astype(F32),
                                                        s5_b_re[e].astype(F32), s5_b_im[e].astype(F32))
            wc = jnp.concatenate([_block_diag_out(s5_c_re[e].astype(F32)), -_block_diag_out(s5_c_im[e].astype(F32))], axis=1)
            prm = (_row(norm_mix_even[e]), w_perm.astype(BF16), wa2, _row(gla_b_a2[e]), _row(gla_norm[e]),
                   ab_re.reshape(1, -1), ab_im.reshape(1, -1), _block_diag_in(bb_re).astype(BF16), _block_diag_in(bb_im).astype(BF16),
                   wc.astype(BF16), _row(s5_d[e]), s5_w_glu[e].astype(BF16), _row(s5_b_glu[e]), w_out_even[e].astype(BF16),
                   (nh, dk, dv, sw))
            yp, sg, sr, si = _even_layer(yp, bp, None, None, None, prm, mlp, final=final)
            record(outs_p, sg, sr, si)
            ys, sg, sr, si = _even_layer(ys, bs, state_gla[e], state_s5_re[e], state_s5_im[e], prm, mlp, final=final)
            record(outs_s, sg, sr, si)
        else:
            o = layer // 2
            prm = (_row(norm_mix_odd[o]), w_in_odd[o].astype(BF16), conv_w[o].astype(F32), _row(conv_b[o]),
                   lru_w_r[o].astype(BF16), _row(lru_b_r[o]), lru_w_i[o].astype(BF16), _row(lru_b_i[o]), _row(lru_lam[o]),
                   w_out_odd[o].astype(BF16))
            yp, sl, sc = _odd_layer(yp, bp, None, None, prm, mlp, final=final)
            outs_p["lru"].append(sl)
            outs_p["conv"].append(sc)
            ys, sl, sc = _odd_layer(ys, bs, state_lru[o], state_conv[o], prm, mlp, final=final)
            outs_s["lru"].append(sl)
            outs_s["conv"].append(sc)

    stack = lambda outs: tuple(jnp.stack(outs[key]) for key in keys)
    return (_from_rows(yp, bp, sp), _from_rows(ys, bs, ss)) + stack(outs_p) + stack(outs_s)
```

```python
import functools
import math

import jax
import jax.numpy as jnp
from jax import lax
from jax.experimental import pallas as pl
from jax.experimental.pallas import tpu as pltpu

F32 = jnp.float32
BF16 = jnp.bfloat16

NORM_EPS = 1e-6
GLA_TAU = 16.0
GLA_CHUNK = 16
LRU_C = 8.0

SUBLANES = 8
LANES = 128
ROW_TILE = 512
FF_CHUNK = 1024
VMEM_LIMIT = 48 * 1024 * 1024


def _params(n_axes):
    return pltpu.CompilerParams(dimension_semantics=("arbitrary",) * n_axes, vmem_limit_bytes=VMEM_LIMIT)


def _const_spec(shape):
    zeros = (0,) * len(shape)
    return pl.BlockSpec(shape, lambda *_: zeros, pipeline_mode=pl.Buffered(1))


def _dot(a, b):
    return jnp.dot(a, b, preferred_element_type=F32)


def _dot_nt(a, b):
    return lax.dot_general(a, b, (((1,), (1,)), ((), ())), preferred_element_type=F32)


def _dot_tn(a, b):
    return lax.dot_general(a, b, (((0,), (0,)), ((), ())), preferred_element_type=F32)


def _rmsnorm(x, g):
    return x * lax.rsqrt(jnp.mean(x * x, axis=-1, keepdims=True) + NORM_EPS) * g


def _log_sigmoid(z):
    return jnp.minimum(z, 0.0) - jnp.log1p(jnp.exp(-jnp.abs(z)))


def _softplus(z):
    return jnp.maximum(z, 0.0) + jnp.log1p(jnp.exp(-jnp.abs(z)))


_GELU_C1 = math.sqrt(2.0 / math.pi)
_GELU_C2 = _GELU_C1 * 0.044715


def _gelu(x):
    half = 0.5 * x
    return half + half * jnp.tanh(x * (_GELU_C1 + _GELU_C2 * (x * x)))


def _sigmoid(x):
    return 0.5 * jnp.tanh(0.5 * x) + 0.5


def _rows(ref, start, size):
    return ref[pl.ds(pl.multiple_of(start, size), size), :]


def _inproj_even_body(x_ref, g_ref, w_ref, wa2_ref, ba2_ref, q_ref, k_ref, v_ref, gt_ref, u_ref, la_ref, *, kw, vw, sw):
    xn = _rmsnorm(x_ref[...], g_ref[...]).astype(BF16)
    o1, o2 = kw, 2 * kw
    o3 = o2 + vw
    o4 = o3 + vw
    o5 = o4 + sw
    q_ref[...] = _dot(xn, w_ref[:, 0:o1])
    k_ref[...] = _dot(xn, w_ref[:, o1:o2])
    v_ref[...] = _dot(xn, w_ref[:, o2:o3]).astype(BF16)
    gt_ref[...] = _dot(xn, w_ref[:, o3:o4])
    u_ref[...] = _dot(xn, w_ref[:, o4:o5])
    a_lr = _dot(xn, w_ref[:, o5:o5 + LANES]).astype(BF16)
    z = _dot(a_lr, wa2_ref[...]) + ba2_ref[...]
    la_ref[...] = _log_sigmoid(z) * (1.0 / GLA_TAU)


def _inproj_even(x2d, g, w, wa2, ba2, *, kw, vw, sw):
    rows, d = x2d.shape
    tm = min(ROW_TILE, rows)
    row_spec = lambda width: pl.BlockSpec((tm, width), lambda i: (i, 0))
    out_shapes = (
        jax.ShapeDtypeStruct((rows, kw), F32),
        jax.ShapeDtypeStruct((rows, kw), F32),
        jax.ShapeDtypeStruct((rows, vw), BF16),
        jax.ShapeDtypeStruct((rows, vw), F32),
        jax.ShapeDtypeStruct((rows, sw), F32),
        jax.ShapeDtypeStruct((rows, kw), F32),
    )
    return pl.pallas_call(
        functools.partial(_inproj_even_body, kw=kw, vw=vw, sw=sw),
        grid=(rows // tm,),
        in_specs=[row_spec(d), _const_spec(g.shape), _const_spec(w.shape), _const_spec(wa2.shape), _const_spec(ba2.shape)],
        out_specs=(row_spec(kw), row_spec(kw), row_spec(vw), row_spec(vw), row_spec(sw), row_spec(kw)),
        out_shape=out_shapes,
        compiler_params=_params(1),
        name="inproj_even",
    )(x2d, g, w, wa2, ba2)


def _inproj_odd_body(x_ref, g_ref, w_ref, gate_ref, xin_ref, *, width):
    xn = _rmsnorm(x_ref[...], g_ref[...]).astype(BF16)
    gate_ref[...] = _dot(xn, w_ref[:, 0:width])
    xin_ref[...] = _dot(xn, w_ref[:, width:2 * width])


def _inproj_odd(x2d, g, w):
    rows, d = x2d.shape
    width = w.shape[1] // 2
    tm = min(ROW_TILE, rows)
    row_spec = lambda wd: pl.BlockSpec((tm, wd), lambda i: (i, 0))
    return pl.pallas_call(
        functools.partial(_inproj_odd_body, width=width),
        grid=(rows // tm,),
        in_specs=[row_spec(d), _const_spec(g.shape), _const_spec(w.shape)],
        out_specs=(row_spec(width), row_spec(width)),
        out_shape=(jax.ShapeDtypeStruct((rows, width), F32), jax.ShapeDtypeStruct((rows, width), F32)),
        compiler_params=_params(1),
        name="inproj_odd",
    )(x2d, g, w)


def _out_mlp_body(x_ref, a1_ref, a2_ref, wo_ref, g_ref, wu_ref, wd_ref, gf_ref, o_ref, *, final):
    half = a1_ref.shape[1]
    mix = _dot(a1_ref[...], wo_ref[0:half, :]) + _dot(a2_ref[...], wo_ref[half:2 * half, :])
    x1 = x_ref[...] + mix
    xn = _rmsnorm(x1, g_ref[...]).astype(BF16)
    acc = x1
    for c in range(wu_ref.shape[1] // FF_CHUNK):
        h = _dot(xn, wu_ref[:, c * FF_CHUNK:(c + 1) * FF_CHUNK])
        h = jnp.square(jnp.maximum(h, 0.0)).astype(BF16)
        acc = acc + _dot(h, wd_ref[c * FF_CHUNK:(c + 1) * FF_CHUNK, :])
    if final:
        acc = _rmsnorm(acc, gf_ref[...])
    o_ref[...] = acc


def _out_mlp(x2d, a1, a2, col2, wo, g, wu, wd, gf, *, final):
    rows, d = x2d.shape
    half = d // 2
    tm = min(ROW_TILE, rows)
    return pl.pallas_call(
        functools.partial(_out_mlp_body, final=final),
        grid=(rows // tm,),
        in_specs=[
            pl.BlockSpec((tm, d), lambda i: (i, 0)),
            pl.BlockSpec((tm, half), lambda i: (i, 0)),
            pl.BlockSpec((tm, half), lambda i: (i, col2)),
            _const_spec(wo.shape), _const_spec(g.shape), _const_spec(wu.shape), _const_spec(wd.shape), _const_spec(gf.shape),
        ],
        out_specs=pl.BlockSpec((tm, d), lambda i: (i, 0)),
        out_shape=jax.ShapeDtypeStruct((rows, d), F32),
        compiler_params=_params(1),
        name="out_mlp",
    )(x2d, a1, a2, wo, g, wu, wd, gf)


def _time_grid(total_rows, n_bg, tblock):
    steps = total_rows // (n_bg * SUBLANES)
    tb = min(tblock, steps)
    n_t = steps // tb
    blk = lambda width: pl.BlockSpec((tb * SUBLANES, width), lambda bg, t: (bg * n_t + t, 0))
    per_group = lambda rows, width: pl.BlockSpec((rows, width), lambda bg, t: (bg, 0))
    return tb, (n_bg, n_t), blk, per_group


def _gla_body(q_ref, k_ref, la_ref, v_ref, gt_ref, gh_ref, s0_ref, o_ref, sfin_ref, st_ref, *, c, nh, dk, dv):
    tb = pl.program_id(1)
    rows = SUBLANES * c
    n_chunks = q_ref.shape[0] // rows
    dk_shift = int(math.log2(dk))
    heads_per_tile = LANES // dk

    @pl.when(tb == 0)
    def _():
        st_ref[...] = s0_ref[...]

    def iota(shape, axis):
        return lax.broadcasted_iota(jnp.int32, shape, axis)

    causal = ((iota((rows, rows), 0) & 7) == (iota((rows, rows), 1) & 7)) & (iota((rows, rows), 1) <= iota((rows, rows), 0))
    emask = (iota((rows, SUBLANES * dk), 0) & 7) == lax.shift_right_logical(iota((rows, SUBLANES * dk), 1), dk_shift)
    dmask = iota((SUBLANES, SUBLANES * dk), 0) == lax.shift_right_logical(iota((SUBLANES, SUBLANES * dk), 1), dk_shift)
    lo_rows = iota((rows, LANES), 1) < dk
    lo_8 = iota((SUBLANES, LANES), 1) < dk

    def expand(tile, h, lo, mask):
        rolled = pltpu.roll(tile, dk, axis=1)
        d = jnp.where(lo, tile, rolled) if h % heads_per_tile == 0 else jnp.where(lo, rolled, tile)
        return jnp.where(mask, jnp.concatenate([d] * (SUBLANES * dk // LANES), axis=1), 0.0)

    def chunk(n, carry):
        r0 = n * rows
        q = _rows(q_ref, r0, rows)
        k = _rows(k_ref, r0, rows)
        la = _rows(la_ref, r0, rows)
        v = _rows(v_ref, r0, rows)
        gt = _rows(gt_ref, r0, rows)
        acc = jnp.zeros((SUBLANES, nh * dk), F32)
        pieces = []
        for t in range(c):
            acc = acc + la[t * SUBLANES:(t + 1) * SUBLANES, :]
            pieces.append(acc)
        cum = jnp.concatenate(pieces, axis=0)
        last = jnp.concatenate([acc] * c, axis=0)
        q_t = q * jnp.exp(cum) * (dk ** -0.5)
        k_t = k * jnp.exp(-cum)
        k_end = k * jnp.exp(last - cum)
        dec = jnp.exp(acc)
        outs = []
        for h in range(nh):
            tile = (h // heads_per_tile) * LANES
            sel = lo_rows if h % heads_per_tile == 0 else jnp.logical_not(lo_rows)
            q_tile = q_t[:, tile:tile + LANES]
            scores = _dot_nt(jnp.where(sel, q_tile, 0.0).astype(BF16), k_t[:, tile:tile + LANES].astype(BF16))
            scores = jnp.where(causal, scores, 0.0).astype(BF16)
            v_h = v[:, h * dv:(h + 1) * dv]
            st = st_ref[h]
            q_exp = expand(q_tile, h, lo_rows, emask).astype(BF16)
            o = _dot(scores, v_h) + _dot_nt(q_exp, st.astype(BF16))
            k_exp = expand(k_end[:, tile:tile + LANES], h, lo_rows, emask).astype(BF16)
            dec_row = jnp.sum(expand(dec[:, tile:tile + LANES], h, lo_8, dmask), axis=0, keepdims=True)
            st_ref[h] = st * dec_row + _dot_tn(v_h, k_exp)
            o = _rmsnorm(o, gh_ref[:, h * dv:(h + 1) * dv])
            g_h = gt[:, h * dv:(h + 1) * dv]
            outs.append(o * (g_h * _sigmoid(g_h)))
        o_ref[pl.ds(pl.multiple_of(r0, rows), rows), :] = jnp.concatenate(outs, axis=1).astype(BF16)
        return carry

    lax.fori_loop(0, n_chunks, chunk, 0, unroll=min(n_chunks, 2))

    @pl.when(tb == pl.num_programs(1) - 1)
    def _():
        sfin_ref[...] = st_ref[...]


def _gla(q, k, la, v, gt, gh, s0t, *, c, nh, dk, dv, tblock):
    n_bg = s0t.shape[0]
    tb, grid, blk, _ = _time_grid(q.shape[0], n_bg, tblock)
    st_spec = pl.BlockSpec((None,) + s0t.shape[1:], lambda bg, t: (bg, 0, 0, 0))
    return pl.pallas_call(
        functools.partial(_gla_body, c=c, nh=nh, dk=dk, dv=dv),
        grid=grid,
        in_specs=[blk(nh * dk), blk(nh * dk), blk(nh * dk), blk(nh * dv), blk(nh * dv), _const_spec(gh.shape), st_spec],
        out_specs=(blk(nh * dv), st_spec),
        out_shape=(jax.ShapeDtypeStruct((q.shape[0], nh * dv), BF16), jax.ShapeDtypeStruct(s0t.shape, F32)),
        scratch_shapes=[pltpu.VMEM(s0t.shape[1:], F32)],
        compiler_params=_params(2),
        name="gla",
    )(q, k, la, v, gt, gh, s0t)


def _s5_disc_body(a_re_ref, a_im_ref, log_dt_ref, b_re_ref, b_im_ref, ab_re_ref, ab_im_ref, bb_re_ref, bb_im_ref):
    a_re = a_re_ref[...]
    a_im = a_im_ref[...]
    dt = jnp.exp(log_dt_ref[...])
    mag = jnp.exp(dt * a_re)
    ab_re = mag * jnp.cos(dt * a_im)
    ab_im = mag * jnp.sin(dt * a_im)
    den = a_re * a_re + a_im * a_im
    coef_re = ((ab_re - 1.0) * a_re + ab_im * a_im) / den
    coef_im = (ab_im * a_re - (ab_re - 1.0) * a_im) / den
    b_re = b_re_ref[...]
    b_im = b_im_ref[...]
    ab_re_ref[...] = ab_re
    ab_im_ref[...] = ab_im
    bb_re_ref[...] = coef_re * b_re - coef_im * b_im
    bb_im_ref[...] = coef_re * b_im + coef_im * b_re


def _s5_discretise(a_re, a_im, log_dt, b_re, b_im):
    g, p, ch = b_re.shape
    rep = lambda t: jnp.repeat(t, ch, axis=0)
    flat = lambda t: jnp.transpose(t, (0, 2, 1)).reshape(g * ch, p)
    args = (rep(a_re), rep(a_im), rep(jnp.broadcast_to(log_dt[:, None], (g, p))), flat(b_re), flat(b_im))
    spec = _const_spec((g * ch, p))
    shape = jax.ShapeDtypeStruct((g * ch, p), F32)
    ab_re, ab_im, bb_re, bb_im = pl.pallas_call(
        _s5_disc_body, grid=(1,), in_specs=[spec] * 5, out_specs=(spec,) * 4, out_shape=(shape,) * 4,
        compiler_params=_params(1), name="s5_discretise",
    )(*args)
    return ab_re[::ch], ab_im[::ch], bb_re.reshape(g, ch, p), bb_im.reshape(g, ch, p)


def _block_diag_in(w_gcp):
    g, ch, p = w_gcp.shape
    gpt = LANES // ch
    w = w_gcp.reshape(g // gpt, gpt, ch, p)
    eye = jnp.eye(gpt, dtype=w.dtype)
    return jnp.einsum("ab,jacp->jacbp", eye, w).reshape(g // gpt, gpt * ch, gpt * p)


def _block_diag_out(w_gcp):
    g, ch, p = w_gcp.shape
    gpt = LANES // ch
    w = w_gcp.reshape(g // gpt, gpt, ch, p)
    eye = jnp.eye(gpt, dtype=w.dtype)
    return jnp.einsum("ab,jacp->japbc", eye, w).reshape(g // gpt, gpt * p, gpt * ch)


def _s5_body(u_ref, ab_re_ref, ab_im_ref, wb_re_ref, wb_im_ref, wc_ref, d_ref, wglu_ref, bglu_ref, h0_re_ref, h0_im_ref,
             z_ref, hf_re_ref, hf_im_ref, h_re_ref, h_im_ref, bu_re_ref, bu_im_ref, y_ref):
    tb = pl.program_id(1)
    rows = u_ref.shape[0]
    steps = rows // SUBLANES
    n_tiles = wb_re_ref.shape[0]
    sw = wb_re_ref.shape[2]

    @pl.when(tb == 0)
    def _():
        h_re_ref[...] = h0_re_ref[...]
        h_im_ref[...] = h0_im_ref[...]

    for j in range(n_tiles):
        u_j = u_ref[:, j * LANES:(j + 1) * LANES]
        u_b = u_j.astype(BF16)
        bu_re_ref[j] = _dot(u_b, wb_re_ref[j])
        bu_im_ref[j] = _dot(u_b, wb_im_ref[j])
        a_re = jnp.broadcast_to(ab_re_ref[:, j * sw:(j + 1) * sw], (SUBLANES, sw))
        a_im = jnp.broadcast_to(ab_im_ref[:, j * sw:(j + 1) * sw], (SUBLANES, sw))
        h_re = h_re_ref[:, j * sw:(j + 1) * sw]
        h_im = h_im_ref[:, j * sw:(j + 1) * sw]
        for t in range(steps):
            sl = slice(t * SUBLANES, (t + 1) * SUBLANES)
            h_re, h_im = (a_re * h_re - a_im * h_im + bu_re_ref[j, sl, :], a_re * h_im + a_im * h_re + bu_im_ref[j, sl, :])
            bu_re_ref[j, sl, :] = h_re
            bu_im_ref[j, sl, :] = h_im
        h_re_ref[:, j * sw:(j + 1) * sw] = h_re
        h_im_ref[:, j * sw:(j + 1) * sw] = h_im
        h_cat = jnp.concatenate([bu_re_ref[j].astype(BF16), bu_im_ref[j].astype(BF16)], axis=1)
        y_ref[:, j * LANES:(j + 1) * LANES] = _dot(h_cat, wc_ref[j]) + d_ref[:, j * LANES:(j + 1) * LANES] * u_j

    z = _gelu(y_ref[...])
    gate = _sigmoid(_dot(z.astype(BF16), wglu_ref[...]) + bglu_ref[...])
    z_ref[...] = (z * gate).astype(BF16)

    @pl.when(tb == pl.num_programs(1) - 1)
    def _():
        hf_re_ref[...] = h_re_ref[...]
        hf_im_ref[...] = h_im_ref[...]


def _s5(u, ab_re, ab_im, wb_re, wb_im, wc, d_skip, wglu, bglu, h0_re, h0_im, *, tblock):
    total, width = u.shape
    n_state = ab_re.shape[1]
    n_bg = h0_re.shape[0] // SUBLANES
    sw = wb_re.shape[2]
    tb, grid, blk, per_group = _time_grid(total, n_bg, tblock)
    rows = tb * SUBLANES
    st_spec = per_group(SUBLANES, n_state)
    consts = (ab_re, ab_im, wb_re, wb_im, wc, d_skip, wglu, bglu)
    st_shape = jax.ShapeDtypeStruct((n_bg * SUBLANES, n_state), F32)
    return pl.pallas_call(
        _s5_body,
        grid=grid,
        in_specs=[blk(width)] + [_const_spec(t.shape) for t in consts] + [st_spec, st_spec],
        out_specs=(blk(width), st_spec, st_spec),
        out_shape=(jax.ShapeDtypeStruct((total, width), BF16), st_shape, st_shape),
        scratch_shapes=[pltpu.VMEM((SUBLANES, n_state), F32), pltpu.VMEM((SUBLANES, n_state), F32),
                        pltpu.VMEM((wb_re.shape[0], rows, sw), F32), pltpu.VMEM((wb_re.shape[0], rows, sw), F32),
                        pltpu.VMEM((rows, width), F32)],
        compiler_params=_params(2),
        name="s5",
    )(u, *consts, h0_re, h0_im)


def _lru_body(gate_ref, xin_ref, conv0_ref, h0_ref, cw_ref, cb_ref, wr_ref, br_ref, wi_ref, bi_ref, lam_ref,
              o_ref, hf_ref, convf_ref, xbuf_ref, a_ref, b_ref, h_ref):
    tb = pl.program_id(1)
    rows, width = xin_ref.shape
    steps = rows // SUBLANES
    taps = cw_ref.shape[0]
    tail = (taps - 1) * SUBLANES
    n_blocks = wr_ref.shape[0]
    bw = wr_ref.shape[1]

    @pl.when(tb == 0)
    def _():
        xbuf_ref[0:tail, :] = conv0_ref[...]
        h_ref[...] = h0_ref[...]

    xbuf_ref[tail:tail + rows, :] = xin_ref[...]
    for n in range(n_blocks):
        cols = slice(n * bw, (n + 1) * bw)
        xc = cb_ref[:, cols] + xbuf_ref[0:rows, cols] * cw_ref[0:1, cols]
        for j in range(1, taps):
            xc = xc + xbuf_ref[j * SUBLANES:j * SUBLANES + rows, cols] * cw_ref[j:j + 1, cols]
        xb = xc.astype(BF16)
        i = _sigmoid(_dot(xb, wi_ref[n]) + bi_ref[:, cols])
        half_rate = (-0.5 * LRU_C) * _softplus(-lam_ref[:, cols])
        log_a = jnp.tanh(0.5 * (_dot(xb, wr_ref[n]) + br_ref[:, cols])) * half_rate + half_rate
        a_ref[:, cols] = jnp.exp(log_a)
        th = jnp.tanh(log_a)
        y = -2.0 * th / (1.0 - th)
        root = jnp.where(y > 0.0, y * lax.rsqrt(y), 0.0)
        b_ref[:, cols] = root * (i * xc)
        h = h_ref[:, cols]
        for t in range(steps):
            sl = slice(t * SUBLANES, (t + 1) * SUBLANES)
            h = a_ref[sl, cols] * h + b_ref[sl, cols]
            b_ref[sl, cols] = h
        h_ref[:, cols] = h
        o_ref[:, cols] = (_gelu(gate_ref[:, cols]) * b_ref[:, cols]).astype(BF16)
    xbuf_ref[0:tail, :] = xbuf_ref[rows:rows + tail, :]

    @pl.when(tb == pl.num_programs(1) - 1)
    def _():
        hf_ref[...] = h_ref[...]
        convf_ref[...] = xbuf_ref[0:tail, :]


def _lru(gate, xin, conv0, h0, cw, cb, wr, br, wi, bi, lam, *, tblock):
    total, width = xin.shape
    n_bg = h0.shape[0] // SUBLANES
    tail = (cw.shape[0] - 1) * SUBLANES
    tb, grid, blk, per_group = _time_grid(total, n_bg, tblock)
    rows = tb * SUBLANES
    h_spec = per_group(SUBLANES, width)
    c_spec = per_group(tail, width)
    consts = (cw, cb, wr, br, wi, bi, lam)
    return pl.pallas_call(
        _lru_body,
        grid=grid,
        in_specs=[blk(width), blk(width), c_spec, h_spec] + [_const_spec(t.shape) for t in consts],
        out_specs=(blk(width), h_spec, c_spec),
        out_shape=(jax.ShapeDtypeStruct((total, width), BF16), jax.ShapeDtypeStruct((n_bg * SUBLANES, width), F32),
                   jax.ShapeDtypeStruct((n_bg * tail, width), F32)),
        scratch_shapes=[pltpu.VMEM((rows + tail, width), F32), pltpu.VMEM((rows, width), F32),
                        pltpu.VMEM((rows, width), F32), pltpu.VMEM((SUBLANES, width), F32)],
        compiler_params=_params(2),
        name="rg_lru",
    )(gate, xin, conv0, h0, *consts)


def _row(t):
    return t.reshape(1, -1).astype(F32)


def _to_rows(x):
    b, s, d = x.shape
    return jnp.transpose(x.reshape(b // SUBLANES, SUBLANES, s, d), (0, 2, 1, 3)).reshape(b * s, d)


def _from_rows(y, b, s):
    d = y.shape[-1]
    return jnp.transpose(y.reshape(b // SUBLANES, s, SUBLANES, d), (0, 2, 1, 3)).reshape(b, s, d)


def _gla_state_in(s):
    b, h, dk, dv = s.shape
    return jnp.transpose(s.reshape(b // SUBLANES, SUBLANES, h, dk, dv), (0, 2, 4, 1, 3)).reshape(b // SUBLANES, h, dv, SUBLANES * dk)


def _gla_state_out(st, dk):
    n_bg, h, dv, _ = st.shape
    return jnp.transpose(st.reshape(n_bg, h, dv, SUBLANES, dk), (0, 3, 1, 4, 2)).reshape(n_bg * SUBLANES, h, dk, dv)


def _even_layer(x2d, nb, s_gla, s_re, s_im, prm, mlp, *, final):
    (g_norm, w_in, wa2, ba2, g_head, ab_re, ab_im, wb_re, wb_im, wc, d_skip, wglu, bglu, w_out, dims) = prm
    nh, dk, dv, sw = dims
    n_bg = nb // SUBLANES
    steps = x2d.shape[0] // nb
    q, k, v, gt, u, la = _inproj_even(x2d, g_norm, w_in, wa2, ba2, kw=nh * dk, vw=nh * dv, sw=sw)
    s0t = jnp.zeros((n_bg, nh, dv, SUBLANES * dk), F32) if s_gla is None else _gla_state_in(s_gla.astype(F32))
    og, sfin = _gla(q, k, la, v, gt, g_head, s0t, c=math.gcd(steps, GLA_CHUNK), nh=nh, dk=dk, dv=dv, tblock=128)
    n_state = ab_re.shape[1]
    h0_re = jnp.zeros((nb, n_state), F32) if s_re is None else s_re.astype(F32).reshape(nb, n_state)
    h0_im = jnp.zeros((nb, n_state), F32) if s_im is None else s_im.astype(F32).reshape(nb, n_state)
    zz, hf_re, hf_im = _s5(u, ab_re, ab_im, wb_re, wb_im, wc, d_skip, wglu, bglu, h0_re, h0_im, tblock=64)
    g_mlp, w_up, w_down, g_fin = mlp
    y = _out_mlp(x2d, og, zz, 0, w_out, g_mlp, w_up, w_down, g_fin, final=final)
    return y, _gla_state_out(sfin, dk), hf_re, hf_im


def _odd_layer(x2d, nb, s_lru, s_conv, prm, mlp, *, final):
    (g_norm, w_in, cw, cb, wr, br, wi, bi, lam, w_out) = prm
    n_bg = nb // SUBLANES
    taps = cw.shape[0]
    width = w_in.shape[1] // 2
    gate, xin = _inproj_odd(x2d, g_norm, w_in)
    h0 = jnp.zeros((nb, width), F32) if s_lru is None else s_lru.astype(F32)
    if s_conv is None:
        conv0 = jnp.zeros((n_bg * (taps - 1) * SUBLANES, width), F32)
    else:
        conv0 = jnp.transpose(s_conv.astype(F32).reshape(n_bg, SUBLANES, taps - 1, width), (0, 2, 1, 3)).reshape(-1, width)
    gh, hf, convf = _lru(gate, xin, conv0, h0, cw, cb, wr, br, wi, bi, lam, tblock=64)
    g_mlp, w_up, w_down, g_fin = mlp
    y = _out_mlp(x2d, gh, gh, 1, w_out, g_mlp, w_up, w_down, g_fin, final=final)
    conv_new = jnp.transpose(convf.reshape(n_bg, taps - 1, SUBLANES, width), (0, 2, 1, 3)).reshape(nb, taps - 1, width)
    return y, hf, conv_new


def kernel(x_prompt, x_sample, state_gla, state_s5_re, state_s5_im, state_lru, state_conv, norm_mix_even, w_in_even, gla_w_a2, gla_b_a2, gla_norm, s5_a_re, s5_a_im, s5_log_dt, s5_b_re, s5_b_im, s5_c_re, s5_c_im, s5_d, s5_w_glu, s5_b_glu, w_out_even, norm_mix_odd, w_in_odd, conv_w, conv_b, lru_w_r, lru_b_r, lru_w_i, lru_b_i, lru_lam, w_out_odd, norm_mlp, w_up, w_down, norm_final):
    depth = norm_mlp.shape[0]
    nh, dk, dv = state_gla.shape[2:]
    rank = gla_w_a2.shape[1]
    n_groups, n_p, ch = s5_b_re.shape[1:]
    sw = n_groups * ch
    kw, vw = nh * dk, nh * dv
    assert rank <= LANES and LANES % ch == 0 and LANES % dk == 0 and dv == LANES
    bp, sp, _ = x_prompt.shape
    bs, ss, _ = x_sample.shape

    yp = _to_rows(x_prompt.astype(F32))
    ys = _to_rows(x_sample.astype(F32))
    g_fin = _row(norm_final)
    keys = ("gla", "re", "im", "lru", "conv")
    outs_p = {key: [] for key in keys}
    outs_s = {key: [] for key in keys}

    def record(outs, sg, sr, si):
        outs["gla"].append(sg)
        outs["re"].append(sr.reshape(-1, n_groups, n_p))
        outs["im"].append(si.reshape(-1, n_groups, n_p))

    for layer in range(depth):
        final = layer == depth - 1
        mlp = (_row(norm_mlp[layer]), w_up[layer].astype(BF16), w_down[layer].astype(BF16), g_fin)
        if layer % 2 == 0:
            e = layer // 2
            w = w_in_even[e]
            o4 = 2 * kw + 2 * vw
            o5 = o4 + rank
            w_perm = jnp.concatenate([w[:, :o4], w[:, o5:], w[:, o4:o5], jnp.zeros((w.shape[0], LANES - rank), w.dtype)], axis=1)
            wa2 = jnp.concatenate([gla_w_a2[e], jnp.zeros((LANES - rank, kw), gla_w_a2.dtype)], axis=0).astype(BF16)
            ab_re, ab_im, bb_re, bb_im = _s5_discretise(s5_a_re[e].astype(F32), s5_a_im[e].astype(F32), s5_log_dt[e].astype(F32),
                                                        s5_b_re[e].astype(F32), s5_b_im[e].astype(F32))
            wc = jnp.concatenate([_block_diag_out(s5_c_re[e].astype(F32)), -_block_diag_out(s5_c_im[e].astype(F32))], axis=1)
            prm = (_row(norm_mix_even[e]), w_perm.astype(BF16), wa2, _row(gla_b_a2[e]), _row(gla_norm[e]),
                   ab_re.reshape(1, -1), ab_im.reshape(1, -1), _block_diag_in(bb_re).astype(BF16), _block_diag_in(bb_im).astype(BF16),
                   wc.astype(BF16), _row(s5_d[e]), s5_w_glu[e].astype(BF16), _row(s5_b_glu[e]), w_out_even[e].astype(BF16),
                   (nh, dk, dv, sw))
            yp, sg, sr, si = _even_layer(yp, bp, None, None, None, prm, mlp, final=final)
            record(outs_p, sg, sr, si)
            ys, sg, sr, si = _even_layer(ys, bs, state_gla[e], state_s5_re[e], state_s5_im[e], prm, mlp, final=final)
            record(outs_s, sg, sr, si)
        else:
            o = layer // 2
            prm = (_row(norm_mix_odd[o]), w_in_odd[o].astype(BF16), conv_w[o].astype(F32), _row(conv_b[o]),
                   lru_w_r[o].astype(BF16), _row(lru_b_r[o]), lru_w_i[o].astype(BF16), _row(lru_b_i[o]), _row(lru_lam[o]),
                   w_out_odd[o].astype(BF16))
            yp, sl, sc = _odd_layer(yp, bp, None, None, prm, mlp, final=final)
            outs_p["lru"].append(sl)
            outs_p["conv"].append(sc)
            ys, sl, sc = _odd_layer(ys, bs, state_lru[o], state_conv[o], prm, mlp, final=final)
            outs_s["lru"].append(sl)
            outs_s["conv"].append(sc)

    stack = lambda outs: tuple(jnp.stack(outs[key]) for key in keys)
    return (_from_rows(yp, bp, sp), _from_rows(ys, bs, ss)) + stack(outs_p) + stack(outs_s)
```

```python
import functools
import math

import jax
import jax.numpy as jnp
from jax import lax
from jax.experimental import pallas as pl
from jax.experimental.pallas import tpu as pltpu

F32 = jnp.float32
BF16 = jnp.bfloat16

NORM_EPS = 1e-6
GLA_TAU = 16.0
GLA_CHUNK = 16
LRU_C = 8.0

SUBLANES = 8
LANES = 128
ROW_TILE = 512
FF_CHUNK = 1024
VMEM_LIMIT = 48 * 1024 * 1024


def _params(n_axes):
    return pltpu.CompilerParams(dimension_semantics=("arbitrary",) * n_axes, vmem_limit_bytes=VMEM_LIMIT)


def _const_spec(shape):
    zeros = (0,) * len(shape)
    return pl.BlockSpec(shape, lambda *_: zeros, pipeline_mode=pl.Buffered(1))


def _dot(a, b):
    return jnp.dot(a, b, preferred_element_type=F32)


def _dot_nt(a, b):
    return lax.dot_general(a, b, (((1,), (1,)), ((), ())), preferred_element_type=F32)


def _dot_tn(a, b):
    return lax.dot_general(a, b, (((0,), (0,)), ((), ())), preferred_element_type=F32)


def _rmsnorm(x, g):
    return x * lax.rsqrt(jnp.mean(x * x, axis=-1, keepdims=True) + NORM_EPS) * g


def _log_sigmoid(z):
    return jnp.minimum(z, 0.0) - jnp.log1p(jnp.exp(-jnp.abs(z)))


def _softplus(z):
    return jnp.maximum(z, 0.0) + jnp.log1p(jnp.exp(-jnp.abs(z)))


_GELU_C1 = math.sqrt(2.0 / math.pi)
_GELU_C2 = _GELU_C1 * 0.044715


def _gelu(x):
    half = 0.5 * x
    return half + half * jnp.tanh(x * (_GELU_C1 + _GELU_C2 * (x * x)))


def _sigmoid(x):
    return 0.5 * jnp.tanh(0.5 * x) + 0.5


def _rows(ref, start, size):
    return ref[pl.ds(pl.multiple_of(start, size), size), :]


class _RowWalk:
    def __init__(self, rows, nb, batch_major):
        self.rows = rows
        self.interleave = batch_major and nb == SUBLANES and (rows // nb) % ROW_TILE == 0
        if self.interleave:
            self.nb = nb
            self.tm = ROW_TILE
            self.n_t = rows // nb // self.tm
            self.grid = (self.n_t, nb)
        else:
            self.tm = min(ROW_TILE, rows)
            self.grid = (rows // self.tm,)

    def resid(self, width):
        if self.interleave:
            n_t = self.n_t
            return pl.BlockSpec((self.tm, width), lambda t, b: (b * n_t + t, 0))
        return pl.BlockSpec((self.tm, width), lambda i: (i, 0))

    def rec(self, width, col=0, n_cols=1):
        if self.interleave:
            return pl.BlockSpec((self.tm, width), lambda t, b: (t, b * n_cols + col))
        return pl.BlockSpec((self.tm, width), lambda i: (i, col))

    def rec_view(self, arr):
        return arr.reshape(self.rows // self.nb, -1) if self.interleave else arr

    def rec_shape(self, width, dtype):
        shape = (self.rows // self.nb, self.nb * width) if self.interleave else (self.rows, width)
        return jax.ShapeDtypeStruct(shape, dtype)

    def rec_rows(self, arr):
        return arr.reshape(self.rows, -1)

    def params(self):
        return _params(len(self.grid))


def _inproj_even_body(x_ref, g_ref, w_ref, wu_ref, wa_ref, wa2_ref, ba2_ref, q_ref, k_ref, v_ref, gt_ref, u_ref, la_ref, *, kw, vw):
    xn = _rmsnorm(x_ref[...], g_ref[...]).astype(BF16)
    o1, o2 = kw, 2 * kw
    o3 = o2 + vw
    o4 = o3 + vw
    q_ref[...] = _dot(xn, w_ref[:, 0:o1])
    k_ref[...] = _dot(xn, w_ref[:, o1:o2])
    v_ref[...] = _dot(xn, w_ref[:, o2:o3]).astype(BF16)
    gt_ref[...] = _dot(xn, w_ref[:, o3:o4])
    u_ref[...] = _dot(xn, wu_ref[...])
    a_lr = _dot(xn, wa_ref[...]).astype(BF16)
    z = _dot(a_lr, wa2_ref[...]) + ba2_ref[...]
    la_ref[...] = _log_sigmoid(z) * (1.0 / GLA_TAU)


def _inproj_even(walk, x2d, g, w, wu, wa, wa2, ba2, *, kw, vw):
    d = x2d.shape[1]
    sw = wu.shape[1]
    widths = (kw, kw, vw, vw, sw, kw)
    dtypes = (F32, F32, BF16, F32, F32, F32)
    consts = (g, w, wu, wa, wa2, ba2)
    outs = pl.pallas_call(
        functools.partial(_inproj_even_body, kw=kw, vw=vw),
        grid=walk.grid,
        in_specs=[walk.resid(d)] + [_const_spec(t.shape) for t in consts],
        out_specs=tuple(walk.rec(wd) for wd in widths),
        out_shape=tuple(walk.rec_shape(wd, dt) for wd, dt in zip(widths, dtypes)),
        compiler_params=walk.params(),
        name="inproj_even",
    )(x2d, *consts)
    return tuple(walk.rec_rows(t) for t in outs)


def _inproj_odd_body(x_ref, g_ref, w_ref, gate_ref, xin_ref, *, width):
    xn = _rmsnorm(x_ref[...], g_ref[...]).astype(BF16)
    gate_ref[...] = _dot(xn, w_ref[:, 0:width])
    xin_ref[...] = _dot(xn, w_ref[:, width:2 * width])


def _inproj_odd(walk, x2d, g, w):
    d = x2d.shape[1]
    width = w.shape[1] // 2
    outs = pl.pallas_call(
        functools.partial(_inproj_odd_body, width=width),
        grid=walk.grid,
        in_specs=[walk.resid(d), _const_spec(g.shape), _const_spec(w.shape)],
        out_specs=(walk.rec(width), walk.rec(width)),
        out_shape=(walk.rec_shape(width, F32), walk.rec_shape(width, F32)),
        compiler_params=walk.params(),
        name="inproj_odd",
    )(x2d, g, w)
    return tuple(walk.rec_rows(t) for t in outs)


def _out_mlp_body(x_ref, a1_ref, a2_ref, wo_ref, g_ref, wu_ref, wd_ref, gf_ref, o_ref, *, final):
    half = a1_ref.shape[1]
    mix = _dot(a1_ref[...], wo_ref[0:half, :]) + _dot(a2_ref[...], wo_ref[half:2 * half, :])
    x1 = x_ref[...] + mix
    xn = _rmsnorm(x1, g_ref[...]).astype(BF16)
    acc = x1
    for c in range(wu_ref.shape[1] // FF_CHUNK):
        h = _dot(xn, wu_ref[:, c * FF_CHUNK:(c + 1) * FF_CHUNK])
        h = jnp.square(jnp.maximum(h, 0.0)).astype(BF16)
        acc = acc + _dot(h, wd_ref[c * FF_CHUNK:(c + 1) * FF_CHUNK, :])
    if final:
        acc = _rmsnorm(acc, gf_ref[...])
    o_ref[...] = acc


def _out_mlp(walk, x2d, a1, a2, wo, g, wu, wd, gf, *, final):
    rows, d = x2d.shape
    half = d // 2
    n_cols = a1.shape[1] // half
    consts = (wo, g, wu, wd, gf)
    return pl.pallas_call(
        functools.partial(_out_mlp_body, final=final),
        grid=walk.grid,
        in_specs=[walk.resid(d), walk.rec(half, 0, n_cols), walk.rec(half, n_cols - 1, n_cols)] + [_const_spec(t.shape) for t in consts],
        out_specs=walk.resid(d),
        out_shape=jax.ShapeDtypeStruct((rows, d), F32),
        compiler_params=walk.params(),
        name="out_mlp",
    )(x2d, walk.rec_view(a1), walk.rec_view(a2), *consts)


def _time_grid(total_rows, n_bg, tblock):
    steps = total_rows // (n_bg * SUBLANES)
    tb = min(tblock, steps)
    n_t = steps // tb
    blk = lambda width: pl.BlockSpec((tb * SUBLANES, width), lambda bg, t: (bg * n_t + t, 0))
    per_group = lambda rows, width: pl.BlockSpec((rows, width), lambda bg, t: (bg, 0))
    return tb, (n_bg, n_t), blk, per_group


def _gla_body(q_ref, k_ref, la_ref, v_ref, gt_ref, gh_ref, s0_ref, o_ref, sfin_ref, st_ref, *, c, nh, dk, dv, transposed):
    tb = pl.program_id(1)
    rows = SUBLANES * c
    n_chunks = q_ref.shape[0] // rows
    dk_shift = int(math.log2(dk))
    heads_per_tile = LANES // dk

    @pl.when(tb == 0)
    def _():
        st_ref[...] = s0_ref[...]

    def iota(shape, axis):
        return lax.broadcasted_iota(jnp.int32, shape, axis)

    causal = ((iota((rows, rows), 0) & 7) == (iota((rows, rows), 1) & 7)) & (iota((rows, rows), 1) <= iota((rows, rows), 0))
    emask = (iota((rows, SUBLANES * dk), 0) & 7) == lax.shift_right_logical(iota((rows, SUBLANES * dk), 1), dk_shift)
    dmask = iota((SUBLANES, SUBLANES * dk), 0) == lax.shift_right_logical(iota((SUBLANES, SUBLANES * dk), 1), dk_shift)
    lo_rows = iota((rows, LANES), 1) < dk
    lo_8 = iota((SUBLANES, LANES), 1) < dk

    def expand(tile, h, lo=lo_rows, mask=emask):
        rolled = pltpu.roll(tile, dk, axis=1)
        d = jnp.where(lo, tile, rolled) if h % heads_per_tile == 0 else jnp.where(lo, rolled, tile)
        return jnp.where(mask, jnp.concatenate([d] * (SUBLANES * dk // LANES), axis=1), 0.0)

    def chunk(n, carry):
        r0 = n * rows
        q = _rows(q_ref, r0, rows)
        k = _rows(k_ref, r0, rows)
        la = _rows(la_ref, r0, rows)
        v = _rows(v_ref, r0, rows)
        gt = _rows(gt_ref, r0, rows)
        acc = jnp.zeros((SUBLANES, nh * dk), F32)
        pieces = []
        for t in range(c):
            acc = acc + la[t * SUBLANES:(t + 1) * SUBLANES, :]
            pieces.append(acc)
        cum = jnp.concatenate(pieces, axis=0)
        last = jnp.concatenate([acc] * c, axis=0)
        q_t = q * jnp.exp(cum) * (dk ** -0.5)
        k_t = k * jnp.exp(-cum)
        k_end = k * jnp.exp(last - cum)
        dec = jnp.exp(acc)
        pad = jnp.zeros((LANES - SUBLANES, LANES), F32)
        outs = []
        for h in range(nh):
            tile = (h // heads_per_tile) * LANES
            lane0 = (h % heads_per_tile) * dk
            sel = lo_rows if lane0 == 0 else jnp.logical_not(lo_rows)
            q_tile = q_t[:, tile:tile + LANES]
            scores = _dot_nt(jnp.where(sel, q_tile, 0.0).astype(BF16), k_t[:, tile:tile + LANES].astype(BF16))
            scores = jnp.where(causal, scores, 0.0).astype(BF16)
            v_h = v[:, h * dv:(h + 1) * dv]
            q_exp = expand(q_tile, h).astype(BF16)
            k_exp = expand(k_end[:, tile:tile + LANES], h).astype(BF16)
            if transposed:
                st = st_ref[h]
                o = _dot(scores, v_h) + _dot_nt(q_exp, st.astype(BF16))
                dec_row = jnp.sum(expand(dec[:, tile:tile + LANES], h, lo_8, dmask), axis=0, keepdims=True)
                st_ref[h] = st * dec_row + _dot_tn(v_h, k_exp)
            else:
                st = jnp.concatenate([st_ref[b, h] for b in range(SUBLANES)], axis=0)
                o = _dot(scores, v_h) + _dot(q_exp, st.astype(BF16))
                upd = _dot_tn(k_exp, v_h)
                dec_t = jnp.concatenate([dec[:, tile:tile + LANES], pad], axis=0).T
                for b in range(SUBLANES):
                    dec_b = jnp.broadcast_to(dec_t[lane0:lane0 + dk, b:b + 1], (dk, dv))
                    st_ref[b, h] = st[b * dk:(b + 1) * dk, :] * dec_b + upd[b * dk:(b + 1) * dk, :]
            o = _rmsnorm(o, gh_ref[:, h * dv:(h + 1) * dv])
            g_h = gt[:, h * dv:(h + 1) * dv]
            outs.append(o * (g_h * _sigmoid(g_h)))
        o_ref[pl.ds(pl.multiple_of(r0, rows), rows), :] = jnp.concatenate(outs, axis=1).astype(BF16)
        return carry

    lax.fori_loop(0, n_chunks, chunk, 0, unroll=min(n_chunks, 2))

    @pl.when(tb == pl.num_programs(1) - 1)
    def _():
        sfin_ref[...] = st_ref[...]


def _gla_state_in(s):
    b, h, dk, dv = s.shape
    return jnp.transpose(s.reshape(b // SUBLANES, SUBLANES, h, dk, dv), (0, 2, 4, 1, 3)).reshape(b // SUBLANES * h, dv, SUBLANES * dk)


def _gla_state_out(st, nh, dk):
    dv = st.shape[1]
    n_bg = st.shape[0] // nh
    return jnp.transpose(st.reshape(n_bg, nh, dv, SUBLANES, dk), (0, 3, 1, 4, 2)).reshape(n_bg * SUBLANES, nh, dk, dv)


def _gla(q, k, la, v, gt, gh, s0, nb, dims, *, c, tblock):
    nh, dk, dv = dims
    n_bg = nb // SUBLANES
    tb, grid, blk, _ = _time_grid(q.shape[0], n_bg, tblock)
    transposed = q.shape[0] // n_bg > SUBLANES * c
    if transposed:
        st_block = (nh, dv, SUBLANES * dk)
        st_spec = pl.BlockSpec(st_block, lambda bg, t: (bg, 0, 0))
        s_in = jnp.zeros((n_bg * nh, dv, SUBLANES * dk), F32) if s0 is None else _gla_state_in(s0)
    else:
        st_block = (SUBLANES, nh, dk, dv)
        st_spec = pl.BlockSpec(st_block, lambda bg, t: (bg, 0, 0, 0))
        s_in = jnp.zeros((nb, nh, dk, dv), F32) if s0 is None else s0
    og, s_fin = pl.pallas_call(
        functools.partial(_gla_body, c=c, nh=nh, dk=dk, dv=dv, transposed=transposed),
        grid=grid,
        in_specs=[blk(nh * dk), blk(nh * dk), blk(nh * dk), blk(nh * dv), blk(nh * dv), _const_spec(gh.shape), st_spec],
        out_specs=(blk(nh * dv), st_spec),
        out_shape=(jax.ShapeDtypeStruct((q.shape[0], nh * dv), BF16), jax.ShapeDtypeStruct(s_in.shape, F32)),
        scratch_shapes=[pltpu.VMEM(st_block, F32)],
        compiler_params=_params(2),
        name="gla",
    )(q, k, la, v, gt, gh, s_in)
    return og, (_gla_state_out(s_fin, nh, dk) if transposed else s_fin)


def _s5_disc_body(a_re_ref, a_im_ref, log_dt_ref, b_re_ref, b_im_ref, ab_re_ref, ab_im_ref, bb_re_ref, bb_im_ref):
    a_re = a_re_ref[...]
    a_im = a_im_ref[...]
    dt = jnp.exp(log_dt_ref[...])
    mag = jnp.exp(dt * a_re)
    ab_re = mag * jnp.cos(dt * a_im)
    ab_im = mag * jnp.sin(dt * a_im)
    den = a_re * a_re + a_im * a_im
    coef_re = ((ab_re - 1.0) * a_re + ab_im * a_im) / den
    coef_im = (ab_im * a_re - (ab_re - 1.0) * a_im) / den
    b_re = b_re_ref[...]
    b_im = b_im_ref[...]
    ab_re_ref[...] = ab_re
    ab_im_ref[...] = ab_im
    bb_re_ref[...] = coef_re * b_re - coef_im * b_im
    bb_im_ref[...] = coef_re * b_im + coef_im * b_re


def _s5_discretise(a_re, a_im, log_dt, b_re, b_im):
    g, p, ch = b_re.shape
    rep = lambda t: jnp.repeat(t, ch, axis=0)
    flat = lambda t: jnp.transpose(t, (0, 2, 1)).reshape(g * ch, p)
    args = (rep(a_re), rep(a_im), rep(jnp.broadcast_to(log_dt[:, None], (g, p))), flat(b_re), flat(b_im))
    spec = _const_spec((g * ch, p))
    shape = jax.ShapeDtypeStruct((g * ch, p), F32)
    ab_re, ab_im, bb_re, bb_im = pl.pallas_call(
        _s5_disc_body, grid=(1,), in_specs=[spec] * 5, out_specs=(spec,) * 4, out_shape=(shape,) * 4,
        compiler_params=_params(1), name="s5_discretise",
    )(*args)
    return ab_re[::ch], ab_im[::ch], bb_re.reshape(g, ch, p), bb_im.reshape(g, ch, p)


def _block_diag_in(w_gcp):
    g, ch, p = w_gcp.shape
    gpt = LANES // ch
    w = w_gcp.reshape(g // gpt, gpt, ch, p)
    eye = jnp.eye(gpt, dtype=w.dtype)
    return jnp.einsum("ab,jacp->jacbp", eye, w).reshape(g // gpt, gpt * ch, gpt * p)


def _block_diag_out(w_gcp):
    g, ch, p = w_gcp.shape
    gpt = LANES // ch
    w = w_gcp.reshape(g // gpt, gpt, ch, p)
    eye = jnp.eye(gpt, dtype=w.dtype)
    return jnp.einsum("ab,jacp->japbc", eye, w).reshape(g // gpt, gpt * p, gpt * ch)


def _s5_body(u_ref, ab_re_ref, ab_im_ref, wb_re_ref, wb_im_ref, wc_ref, d_ref, wglu_ref, bglu_ref, h0_re_ref, h0_im_ref,
             z_ref, hf_re_ref, hf_im_ref, h_re_ref, h_im_ref, bu_re_ref, bu_im_ref, y_ref):
    tb = pl.program_id(1)
    rows = u_ref.shape[0]
    steps = rows // SUBLANES
    n_tiles = wb_re_ref.shape[0]
    sw = wb_re_ref.shape[2]

    @pl.when(tb == 0)
    def _():
        h_re_ref[...] = h0_re_ref[...]
        h_im_ref[...] = h0_im_ref[...]

    for j in range(n_tiles):
        u_j = u_ref[:, j * LANES:(j + 1) * LANES]
        u_b = u_j.astype(BF16)
        bu_re_ref[j] = _dot(u_b, wb_re_ref[j])
        bu_im_ref[j] = _dot(u_b, wb_im_ref[j])
        a_re = jnp.broadcast_to(ab_re_ref[:, j * sw:(j + 1) * sw], (SUBLANES, sw))
        a_im = jnp.broadcast_to(ab_im_ref[:, j * sw:(j + 1) * sw], (SUBLANES, sw))
        h_re = h_re_ref[:, j * sw:(j + 1) * sw]
        h_im = h_im_ref[:, j * sw:(j + 1) * sw]
        for t in range(steps):
            sl = slice(t * SUBLANES, (t + 1) * SUBLANES)
            h_re, h_im = (a_re * h_re - a_im * h_im + bu_re_ref[j, sl, :], a_re * h_im + a_im * h_re + bu_im_ref[j, sl, :])
            bu_re_ref[j, sl, :] = h_re
            bu_im_ref[j, sl, :] = h_im
        h_re_ref[:, j * sw:(j + 1) * sw] = h_re
        h_im_ref[:, j * sw:(j + 1) * sw] = h_im
        h_cat = jnp.concatenate([bu_re_ref[j].astype(BF16), bu_im_ref[j].astype(BF16)], axis=1)
        y_ref[:, j * LANES:(j + 1) * LANES] = _dot(h_cat, wc_ref[j]) + d_ref[:, j * LANES:(j + 1) * LANES] * u_j

    z = _gelu(y_ref[...])
    gate = _sigmoid(_dot(z.astype(BF16), wglu_ref[...]) + bglu_ref[...])
    z_ref[...] = (z * gate).astype(BF16)

    @pl.when(tb == pl.num_programs(1) - 1)
    def _():
        hf_re_ref[...] = h_re_ref[...]
        hf_im_ref[...] = h_im_ref[...]


def _s5(u, ab_re, ab_im, wb_re, wb_im, wc, d_skip, wglu, bglu, h0_re, h0_im, *, tblock):
    total, width = u.shape
    n_state = ab_re.shape[1]
    n_bg = h0_re.shape[0] // SUBLANES
    sw = wb_re.shape[2]
    tb, grid, blk, per_group = _time_grid(total, n_bg, tblock)
    rows = tb * SUBLANES
    st_spec = per_group(SUBLANES, n_state)
    consts = (ab_re, ab_im, wb_re, wb_im, wc, d_skip, wglu, bglu)
    st_shape = jax.ShapeDtypeStruct((n_bg * SUBLANES, n_state), F32)
    return pl.pallas_call(
        _s5_body,
        grid=grid,
        in_specs=[blk(width)] + [_const_spec(t.shape) for t in consts] + [st_spec, st_spec],
        out_specs=(blk(width), st_spec, st_spec),
        out_shape=(jax.ShapeDtypeStruct((total, width), BF16), st_shape, st_shape),
        scratch_shapes=[pltpu.VMEM((SUBLANES, n_state), F32), pltpu.VMEM((SUBLANES, n_state), F32),
                        pltpu.VMEM((wb_re.shape[0], rows, sw), F32), pltpu.VMEM((wb_re.shape[0], rows, sw), F32),
                        pltpu.VMEM((rows, width), F32)],
        compiler_params=_params(2),
        name="s5",
    )(u, *consts, h0_re, h0_im)


def _lru_body(gate_ref, xin_ref, conv0_ref, h0_ref, cw_ref, cb_ref, wr_ref, br_ref, wi_ref, bi_ref, lam_ref,
              o_ref, hf_ref, convf_ref, xbuf_ref, a_ref, b_ref, h_ref):
    tb = pl.program_id(1)
    rows, width = xin_ref.shape
    steps = rows // SUBLANES
    taps = cw_ref.shape[0]
    tail = (taps - 1) * SUBLANES
    n_blocks = wr_ref.shape[0]
    bw = wr_ref.shape[1]

    @pl.when(tb == 0)
    def _():
        xbuf_ref[0:tail, :] = conv0_ref[...]
        h_ref[...] = h0_ref[...]

    xbuf_ref[tail:tail + rows, :] = xin_ref[...]
    for n in range(n_blocks):
        cols = slice(n * bw, (n + 1) * bw)
        xc = cb_ref[:, cols] + xbuf_ref[0:rows, cols] * cw_ref[0:1, cols]
        for j in range(1, taps):
            xc = xc + xbuf_ref[j * SUBLANES:j * SUBLANES + rows, cols] * cw_ref[j:j + 1, cols]
        xb = xc.astype(BF16)
        i = _sigmoid(_dot(xb, wi_ref[n]) + bi_ref[:, cols])
        half_rate = (-0.5 * LRU_C) * _softplus(-lam_ref[:, cols])
        log_a = jnp.tanh(0.5 * (_dot(xb, wr_ref[n]) + br_ref[:, cols])) * half_rate + half_rate
        a_ref[:, cols] = jnp.exp(log_a)
        th = jnp.tanh(log_a)
        y = -2.0 * th / (1.0 - th)
        root = jnp.where(y > 0.0, y * lax.rsqrt(y), 0.0)
        b_ref[:, cols] = root * (i * xc)
        h = h_ref[:, cols]
        for t in range(steps):
            sl = slice(t * SUBLANES, (t + 1) * SUBLANES)
            h = a_ref[sl, cols] * h + b_ref[sl, cols]
            b_ref[sl, cols] = h
        h_ref[:, cols] = h
        o_ref[:, cols] = (_gelu(gate_ref[:, cols]) * b_ref[:, cols]).astype(BF16)
    xbuf_ref[0:tail, :] = xbuf_ref[rows:rows + tail, :]

    @pl.when(tb == pl.num_programs(1) - 1)
    def _():
        hf_ref[...] = h_ref[...]
        convf_ref[...] = xbuf_ref[0:tail, :]


def _lru(gate, xin, conv0, h0, cw, cb, wr, br, wi, bi, lam, *, tblock):
    total, width = xin.shape
    n_bg = h0.shape[0] // SUBLANES
    tail = (cw.shape[0] - 1) * SUBLANES
    tb, grid, blk, per_group = _time_grid(total, n_bg, tblock)
    rows = tb * SUBLANES
    h_spec = per_group(SUBLANES, width)
    c_spec = per_group(tail, width)
    consts = (cw, cb, wr, br, wi, bi, lam)
    return pl.pallas_call(
        _lru_body,
        grid=grid,
        in_specs=[blk(width), blk(width), c_spec, h_spec] + [_const_spec(t.shape) for t in consts],
        out_specs=(blk(width), h_spec, c_spec),
        out_shape=(jax.ShapeDtypeStruct((total, width), BF16), jax.ShapeDtypeStruct((n_bg * SUBLANES, width), F32),
                   jax.ShapeDtypeStruct((n_bg * tail, width), F32)),
        scratch_shapes=[pltpu.VMEM((rows + tail, width), F32), pltpu.VMEM((rows, width), F32),
                        pltpu.VMEM((rows, width), F32), pltpu.VMEM((SUBLANES, width), F32)],
        compiler_params=_params(2),
        name="rg_lru",
    )(gate, xin, conv0, h0, *consts)


def _row(t):
    return t.reshape(1, -1).astype(F32)


def _to_rows(x):
    b, s, d = x.shape
    return jnp.transpose(x.reshape(b // SUBLANES, SUBLANES, s, d), (0, 2, 1, 3)).reshape(b * s, d)


def _from_rows(y, b, s):
    d = y.shape[-1]
    return jnp.transpose(y.reshape(b // SUBLANES, s, SUBLANES, d), (0, 2, 1, 3)).reshape(b, s, d)


def _enter(x):
    b, s, d = x.shape
    walk = _RowWalk(b * s, b, batch_major=True)
    if walk.interleave:
        return x.reshape(b * s, d), walk, lambda y: y.reshape(b, s, d)
    return _to_rows(x), _RowWalk(b * s, b, batch_major=False), lambda y: _from_rows(y, b, s)


def _even_layer(walk, x2d, nb, s_gla, s_re, s_im, prm, mlp, *, final):
    (g_norm, w_in, w_u, w_a, wa2, ba2, g_head, ab_re, ab_im, wb_re, wb_im, wc, d_skip, wglu, bglu, w_out, dims) = prm
    nh, dk, dv = dims
    steps = x2d.shape[0] // nb
    q, k, v, gt, u, la = _inproj_even(walk, x2d, g_norm, w_in, w_u, w_a, wa2, ba2, kw=nh * dk, vw=nh * dv)
    s0 = None if s_gla is None else s_gla.astype(F32)
    og, s_fin = _gla(q, k, la, v, gt, g_head, s0, nb, dims, c=math.gcd(steps, GLA_CHUNK), tblock=128)
    n_state = ab_re.shape[1]
    h0_re = jnp.zeros((nb, n_state), F32) if s_re is None else s_re.astype(F32).reshape(nb, n_state)
    h0_im = jnp.zeros((nb, n_state), F32) if s_im is None else s_im.astype(F32).reshape(nb, n_state)
    zz, hf_re, hf_im = _s5(u, ab_re, ab_im, wb_re, wb_im, wc, d_skip, wglu, bglu, h0_re, h0_im, tblock=64)
    g_mlp, w_up, w_down, g_fin = mlp
    y = _out_mlp(walk, x2d, og, zz, w_out, g_mlp, w_up, w_down, g_fin, final=final)
    return y, s_fin, hf_re, hf_im


def _odd_layer(walk, x2d, nb, s_lru, s_conv, prm, mlp, *, final):
    (g_norm, w_in, cw, cb, wr, br, wi, bi, lam, w_out) = prm
    n_bg = nb // SUBLANES
    taps = cw.shape[0]
    width = w_in.shape[1] // 2
    gate, xin = _inproj_odd(walk, x2d, g_norm, w_in)
    h0 = jnp.zeros((nb, width), F32) if s_lru is None else s_lru.astype(F32)
    if s_conv is None:
        conv0 = jnp.zeros((n_bg * (taps - 1) * SUBLANES, width), F32)
    else:
        conv0 = jnp.transpose(s_conv.astype(F32).reshape(n_bg, SUBLANES, taps - 1, width), (0, 2, 1, 3)).reshape(-1, width)
    gh, hf, convf = _lru(gate, xin, conv0, h0, cw, cb, wr, br, wi, bi, lam, tblock=64)
    g_mlp, w_up, w_down, g_fin = mlp
    y = _out_mlp(walk, x2d, gh, gh, w_out, g_mlp, w_up, w_down, g_fin, final=final)
    conv_new = jnp.transpose(convf.reshape(n_bg, taps - 1, SUBLANES, width), (0, 2, 1, 3)).reshape(nb, taps - 1, width)
    return y, hf, conv_new


def kernel(x_prompt, x_sample, state_gla, state_s5_re, state_s5_im, state_lru, state_conv, norm_mix_even, w_in_even, gla_w_a2, gla_b_a2, gla_norm, s5_a_re, s5_a_im, s5_log_dt, s5_b_re, s5_b_im, s5_c_re, s5_c_im, s5_d, s5_w_glu, s5_b_glu, w_out_even, norm_mix_odd, w_in_odd, conv_w, conv_b, lru_w_r, lru_b_r, lru_w_i, lru_b_i, lru_lam, w_out_odd, norm_mlp, w_up, w_down, norm_final):
    depth = norm_mlp.shape[0]
    nh, dk, dv = state_gla.shape[2:]
    rank = gla_w_a2.shape[1]
    n_groups, n_p, ch = s5_b_re.shape[1:]
    sw = n_groups * ch
    kw, vw = nh * dk, nh * dv
    assert rank <= LANES and LANES % ch == 0 and LANES % dk == 0 and dv == LANES
    bp = x_prompt.shape[0]
    bs = x_sample.shape[0]

    yp, walk_p, leave_p = _enter(x_prompt.astype(F32))
    ys, walk_s, leave_s = _enter(x_sample.astype(F32))
    g_fin = _row(norm_final)
    o4 = 2 * kw + 2 * vw
    o5 = o4 + rank
    w_in_main = w_in_even[:, :, :o4].astype(BF16)
    w_in_u = w_in_even[:, :, o5:].astype(BF16)
    w_in_a = jnp.pad(w_in_even[:, :, o4:o5], ((0, 0), (0, 0), (0, LANES - rank))).astype(BF16)
    w_a2 = jnp.pad(gla_w_a2, ((0, 0), (0, LANES - rank), (0, 0))).astype(BF16)
    w_up_b, w_down_b = w_up.astype(BF16), w_down.astype(BF16)
    w_out_even_b, w_out_odd_b, w_in_odd_b = w_out_even.astype(BF16), w_out_odd.astype(BF16), w_in_odd.astype(BF16)
    w_glu_b, w_r_b, w_i_b = s5_w_glu.astype(BF16), lru_w_r.astype(BF16), lru_w_i.astype(BF16)
    keys = ("gla", "re", "im", "lru", "conv")
    outs_p = {key: [] for key in keys}
    outs_s = {key: [] for key in keys}

    def record(outs, sg, sr, si):
        outs["gla"].append(sg)
        outs["re"].append(sr.reshape(-1, n_groups, n_p))
        outs["im"].append(si.reshape(-1, n_groups, n_p))

    for layer in range(depth):
        final = layer == depth - 1
        mlp = (_row(norm_mlp[layer]), w_up_b[layer], w_down_b[layer], g_fin)
        if layer % 2 == 0:
            e = layer // 2
            ab_re, ab_im, bb_re, bb_im = _s5_discretise(s5_a_re[e].astype(F32), s5_a_im[e].astype(F32), s5_log_dt[e].astype(F32),
                                                        s5_b_re[e].astype(F32), s5_b_im[e].astype(F32))
            wc = jnp.concatenate([_block_diag_out(s5_c_re[e].astype(F32)), -_block_diag_out(s5_c_im[e].astype(F32))], axis=1)
            prm = (_row(norm_mix_even[e]), w_in_main[e], w_in_u[e], w_in_a[e], w_a2[e], _row(gla_b_a2[e]), _row(gla_norm[e]),
                   ab_re.reshape(1, -1), ab_im.reshape(1, -1), _block_diag_in(bb_re).astype(BF16), _block_diag_in(bb_im).astype(BF16),
                   wc.astype(BF16), _row(s5_d[e]), w_glu_b[e], _row(s5_b_glu[e]), w_out_even_b[e], (nh, dk, dv))
            yp, sg, sr, si = _even_layer(walk_p, yp, bp, None, None, None, prm, mlp, final=final)
            record(outs_p, sg, sr, si)
            ys, sg, sr, si = _even_layer(walk_s, ys, bs, state_gla[e], state_s5_re[e], state_s5_im[e], prm, mlp, final=final)
            record(outs_s, sg, sr, si)
        else:
            o = layer // 2
            prm = (_row(norm_mix_odd[o]), w_in_odd_b[o], conv_w[o].astype(F32), _row(conv_b[o]),
                   w_r_b[o], _row(lru_b_r[o]), w_i_b[o], _row(lru_b_i[o]), _row(lru_lam[o]), w_out_odd_b[o])
            yp, sl, sc = _odd_layer(walk_p, yp, bp, None, None, prm, mlp, final=final)
            outs_p["lru"].append(sl)
            outs_p["conv"].append(sc)
            ys, sl, sc = _odd_layer(walk_s, ys, bs, state_lru[o], state_conv[o], prm, mlp, final=final)
            outs_s["lru"].append(sl)
            outs_s["conv"].append(sc)

    stack = lambda outs: tuple(jnp.stack(outs[key]) for key in keys)
    return (leave_p(yp), leave_s(ys)) + stack(outs_p) + stack(outs_s)
```

```python
import functools
import math

import jax
import jax.numpy as jnp
from jax import lax
from jax.experimental import pallas as pl
from jax.experimental.pallas import tpu as pltpu

F32 = jnp.float32
BF16 = jnp.bfloat16

NORM_EPS = 1e-6
GLA_TAU = 16.0
GLA_CHUNK = 16
LRU_C = 8.0

SUBLANES = 8
LANES = 128
ROW_TILE = 512
FF_CHUNK = 1024
VMEM_LIMIT = 48 * 1024 * 1024


def _params(n_axes):
    return pltpu.CompilerParams(dimension_semantics=("arbitrary",) * n_axes, vmem_limit_bytes=VMEM_LIMIT)


def _const_spec(shape):
    zeros = (0,) * len(shape)
    return pl.BlockSpec(shape, lambda *_: zeros, pipeline_mode=pl.Buffered(1))


def _resident(p):
    if not isinstance(p, tuple):
        return _const_spec(p.shape)
    stacked, layer = p
    index = (layer,) + (0,) * (stacked.ndim - 1)
    return pl.BlockSpec((None,) + stacked.shape[1:], lambda *_: index, pipeline_mode=pl.Buffered(1))


def _resident_array(p):
    return p[0] if isinstance(p, tuple) else p


def _resident_shape(p):
    return p[0].shape[1:] if isinstance(p, tuple) else p.shape


def _dot(a, b):
    return jnp.dot(a, b, preferred_element_type=F32)


def _dot_nt(a, b):
    return lax.dot_general(a, b, (((1,), (1,)), ((), ())), preferred_element_type=F32)


def _dot_tn(a, b):
    return lax.dot_general(a, b, (((0,), (0,)), ((), ())), preferred_element_type=F32)


def _rmsnorm(x, g):
    return x * lax.rsqrt(jnp.mean(x * x, axis=-1, keepdims=True) + NORM_EPS) * g


def _log_sigmoid(z):
    return jnp.minimum(z, 0.0) - jnp.log1p(jnp.exp(-jnp.abs(z)))


def _softplus(z):
    return jnp.maximum(z, 0.0) + jnp.log1p(jnp.exp(-jnp.abs(z)))


_GELU_C1 = math.sqrt(2.0 / math.pi)
_GELU_C2 = _GELU_C1 * 0.044715


def _gelu(x):
    half = 0.5 * x
    return half + half * jnp.tanh(x * (_GELU_C1 + _GELU_C2 * (x * x)))


def _sigmoid(x):
    return 0.5 * jnp.tanh(0.5 * x) + 0.5


def _rows(ref, start, size):
    return ref[pl.ds(pl.multiple_of(start, size), size), :]


def _row_tiles(rows):
    tm = min(ROW_TILE, rows)
    return (rows // tm,), lambda width, col=0: pl.BlockSpec((tm, width), lambda i: (i, col))


def _inproj_even_body(x_ref, g_ref, w_ref, wu_ref, wa_ref, wa2_ref, ba2_ref, q_ref, k_ref, v_ref, gt_ref, u_ref, la_ref, *, kw, vw):
    xn = _rmsnorm(x_ref[...], g_ref[...]).astype(BF16)
    o1, o2 = kw, 2 * kw
    o3 = o2 + vw
    o4 = o3 + vw
    q_ref[...] = _dot(xn, w_ref[:, 0:o1])
    k_ref[...] = _dot(xn, w_ref[:, o1:o2])
    v_ref[...] = _dot(xn, w_ref[:, o2:o3]).astype(BF16)
    gt_ref[...] = _dot(xn, w_ref[:, o3:o4])
    u_ref[...] = _dot(xn, wu_ref[...])
    a_lr = _dot(xn, wa_ref[...]).astype(BF16)
    z = _dot(a_lr, wa2_ref[...]) + ba2_ref[...]
    la_ref[...] = _log_sigmoid(z) * (1.0 / GLA_TAU)


def _inproj_even(x2d, g, w, wu, wa, wa2, ba2, *, kw, vw):
    rows, d = x2d.shape
    sw = _resident_shape(wu)[1]
    widths = (kw, kw, vw, vw, sw, kw)
    dtypes = (F32, F32, BF16, F32, F32, F32)
    consts = (g, w, wu, wa, wa2, ba2)
    grid, blk = _row_tiles(rows)
    return pl.pallas_call(
        functools.partial(_inproj_even_body, kw=kw, vw=vw),
        grid=grid,
        in_specs=[blk(d)] + [_resident(p) for p in consts],
        out_specs=tuple(blk(wd) for wd in widths),
        out_shape=tuple(jax.ShapeDtypeStruct((rows, wd), dt) for wd, dt in zip(widths, dtypes)),
        compiler_params=_params(1),
        name="inproj_even",
    )(x2d, *[_resident_array(p) for p in consts])


def _inproj_odd_body(x_ref, g_ref, w_ref, gate_ref, xin_ref, *, width):
    xn = _rmsnorm(x_ref[...], g_ref[...]).astype(BF16)
    gate_ref[...] = _dot(xn, w_ref[:, 0:width])
    xin_ref[...] = _dot(xn, w_ref[:, width:2 * width])


def _inproj_odd(x2d, g, w):
    rows, d = x2d.shape
    width = _resident_shape(w)[1] // 2
    grid, blk = _row_tiles(rows)
    shape = jax.ShapeDtypeStruct((rows, width), F32)
    return pl.pallas_call(
        functools.partial(_inproj_odd_body, width=width),
        grid=grid,
        in_specs=[blk(d), _resident(g), _resident(w)],
        out_specs=(blk(width), blk(width)),
        out_shape=(shape, shape),
        compiler_params=_params(1),
        name="inproj_odd",
    )(x2d, _resident_array(g), _resident_array(w))


def _out_mlp_body(x_ref, a1_ref, a2_ref, wo_ref, g_ref, wu_ref, wd_ref, gf_ref, o_ref, *, final):
    half = a1_ref.shape[1]
    mix = _dot(a1_ref[...], wo_ref[0:half, :]) + _dot(a2_ref[...], wo_ref[half:2 * half, :])
    x1 = x_ref[...] + mix
    xn = _rmsnorm(x1, g_ref[...]).astype(BF16)
    acc = x1
    for c in range(wu_ref.shape[1] // FF_CHUNK):
        h = _dot(xn, wu_ref[:, c * FF_CHUNK:(c + 1) * FF_CHUNK])
        h = jnp.square(jnp.maximum(h, 0.0)).astype(BF16)
        acc = acc + _dot(h, wd_ref[c * FF_CHUNK:(c + 1) * FF_CHUNK, :])
    if final:
        acc = _rmsnorm(acc, gf_ref[...])
    o_ref[...] = acc


def _out_mlp(x2d, a1, a2, wo, g, wu, wd, gf, *, final):
    rows, d = x2d.shape
    half = d // 2
    consts = (wo, g, wu, wd, gf)
    grid, blk = _row_tiles(rows)
    return pl.pallas_call(
        functools.partial(_out_mlp_body, final=final),
        grid=grid,
        in_specs=[blk(d), blk(half, 0), blk(half, a1.shape[1] // half - 1)] + [_resident(p) for p in consts],
        out_specs=blk(d),
        out_shape=jax.ShapeDtypeStruct((rows, d), F32),
        compiler_params=_params(1),
        name="out_mlp",
    )(x2d, a1, a2, *[_resident_array(p) for p in consts])


def _time_grid(total_rows, n_bg, tblock):
    steps = total_rows // (n_bg * SUBLANES)
    tb = min(tblock, steps)
    n_t = steps // tb
    blk = lambda width: pl.BlockSpec((tb * SUBLANES, width), lambda bg, t: (bg * n_t + t, 0))
    per_group = lambda rows, width: pl.BlockSpec((rows, width), lambda bg, t: (bg, 0))
    return tb, (n_bg, n_t), blk, per_group


def _gla_body(q_ref, k_ref, la_ref, v_ref, gt_ref, gh_ref, s0_ref, *rest, c, nh, dk, dv, transposed):
    o_ref, sfin_ref, st_ref = rest[-3:]
    tb = pl.program_id(1)
    rows = SUBLANES * c
    n_chunks = q_ref.shape[0] // rows
    dk_shift = int(math.log2(dk))
    heads_per_tile = LANES // dk

    @pl.when(tb == 0)
    def _():
        st_ref[...] = s0_ref[...]

    def iota(shape, axis):
        return lax.broadcasted_iota(jnp.int32, shape, axis)

    causal = ((iota((rows, rows), 0) & 7) == (iota((rows, rows), 1) & 7)) & (iota((rows, rows), 1) <= iota((rows, rows), 0))
    emask = (iota((rows, SUBLANES * dk), 0) & 7) == lax.shift_right_logical(iota((rows, SUBLANES * dk), 1), dk_shift)
    dmask = iota((SUBLANES, SUBLANES * dk), 0) == lax.shift_right_logical(iota((SUBLANES, SUBLANES * dk), 1), dk_shift)
    lo_rows = iota((rows, LANES), 1) < dk
    lo_8 = iota((SUBLANES, LANES), 1) < dk

    def expand(tile, h, lo=lo_rows, mask=emask):
        rolled = pltpu.roll(tile, dk, axis=1)
        d = jnp.where(lo, tile, rolled) if h % heads_per_tile == 0 else jnp.where(lo, rolled, tile)
        return jnp.where(mask, jnp.concatenate([d] * (SUBLANES * dk // LANES), axis=1), 0.0)

    def chunk(n, carry):
        r0 = n * rows
        q = _rows(q_ref, r0, rows)
        k = _rows(k_ref, r0, rows)
        la = _rows(la_ref, r0, rows)
        v = _rows(v_ref, r0, rows)
        gt = _rows(gt_ref, r0, rows)
        acc = jnp.zeros((SUBLANES, nh * dk), F32)
        pieces = []
        for t in range(c):
            acc = acc + la[t * SUBLANES:(t + 1) * SUBLANES, :]
            pieces.append(acc)
        cum = jnp.concatenate(pieces, axis=0)
        last = jnp.concatenate([acc] * c, axis=0)
        q_t = q * jnp.exp(cum) * (dk ** -0.5)
        k_t = k * jnp.exp(-cum)
        k_end = k * jnp.exp(last - cum)
        dec = jnp.exp(acc)
        pad = jnp.zeros((LANES - SUBLANES, LANES), F32)
        outs = []
        for h in range(nh):
            tile = (h // heads_per_tile) * LANES
            lane0 = (h % heads_per_tile) * dk
            sel = lo_rows if lane0 == 0 else jnp.logical_not(lo_rows)
            q_tile = q_t[:, tile:tile + LANES]
            scores = _dot_nt(jnp.where(sel, q_tile, 0.0).astype(BF16), k_t[:, tile:tile + LANES].astype(BF16))
            scores = jnp.where(causal, scores, 0.0).astype(BF16)
            v_h = v[:, h * dv:(h + 1) * dv]
            q_exp = expand(q_tile, h).astype(BF16)
            k_exp = expand(k_end[:, tile:tile + LANES], h).astype(BF16)
            if transposed:
                st = st_ref[h]
                o = _dot(scores, v_h) + _dot_nt(q_exp, st.astype(BF16))
                dec_row = jnp.sum(expand(dec[:, tile:tile + LANES], h, lo_8, dmask), axis=0, keepdims=True)
                st_ref[h] = st * dec_row + _dot_tn(v_h, k_exp)
            else:
                st = jnp.concatenate([st_ref[b, h] for b in range(SUBLANES)], axis=0)
                o = _dot(scores, v_h) + _dot(q_exp, st.astype(BF16))
                upd = _dot_tn(k_exp, v_h)
                dec_t = jnp.concatenate([dec[:, tile:tile + LANES], pad], axis=0).T
                for b in range(SUBLANES):
                    dec_b = jnp.broadcast_to(dec_t[lane0:lane0 + dk, b:b + 1], (dk, dv))
                    st_ref[b, h] = st[b * dk:(b + 1) * dk, :] * dec_b + upd[b * dk:(b + 1) * dk, :]
            o = _rmsnorm(o, gh_ref[:, h * dv:(h + 1) * dv])
            g_h = gt[:, h * dv:(h + 1) * dv]
            outs.append(o * (g_h * _sigmoid(g_h)))
        o_ref[pl.ds(pl.multiple_of(r0, rows), rows), :] = jnp.concatenate(outs, axis=1).astype(BF16)
        return carry

    lax.fori_loop(0, n_chunks, chunk, 0, unroll=min(n_chunks, 2))

    @pl.when(tb == pl.num_programs(1) - 1)
    def _():
        sfin_ref[...] = st_ref[...]


def _gla_state_in(s):
    b, h, dk, dv = s.shape
    return jnp.transpose(s.reshape(b // SUBLANES, SUBLANES, h, dk, dv), (0, 2, 4, 1, 3)).reshape(b // SUBLANES * h, dv, SUBLANES * dk)


def _gla_state_out(st, nh, dk):
    dv = st.shape[1]
    n_bg = st.shape[0] // nh
    return jnp.transpose(st.reshape(n_bg, nh, dv, SUBLANES, dk), (0, 3, 1, 4, 2)).reshape(n_bg * SUBLANES, nh, dk, dv)


def _gla(q, k, la, v, gt, gh, states, layer, n_layers, acc, nb, dims, *, c, tblock):
    nh, dk, dv = dims
    n_bg = nb // SUBLANES
    tb, grid, blk, _ = _time_grid(q.shape[0], n_bg, tblock)
    transposed = q.shape[0] // n_bg > SUBLANES * c
    body = functools.partial(_gla_body, c=c, nh=nh, dk=dk, dv=dv, transposed=transposed)
    row_specs = [blk(nh * dk), blk(nh * dk), blk(nh * dk), blk(nh * dv), blk(nh * dv), _const_spec(gh.shape)]
    og_shape = jax.ShapeDtypeStruct((q.shape[0], nh * dv), BF16)
    if transposed:
        st_block = (nh, dv, SUBLANES * dk)
        st_spec = pl.BlockSpec(st_block, lambda bg, t: (bg, 0, 0))
        s_in = jnp.zeros((n_bg * nh, dv, SUBLANES * dk), F32) if states is None else _gla_state_in(states[layer].astype(F32))
        og, s_fin = pl.pallas_call(
            body, grid=grid, in_specs=row_specs + [st_spec], out_specs=(blk(nh * dv), st_spec),
            out_shape=(og_shape, jax.ShapeDtypeStruct(s_in.shape, F32)), scratch_shapes=[pltpu.VMEM(st_block, F32)],
            compiler_params=_params(2), name="gla",
        )(q, k, la, v, gt, gh, s_in)
        s_new = _gla_state_out(s_fin, nh, dk)[None]
        return og, (jnp.pad(s_new, ((0, n_layers - 1),) + ((0, 0),) * 4) if acc is None else lax.dynamic_update_slice(acc, s_new, (layer, 0, 0, 0, 0)))
    st_block = (SUBLANES, nh, dk, dv)
    stacked_spec = pl.BlockSpec((None,) + st_block, lambda bg, t: (layer, bg, 0, 0, 0))
    if states is None:
        s_in, in_spec = jnp.zeros((nb, nh, dk, dv), F32), pl.BlockSpec(st_block, lambda bg, t: (bg, 0, 0, 0))
    else:
        s_in, in_spec = states.astype(F32), stacked_spec
    operands = [q, k, la, v, gt, gh, s_in]
    in_specs = row_specs + [in_spec]
    aliases = {}
    if acc is not None:
        aliases = {len(operands): 1}
        operands.append(acc)
        in_specs.append(pl.BlockSpec(memory_space=pl.ANY))
    return pl.pallas_call(
        body, grid=grid, in_specs=in_specs, out_specs=(blk(nh * dv), stacked_spec),
        out_shape=(og_shape, jax.ShapeDtypeStruct((n_layers, nb, nh, dk, dv), F32)), scratch_shapes=[pltpu.VMEM(st_block, F32)],
        input_output_aliases=aliases, compiler_params=_params(2), name="gla",
    )(*operands)


def _s5_disc_body(a_re_ref, a_im_ref, log_dt_ref, b_re_ref, b_im_ref, ab_re_ref, ab_im_ref, bb_re_ref, bb_im_ref):
    a_re = a_re_ref[...]
    a_im = a_im_ref[...]
    dt = jnp.exp(log_dt_ref[...])
    mag = jnp.exp(dt * a_re)
    ab_re = mag * jnp.cos(dt * a_im)
    ab_im = mag * jnp.sin(dt * a_im)
    den = a_re * a_re + a_im * a_im
    coef_re = ((ab_re - 1.0) * a_re + ab_im * a_im) / den
    coef_im = (ab_im * a_re - (ab_re - 1.0) * a_im) / den
    b_re = b_re_ref[...]
    b_im = b_im_ref[...]
    ab_re_ref[...] = ab_re
    ab_im_ref[...] = ab_im
    bb_re_ref[...] = coef_re * b_re - coef_im * b_im
    bb_im_ref[...] = coef_re * b_im + coef_im * b_re


def _s5_discretise(a_re, a_im, log_dt, b_re, b_im):
    g, p, ch = b_re.shape
    rep = lambda t: jnp.repeat(t, ch, axis=0)
    flat = lambda t: jnp.transpose(t, (0, 2, 1)).reshape(g * ch, p)
    args = (rep(a_re), rep(a_im), rep(jnp.broadcast_to(log_dt[:, None], (g, p))), flat(b_re), flat(b_im))
    spec = _const_spec((g * ch, p))
    shape = jax.ShapeDtypeStruct((g * ch, p), F32)
    ab_re, ab_im, bb_re, bb_im = pl.pallas_call(
        _s5_disc_body, grid=(1,), in_specs=[spec] * 5, out_specs=(spec,) * 4, out_shape=(shape,) * 4,
        compiler_params=_params(1), name="s5_discretise",
    )(*args)
    return ab_re[::ch], ab_im[::ch], bb_re.reshape(g, ch, p), bb_im.reshape(g, ch, p)


def _block_diag_in(w_gcp):
    g, ch, p = w_gcp.shape
    gpt = LANES // ch
    w = w_gcp.reshape(g // gpt, gpt, ch, p)
    eye = jnp.eye(gpt, dtype=w.dtype)
    return jnp.einsum("ab,jacp->jacbp", eye, w).reshape(g // gpt, gpt * ch, gpt * p)


def _block_diag_out(w_gcp):
    g, ch, p = w_gcp.shape
    gpt = LANES // ch
    w = w_gcp.reshape(g // gpt, gpt, ch, p)
    eye = jnp.eye(gpt, dtype=w.dtype)
    return jnp.einsum("ab,jacp->japbc", eye, w).reshape(g // gpt, gpt * p, gpt * ch)


def _s5_body(u_ref, ab_re_ref, ab_im_ref, wb_re_ref, wb_im_ref, wc_ref, d_ref, wglu_ref, bglu_ref, h0_re_ref, h0_im_ref,
             z_ref, hf_re_ref, hf_im_ref, h_re_ref, h_im_ref, bu_re_ref, bu_im_ref, y_ref):
    tb = pl.program_id(1)
    rows = u_ref.shape[0]
    steps = rows // SUBLANES
    n_tiles = wb_re_ref.shape[0]
    sw = wb_re_ref.shape[2]

    @pl.when(tb == 0)
    def _():
        h_re_ref[...] = h0_re_ref[...]
        h_im_ref[...] = h0_im_ref[...]

    for j in range(n_tiles):
        u_j = u_ref[:, j * LANES:(j + 1) * LANES]
        u_b = u_j.astype(BF16)
        bu_re_ref[j] = _dot(u_b, wb_re_ref[j])
        bu_im_ref[j] = _dot(u_b, wb_im_ref[j])
        a_re = jnp.broadcast_to(ab_re_ref[:, j * sw:(j + 1) * sw], (SUBLANES, sw))
        a_im = jnp.broadcast_to(ab_im_ref[:, j * sw:(j + 1) * sw], (SUBLANES, sw))
        h_re = h_re_ref[:, j * sw:(j + 1) * sw]
        h_im = h_im_ref[:, j * sw:(j + 1) * sw]
        for t in range(steps):
            sl = slice(t * SUBLANES, (t + 1) * SUBLANES)
            h_re, h_im = (a_re * h_re - a_im * h_im + bu_re_ref[j, sl, :], a_re * h_im + a_im * h_re + bu_im_ref[j, sl, :])
            bu_re_ref[j, sl, :] = h_re
            bu_im_ref[j, sl, :] = h_im
        h_re_ref[:, j * sw:(j + 1) * sw] = h_re
        h_im_ref[:, j * sw:(j + 1) * sw] = h_im
        h_cat = jnp.concatenate([bu_re_ref[j].astype(BF16), bu_im_ref[j].astype(BF16)], axis=1)
        y_ref[:, j * LANES:(j + 1) * LANES] = _dot(h_cat, wc_ref[j]) + d_ref[:, j * LANES:(j + 1) * LANES] * u_j

    z = _gelu(y_ref[...])
    gate = _sigmoid(_dot(z.astype(BF16), wglu_ref[...]) + bglu_ref[...])
    z_ref[...] = (z * gate).astype(BF16)

    @pl.when(tb == pl.num_programs(1) - 1)
    def _():
        hf_re_ref[...] = h_re_ref[...]
        hf_im_ref[...] = h_im_ref[...]


def _s5(u, ab_re, ab_im, wb_re, wb_im, wc, d_skip, wglu, bglu, h0_re, h0_im, *, tblock):
    total, width = u.shape
    n_state = ab_re.shape[1]
    n_bg = h0_re.shape[0] // SUBLANES
    sw = wb_re.shape[2]
    tb, grid, blk, per_group = _time_grid(total, n_bg, tblock)
    rows = tb * SUBLANES
    st_spec = per_group(SUBLANES, n_state)
    consts = (ab_re, ab_im, wb_re, wb_im, wc, d_skip, wglu, bglu)
    st_shape = jax.ShapeDtypeStruct((n_bg * SUBLANES, n_state), F32)
    return pl.pallas_call(
        _s5_body,
        grid=grid,
        in_specs=[blk(width)] + [_resident(p) for p in consts] + [st_spec, st_spec],
        out_specs=(blk(width), st_spec, st_spec),
        out_shape=(jax.ShapeDtypeStruct((total, width), BF16), st_shape, st_shape),
        scratch_shapes=[pltpu.VMEM((SUBLANES, n_state), F32), pltpu.VMEM((SUBLANES, n_state), F32),
                        pltpu.VMEM((wb_re.shape[0], rows, sw), F32), pltpu.VMEM((wb_re.shape[0], rows, sw), F32),
                        pltpu.VMEM((rows, width), F32)],
        compiler_params=_params(2),
        name="s5",
    )(u, *[_resident_array(p) for p in consts], h0_re, h0_im)


def _lru_body(gate_ref, xin_ref, conv0_ref, h0_ref, cw_ref, cb_ref, wr_ref, br_ref, wi_ref, bi_ref, lam_ref,
              o_ref, hf_ref, convf_ref, xbuf_ref, a_ref, b_ref, h_ref):
    tb = pl.program_id(1)
    rows, width = xin_ref.shape
    steps = rows // SUBLANES
    taps = cw_ref.shape[0]
    tail = (taps - 1) * SUBLANES
    n_blocks = wr_ref.shape[0]
    bw = wr_ref.shape[1]

    @pl.when(tb == 0)
    def _():
        xbuf_ref[0:tail, :] = conv0_ref[...]
        h_ref[...] = h0_ref[...]

    xbuf_ref[tail:tail + rows, :] = xin_ref[...]
    for n in range(n_blocks):
        cols = slice(n * bw, (n + 1) * bw)
        xc = cb_ref[:, cols] + xbuf_ref[0:rows, cols] * cw_ref[0:1, cols]
        for j in range(1, taps):
            xc = xc + xbuf_ref[j * SUBLANES:j * SUBLANES + rows, cols] * cw_ref[j:j + 1, cols]
        xb = xc.astype(BF16)
        i = _sigmoid(_dot(xb, wi_ref[n]) + bi_ref[:, cols])
        half_rate = (-0.5 * LRU_C) * _softplus(-lam_ref[:, cols])
        log_a = jnp.tanh(0.5 * (_dot(xb, wr_ref[n]) + br_ref[:, cols])) * half_rate + half_rate
        a_ref[:, cols] = jnp.exp(log_a)
        th = jnp.tanh(log_a)
        y = -2.0 * th / (1.0 - th)
        root = jnp.where(y > 0.0, y * lax.rsqrt(y), 0.0)
        b_ref[:, cols] = root * (i * xc)
        h = h_ref[:, cols]
        for t in range(steps):
            sl = slice(t * SUBLANES, (t + 1) * SUBLANES)
            h = a_ref[sl, cols] * h + b_ref[sl, cols]
            b_ref[sl, cols] = h
        h_ref[:, cols] = h
        o_ref[:, cols] = (_gelu(gate_ref[:, cols]) * b_ref[:, cols]).astype(BF16)
    xbuf_ref[0:tail, :] = xbuf_ref[rows:rows + tail, :]

    @pl.when(tb == pl.num_programs(1) - 1)
    def _():
        hf_ref[...] = h_ref[...]
        convf_ref[...] = xbuf_ref[0:tail, :]


def _lru(gate, xin, conv0, h0, cw, cb, wr, br, wi, bi, lam, *, tblock):
    total, width = xin.shape
    n_bg = h0.shape[0] // SUBLANES
    tail = (cw.shape[0] - 1) * SUBLANES
    tb, grid, blk, per_group = _time_grid(total, n_bg, tblock)
    rows = tb * SUBLANES
    h_spec = per_group(SUBLANES, width)
    c_spec = per_group(tail, width)
    consts = (cw, cb, wr, br, wi, bi, lam)
    return pl.pallas_call(
        _lru_body,
        grid=grid,
        in_specs=[blk(width), blk(width), c_spec, h_spec] + [_resident(p) for p in consts],
        out_specs=(blk(width), h_spec, c_spec),
        out_shape=(jax.ShapeDtypeStruct((total, width), BF16), jax.ShapeDtypeStruct((n_bg * SUBLANES, width), F32),
                   jax.ShapeDtypeStruct((n_bg * tail, width), F32)),
        scratch_shapes=[pltpu.VMEM((rows + tail, width), F32), pltpu.VMEM((rows, width), F32),
                        pltpu.VMEM((rows, width), F32), pltpu.VMEM((SUBLANES, width), F32)],
        compiler_params=_params(2),
        name="rg_lru",
    )(gate, xin, conv0, h0, *[_resident_array(p) for p in consts])


def _row(t):
    return t.reshape(1, -1).astype(F32)


def _to_rows(x):
    b, s, d = x.shape
    return jnp.transpose(x.reshape(b // SUBLANES, SUBLANES, s, d), (0, 2, 1, 3)).reshape(b * s, d)


def _from_rows(y, b, s):
    d = y.shape[-1]
    return jnp.transpose(y.reshape(b // SUBLANES, s, SUBLANES, d), (0, 2, 1, 3)).reshape(b, s, d)


def _even_layer(x2d, nb, gla_states, gla_acc, e, n_even, s_re, s_im, prm, mlp, *, final):
    (g_norm, w_in, w_u, w_a, wa2, ba2, g_head, ab_re, ab_im, wb_re, wb_im, wc, d_skip, wglu, bglu, w_out, dims) = prm
    nh, dk, dv = dims
    steps = x2d.shape[0] // nb
    q, k, v, gt, u, la = _inproj_even(x2d, g_norm, w_in, w_u, w_a, wa2, ba2, kw=nh * dk, vw=nh * dv)
    og, gla_acc = _gla(q, k, la, v, gt, g_head, gla_states, e, n_even, gla_acc, nb, dims, c=math.gcd(steps, GLA_CHUNK), tblock=128)
    n_state = ab_re.shape[1]
    h0_re = jnp.zeros((nb, n_state), F32) if s_re is None else s_re.astype(F32).reshape(nb, n_state)
    h0_im = jnp.zeros((nb, n_state), F32) if s_im is None else s_im.astype(F32).reshape(nb, n_state)
    zz, hf_re, hf_im = _s5(u, ab_re, ab_im, wb_re, wb_im, wc, d_skip, wglu, bglu, h0_re, h0_im, tblock=64)
    g_mlp, w_up, w_down, g_fin = mlp
    y = _out_mlp(x2d, og, zz, w_out, g_mlp, w_up, w_down, g_fin, final=final)
    return y, gla_acc, hf_re, hf_im


def _odd_layer(x2d, nb, s_lru, s_conv, prm, mlp, *, final):
    (g_norm, w_in, cw, cb, wr, br, wi, bi, lam, w_out) = prm
    n_bg = nb // SUBLANES
    taps = cw.shape[0]
    width = _resident_shape(w_in)[1] // 2
    gate, xin = _inproj_odd(x2d, g_norm, w_in)
    h0 = jnp.zeros((nb, width), F32) if s_lru is None else s_lru.astype(F32)
    if s_conv is None:
        conv0 = jnp.zeros((n_bg * (taps - 1) * SUBLANES, width), F32)
    else:
        conv0 = jnp.transpose(s_conv.astype(F32).reshape(n_bg, SUBLANES, taps - 1, width), (0, 2, 1, 3)).reshape(-1, width)
    gh, hf, convf = _lru(gate, xin, conv0, h0, cw, cb, wr, br, wi, bi, lam, tblock=64)
    g_mlp, w_up, w_down, g_fin = mlp
    y = _out_mlp(x2d, gh, gh, w_out, g_mlp, w_up, w_down, g_fin, final=final)
    conv_new = jnp.transpose(convf.reshape(n_bg, taps - 1, SUBLANES, width), (0, 2, 1, 3)).reshape(nb, taps - 1, width)
    return y, hf, conv_new


def kernel(x_prompt, x_sample, state_gla, state_s5_re, state_s5_im, state_lru, state_conv, norm_mix_even, w_in_even, gla_w_a2, gla_b_a2, gla_norm, s5_a_re, s5_a_im, s5_log_dt, s5_b_re, s5_b_im, s5_c_re, s5_c_im, s5_d, s5_w_glu, s5_b_glu, w_out_even, norm_mix_odd, w_in_odd, conv_w, conv_b, lru_w_r, lru_b_r, lru_w_i, lru_b_i, lru_lam, w_out_odd, norm_mlp, w_up, w_down, norm_final):
    depth = norm_mlp.shape[0]
    nh, dk, dv = state_gla.shape[2:]
    rank = gla_w_a2.shape[1]
    n_groups, n_p, ch = s5_b_re.shape[1:]
    sw = n_groups * ch
    kw, vw = nh * dk, nh * dv
    assert rank <= LANES and LANES % ch == 0 and LANES % dk == 0 and dv == LANES
    bp, sp, _ = x_prompt.shape
    bs, ss, _ = x_sample.shape
    n_even = state_gla.shape[0]

    yp = _to_rows(x_prompt.astype(F32))
    ys = _to_rows(x_sample.astype(F32))
    g_fin = _row(norm_final)
    o4 = 2 * kw + 2 * vw
    o5 = o4 + rank
    w_in_main = w_in_even[:, :, :o4].astype(BF16)
    w_in_u = w_in_even[:, :, o5:].astype(BF16)
    w_in_a = jnp.pad(w_in_even[:, :, o4:o5], ((0, 0), (0, 0), (0, LANES - rank))).astype(BF16)
    w_a2 = jnp.pad(gla_w_a2, ((0, 0), (0, LANES - rank), (0, 0))).astype(BF16)
    w_up_b, w_down_b = w_up.astype(BF16), w_down.astype(BF16)
    w_out_even_b, w_out_odd_b, w_in_odd_b = w_out_even.astype(BF16), w_out_odd.astype(BF16), w_in_odd.astype(BF16)
    w_glu_b, w_r_b, w_i_b = s5_w_glu.astype(BF16), lru_w_r.astype(BF16), lru_w_i.astype(BF16)
    keys = ("re", "im", "lru", "conv")
    outs_p = {key: [] for key in keys}
    outs_s = {key: [] for key in keys}
    gla_p = gla_s = None

    def record(outs, sr, si):
        outs["re"].append(sr.reshape(-1, n_groups, n_p))
        outs["im"].append(si.reshape(-1, n_groups, n_p))

    for layer in range(depth):
        final = layer == depth - 1
        mlp = (_row(norm_mlp[layer]), (w_up_b, layer), (w_down_b, layer), g_fin)
        if layer % 2 == 0:
            e = layer // 2
            ab_re, ab_im, bb_re, bb_im = _s5_discretise(s5_a_re[e].astype(F32), s5_a_im[e].astype(F32), s5_log_dt[e].astype(F32),
                                                        s5_b_re[e].astype(F32), s5_b_im[e].astype(F32))
            wc = jnp.concatenate([_block_diag_out(s5_c_re[e].astype(F32)), -_block_diag_out(s5_c_im[e].astype(F32))], axis=1)
            prm = (_row(norm_mix_even[e]), (w_in_main, e), (w_in_u, e), (w_in_a, e), (w_a2, e), _row(gla_b_a2[e]), _row(gla_norm[e]),
                   ab_re.reshape(1, -1), ab_im.reshape(1, -1), _block_diag_in(bb_re).astype(BF16), _block_diag_in(bb_im).astype(BF16),
                   wc.astype(BF16), _row(s5_d[e]), (w_glu_b, e), _row(s5_b_glu[e]), (w_out_even_b, e), (nh, dk, dv))
            yp, gla_p, sr, si = _even_layer(yp, bp, None, gla_p, e, n_even, None, None, prm, mlp, final=final)
            record(outs_p, sr, si)
            ys, gla_s, sr, si = _even_layer(ys, bs, state_gla, gla_s, e, n_even, state_s5_re[e], state_s5_im[e], prm, mlp, final=final)
            record(outs_s, sr, si)
        else:
            o = layer // 2
            prm = (_row(norm_mix_odd[o]), (w_in_odd_b, o), conv_w[o].astype(F32), _row(conv_b[o]),
                   (w_r_b, o), _row(lru_b_r[o]), (w_i_b, o), _row(lru_b_i[o]), _row(lru_lam[o]), (w_out_odd_b, o))
            yp, sl, sc = _odd_layer(yp, bp, None, None, prm, mlp, final=final)
            outs_p["lru"].append(sl)
            outs_p["conv"].append(sc)
            ys, sl, sc = _odd_layer(ys, bs, state_lru[o], state_conv[o], prm, mlp, final=final)
            outs_s["lru"].append(sl)
            outs_s["conv"].append(sc)

    stack = lambda outs: tuple(jnp.stack(outs[key]) for key in keys)
    return (_from_rows(yp, bp, sp), _from_rows(ys, bs, ss), gla_p) + stack(outs_p) + (gla_s,) + stack(outs_s)
```

```python
import functools
import math

import jax
import jax.numpy as jnp
from jax import lax
from jax.experimental import pallas as pl
from jax.experimental.pallas import tpu as pltpu

F32 = jnp.float32
BF16 = jnp.bfloat16

NORM_EPS = 1e-6
GLA_TAU = 16.0
GLA_CHUNK = 16
LRU_C = 8.0

SUBLANES = 8
LANES = 128
ROW_TILE = 512
FF_CHUNK = 1024
VMEM_LIMIT = 48 * 1024 * 1024


def _params(n_axes):
    return pltpu.CompilerParams(dimension_semantics=("arbitrary",) * n_axes, vmem_limit_bytes=VMEM_LIMIT)


def _const_spec(shape):
    zeros = (0,) * len(shape)
    return pl.BlockSpec(shape, lambda *_: zeros, pipeline_mode=pl.Buffered(1))


def _resident(p):
    if not isinstance(p, tuple):
        return _const_spec(p.shape)
    stacked, layer = p
    index = (layer,) + (0,) * (stacked.ndim - 1)
    return pl.BlockSpec((None,) + stacked.shape[1:], lambda *_: index, pipeline_mode=pl.Buffered(1))


def _resident_array(p):
    return p[0] if isinstance(p, tuple) else p


def _resident_shape(p):
    return p[0].shape[1:] if isinstance(p, tuple) else p.shape


def _dot(a, b):
    return jnp.dot(a, b, preferred_element_type=F32)


def _dot_nt(a, b):
    return lax.dot_general(a, b, (((1,), (1,)), ((), ())), preferred_element_type=F32)


def _dot_tn(a, b):
    return lax.dot_general(a, b, (((0,), (0,)), ((), ())), preferred_element_type=F32)


def _rmsnorm(x, g):
    return x * lax.rsqrt(jnp.mean(x * x, axis=-1, keepdims=True) + NORM_EPS) * g


def _log_sigmoid(z):
    return jnp.minimum(z, 0.0) - jnp.log1p(jnp.exp(-jnp.abs(z)))


def _softplus(z):
    return jnp.maximum(z, 0.0) + jnp.log1p(jnp.exp(-jnp.abs(z)))


_GELU_C1 = math.sqrt(2.0 / math.pi)
_GELU_C2 = _GELU_C1 * 0.044715


def _gelu(x):
    half = 0.5 * x
    return half + half * jnp.tanh(x * (_GELU_C1 + _GELU_C2 * (x * x)))


def _sigmoid(x):
    return 0.5 * jnp.tanh(0.5 * x) + 0.5


def _rows(ref, start, size):
    return ref[pl.ds(pl.multiple_of(start, size), size), :]


def _row_tiles(rows):
    tm = min(ROW_TILE, rows)
    return (rows // tm,), lambda width, col=0: pl.BlockSpec((tm, width), lambda i: (i, col))


def _inproj_even_body(x_ref, g_ref, w_ref, wu_ref, wa_ref, wa2_ref, ba2_ref, q_ref, k_ref, v_ref, gt_ref, u_ref, la_ref, *, kw, vw):
    xn = _rmsnorm(x_ref[...], g_ref[...]).astype(BF16)
    o1, o2 = kw, 2 * kw
    o3 = o2 + vw
    o4 = o3 + vw
    q_ref[...] = _dot(xn, w_ref[:, 0:o1])
    k_ref[...] = _dot(xn, w_ref[:, o1:o2])
    v_ref[...] = _dot(xn, w_ref[:, o2:o3]).astype(BF16)
    gt_ref[...] = _dot(xn, w_ref[:, o3:o4])
    u_ref[...] = _dot(xn, wu_ref[...])
    a_lr = _dot(xn, wa_ref[...]).astype(BF16)
    z = _dot(a_lr, wa2_ref[...]) + ba2_ref[...]
    la_ref[...] = _log_sigmoid(z) * (1.0 / GLA_TAU)


def _inproj_even(x2d, g, w, wu, wa, wa2, ba2, *, kw, vw):
    rows, d = x2d.shape
    sw = _resident_shape(wu)[1]
    widths = (kw, kw, vw, vw, sw, kw)
    dtypes = (F32, F32, BF16, F32, F32, F32)
    consts = (g, w, wu, wa, wa2, ba2)
    grid, blk = _row_tiles(rows)
    return pl.pallas_call(
        functools.partial(_inproj_even_body, kw=kw, vw=vw),
        grid=grid,
        in_specs=[blk(d)] + [_resident(p) for p in consts],
        out_specs=tuple(blk(wd) for wd in widths),
        out_shape=tuple(jax.ShapeDtypeStruct((rows, wd), dt) for wd, dt in zip(widths, dtypes)),
        compiler_params=_params(1),
        name="inproj_even",
    )(x2d, *[_resident_array(p) for p in consts])


def _out_mlp_body(x_ref, a1_ref, a2_ref, wo_ref, g_ref, wu_ref, wd_ref, gf_ref, o_ref, *, final):
    half = a1_ref.shape[1]
    mix = _dot(a1_ref[...], wo_ref[0:half, :]) + _dot(a2_ref[...], wo_ref[half:2 * half, :])
    x1 = x_ref[...] + mix
    xn = _rmsnorm(x1, g_ref[...]).astype(BF16)
    acc = x1
    for c in range(wu_ref.shape[1] // FF_CHUNK):
        h = _dot(xn, wu_ref[:, c * FF_CHUNK:(c + 1) * FF_CHUNK])
        h = jnp.square(jnp.maximum(h, 0.0)).astype(BF16)
        acc = acc + _dot(h, wd_ref[c * FF_CHUNK:(c + 1) * FF_CHUNK, :])
    if final:
        acc = _rmsnorm(acc, gf_ref[...])
    o_ref[...] = acc


def _out_mlp(x2d, a1, a2, wo, g, wu, wd, gf, *, final):
    rows, d = x2d.shape
    half = d // 2
    consts = (wo, g, wu, wd, gf)
    grid, blk = _row_tiles(rows)
    return pl.pallas_call(
        functools.partial(_out_mlp_body, final=final),
        grid=grid,
        in_specs=[blk(d), blk(half, 0), blk(half, a1.shape[1] // half - 1)] + [_resident(p) for p in consts],
        out_specs=blk(d),
        out_shape=jax.ShapeDtypeStruct((rows, d), F32),
        compiler_params=_params(1),
        name="out_mlp",
    )(x2d, a1, a2, *[_resident_array(p) for p in consts])


def _time_grid(total_rows, n_bg, tblock):
    steps = total_rows // (n_bg * SUBLANES)
    tb = min(tblock, steps)
    n_t = steps // tb
    blk = lambda width: pl.BlockSpec((tb * SUBLANES, width), lambda bg, t: (bg * n_t + t, 0))
    per_group = lambda rows, width: pl.BlockSpec((rows, width), lambda bg, t: (bg, 0))
    return tb, (n_bg, n_t), blk, per_group


def _gla_body(q_ref, k_ref, la_ref, v_ref, gt_ref, gh_ref, s0_ref, *rest, c, nh, dk, dv, transposed):
    o_ref, sfin_ref, st_ref = rest[-3:]
    tb = pl.program_id(1)
    rows = SUBLANES * c
    n_chunks = q_ref.shape[0] // rows
    dk_shift = int(math.log2(dk))
    heads_per_tile = LANES // dk

    @pl.when(tb == 0)
    def _():
        st_ref[...] = s0_ref[...]

    def iota(shape, axis):
        return lax.broadcasted_iota(jnp.int32, shape, axis)

    causal = ((iota((rows, rows), 0) & 7) == (iota((rows, rows), 1) & 7)) & (iota((rows, rows), 1) <= iota((rows, rows), 0))
    emask = (iota((rows, SUBLANES * dk), 0) & 7) == lax.shift_right_logical(iota((rows, SUBLANES * dk), 1), dk_shift)
    dmask = iota((SUBLANES, SUBLANES * dk), 0) == lax.shift_right_logical(iota((SUBLANES, SUBLANES * dk), 1), dk_shift)
    lo_rows = iota((rows, LANES), 1) < dk
    lo_8 = iota((SUBLANES, LANES), 1) < dk

    def expand(tile, h, lo=lo_rows, mask=emask):
        rolled = pltpu.roll(tile, dk, axis=1)
        d = jnp.where(lo, tile, rolled) if h % heads_per_tile == 0 else jnp.where(lo, rolled, tile)
        return jnp.where(mask, jnp.concatenate([d] * (SUBLANES * dk // LANES), axis=1), 0.0)

    def chunk(n, carry):
        r0 = n * rows
        q = _rows(q_ref, r0, rows)
        k = _rows(k_ref, r0, rows)
        la = _rows(la_ref, r0, rows)
        v = _rows(v_ref, r0, rows)
        gt = _rows(gt_ref, r0, rows)
        acc = jnp.zeros((SUBLANES, nh * dk), F32)
        pieces = []
        for t in range(c):
            acc = acc + la[t * SUBLANES:(t + 1) * SUBLANES, :]
            pieces.append(acc)
        cum = jnp.concatenate(pieces, axis=0)
        last = jnp.concatenate([acc] * c, axis=0)
        q_t = q * jnp.exp(cum) * (dk ** -0.5)
        k_t = k * jnp.exp(-cum)
        k_end = k * jnp.exp(last - cum)
        dec = jnp.exp(acc)
        pad = jnp.zeros((LANES - SUBLANES, LANES), F32)
        outs = []
        for h in range(nh):
            tile = (h // heads_per_tile) * LANES
            lane0 = (h % heads_per_tile) * dk
            sel = lo_rows if lane0 == 0 else jnp.logical_not(lo_rows)
            q_tile = q_t[:, tile:tile + LANES]
            scores = _dot_nt(jnp.where(sel, q_tile, 0.0).astype(BF16), k_t[:, tile:tile + LANES].astype(BF16))
            scores = jnp.where(causal, scores, 0.0).astype(BF16)
            v_h = v[:, h * dv:(h + 1) * dv]
            q_exp = expand(q_tile, h).astype(BF16)
            k_exp = expand(k_end[:, tile:tile + LANES], h).astype(BF16)
            if transposed:
                st = st_ref[h]
                o = _dot(scores, v_h) + _dot_nt(q_exp, st.astype(BF16))
                dec_row = jnp.sum(expand(dec[:, tile:tile + LANES], h, lo_8, dmask), axis=0, keepdims=True)
                st_ref[h] = st * dec_row + _dot_tn(v_h, k_exp)
            else:
                st = jnp.concatenate([st_ref[b, h] for b in range(SUBLANES)], axis=0)
                o = _dot(scores, v_h) + _dot(q_exp, st.astype(BF16))
                upd = _dot_tn(k_exp, v_h)
                dec_t = jnp.concatenate([dec[:, tile:tile + LANES], pad], axis=0).T
                for b in range(SUBLANES):
                    dec_b = jnp.broadcast_to(dec_t[lane0:lane0 + dk, b:b + 1], (dk, dv))
                    st_ref[b, h] = st[b * dk:(b + 1) * dk, :] * dec_b + upd[b * dk:(b + 1) * dk, :]
            o = _rmsnorm(o, gh_ref[:, h * dv:(h + 1) * dv])
            g_h = gt[:, h * dv:(h + 1) * dv]
            outs.append(o * (g_h * _sigmoid(g_h)))
        o_ref[pl.ds(pl.multiple_of(r0, rows), rows), :] = jnp.concatenate(outs, axis=1).astype(BF16)
        return carry

    lax.fori_loop(0, n_chunks, chunk, 0, unroll=min(n_chunks, 2))

    @pl.when(tb == pl.num_programs(1) - 1)
    def _():
        sfin_ref[...] = st_ref[...]


def _gla_state_in(s):
    b, h, dk, dv = s.shape
    return jnp.transpose(s.reshape(b // SUBLANES, SUBLANES, h, dk, dv), (0, 2, 4, 1, 3)).reshape(b // SUBLANES * h, dv, SUBLANES * dk)


def _gla_state_out(st, nh, dk):
    dv = st.shape[1]
    n_bg = st.shape[0] // nh
    return jnp.transpose(st.reshape(n_bg, nh, dv, SUBLANES, dk), (0, 3, 1, 4, 2)).reshape(n_bg * SUBLANES, nh, dk, dv)


def _gla(q, k, la, v, gt, gh, states, layer, n_layers, acc, nb, dims, *, c, tblock):
    nh, dk, dv = dims
    n_bg = nb // SUBLANES
    tb, grid, blk, _ = _time_grid(q.shape[0], n_bg, tblock)
    transposed = q.shape[0] // n_bg > SUBLANES * c
    body = functools.partial(_gla_body, c=c, nh=nh, dk=dk, dv=dv, transposed=transposed)
    row_specs = [blk(nh * dk), blk(nh * dk), blk(nh * dk), blk(nh * dv), blk(nh * dv), _const_spec(gh.shape)]
    og_shape = jax.ShapeDtypeStruct((q.shape[0], nh * dv), BF16)
    if transposed:
        st_block = (nh, dv, SUBLANES * dk)
        st_spec = pl.BlockSpec(st_block, lambda bg, t: (bg, 0, 0))
        s_in = jnp.zeros((n_bg * nh, dv, SUBLANES * dk), F32) if states is None else _gla_state_in(states[layer].astype(F32))
        og, s_fin = pl.pallas_call(
            body, grid=grid, in_specs=row_specs + [st_spec], out_specs=(blk(nh * dv), st_spec),
            out_shape=(og_shape, jax.ShapeDtypeStruct(s_in.shape, F32)), scratch_shapes=[pltpu.VMEM(st_block, F32)],
            compiler_params=_params(2), name="gla",
        )(q, k, la, v, gt, gh, s_in)
        s_new = _gla_state_out(s_fin, nh, dk)[None]
        return og, (jnp.pad(s_new, ((0, n_layers - 1),) + ((0, 0),) * 4) if acc is None else lax.dynamic_update_slice(acc, s_new, (layer, 0, 0, 0, 0)))
    st_block = (SUBLANES, nh, dk, dv)
    stacked_spec = pl.BlockSpec((None,) + st_block, lambda bg, t: (layer, bg, 0, 0, 0))
    if states is None:
        s_in, in_spec = jnp.zeros((nb, nh, dk, dv), F32), pl.BlockSpec(st_block, lambda bg, t: (bg, 0, 0, 0))
    else:
        s_in, in_spec = states.astype(F32), stacked_spec
    operands = [q, k, la, v, gt, gh, s_in]
    in_specs = row_specs + [in_spec]
    aliases = {}
    if acc is not None:
        aliases = {len(operands): 1}
        operands.append(acc)
        in_specs.append(pl.BlockSpec(memory_space=pl.ANY))
    return pl.pallas_call(
        body, grid=grid, in_specs=in_specs, out_specs=(blk(nh * dv), stacked_spec),
        out_shape=(og_shape, jax.ShapeDtypeStruct((n_layers, nb, nh, dk, dv), F32)), scratch_shapes=[pltpu.VMEM(st_block, F32)],
        input_output_aliases=aliases, compiler_params=_params(2), name="gla",
    )(*operands)


def _s5_disc_body(a_re_ref, a_im_ref, log_dt_ref, b_re_ref, b_im_ref, ab_re_ref, ab_im_ref, bb_re_ref, bb_im_ref):
    a_re = a_re_ref[...]
    a_im = a_im_ref[...]
    dt = jnp.exp(log_dt_ref[...])
    mag = jnp.exp(dt * a_re)
    ab_re = mag * jnp.cos(dt * a_im)
    ab_im = mag * jnp.sin(dt * a_im)
    den = a_re * a_re + a_im * a_im
    coef_re = ((ab_re - 1.0) * a_re + ab_im * a_im) / den
    coef_im = (ab_im * a_re - (ab_re - 1.0) * a_im) / den
    b_re = b_re_ref[...]
    b_im = b_im_ref[...]
    ab_re_ref[...] = ab_re
    ab_im_ref[...] = ab_im
    bb_re_ref[...] = coef_re * b_re - coef_im * b_im
    bb_im_ref[...] = coef_re * b_im + coef_im * b_re


def _s5_discretise(a_re, a_im, log_dt, b_re, b_im):
    g, p, ch = b_re.shape
    rep = lambda t: jnp.repeat(t, ch, axis=0)
    flat = lambda t: jnp.transpose(t, (0, 2, 1)).reshape(g * ch, p)
    args = (rep(a_re), rep(a_im), rep(jnp.broadcast_to(log_dt[:, None], (g, p))), flat(b_re), flat(b_im))
    spec = _const_spec((g * ch, p))
    shape = jax.ShapeDtypeStruct((g * ch, p), F32)
    ab_re, ab_im, bb_re, bb_im = pl.pallas_call(
        _s5_disc_body, grid=(1,), in_specs=[spec] * 5, out_specs=(spec,) * 4, out_shape=(shape,) * 4,
        compiler_params=_params(1), name="s5_discretise",
    )(*args)
    return ab_re[::ch], ab_im[::ch], bb_re.reshape(g, ch, p), bb_im.reshape(g, ch, p)


def _block_diag_in(w_gcp):
    g, ch, p = w_gcp.shape
    gpt = LANES // ch
    w = w_gcp.reshape(g // gpt, gpt, ch, p)
    eye = jnp.eye(gpt, dtype=w.dtype)
    return jnp.einsum("ab,jacp->jacbp", eye, w).reshape(g // gpt, gpt * ch, gpt * p)


def _block_diag_out(w_gcp):
    g, ch, p = w_gcp.shape
    gpt = LANES // ch
    w = w_gcp.reshape(g // gpt, gpt, ch, p)
    eye = jnp.eye(gpt, dtype=w.dtype)
    return jnp.einsum("ab,jacp->japbc", eye, w).reshape(g // gpt, gpt * p, gpt * ch)


def _s5_body(u_ref, ab_re_ref, ab_im_ref, wb_re_ref, wb_im_ref, wc_ref, d_ref, wglu_ref, bglu_ref, h0_re_ref, h0_im_ref,
             z_ref, hf_re_ref, hf_im_ref, h_re_ref, h_im_ref, bu_re_ref, bu_im_ref, y_ref):
    tb = pl.program_id(1)
    rows = u_ref.shape[0]
    steps = rows // SUBLANES
    n_tiles = wb_re_ref.shape[0]
    sw = wb_re_ref.shape[2]

    @pl.when(tb == 0)
    def _():
        h_re_ref[...] = h0_re_ref[...]
        h_im_ref[...] = h0_im_ref[...]

    for j in range(n_tiles):
        u_j = u_ref[:, j * LANES:(j + 1) * LANES]
        u_b = u_j.astype(BF16)
        bu_re_ref[j] = _dot(u_b, wb_re_ref[j])
        bu_im_ref[j] = _dot(u_b, wb_im_ref[j])
        a_re = jnp.broadcast_to(ab_re_ref[:, j * sw:(j + 1) * sw], (SUBLANES, sw))
        a_im = jnp.broadcast_to(ab_im_ref[:, j * sw:(j + 1) * sw], (SUBLANES, sw))
        h_re = h_re_ref[:, j * sw:(j + 1) * sw]
        h_im = h_im_ref[:, j * sw:(j + 1) * sw]
        for t in range(steps):
            sl = slice(t * SUBLANES, (t + 1) * SUBLANES)
            h_re, h_im = (a_re * h_re - a_im * h_im + bu_re_ref[j, sl, :], a_re * h_im + a_im * h_re + bu_im_ref[j, sl, :])
            bu_re_ref[j, sl, :] = h_re
            bu_im_ref[j, sl, :] = h_im
        h_re_ref[:, j * sw:(j + 1) * sw] = h_re
        h_im_ref[:, j * sw:(j + 1) * sw] = h_im
        h_cat = jnp.concatenate([bu_re_ref[j].astype(BF16), bu_im_ref[j].astype(BF16)], axis=1)
        y_ref[:, j * LANES:(j + 1) * LANES] = _dot(h_cat, wc_ref[j]) + d_ref[:, j * LANES:(j + 1) * LANES] * u_j

    z = _gelu(y_ref[...])
    gate = _sigmoid(_dot(z.astype(BF16), wglu_ref[...]) + bglu_ref[...])
    z_ref[...] = (z * gate).astype(BF16)

    @pl.when(tb == pl.num_programs(1) - 1)
    def _():
        hf_re_ref[...] = h_re_ref[...]
        hf_im_ref[...] = h_im_ref[...]


def _s5(u, ab_re, ab_im, wb_re, wb_im, wc, d_skip, wglu, bglu, h0_re, h0_im, *, tblock):
    total, width = u.shape
    n_state = ab_re.shape[1]
    n_bg = h0_re.shape[0] // SUBLANES
    sw = wb_re.shape[2]
    tb, grid, blk, per_group = _time_grid(total, n_bg, tblock)
    rows = tb * SUBLANES
    st_spec = per_group(SUBLANES, n_state)
    consts = (ab_re, ab_im, wb_re, wb_im, wc, d_skip, wglu, bglu)
    st_shape = jax.ShapeDtypeStruct((n_bg * SUBLANES, n_state), F32)
    return pl.pallas_call(
        _s5_body,
        grid=grid,
        in_specs=[blk(width)] + [_resident(p) for p in consts] + [st_spec, st_spec],
        out_specs=(blk(width), st_spec, st_spec),
        out_shape=(jax.ShapeDtypeStruct((total, width), BF16), st_shape, st_shape),
        scratch_shapes=[pltpu.VMEM((SUBLANES, n_state), F32), pltpu.VMEM((SUBLANES, n_state), F32),
                        pltpu.VMEM((wb_re.shape[0], rows, sw), F32), pltpu.VMEM((wb_re.shape[0], rows, sw), F32),
                        pltpu.VMEM((rows, width), F32)],
        compiler_params=_params(2),
        name="s5",
    )(u, *[_resident_array(p) for p in consts], h0_re, h0_im)


def _odd_mixer_body(x_ref, g_ref, w_ref, conv0_ref, h0_ref, cw_ref, cb_ref, wr_ref, br_ref, wi_ref, bi_ref, lam_ref,
                    o_ref, hf_ref, convf_ref, gate_a, gate_b, xin_a, xin_b, tail_ref, a_ref, b_ref, h_ref, hn_ref):
    s = pl.program_id(1)
    last = pl.num_programs(1) - 1
    rows, width = o_ref.shape
    steps = rows // SUBLANES
    taps = cw_ref.shape[0]
    tail = (taps - 1) * SUBLANES
    n_blocks = wr_ref.shape[0]
    bw = wr_ref.shape[1]

    @pl.when(s == 0)
    def _():
        tail_ref[...] = conv0_ref[...]
        h_ref[...] = h0_ref[...]
        gate_b[...] = jnp.zeros(gate_b.shape, F32)
        xin_b[...] = jnp.zeros(xin_b.shape, F32)

    def step(gate_in, xin_in, gate_rec, xin_rec):
        xn = _rmsnorm(x_ref[...], g_ref[...]).astype(BF16)
        xin_rec[0:tail, :] = tail_ref[...]
        pw = 2 * width // n_blocks
        for n in range(n_blocks):
            c0 = n * pw
            proj = _dot(xn, w_ref[:, c0:c0 + pw])
            if c0 < width:
                gate_in[:, c0:c0 + pw] = proj
            else:
                xin_in[tail:tail + rows, c0 - width:c0 - width + pw] = proj
            cols = slice(n * bw, (n + 1) * bw)
            xc = cb_ref[:, cols] + xin_rec[0:rows, cols] * cw_ref[0:1, cols]
            for j in range(1, taps):
                xc = xc + xin_rec[j * SUBLANES:j * SUBLANES + rows, cols] * cw_ref[j:j + 1, cols]
            xb = xc.astype(BF16)
            i = _sigmoid(_dot(xb, wi_ref[n]) + bi_ref[:, cols])
            half_rate = (-0.5 * LRU_C) * _softplus(-lam_ref[:, cols])
            log_a = jnp.tanh(0.5 * (_dot(xb, wr_ref[n]) + br_ref[:, cols])) * half_rate + half_rate
            a_ref[:, cols] = jnp.exp(log_a)
            th = jnp.tanh(log_a)
            y = -2.0 * th / (1.0 - th)
            root = jnp.where(y > 0.0, y * lax.rsqrt(y), 0.0)
            b_ref[:, cols] = root * (i * xc)
            h = h_ref[:, cols]
            for t in range(steps):
                sl = slice(t * SUBLANES, (t + 1) * SUBLANES)
                h = a_ref[sl, cols] * h + b_ref[sl, cols]
                b_ref[sl, cols] = h
            hn_ref[:, cols] = h
            o_ref[:, cols] = (_gelu(gate_rec[:, cols]) * b_ref[:, cols]).astype(BF16)

    def run(gate_in, xin_in, gate_rec, xin_rec):
        step(gate_in, xin_in, gate_rec, xin_rec)

        @pl.when(s > 0)
        def _():
            h_ref[...] = hn_ref[...]
            tail_ref[...] = xin_rec[rows:rows + tail, :]

        @pl.when(s == last)
        def _():
            hf_ref[...] = hn_ref[...]
            convf_ref[...] = xin_rec[rows:rows + tail, :]

    pl.when(lax.rem(s, 2) == 0)(lambda: run(gate_a, xin_a, gate_b, xin_b))
    pl.when(lax.rem(s, 2) == 1)(lambda: run(gate_b, xin_b, gate_a, xin_a))


def _odd_mixer(x2d, conv0, h0, g, w, cw, cb, wr, br, wi, bi, lam, *, tblock):
    total, d = x2d.shape
    width = _resident_shape(w)[1] // 2
    n_bg = h0.shape[0] // SUBLANES
    tail = (cw.shape[0] - 1) * SUBLANES
    tb, (_, n_t), _, per_group = _time_grid(total, n_bg, tblock)
    rows = tb * SUBLANES
    x_spec = pl.BlockSpec((rows, d), lambda bg, t: (bg * n_t + jnp.minimum(t, n_t - 1), 0))
    o_spec = pl.BlockSpec((rows, width), lambda bg, t: (bg * n_t + jnp.maximum(t - 1, 0), 0))
    h_spec = per_group(SUBLANES, width)
    c_spec = per_group(tail, width)
    consts = (g, w, cw, cb, wr, br, wi, bi, lam)
    return pl.pallas_call(
        _odd_mixer_body,
        grid=(n_bg, n_t + 1),
        in_specs=[x_spec, _resident(g), _resident(w), c_spec, h_spec] + [_resident(p) for p in consts[2:]],
        out_specs=(o_spec, h_spec, c_spec),
        out_shape=(jax.ShapeDtypeStruct((total, width), BF16), jax.ShapeDtypeStruct((n_bg * SUBLANES, width), F32),
                   jax.ShapeDtypeStruct((n_bg * tail, width), F32)),
        scratch_shapes=[pltpu.VMEM((rows, width), F32), pltpu.VMEM((rows, width), F32),
                        pltpu.VMEM((rows + tail, width), F32), pltpu.VMEM((rows + tail, width), F32),
                        pltpu.VMEM((tail, width), F32), pltpu.VMEM((rows, width), F32), pltpu.VMEM((rows, width), F32),
                        pltpu.VMEM((SUBLANES, width), F32), pltpu.VMEM((SUBLANES, width), F32)],
        compiler_params=_params(2),
        name="odd_mixer",
    )(x2d, _resident_array(g), _resident_array(w), conv0, h0, *[_resident_array(p) for p in consts[2:]])


def _row(t):
    return t.reshape(1, -1).astype(F32)


def _to_rows(x):
    b, s, d = x.shape
    return jnp.transpose(x.reshape(b // SUBLANES, SUBLANES, s, d), (0, 2, 1, 3)).reshape(b * s, d)


def _from_rows(y, b, s):
    d = y.shape[-1]
    return jnp.transpose(y.reshape(b // SUBLANES, s, SUBLANES, d), (0, 2, 1, 3)).reshape(b, s, d)


def _even_layer(x2d, nb, gla_states, gla_acc, e, n_even, s_re, s_im, prm, mlp, *, final):
    (g_norm, w_in, w_u, w_a, wa2, ba2, g_head, ab_re, ab_im, wb_re, wb_im, wc, d_skip, wglu, bglu, w_out, dims) = prm
    nh, dk, dv = dims
    steps = x2d.shape[0] // nb
    q, k, v, gt, u, la = _inproj_even(x2d, g_norm, w_in, w_u, w_a, wa2, ba2, kw=nh * dk, vw=nh * dv)
    og, gla_acc = _gla(q, k, la, v, gt, g_head, gla_states, e, n_even, gla_acc, nb, dims, c=math.gcd(steps, GLA_CHUNK), tblock=128)
    n_state = ab_re.shape[1]
    h0_re = jnp.zeros((nb, n_state), F32) if s_re is None else s_re.astype(F32).reshape(nb, n_state)
    h0_im = jnp.zeros((nb, n_state), F32) if s_im is None else s_im.astype(F32).reshape(nb, n_state)
    zz, hf_re, hf_im = _s5(u, ab_re, ab_im, wb_re, wb_im, wc, d_skip, wglu, bglu, h0_re, h0_im, tblock=64)
    g_mlp, w_up, w_down, g_fin = mlp
    y = _out_mlp(x2d, og, zz, w_out, g_mlp, w_up, w_down, g_fin, final=final)
    return y, gla_acc, hf_re, hf_im


def _odd_layer(x2d, nb, s_lru, s_conv, prm, mlp, *, final):
    (g_norm, w_in, cw, cb, wr, br, wi, bi, lam, w_out) = prm
    n_bg = nb // SUBLANES
    taps = cw.shape[0]
    width = _resident_shape(w_in)[1] // 2
    h0 = jnp.zeros((nb, width), F32) if s_lru is None else s_lru.astype(F32)
    if s_conv is None:
        conv0 = jnp.zeros((n_bg * (taps - 1) * SUBLANES, width), F32)
    else:
        conv0 = jnp.transpose(s_conv.astype(F32).reshape(n_bg, SUBLANES, taps - 1, width), (0, 2, 1, 3)).reshape(-1, width)
    gh, hf, convf = _odd_mixer(x2d, conv0, h0, g_norm, w_in, cw, cb, wr, br, wi, bi, lam, tblock=64)
    g_mlp, w_up, w_down, g_fin = mlp
    y = _out_mlp(x2d, gh, gh, w_out, g_mlp, w_up, w_down, g_fin, final=final)
    conv_new = jnp.transpose(convf.reshape(n_bg, taps - 1, SUBLANES, width), (0, 2, 1, 3)).reshape(nb, taps - 1, width)
    return y, hf, conv_new


def kernel(x_prompt, x_sample, state_gla, state_s5_re, state_s5_im, state_lru, state_conv, norm_mix_even, w_in_even, gla_w_a2, gla_b_a2, gla_norm, s5_a_re, s5_a_im, s5_log_dt, s5_b_re, s5_b_im, s5_c_re, s5_c_im, s5_d, s5_w_glu, s5_b_glu, w_out_even, norm_mix_odd, w_in_odd, conv_w, conv_b, lru_w_r, lru_b_r, lru_w_i, lru_b_i, lru_lam, w_out_odd, norm_mlp, w_up, w_down, norm_final):
    depth = norm_mlp.shape[0]
    nh, dk, dv = state_gla.shape[2:]
    rank = gla_w_a2.shape[1]
    n_groups, n_p, ch = s5_b_re.shape[1:]
    sw = n_groups * ch
    kw, vw = nh * dk, nh * dv
    assert rank <= LANES and LANES % ch == 0 and LANES % dk == 0 and dv == LANES
    bp, sp, _ = x_prompt.shape
    bs, ss, _ = x_sample.shape
    n_even = state_gla.shape[0]

    yp = _to_rows(x_prompt.astype(F32))
    ys = _to_rows(x_sample.astype(F32))
    g_fin = _row(norm_final)
    o4 = 2 * kw + 2 * vw
    o5 = o4 + rank
    w_in_main = w_in_even[:, :, :o4].astype(BF16)
    w_in_u = w_in_even[:, :, o5:].astype(BF16)
    w_in_a = jnp.pad(w_in_even[:, :, o4:o5], ((0, 0), (0, 0), (0, LANES - rank))).astype(BF16)
    w_a2 = jnp.pad(gla_w_a2, ((0, 0), (0, LANES - rank), (0, 0))).astype(BF16)
    w_up_b, w_down_b = w_up.astype(BF16), w_down.astype(BF16)
    w_out_even_b, w_out_odd_b, w_in_odd_b = w_out_even.astype(BF16), w_out_odd.astype(BF16), w_in_odd.astype(BF16)
    w_glu_b, w_r_b, w_i_b = s5_w_glu.astype(BF16), lru_w_r.astype(BF16), lru_w_i.astype(BF16)
    keys = ("re", "im", "lru", "conv")
    outs_p = {key: [] for key in keys}
    outs_s = {key: [] for key in keys}
    gla_p = gla_s = None

    def record(outs, sr, si):
        outs["re"].append(sr.reshape(-1, n_groups, n_p))
        outs["im"].append(si.reshape(-1, n_groups, n_p))

    for layer in range(depth):
        final = layer == depth - 1
        mlp = (_row(norm_mlp[layer]), (w_up_b, layer), (w_down_b, layer), g_fin)
        if layer % 2 == 0:
            e = layer // 2
            ab_re, ab_im, bb_re, bb_im = _s5_discretise(s5_a_re[e].astype(F32), s5_a_im[e].astype(F32), s5_log_dt[e].astype(F32),
                                                        s5_b_re[e].astype(F32), s5_b_im[e].astype(F32))
            wc = jnp.concatenate([_block_diag_out(s5_c_re[e].astype(F32)), -_block_diag_out(s5_c_im[e].astype(F32))], axis=1)
            prm = (_row(norm_mix_even[e]), (w_in_main, e), (w_in_u, e), (w_in_a, e), (w_a2, e), _row(gla_b_a2[e]), _row(gla_norm[e]),
                   ab_re.reshape(1, -1), ab_im.reshape(1, -1), _block_diag_in(bb_re).astype(BF16), _block_diag_in(bb_im).astype(BF16),
                   wc.astype(BF16), _row(s5_d[e]), (w_glu_b, e), _row(s5_b_glu[e]), (w_out_even_b, e), (nh, dk, dv))
            yp, gla_p, sr, si = _even_layer(yp, bp, None, gla_p, e, n_even, None, None, prm, mlp, final=final)
            record(outs_p, sr, si)
            ys, gla_s, sr, si = _even_layer(ys, bs, state_gla, gla_s, e, n_even, state_s5_re[e], state_s5_im[e], prm, mlp, final=final)
            record(outs_s, sr, si)
        else:
            o = layer // 2
            prm = (_row(norm_mix_odd[o]), (w_in_odd_b, o), conv_w[o].astype(F32), _row(conv_b[o]),
                   (w_r_b, o), _row(lru_b_r[o]), (w_i_b, o), _row(lru_b_i[o]), _row(lru_lam[o]), (w_out_odd_b, o))
            yp, sl, sc = _odd_layer(yp, bp, None, None, prm, mlp, final=final)
            outs_p["lru"].append(sl)
            outs_p["conv"].append(sc)
            ys, sl, sc = _odd_layer(ys, bs, state_lru[o], state_conv[o], prm, mlp, final=final)
            outs_s["lru"].append(sl)
            outs_s["conv"].append(sc)

    stack = lambda outs: tuple(jnp.stack(outs[key]) for key in keys)
    return (_from_rows(yp, bp, sp), _from_rows(ys, bs, ss), gla_p) + stack(outs_p) + (gla_s,) + stack(outs_s)
```

```python
import functools
import math

import jax
import jax.numpy as jnp
from jax import lax
from jax.experimental import pallas as pl
from jax.experimental.pallas import tpu as pltpu

F32 = jnp.float32
BF16 = jnp.bfloat16

NORM_EPS = 1e-6
GLA_TAU = 16.0
GLA_CHUNK = 16
LRU_C = 8.0

SUBLANES = 8
LANES = 128
ROW_TILE = 512
FF_CHUNK = 1024
VMEM_LIMIT = 48 * 1024 * 1024


def _params(n_axes):
    return pltpu.CompilerParams(dimension_semantics=("arbitrary",) * n_axes, vmem_limit_bytes=VMEM_LIMIT)


def _const_spec(shape):
    zeros = (0,) * len(shape)
    return pl.BlockSpec(shape, lambda *_: zeros, pipeline_mode=pl.Buffered(1))


def _resident(p):
    if not isinstance(p, tuple):
        return _const_spec(p.shape)
    stacked, layer = p
    index = (layer,) + (0,) * (stacked.ndim - 1)
    return pl.BlockSpec((None,) + stacked.shape[1:], lambda *_: index, pipeline_mode=pl.Buffered(1))


def _resident_array(p):
    return p[0] if isinstance(p, tuple) else p


def _resident_shape(p):
    return p[0].shape[1:] if isinstance(p, tuple) else p.shape


def _dot(a, b):
    return jnp.dot(a, b, preferred_element_type=F32)


def _dot_nt(a, b):
    return lax.dot_general(a, b, (((1,), (1,)), ((), ())), preferred_element_type=F32)


def _dot_tn(a, b):
    return lax.dot_general(a, b, (((0,), (0,)), ((), ())), preferred_element_type=F32)


def _rmsnorm(x, g):
    return x * lax.rsqrt(jnp.mean(x * x, axis=-1, keepdims=True) + NORM_EPS) * g


def _log_sigmoid(z):
    return jnp.minimum(z, 0.0) - jnp.log1p(jnp.exp(-jnp.abs(z)))


def _softplus(z):
    return jnp.maximum(z, 0.0) + jnp.log1p(jnp.exp(-jnp.abs(z)))


_GELU_C1 = math.sqrt(2.0 / math.pi)
_GELU_C2 = _GELU_C1 * 0.044715


def _gelu(x):
    half = 0.5 * x
    return half + half * jnp.tanh(x * (_GELU_C1 + _GELU_C2 * (x * x)))


def _sigmoid(x):
    return 0.5 * jnp.tanh(0.5 * x) + 0.5


def _rows(ref, start, size):
    return ref[pl.ds(pl.multiple_of(start, size), size), :]


def _interleave_rows(x_ref, slab_ref):
    nb, ts, d = x_ref.shape
    for b in range(nb):
        for l in range(d // LANES):
            slab_ref[l, pl.ds(b, ts, stride=nb), :] = x_ref[b, :, l * LANES:(l + 1) * LANES]
    return jnp.concatenate([slab_ref[l] for l in range(d // LANES)], axis=1)


def _deinterleave_rows(y, slab_ref, o_ref):
    nb, ts, d = o_ref.shape
    for l in range(d // LANES):
        slab_ref[l] = y[:, l * LANES:(l + 1) * LANES]
    for b in range(nb):
        for l in range(d // LANES):
            o_ref[b, :, l * LANES:(l + 1) * LANES] = slab_ref[l, pl.ds(b, ts, stride=nb), :]


def _row_tiles(rows):
    tm = min(ROW_TILE, rows)
    return (rows // tm,), lambda width, col=0: pl.BlockSpec((tm, width), lambda i: (i, col))


def _inproj_even_body(x_ref, g_ref, w_ref, wu_ref, wa_ref, wa2_ref, ba2_ref, q_ref, k_ref, v_ref, gt_ref, u_ref, la_ref, *rest,
                      kw, vw):
    if rest:
        xt_ref, slab_ref = rest
        x = _interleave_rows(x_ref, slab_ref)
        xt_ref[...] = x
    else:
        x = x_ref[...]
    xn = _rmsnorm(x, g_ref[...]).astype(BF16)
    o1, o2 = kw, 2 * kw
    o3 = o2 + vw
    o4 = o3 + vw
    q_ref[...] = _dot(xn, w_ref[:, 0:o1])
    k_ref[...] = _dot(xn, w_ref[:, o1:o2])
    v_ref[...] = _dot(xn, w_ref[:, o2:o3]).astype(BF16)
    gt_ref[...] = _dot(xn, w_ref[:, o3:o4])
    u_ref[...] = _dot(xn, wu_ref[...])
    a_lr = _dot(xn, wa_ref[...]).astype(BF16)
    z = _dot(a_lr, wa2_ref[...]) + ba2_ref[...]
    la_ref[...] = _log_sigmoid(z) * (1.0 / GLA_TAU)


def _inproj_even(x, g, w, wu, wa, wa2, ba2, *, kw, vw):
    batch_major = x.ndim == 3
    d = x.shape[-1]
    rows = x.size // d
    sw = _resident_shape(wu)[1]
    widths = [kw, kw, vw, vw, sw, kw]
    dtypes = [F32, F32, BF16, F32, F32, F32]
    consts = (g, w, wu, wa, wa2, ba2)
    grid, blk = _row_tiles(rows)
    x_spec, scratch = blk(d), []
    if batch_major:
        nb = x.shape[0]
        x_spec = pl.BlockSpec((nb, ROW_TILE // nb, d), lambda i: (0, i, 0))
        widths.append(d)
        dtypes.append(F32)
        scratch = [pltpu.VMEM((d // LANES, ROW_TILE, LANES), F32)]
    return pl.pallas_call(
        functools.partial(_inproj_even_body, kw=kw, vw=vw),
        grid=grid,
        in_specs=[x_spec] + [_resident(p) for p in consts],
        out_specs=tuple(blk(wd) for wd in widths),
        out_shape=tuple(jax.ShapeDtypeStruct((rows, wd), dt) for wd, dt in zip(widths, dtypes)),
        scratch_shapes=scratch,
        compiler_params=_params(1),
        name="inproj_even",
    )(x, *[_resident_array(p) for p in consts])


def _inproj_odd_body(x_ref, g_ref, w_ref, gate_ref, xin_ref, *, width):
    xn = _rmsnorm(x_ref[...], g_ref[...]).astype(BF16)
    gate_ref[...] = _dot(xn, w_ref[:, 0:width])
    xin_ref[...] = _dot(xn, w_ref[:, width:2 * width])


def _inproj_odd(x2d, g, w):
    rows, d = x2d.shape
    width = _resident_shape(w)[1] // 2
    grid, blk = _row_tiles(rows)
    shape = jax.ShapeDtypeStruct((rows, width), F32)
    return pl.pallas_call(
        functools.partial(_inproj_odd_body, width=width),
        grid=grid,
        in_specs=[blk(d), _resident(g), _resident(w)],
        out_specs=(blk(width), blk(width)),
        out_shape=(shape, shape),
        compiler_params=_params(1),
        name="inproj_odd",
    )(x2d, _resident_array(g), _resident_array(w))


def _out_mlp_body(x_ref, a1_ref, a2_ref, wo_ref, g_ref, wu_ref, wd_ref, gf_ref, o_ref, *slab, final):
    half = a1_ref.shape[1]
    mix = _dot(a1_ref[...], wo_ref[0:half, :]) + _dot(a2_ref[...], wo_ref[half:2 * half, :])
    x1 = x_ref[...] + mix
    xn = _rmsnorm(x1, g_ref[...]).astype(BF16)
    acc = x1
    for c in range(wu_ref.shape[1] // FF_CHUNK):
        h = _dot(xn, wu_ref[:, c * FF_CHUNK:(c + 1) * FF_CHUNK])
        h = jnp.square(jnp.maximum(h, 0.0)).astype(BF16)
        acc = acc + _dot(h, wd_ref[c * FF_CHUNK:(c + 1) * FF_CHUNK, :])
    if final:
        acc = _rmsnorm(acc, gf_ref[...])
    if slab:
        _deinterleave_rows(acc, slab[0], o_ref)
    else:
        o_ref[...] = acc


def _out_mlp(x2d, a1, a2, wo, g, wu, wd, gf, *, final, batch_major_out=None):
    rows, d = x2d.shape
    half = d // 2
    consts = (wo, g, wu, wd, gf)
    grid, blk = _row_tiles(rows)
    out_spec, out_shape, scratch = blk(d), jax.ShapeDtypeStruct((rows, d), F32), []
    if batch_major_out is not None:
        nb, s = batch_major_out
        out_spec = pl.BlockSpec((nb, ROW_TILE // nb, d), lambda i: (0, i, 0))
        out_shape = jax.ShapeDtypeStruct((nb, s, d), F32)
        scratch = [pltpu.VMEM((d // LANES, ROW_TILE, LANES), F32)]
    return pl.pallas_call(
        functools.partial(_out_mlp_body, final=final),
        grid=grid,
        in_specs=[blk(d), blk(half, 0), blk(half, a1.shape[1] // half - 1)] + [_resident(p) for p in consts],
        out_specs=out_spec,
        out_shape=out_shape,
        scratch_shapes=scratch,
        compiler_params=_params(1),
        name="out_mlp",
    )(x2d, a1, a2, *[_resident_array(p) for p in consts])


def _time_grid(total_rows, n_bg, tblock):
    steps = total_rows // (n_bg * SUBLANES)
    tb = min(tblock, steps)
    n_t = steps // tb
    blk = lambda width: pl.BlockSpec((tb * SUBLANES, width), lambda bg, t: (bg * n_t + t, 0))
    per_group = lambda rows, width: pl.BlockSpec((rows, width), lambda bg, t: (bg, 0))
    return tb, (n_bg, n_t), blk, per_group


def _gla_body(q_ref, k_ref, la_ref, v_ref, gt_ref, gh_ref, s0_ref, *rest, c, nh, dk, dv, transposed):
    o_ref, sfin_ref, st_ref = rest[-3:]
    tb = pl.program_id(1)
    rows = SUBLANES * c
    n_chunks = q_ref.shape[0] // rows
    dk_shift = int(math.log2(dk))
    heads_per_tile = LANES // dk

    @pl.when(tb == 0)
    def _():
        st_ref[...] = s0_ref[...]

    def iota(shape, axis):
        return lax.broadcasted_iota(jnp.int32, shape, axis)

    causal = ((iota((rows, rows), 0) & 7) == (iota((rows, rows), 1) & 7)) & (iota((rows, rows), 1) <= iota((rows, rows), 0))
    emask = (iota((rows, SUBLANES * dk), 0) & 7) == lax.shift_right_logical(iota((rows, SUBLANES * dk), 1), dk_shift)
    dmask = iota((SUBLANES, SUBLANES * dk), 0) == lax.shift_right_logical(iota((SUBLANES, SUBLANES * dk), 1), dk_shift)
    lo_rows = iota((rows, LANES), 1) < dk
    lo_8 = iota((SUBLANES, LANES), 1) < dk

    def expand(tile, h, lo=lo_rows, mask=emask):
        rolled = pltpu.roll(tile, dk, axis=1)
        d = jnp.where(lo, tile, rolled) if h % heads_per_tile == 0 else jnp.where(lo, rolled, tile)
        return jnp.where(mask, jnp.concatenate([d] * (SUBLANES * dk // LANES), axis=1), 0.0)

    def chunk(n, carry):
        r0 = n * rows
        q = _rows(q_ref, r0, rows)
        k = _rows(k_ref, r0, rows)
        la = _rows(la_ref, r0, rows)
        v = _rows(v_ref, r0, rows)
        gt = _rows(gt_ref, r0, rows)
        acc = jnp.zeros((SUBLANES, nh * dk), F32)
        pieces = []
        for t in range(c):
            acc = acc + la[t * SUBLANES:(t + 1) * SUBLANES, :]
            pieces.append(acc)
        cum = jnp.concatenate(pieces, axis=0)
        last = jnp.concatenate([acc] * c, axis=0)
        q_t = q * jnp.exp(cum) * (dk ** -0.5)
        k_t = k * jnp.exp(-cum)
        k_end = k * jnp.exp(last - cum)
        dec = jnp.exp(acc)
        pad = jnp.zeros((LANES - SUBLANES, LANES), F32)
        outs = []
        for h in range(nh):
            tile = (h // heads_per_tile) * LANES
            lane0 = (h % heads_per_tile) * dk
            sel = lo_rows if lane0 == 0 else jnp.logical_not(lo_rows)
            q_tile = q_t[:, tile:tile + LANES]
            scores = _dot_nt(jnp.where(sel, q_tile, 0.0).astype(BF16), k_t[:, tile:tile + LANES].astype(BF16))
            scores = jnp.where(causal, scores, 0.0).astype(BF16)
            v_h = v[:, h * dv:(h + 1) * dv]
            q_exp = expand(q_tile, h).astype(BF16)
            k_exp = expand(k_end[:, tile:tile + LANES], h).astype(BF16)
            if transposed:
                st = st_ref[h]
                o = _dot(scores, v_h) + _dot_nt(q_exp, st.astype(BF16))
                dec_row = jnp.sum(expand(dec[:, tile:tile + LANES], h, lo_8, dmask), axis=0, keepdims=True)
                st_ref[h] = st * dec_row + _dot_tn(v_h, k_exp)
            else:
                st = jnp.concatenate([st_ref[b, h] for b in range(SUBLANES)], axis=0)
                o = _dot(scores, v_h) + _dot(q_exp, st.astype(BF16))
                upd = _dot_tn(k_exp, v_h)
                dec_t = jnp.concatenate([dec[:, tile:tile + LANES], pad], axis=0).T
                for b in range(SUBLANES):
                    dec_b = jnp.broadcast_to(dec_t[lane0:lane0 + dk, b:b + 1], (dk, dv))
                    st_ref[b, h] = st[b * dk:(b + 1) * dk, :] * dec_b + upd[b * dk:(b + 1) * dk, :]
            o = _rmsnorm(o, gh_ref[:, h * dv:(h + 1) * dv])
            g_h = gt[:, h * dv:(h + 1) * dv]
            outs.append(o * (g_h * _sigmoid(g_h)))
        o_ref[pl.ds(pl.multiple_of(r0, rows), rows), :] = jnp.concatenate(outs, axis=1).astype(BF16)
        return carry

    lax.fori_loop(0, n_chunks, chunk, 0, unroll=min(n_chunks, 2))

    @pl.when(tb == pl.num_programs(1) - 1)
    def _():
        sfin_ref[...] = st_ref[...]


def _gla_state_in(s):
    b, h, dk, dv = s.shape
    return jnp.transpose(s.reshape(b // SUBLANES, SUBLANES, h, dk, dv), (0, 2, 4, 1, 3)).reshape(b // SUBLANES * h, dv, SUBLANES * dk)


def _gla_state_out(st, nh, dk):
    dv = st.shape[1]
    n_bg = st.shape[0] // nh
    return jnp.transpose(st.reshape(n_bg, nh, dv, SUBLANES, dk), (0, 3, 1, 4, 2)).reshape(n_bg * SUBLANES, nh, dk, dv)


def _gla(q, k, la, v, gt, gh, states, layer, n_layers, acc, nb, dims, *, c, tblock):
    nh, dk, dv = dims
    n_bg = nb // SUBLANES
    tb, grid, blk, _ = _time_grid(q.shape[0], n_bg, tblock)
    transposed = q.shape[0] // n_bg > SUBLANES * c
    body = functools.partial(_gla_body, c=c, nh=nh, dk=dk, dv=dv, transposed=transposed)
    row_specs = [blk(nh * dk), blk(nh * dk), blk(nh * dk), blk(nh * dv), blk(nh * dv), _const_spec(gh.shape)]
    og_shape = jax.ShapeDtypeStruct((q.shape[0], nh * dv), BF16)
    if transposed:
        st_block = (nh, dv, SUBLANES * dk)
        st_spec = pl.BlockSpec(st_block, lambda bg, t: (bg, 0, 0))
        s_in = jnp.zeros((n_bg * nh, dv, SUBLANES * dk), F32) if states is None else _gla_state_in(states[layer].astype(F32))
        og, s_fin = pl.pallas_call(
            body, grid=grid, in_specs=row_specs + [st_spec], out_specs=(blk(nh * dv), st_spec),
            out_shape=(og_shape, jax.ShapeDtypeStruct(s_in.shape, F32)), scratch_shapes=[pltpu.VMEM(st_block, F32)],
            compiler_params=_params(2), name="gla",
        )(q, k, la, v, gt, gh, s_in)
        s_new = _gla_state_out(s_fin, nh, dk)[None]
        return og, (jnp.pad(s_new, ((0, n_layers - 1),) + ((0, 0),) * 4) if acc is None else lax.dynamic_update_slice(acc, s_new, (layer, 0, 0, 0, 0)))
    st_block = (SUBLANES, nh, dk, dv)
    stacked_spec = pl.BlockSpec((None,) + st_block, lambda bg, t: (layer, bg, 0, 0, 0))
    if states is None:
        s_in, in_spec = jnp.zeros((nb, nh, dk, dv), F32), pl.BlockSpec(st_block, lambda bg, t: (bg, 0, 0, 0))
    else:
        s_in, in_spec = states.astype(F32), stacked_spec
    if acc is None:
        acc = jnp.zeros((n_layers, nb, nh, dk, dv), F32)
    return pl.pallas_call(
        body, grid=grid, in_specs=row_specs + [in_spec, pl.BlockSpec(memory_space=pl.ANY)], out_specs=(blk(nh * dv), stacked_spec),
        out_shape=(og_shape, jax.ShapeDtypeStruct(acc.shape, F32)), scratch_shapes=[pltpu.VMEM(st_block, F32)],
        input_output_aliases={7: 1}, compiler_params=_params(2), name="gla",
    )(q, k, la, v, gt, gh, s_in, acc)


def _s5_disc_body(a_re_ref, a_im_ref, log_dt_ref, b_re_ref, b_im_ref, ab_re_ref, ab_im_ref, bb_re_ref, bb_im_ref):
    a_re = a_re_ref[...]
    a_im = a_im_ref[...]
    dt = jnp.exp(log_dt_ref[...])
    mag = jnp.exp(dt * a_re)
    ab_re = mag * jnp.cos(dt * a_im)
    ab_im = mag * jnp.sin(dt * a_im)
    den = a_re * a_re + a_im * a_im
    coef_re = ((ab_re - 1.0) * a_re + ab_im * a_im) / den
    coef_im = (ab_im * a_re - (ab_re - 1.0) * a_im) / den
    b_re = b_re_ref[...]
    b_im = b_im_ref[...]
    ab_re_ref[...] = ab_re
    ab_im_ref[...] = ab_im
    bb_re_ref[...] = coef_re * b_re - coef_im * b_im
    bb_im_ref[...] = coef_re * b_im + coef_im * b_re


def _s5_discretise(a_re, a_im, log_dt, b_re, b_im):
    g, p, ch = b_re.shape
    rep = lambda t: jnp.repeat(t, ch, axis=0)
    flat = lambda t: jnp.transpose(t, (0, 2, 1)).reshape(g * ch, p)
    args = (rep(a_re), rep(a_im), rep(jnp.broadcast_to(log_dt[:, None], (g, p))), flat(b_re), flat(b_im))
    spec = _const_spec((g * ch, p))
    shape = jax.ShapeDtypeStruct((g * ch, p), F32)
    ab_re, ab_im, bb_re, bb_im = pl.pallas_call(
        _s5_disc_body, grid=(1,), in_specs=[spec] * 5, out_specs=(spec,) * 4, out_shape=(shape,) * 4,
        compiler_params=_params(1), name="s5_discretise",
    )(*args)
    return ab_re[::ch], ab_im[::ch], bb_re.reshape(g, ch, p), bb_im.reshape(g, ch, p)


def _block_diag_in(w_gcp):
    g, ch, p = w_gcp.shape
    gpt = LANES // ch
    w = w_gcp.reshape(g // gpt, gpt, ch, p)
    eye = jnp.eye(gpt, dtype=w.dtype)
    return jnp.einsum("ab,jacp->jacbp", eye, w).reshape(g // gpt, gpt * ch, gpt * p)


def _block_diag_out(w_gcp):
    g, ch, p = w_gcp.shape
    gpt = LANES // ch
    w = w_gcp.reshape(g // gpt, gpt, ch, p)
    eye = jnp.eye(gpt, dtype=w.dtype)
    return jnp.einsum("ab,jacp->japbc", eye, w).reshape(g // gpt, gpt * p, gpt * ch)


def _s5_body(u_ref, ab_re_ref, ab_im_ref, wb_re_ref, wb_im_ref, wc_ref, d_ref, wglu_ref, bglu_ref, h0_re_ref, h0_im_ref,
             z_ref, hf_re_ref, hf_im_ref, h_re_ref, h_im_ref, bu_re_ref, bu_im_ref, y_ref):
    tb = pl.program_id(1)
    rows = u_ref.shape[0]
    steps = rows // SUBLANES
    n_tiles = wb_re_ref.shape[0]
    sw = wb_re_ref.shape[2]

    @pl.when(tb == 0)
    def _():
        h_re_ref[...] = h0_re_ref[...]
        h_im_ref[...] = h0_im_ref[...]

    for j in range(n_tiles):
        u_j = u_ref[:, j * LANES:(j + 1) * LANES]
        u_b = u_j.astype(BF16)
        bu_re_ref[j] = _dot(u_b, wb_re_ref[j])
        bu_im_ref[j] = _dot(u_b, wb_im_ref[j])
        a_re = jnp.broadcast_to(ab_re_ref[:, j * sw:(j + 1) * sw], (SUBLANES, sw))
        a_im = jnp.broadcast_to(ab_im_ref[:, j * sw:(j + 1) * sw], (SUBLANES, sw))
        h_re = h_re_ref[:, j * sw:(j + 1) * sw]
        h_im = h_im_ref[:, j * sw:(j + 1) * sw]
        for t in range(steps):
            sl = slice(t * SUBLANES, (t + 1) * SUBLANES)
            h_re, h_im = (a_re * h_re - a_im * h_im + bu_re_ref[j, sl, :], a_re * h_im + a_im * h_re + bu_im_ref[j, sl, :])
            bu_re_ref[j, sl, :] = h_re
            bu_im_ref[j, sl, :] = h_im
        h_re_ref[:, j * sw:(j + 1) * sw] = h_re
        h_im_ref[:, j * sw:(j + 1) * sw] = h_im
        h_cat = jnp.concatenate([bu_re_ref[j].astype(BF16), bu_im_ref[j].astype(BF16)], axis=1)
        y_ref[:, j * LANES:(j + 1) * LANES] = _dot(h_cat, wc_ref[j]) + d_ref[:, j * LANES:(j + 1) * LANES] * u_j

    z = _gelu(y_ref[...])
    gate = _sigmoid(_dot(z.astype(BF16), wglu_ref[...]) + bglu_ref[...])
    z_ref[...] = (z * gate).astype(BF16)

    @pl.when(tb == pl.num_programs(1) - 1)
    def _():
        hf_re_ref[...] = h_re_ref[...]
        hf_im_ref[...] = h_im_ref[...]


def _s5(u, ab_re, ab_im, wb_re, wb_im, wc, d_skip, wglu, bglu, h0_re, h0_im, *, tblock):
    total, width = u.shape
    n_state = ab_re.shape[1]
    n_bg = h0_re.shape[0] // SUBLANES
    sw = wb_re.shape[2]
    tb, grid, blk, per_group = _time_grid(total, n_bg, tblock)
    rows = tb * SUBLANES
    st_spec = per_group(SUBLANES, n_state)
    consts = (ab_re, ab_im, wb_re, wb_im, wc, d_skip, wglu, bglu)
    st_shape = jax.ShapeDtypeStruct((n_bg * SUBLANES, n_state), F32)
    return pl.pallas_call(
        _s5_body,
        grid=grid,
        in_specs=[blk(width)] + [_resident(p) for p in consts] + [st_spec, st_spec],
        out_specs=(blk(width), st_spec, st_spec),
        out_shape=(jax.ShapeDtypeStruct((total, width), BF16), st_shape, st_shape),
        scratch_shapes=[pltpu.VMEM((SUBLANES, n_state), F32), pltpu.VMEM((SUBLANES, n_state), F32),
                        pltpu.VMEM((wb_re.shape[0], rows, sw), F32), pltpu.VMEM((wb_re.shape[0], rows, sw), F32),
                        pltpu.VMEM((rows, width), F32)],
        compiler_params=_params(2),
        name="s5",
    )(u, *[_resident_array(p) for p in consts], h0_re, h0_im)


def _lru_body(gate_ref, xin_ref, conv0_ref, h0_ref, cw_ref, cb_ref, wr_ref, br_ref, wi_ref, bi_ref, lam_ref,
              o_ref, hf_ref, convf_ref, xbuf_ref, a_ref, b_ref, h_ref):
    tb = pl.program_id(1)
    rows, width = xin_ref.shape
    steps = rows // SUBLANES
    taps = cw_ref.shape[0]
    tail = (taps - 1) * SUBLANES
    n_blocks = wr_ref.shape[0]
    bw = wr_ref.shape[1]

    @pl.when(tb == 0)
    def _():
        xbuf_ref[0:tail, :] = conv0_ref[...]
        h_ref[...] = h0_ref[...]

    xbuf_ref[tail:tail + rows, :] = xin_ref[...]
    for n in range(n_blocks):
        cols = slice(n * bw, (n + 1) * bw)
        xc = cb_ref[:, cols] + xbuf_ref[0:rows, cols] * cw_ref[0:1, cols]
        for j in range(1, taps):
            xc = xc + xbuf_ref[j * SUBLANES:j * SUBLANES + rows, cols] * cw_ref[j:j + 1, cols]
        xb = xc.astype(BF16)
        i = _sigmoid(_dot(xb, wi_ref[n]) + bi_ref[:, cols])
        half_rate = (-0.5 * LRU_C) * _softplus(-lam_ref[:, cols])
        log_a = jnp.tanh(0.5 * (_dot(xb, wr_ref[n]) + br_ref[:, cols])) * half_rate + half_rate
        a_ref[:, cols] = jnp.exp(log_a)
        th = jnp.tanh(log_a)
        y = -2.0 * th / (1.0 - th)
        root = jnp.where(y > 0.0, y * lax.rsqrt(y), 0.0)
        b_ref[:, cols] = root * (i * xc)
        h = h_ref[:, cols]
        for t in range(steps):
            sl = slice(t * SUBLANES, (t + 1) * SUBLANES)
            h = a_ref[sl, cols] * h + b_ref[sl, cols]
            b_ref[sl, cols] = h
        h_ref[:, cols] = h
        o_ref[:, cols] = (_gelu(gate_ref[:, cols]) * b_ref[:, cols]).astype(BF16)
    xbuf_ref[0:tail, :] = xbuf_ref[rows:rows + tail, :]

    @pl.when(tb == pl.num_programs(1) - 1)
    def _():
        hf_ref[...] = h_ref[...]
        convf_ref[...] = xbuf_ref[0:tail, :]


def _lru(gate, xin, conv0, h0, cw, cb, wr, br, wi, bi, lam, *, tblock):
    total, width = xin.shape
    n_bg = h0.shape[0] // SUBLANES
    tail = (cw.shape[0] - 1) * SUBLANES
    tb, grid, blk, per_group = _time_grid(total, n_bg, tblock)
    rows = tb * SUBLANES
    h_spec = per_group(SUBLANES, width)
    c_spec = per_group(tail, width)
    consts = (cw, cb, wr, br, wi, bi, lam)
    return pl.pallas_call(
        _lru_body,
        grid=grid,
        in_specs=[blk(width), blk(width), c_spec, h_spec] + [_resident(p) for p in consts],
        out_specs=(blk(width), h_spec, c_spec),
        out_shape=(jax.ShapeDtypeStruct((total, width), BF16), jax.ShapeDtypeStruct((n_bg * SUBLANES, width), F32),
                   jax.ShapeDtypeStruct((n_bg * tail, width), F32)),
        scratch_shapes=[pltpu.VMEM((rows + tail, width), F32), pltpu.VMEM((rows, width), F32),
                        pltpu.VMEM((rows, width), F32), pltpu.VMEM((SUBLANES, width), F32)],
        compiler_params=_params(2),
        name="rg_lru",
    )(gate, xin, conv0, h0, *[_resident_array(p) for p in consts])


def _row(t):
    return t.reshape(1, -1).astype(F32)


def _to_rows(x):
    b, s, d = x.shape
    return jnp.transpose(x.reshape(b // SUBLANES, SUBLANES, s, d), (0, 2, 1, 3)).reshape(b * s, d)


def _from_rows(y, b, s):
    d = y.shape[-1]
    return jnp.transpose(y.reshape(b // SUBLANES, s, SUBLANES, d), (0, 2, 1, 3)).reshape(b, s, d)


def _even_layer(x, nb, gla_states, gla_acc, e, n_even, s_re, s_im, prm, mlp, *, final, batch_major_out):
    (g_norm, w_in, w_u, w_a, wa2, ba2, g_head, ab_re, ab_im, wb_re, wb_im, wc, d_skip, wglu, bglu, w_out, dims) = prm
    nh, dk, dv = dims
    q, k, v, gt, u, la, *reordered = _inproj_even(x, g_norm, w_in, w_u, w_a, wa2, ba2, kw=nh * dk, vw=nh * dv)
    x2d = reordered[0] if reordered else x
    steps = x2d.shape[0] // nb
    og, gla_acc = _gla(q, k, la, v, gt, g_head, gla_states, e, n_even, gla_acc, nb, dims, c=math.gcd(steps, GLA_CHUNK), tblock=128)
    n_state = ab_re.shape[1]
    h0_re = jnp.zeros((nb, n_state), F32) if s_re is None else s_re.astype(F32).reshape(nb, n_state)
    h0_im = jnp.zeros((nb, n_state), F32) if s_im is None else s_im.astype(F32).reshape(nb, n_state)
    zz, hf_re, hf_im = _s5(u, ab_re, ab_im, wb_re, wb_im, wc, d_skip, wglu, bglu, h0_re, h0_im, tblock=64)
    g_mlp, w_up, w_down, g_fin = mlp
    y = _out_mlp(x2d, og, zz, w_out, g_mlp, w_up, w_down, g_fin, final=final, batch_major_out=batch_major_out)
    return y, gla_acc, hf_re, hf_im


def _odd_layer(x2d, nb, s_lru, s_conv, prm, mlp, *, final, batch_major_out):
    (g_norm, w_in, cw, cb, wr, br, wi, bi, lam, w_out) = prm
    n_bg = nb // SUBLANES
    taps = cw.shape[0]
    width = _resident_shape(w_in)[1] // 2
    gate, xin = _inproj_odd(x2d, g_norm, w_in)
    h0 = jnp.zeros((nb, width), F32) if s_lru is None else s_lru.astype(F32)
    if s_conv is None:
        conv0 = jnp.zeros((n_bg * (taps - 1) * SUBLANES, width), F32)
    else:
        conv0 = jnp.transpose(s_conv.astype(F32).reshape(n_bg, SUBLANES, taps - 1, width), (0, 2, 1, 3)).reshape(-1, width)
    gh, hf, convf = _lru(gate, xin, conv0, h0, cw, cb, wr, br, wi, bi, lam, tblock=64)
    g_mlp, w_up, w_down, g_fin = mlp
    y = _out_mlp(x2d, gh, gh, w_out, g_mlp, w_up, w_down, g_fin, final=final, batch_major_out=batch_major_out)
    conv_new = jnp.transpose(convf.reshape(n_bg, taps - 1, SUBLANES, width), (0, 2, 1, 3)).reshape(nb, taps - 1, width)
    return y, hf, conv_new


def kernel(x_prompt, x_sample, state_gla, state_s5_re, state_s5_im, state_lru, state_conv, norm_mix_even, w_in_even, gla_w_a2, gla_b_a2, gla_norm, s5_a_re, s5_a_im, s5_log_dt, s5_b_re, s5_b_im, s5_c_re, s5_c_im, s5_d, s5_w_glu, s5_b_glu, w_out_even, norm_mix_odd, w_in_odd, conv_w, conv_b, lru_w_r, lru_b_r, lru_w_i, lru_b_i, lru_lam, w_out_odd, norm_mlp, w_up, w_down, norm_final):
    depth = norm_mlp.shape[0]
    nh, dk, dv = state_gla.shape[2:]
    rank = gla_w_a2.shape[1]
    n_groups, n_p, ch = s5_b_re.shape[1:]
    kw, vw = nh * dk, nh * dv
    assert rank <= LANES and LANES % ch == 0 and LANES % dk == 0 and dv == LANES
    bp, sp, _ = x_prompt.shape
    bs, ss, _ = x_sample.shape
    n_even = state_gla.shape[0]

    in_kernel_order = bp == SUBLANES and sp % (ROW_TILE // SUBLANES) == 0 and depth > 1
    yp = x_prompt.astype(F32) if in_kernel_order else _to_rows(x_prompt.astype(F32))
    ys = _to_rows(x_sample.astype(F32))
    g_fin = _row(norm_final)
    o4 = 2 * kw + 2 * vw
    o5 = o4 + rank
    w_in_main = w_in_even[:, :, :o4].astype(BF16)
    w_in_u = w_in_even[:, :, o5:].astype(BF16)
    w_in_a = jnp.pad(w_in_even[:, :, o4:o5], ((0, 0), (0, 0), (0, LANES - rank))).astype(BF16)
    w_a2 = jnp.pad(gla_w_a2, ((0, 0), (0, LANES - rank), (0, 0))).astype(BF16)
    w_up_b, w_down_b = w_up.astype(BF16), w_down.astype(BF16)
    w_out_even_b, w_out_odd_b, w_in_odd_b = w_out_even.astype(BF16), w_out_odd.astype(BF16), w_in_odd.astype(BF16)
    w_glu_b, w_r_b, w_i_b = s5_w_glu.astype(BF16), lru_w_r.astype(BF16), lru_w_i.astype(BF16)
    keys = ("re", "im", "lru", "conv")
    outs_p = {key: [] for key in keys}
    outs_s = {key: [] for key in keys}
    gla_p = gla_s = None

    def record(outs, sr, si):
        outs["re"].append(sr.reshape(-1, n_groups, n_p))
        outs["im"].append(si.reshape(-1, n_groups, n_p))

    for layer in range(depth):
        final = layer == depth - 1
        bm_out = (bp, sp) if final and in_kernel_order else None
        mlp = (_row(norm_mlp[layer]), (w_up_b, layer), (w_down_b, layer), g_fin)
        if layer % 2 == 0:
            e = layer // 2
            ab_re, ab_im, bb_re, bb_im = _s5_discretise(s5_a_re[e].astype(F32), s5_a_im[e].astype(F32), s5_log_dt[e].astype(F32),
                                                        s5_b_re[e].astype(F32), s5_b_im[e].astype(F32))
            wc = jnp.concatenate([_block_diag_out(s5_c_re[e].astype(F32)), -_block_diag_out(s5_c_im[e].astype(F32))], axis=1)
            prm = (_row(norm_mix_even[e]), (w_in_main, e), (w_in_u, e), (w_in_a, e), (w_a2, e), _row(gla_b_a2[e]), _row(gla_norm[e]),
                   ab_re.reshape(1, -1), ab_im.reshape(1, -1), _block_diag_in(bb_re).astype(BF16), _block_diag_in(bb_im).astype(BF16),
                   wc.astype(BF16), _row(s5_d[e]), (w_glu_b, e), _row(s5_b_glu[e]), (w_out_even_b, e), (nh, dk, dv))
            yp, gla_p, sr, si = _even_layer(yp, bp, None, gla_p, e, n_even, None, None, prm, mlp, final=final, batch_major_out=bm_out)
            record(outs_p, sr, si)
            ys, gla_s, sr, si = _even_layer(ys, bs, state_gla, gla_s, e, n_even, state_s5_re[e], state_s5_im[e], prm, mlp, final=final,
                                            batch_major_out=None)
            record(outs_s, sr, si)
        else:
            o = layer // 2
            prm = (_row(norm_mix_odd[o]), (w_in_odd_b, o), conv_w[o].astype(F32), _row(conv_b[o]),
                   (w_r_b, o), _row(lru_b_r[o]), (w_i_b, o), _row(lru_b_i[o]), _row(lru_lam[o]), (w_out_odd_b, o))
            yp, sl, sc = _odd_layer(yp, bp, None, None, prm, mlp, final=final, batch_major_out=bm_out)
            outs_p["lru"].append(sl)
            outs_p["conv"].append(sc)
            ys, sl, sc = _odd_layer(ys, bs, state_lru[o], state_conv[o], prm, mlp, final=final, batch_major_out=None)
            outs_s["lru"].append(sl)
            outs_s["conv"].append(sc)

    stack = lambda outs: tuple(jnp.stack(outs[key]) for key in keys)
    return (yp if in_kernel_order else _from_rows(yp, bp, sp), _from_rows(ys, bs, ss), gla_p) + stack(outs_p) + (gla_s,) + stack(outs_s)
```

```python
import functools
import math

import jax
import jax.numpy as jnp
from jax import lax
from jax.experimental import pallas as pl
from jax.experimental.pallas import tpu as pltpu

F32 = jnp.float32
BF16 = jnp.bfloat16

NORM_EPS = 1e-6
GLA_TAU = 16.0
GLA_CHUNK = 16
LRU_C = 8.0

SUBLANES = 8
LANES = 128
ROW_TILE = 512
INPROJ_TILE = 1024
FF_CHUNK = 1024
VMEM_LIMIT = 48 * 1024 * 1024


def _params(n_axes):
    return pltpu.CompilerParams(dimension_semantics=("arbitrary",) * n_axes, vmem_limit_bytes=VMEM_LIMIT)


def _const_spec(shape):
    zeros = (0,) * len(shape)
    return pl.BlockSpec(shape, lambda *_: zeros, pipeline_mode=pl.Buffered(1))


def _resident(p):
    if not isinstance(p, tuple):
        return _const_spec(p.shape)
    stacked, layer = p
    index = (layer,) + (0,) * (stacked.ndim - 1)
    return pl.BlockSpec((None,) + stacked.shape[1:], lambda *_: index, pipeline_mode=pl.Buffered(1))


def _resident_array(p):
    return p[0] if isinstance(p, tuple) else p


def _resident_shape(p):
    return p[0].shape[1:] if isinstance(p, tuple) else p.shape


def _dot(a, b):
    return jnp.dot(a, b, preferred_element_type=F32)


def _dot_nt(a, b):
    return lax.dot_general(a, b, (((1,), (1,)), ((), ())), preferred_element_type=F32)


def _dot_tn(a, b):
    return lax.dot_general(a, b, (((0,), (0,)), ((), ())), preferred_element_type=F32)


def _rmsnorm(x, g):
    return x * lax.rsqrt(jnp.mean(x * x, axis=-1, keepdims=True) + NORM_EPS) * g


def _log_sigmoid(z):
    return jnp.minimum(z, 0.0) - jnp.log1p(jnp.exp(-jnp.abs(z)))


def _softplus(z):
    return jnp.maximum(z, 0.0) + jnp.log1p(jnp.exp(-jnp.abs(z)))


_GELU_C1 = math.sqrt(2.0 / math.pi)
_GELU_C2 = _GELU_C1 * 0.044715


def _gelu(x):
    half = 0.5 * x
    return half + half * jnp.tanh(x * (_GELU_C1 + _GELU_C2 * (x * x)))


def _sigmoid(x):
    return 0.5 * jnp.tanh(0.5 * x) + 0.5


def _rows(ref, start, size):
    return ref[pl.ds(pl.multiple_of(start, size), size), :]


def _interleave_rows(x_ref, slab_ref):
    nb, ts, d = x_ref.shape
    for b in range(nb):
        for l in range(d // LANES):
            slab_ref[l, pl.ds(b, ts, stride=nb), :] = x_ref[b, :, l * LANES:(l + 1) * LANES]
    return jnp.concatenate([slab_ref[l] for l in range(d // LANES)], axis=1)


def _deinterleave_rows(y, slab_ref, o_ref):
    nb, ts, d = o_ref.shape
    for l in range(d // LANES):
        slab_ref[l] = y[:, l * LANES:(l + 1) * LANES]
    for b in range(nb):
        for l in range(d // LANES):
            o_ref[b, :, l * LANES:(l + 1) * LANES] = slab_ref[l, pl.ds(b, ts, stride=nb), :]


def _row_tiles(rows, tile=ROW_TILE):
    tm = tile if rows % tile == 0 else min(ROW_TILE, rows)
    return tm, (rows // tm,), lambda width, col=0: pl.BlockSpec((tm, width), lambda i: (i, col))


def _inproj_even_body(x_ref, g_ref, w_ref, wu_ref, wa_ref, wa2_ref, ba2_ref, q_ref, k_ref, v_ref, gt_ref, u_ref, la_ref, *rest,
                      kw, vw):
    if rest:
        xt_ref, slab_ref = rest
        x = _interleave_rows(x_ref, slab_ref)
        xt_ref[...] = x
    else:
        x = x_ref[...]
    xn = _rmsnorm(x, g_ref[...]).astype(BF16)
    o1, o2 = kw, 2 * kw
    o3 = o2 + vw
    o4 = o3 + vw
    q_ref[...] = _dot(xn, w_ref[:, 0:o1])
    k_ref[...] = _dot(xn, w_ref[:, o1:o2])
    v_ref[...] = _dot(xn, w_ref[:, o2:o3]).astype(BF16)
    gt_ref[...] = _dot(xn, w_ref[:, o3:o4])
    u_ref[...] = _dot(xn, wu_ref[...])
    a_lr = _dot(xn, wa_ref[...]).astype(BF16)
    z = _dot(a_lr, wa2_ref[...]) + ba2_ref[...]
    la_ref[...] = _log_sigmoid(z) * (1.0 / GLA_TAU)


def _inproj_even(x, g, w, wu, wa, wa2, ba2, *, kw, vw):
    batch_major = x.ndim == 3
    d = x.shape[-1]
    rows = x.size // d
    sw = _resident_shape(wu)[1]
    widths = [kw, kw, vw, vw, sw, kw]
    dtypes = [F32, F32, BF16, F32, F32, F32]
    consts = (g, w, wu, wa, wa2, ba2)
    tm, grid, blk = _row_tiles(rows, INPROJ_TILE)
    x_spec, scratch = blk(d), []
    if batch_major:
        nb = x.shape[0]
        x_spec = pl.BlockSpec((nb, tm // nb, d), lambda i: (0, i, 0))
        widths.append(d)
        dtypes.append(F32)
        scratch = [pltpu.VMEM((d // LANES, tm, LANES), F32)]
    return pl.pallas_call(
        functools.partial(_inproj_even_body, kw=kw, vw=vw),
        grid=grid,
        in_specs=[x_spec] + [_resident(p) for p in consts],
        out_specs=tuple(blk(wd) for wd in widths),
        out_shape=tuple(jax.ShapeDtypeStruct((rows, wd), dt) for wd, dt in zip(widths, dtypes)),
        scratch_shapes=scratch,
        compiler_params=_params(1),
        name="inproj_even",
    )(x, *[_resident_array(p) for p in consts])


def _inproj_odd_body(x_ref, g_ref, w_ref, gate_ref, xin_ref, *, width):
    xn = _rmsnorm(x_ref[...], g_ref[...]).astype(BF16)
    gate_ref[...] = _dot(xn, w_ref[:, 0:width])
    xin_ref[...] = _dot(xn, w_ref[:, width:2 * width])


def _inproj_odd(x2d, g, w):
    rows, d = x2d.shape
    width = _resident_shape(w)[1] // 2
    _, grid, blk = _row_tiles(rows, INPROJ_TILE)
    shape = jax.ShapeDtypeStruct((rows, width), F32)
    return pl.pallas_call(
        functools.partial(_inproj_odd_body, width=width),
        grid=grid,
        in_specs=[blk(d), _resident(g), _resident(w)],
        out_specs=(blk(width), blk(width)),
        out_shape=(shape, shape),
        compiler_params=_params(1),
        name="inproj_odd",
    )(x2d, _resident_array(g), _resident_array(w))


def _out_mlp_body(x_ref, a1_ref, a2_ref, wo_ref, g_ref, wu_ref, wd_ref, gf_ref, o_ref, *slab, final):
    half = a1_ref.shape[1]
    mix = _dot(a1_ref[...], wo_ref[0:half, :]) + _dot(a2_ref[...], wo_ref[half:2 * half, :])
    x1 = x_ref[...] + mix
    xn = _rmsnorm(x1, g_ref[...]).astype(BF16)
    acc = x1
    for c in range(wu_ref.shape[1] // FF_CHUNK):
        h = _dot(xn, wu_ref[:, c * FF_CHUNK:(c + 1) * FF_CHUNK])
        h = jnp.square(jnp.maximum(h, 0.0)).astype(BF16)
        acc = acc + _dot(h, wd_ref[c * FF_CHUNK:(c + 1) * FF_CHUNK, :])
    if final:
        acc = _rmsnorm(acc, gf_ref[...])
    if slab:
        _deinterleave_rows(acc, slab[0], o_ref)
    else:
        o_ref[...] = acc


def _out_mlp(x2d, a1, a2, wo, g, wu, wd, gf, *, final, batch_major_out=None):
    rows, d = x2d.shape
    half = d // 2
    consts = (wo, g, wu, wd, gf)
    tm, grid, blk = _row_tiles(rows)
    out_spec, out_shape, scratch = blk(d), jax.ShapeDtypeStruct((rows, d), F32), []
    if batch_major_out is not None:
        nb, s = batch_major_out
        out_spec = pl.BlockSpec((nb, tm // nb, d), lambda i: (0, i, 0))
        out_shape = jax.ShapeDtypeStruct((nb, s, d), F32)
        scratch = [pltpu.VMEM((d // LANES, tm, LANES), F32)]
    return pl.pallas_call(
        functools.partial(_out_mlp_body, final=final),
        grid=grid,
        in_specs=[blk(d), blk(half, 0), blk(half, a1.shape[1] // half - 1)] + [_resident(p) for p in consts],
        out_specs=out_spec,
        out_shape=out_shape,
        scratch_shapes=scratch,
        compiler_params=_params(1),
        name="out_mlp",
    )(x2d, a1, a2, *[_resident_array(p) for p in consts])


def _time_grid(total_rows, n_bg, tblock):
    steps = total_rows // (n_bg * SUBLANES)
    tb = min(tblock, steps)
    n_t = steps // tb
    blk = lambda width: pl.BlockSpec((tb * SUBLANES, width), lambda bg, t: (bg * n_t + t, 0))
    per_group = lambda rows, width: pl.BlockSpec((rows, width), lambda bg, t: (bg, 0))
    return tb, (n_bg, n_t), blk, per_group


def _gla_body(q_ref, k_ref, la_ref, v_ref, gt_ref, gh_ref, s0_ref, *rest, c, nh, dk, dv, transposed):
    o_ref, sfin_ref, st_ref = rest[-3:]
    tb = pl.program_id(1)
    rows = SUBLANES * c
    n_chunks = q_ref.shape[0] // rows
    dk_shift = int(math.log2(dk))
    heads_per_tile = LANES // dk

    @pl.when(tb == 0)
    def _():
        st_ref[...] = s0_ref[...]

    def iota(shape, axis):
        return lax.broadcasted_iota(jnp.int32, shape, axis)

    causal = ((iota((rows, rows), 0) & 7) == (iota((rows, rows), 1) & 7)) & (iota((rows, rows), 1) <= iota((rows, rows), 0))
    emask = (iota((rows, SUBLANES * dk), 0) & 7) == lax.shift_right_logical(iota((rows, SUBLANES * dk), 1), dk_shift)
    dmask = iota((SUBLANES, SUBLANES * dk), 0) == lax.shift_right_logical(iota((SUBLANES, SUBLANES * dk), 1), dk_shift)
    lo_rows = iota((rows, LANES), 1) < dk
    lo_8 = iota((SUBLANES, LANES), 1) < dk

    def expand(tile, h, lo=lo_rows, mask=emask):
        rolled = pltpu.roll(tile, dk, axis=1)
        d = jnp.where(lo, tile, rolled) if h % heads_per_tile == 0 else jnp.where(lo, rolled, tile)
        return jnp.where(mask, jnp.concatenate([d] * (SUBLANES * dk // LANES), axis=1), 0.0)

    def chunk(n, carry):
        r0 = n * rows
        q = _rows(q_ref, r0, rows)
        k = _rows(k_ref, r0, rows)
        la = _rows(la_ref, r0, rows)
        v = _rows(v_ref, r0, rows)
        gt = _rows(gt_ref, r0, rows)
        acc = jnp.zeros((SUBLANES, nh * dk), F32)
        pieces = []
        for t in range(c):
            acc = acc + la[t * SUBLANES:(t + 1) * SUBLANES, :]
            pieces.append(acc)
        cum = jnp.concatenate(pieces, axis=0)
        last = jnp.concatenate([acc] * c, axis=0)
        q_t = q * jnp.exp(cum) * (dk ** -0.5)
        k_t = k * jnp.exp(-cum)
        k_end = k * jnp.exp(last - cum)
        dec = jnp.exp(acc)
        pad = jnp.zeros((LANES - SUBLANES, LANES), F32)
        outs = []
        for h in range(nh):
            tile = (h // heads_per_tile) * LANES
            lane0 = (h % heads_per_tile) * dk
            sel = lo_rows if lane0 == 0 else jnp.logical_not(lo_rows)
            q_tile = q_t[:, tile:tile + LANES]
            scores = _dot_nt(jnp.where(sel, q_tile, 0.0).astype(BF16), k_t[:, tile:tile + LANES].astype(BF16))
            scores = jnp.where(causal, scores, 0.0).astype(BF16)
            v_h = v[:, h * dv:(h + 1) * dv]
            q_exp = expand(q_tile, h).astype(BF16)
            k_exp = expand(k_end[:, tile:tile + LANES], h).astype(BF16)
            if transposed:
                st = st_ref[h]
                o = _dot(scores, v_h) + _dot_nt(q_exp, st.astype(BF16))
                dec_row = jnp.sum(expand(dec[:, tile:tile + LANES], h, lo_8, dmask), axis=0, keepdims=True)
                st_ref[h] = st * dec_row + _dot_tn(v_h, k_exp)
            else:
                st = jnp.concatenate([st_ref[b, h] for b in range(SUBLANES)], axis=0)
                o = _dot(scores, v_h) + _dot(q_exp, st.astype(BF16))
                upd = _dot_tn(k_exp, v_h)
                dec_t = jnp.concatenate([dec[:, tile:tile + LANES], pad], axis=0).T
                for b in range(SUBLANES):
                    dec_b = jnp.broadcast_to(dec_t[lane0:lane0 + dk, b:b + 1], (dk, dv))
                    st_ref[b, h] = st[b * dk:(b + 1) * dk, :] * dec_b + upd[b * dk:(b + 1) * dk, :]
            o = _rmsnorm(o, gh_ref[:, h * dv:(h + 1) * dv])
            g_h = gt[:, h * dv:(h + 1) * dv]
            outs.append(o * (g_h * _sigmoid(g_h)))
        o_ref[pl.ds(pl.multiple_of(r0, rows), rows), :] = jnp.concatenate(outs, axis=1).astype(BF16)
        return carry

    lax.fori_loop(0, n_chunks, chunk, 0, unroll=min(n_chunks, 4))

    @pl.when(tb == pl.num_programs(1) - 1)
    def _():
        sfin_ref[...] = st_ref[...]


def _gla_state_in(s):
    b, h, dk, dv = s.shape
    return jnp.transpose(s.reshape(b // SUBLANES, SUBLANES, h, dk, dv), (0, 2, 4, 1, 3)).reshape(b // SUBLANES * h, dv, SUBLANES * dk)


def _gla_state_out(st, nh, dk):
    dv = st.shape[1]
    n_bg = st.shape[0] // nh
    return jnp.transpose(st.reshape(n_bg, nh, dv, SUBLANES, dk), (0, 3, 1, 4, 2)).reshape(n_bg * SUBLANES, nh, dk, dv)


def _gla(q, k, la, v, gt, gh, states, layer, n_layers, acc, nb, dims, *, c, tblock):
    nh, dk, dv = dims
    n_bg = nb // SUBLANES
    tb, grid, blk, _ = _time_grid(q.shape[0], n_bg, tblock)
    transposed = q.shape[0] // n_bg > SUBLANES * c
    body = functools.partial(_gla_body, c=c, nh=nh, dk=dk, dv=dv, transposed=transposed)
    row_specs = [blk(nh * dk), blk(nh * dk), blk(nh * dk), blk(nh * dv), blk(nh * dv), _const_spec(gh.shape)]
    og_shape = jax.ShapeDtypeStruct((q.shape[0], nh * dv), BF16)
    if transposed:
        st_block = (nh, dv, SUBLANES * dk)
        st_spec = pl.BlockSpec(st_block, lambda bg, t: (bg, 0, 0))
        s_in = jnp.zeros((n_bg * nh, dv, SUBLANES * dk), F32) if states is None else _gla_state_in(states[layer].astype(F32))
        og, s_fin = pl.pallas_call(
            body, grid=grid, in_specs=row_specs + [st_spec], out_specs=(blk(nh * dv), st_spec),
            out_shape=(og_shape, jax.ShapeDtypeStruct(s_in.shape, F32)), scratch_shapes=[pltpu.VMEM(st_block, F32)],
            compiler_params=_params(2), name="gla",
        )(q, k, la, v, gt, gh, s_in)
        s_new = _gla_state_out(s_fin, nh, dk)[None]
        return og, (jnp.pad(s_new, ((0, n_layers - 1),) + ((0, 0),) * 4) if acc is None else lax.dynamic_update_slice(acc, s_new, (layer, 0, 0, 0, 0)))
    st_block = (SUBLANES, nh, dk, dv)
    stacked_spec = pl.BlockSpec((None,) + st_block, lambda bg, t: (layer, bg, 0, 0, 0))
    if states is None:
        s_in, in_spec = jnp.zeros((nb, nh, dk, dv), F32), pl.BlockSpec(st_block, lambda bg, t: (bg, 0, 0, 0))
    else:
        s_in, in_spec = states.astype(F32), stacked_spec
    if acc is None:
        acc = jnp.zeros((n_layers, nb, nh, dk, dv), F32)
    return pl.pallas_call(
        body, grid=grid, in_specs=row_specs + [in_spec, pl.BlockSpec(memory_space=pl.ANY)], out_specs=(blk(nh * dv), stacked_spec),
        out_shape=(og_shape, jax.ShapeDtypeStruct(acc.shape, F32)), scratch_shapes=[pltpu.VMEM(st_block, F32)],
        input_output_aliases={7: 1}, compiler_params=_params(2), name="gla",
    )(q, k, la, v, gt, gh, s_in, acc)


def _s5_disc_body(a_re_ref, a_im_ref, log_dt_ref, b_re_ref, b_im_ref, ab_re_ref, ab_im_ref, bb_re_ref, bb_im_ref):
    a_re = a_re_ref[...]
    a_im = a_im_ref[...]
    dt = jnp.exp(log_dt_ref[...])
    mag = jnp.exp(dt * a_re)
    ab_re = mag * jnp.cos(dt * a_im)
    ab_im = mag * jnp.sin(dt * a_im)
    den = a_re * a_re + a_im * a_im
    coef_re = ((ab_re - 1.0) * a_re + ab_im * a_im) / den
    coef_im = (ab_im * a_re - (ab_re - 1.0) * a_im) / den
    b_re = b_re_ref[...]
    b_im = b_im_ref[...]
    ab_re_ref[...] = ab_re
    ab_im_ref[...] = ab_im
    bb_re_ref[...] = coef_re * b_re - coef_im * b_im
    bb_im_ref[...] = coef_re * b_im + coef_im * b_re


def _s5_discretise(a_re, a_im, log_dt, b_re, b_im):
    g, p, ch = b_re.shape
    rep = lambda t: jnp.repeat(t, ch, axis=0)
    flat = lambda t: jnp.transpose(t, (0, 2, 1)).reshape(g * ch, p)
    args = (rep(a_re), rep(a_im), rep(jnp.broadcast_to(log_dt[:, None], (g, p))), flat(b_re), flat(b_im))
    spec = _const_spec((g * ch, p))
    shape = jax.ShapeDtypeStruct((g * ch, p), F32)
    ab_re, ab_im, bb_re, bb_im = pl.pallas_call(
        _s5_disc_body, grid=(1,), in_specs=[spec] * 5, out_specs=(spec,) * 4, out_shape=(shape,) * 4,
        compiler_params=_params(1), name="s5_discretise",
    )(*args)
    return ab_re[::ch], ab_im[::ch], bb_re.reshape(g, ch, p), bb_im.reshape(g, ch, p)


def _block_diag_in(w_gcp):
    g, ch, p = w_gcp.shape
    gpt = LANES // ch
    w = w_gcp.reshape(g // gpt, gpt, ch, p)
    eye = jnp.eye(gpt, dtype=w.dtype)
    return jnp.einsum("ab,jacp->jacbp", eye, w).reshape(g // gpt, gpt * ch, gpt * p)


def _block_diag_out(w_gcp):
    g, ch, p = w_gcp.shape
    gpt = LANES // ch
    w = w_gcp.reshape(g // gpt, gpt, ch, p)
    eye = jnp.eye(gpt, dtype=w.dtype)
    return jnp.einsum("ab,jacp->japbc", eye, w).reshape(g // gpt, gpt * p, gpt * ch)


def _s5_body(u_ref, ab_re_ref, ab_im_ref, wb_re_ref, wb_im_ref, wc_ref, d_ref, wglu_ref, bglu_ref, h0_re_ref, h0_im_ref,
             z_ref, hf_re_ref, hf_im_ref, h_re_ref, h_im_ref, bu_re_ref, bu_im_ref, y_ref):
    tb = pl.program_id(1)
    rows = u_ref.shape[0]
    steps = rows // SUBLANES
    n_tiles = wb_re_ref.shape[0]
    sw = wb_re_ref.shape[2]

    @pl.when(tb == 0)
    def _():
        h_re_ref[...] = h0_re_ref[...]
        h_im_ref[...] = h0_im_ref[...]

    for j in range(n_tiles):
        u_j = u_ref[:, j * LANES:(j + 1) * LANES]
        u_b = u_j.astype(BF16)
        bu_re_ref[j] = _dot(u_b, wb_re_ref[j])
        bu_im_ref[j] = _dot(u_b, wb_im_ref[j])
        a_re = jnp.broadcast_to(ab_re_ref[:, j * sw:(j + 1) * sw], (SUBLANES, sw))
        a_im = jnp.broadcast_to(ab_im_ref[:, j * sw:(j + 1) * sw], (SUBLANES, sw))
        h_re = h_re_ref[:, j * sw:(j + 1) * sw]
        h_im = h_im_ref[:, j * sw:(j + 1) * sw]
        for t in range(steps):
            sl = slice(t * SUBLANES, (t + 1) * SUBLANES)
            h_re, h_im = (a_re * h_re - a_im * h_im + bu_re_ref[j, sl, :], a_re * h_im + a_im * h_re + bu_im_ref[j, sl, :])
            bu_re_ref[j, sl, :] = h_re
            bu_im_ref[j, sl, :] = h_im
        h_re_ref[:, j * sw:(j + 1) * sw] = h_re
        h_im_ref[:, j * sw:(j + 1) * sw] = h_im
        h_cat = jnp.concatenate([bu_re_ref[j].astype(BF16), bu_im_ref[j].astype(BF16)], axis=1)
        y_ref[:, j * LANES:(j + 1) * LANES] = _dot(h_cat, wc_ref[j]) + d_ref[:, j * LANES:(j + 1) * LANES] * u_j

    z = _gelu(y_ref[...])
    gate = _sigmoid(_dot(z.astype(BF16), wglu_ref[...]) + bglu_ref[...])
    z_ref[...] = (z * gate).astype(BF16)

    @pl.when(tb == pl.num_programs(1) - 1)
    def _():
        hf_re_ref[...] = h_re_ref[...]
        hf_im_ref[...] = h_im_ref[...]


def _s5(u, ab_re, ab_im, wb_re, wb_im, wc, d_skip, wglu, bglu, h0_re, h0_im, *, tblock):
    total, width = u.shape
    n_state = ab_re.shape[1]
    n_bg = h0_re.shape[0] // SUBLANES
    sw = wb_re.shape[2]
    tb, grid, blk, per_group = _time_grid(total, n_bg, tblock)
    rows = tb * SUBLANES
    st_spec = per_group(SUBLANES, n_state)
    consts = (ab_re, ab_im, wb_re, wb_im, wc, d_skip, wglu, bglu)
    st_shape = jax.ShapeDtypeStruct((n_bg * SUBLANES, n_state), F32)
    return pl.pallas_call(
        _s5_body,
        grid=grid,
        in_specs=[blk(width)] + [_resident(p) for p in consts] + [st_spec, st_spec],
        out_specs=(blk(width), st_spec, st_spec),
        out_shape=(jax.ShapeDtypeStruct((total, width), BF16), st_shape, st_shape),
        scratch_shapes=[pltpu.VMEM((SUBLANES, n_state), F32), pltpu.VMEM((SUBLANES, n_state), F32),
                        pltpu.VMEM((wb_re.shape[0], rows, sw), F32), pltpu.VMEM((wb_re.shape[0], rows, sw), F32),
                        pltpu.VMEM((rows, width), F32)],
        compiler_params=_params(2),
        name="s5",
    )(u, *[_resident_array(p) for p in consts], h0_re, h0_im)


def _lru_body(gate_ref, xin_ref, conv0_ref, h0_ref, cw_ref, cb_ref, wr_ref, br_ref, wi_ref, bi_ref, lam_ref,
              o_ref, hf_ref, convf_ref, xbuf_ref, a_ref, b_ref, h_ref):
    tb = pl.program_id(1)
    rows, width = xin_ref.shape
    steps = rows // SUBLANES
    taps = cw_ref.shape[0]
    tail = (taps - 1) * SUBLANES
    n_blocks = wr_ref.shape[0]
    bw = wr_ref.shape[1]

    @pl.when(tb == 0)
    def _():
        xbuf_ref[0:tail, :] = conv0_ref[...]
        h_ref[...] = h0_ref[...]

    xbuf_ref[tail:tail + rows, :] = xin_ref[...]
    for n in range(n_blocks):
        cols = slice(n * bw, (n + 1) * bw)
        xc = cb_ref[:, cols] + xbuf_ref[0:rows, cols] * cw_ref[0:1, cols]
        for j in range(1, taps):
            xc = xc + xbuf_ref[j * SUBLANES:j * SUBLANES + rows, cols] * cw_ref[j:j + 1, cols]
        xb = xc.astype(BF16)
        i = _sigmoid(_dot(xb, wi_ref[n]) + bi_ref[:, cols])
        half_rate = (-0.5 * LRU_C) * _softplus(-lam_ref[:, cols])
        log_a = jnp.tanh(0.5 * (_dot(xb, wr_ref[n]) + br_ref[:, cols])) * half_rate + half_rate
        a_ref[:, cols] = jnp.exp(log_a)
        th = jnp.tanh(log_a)
        y = -2.0 * th / (1.0 - th)
        root = jnp.where(y > 0.0, y * lax.rsqrt(y), 0.0)
        b_ref[:, cols] = root * (i * xc)
        h = h_ref[:, cols]
        for t in range(steps):
            sl = slice(t * SUBLANES, (t + 1) * SUBLANES)
            h = a_ref[sl, cols] * h + b_ref[sl, cols]
            b_ref[sl, cols] = h
        h_ref[:, cols] = h
        o_ref[:, cols] = (_gelu(gate_ref[:, cols]) * b_ref[:, cols]).astype(BF16)
    xbuf_ref[0:tail, :] = xbuf_ref[rows:rows + tail, :]

    @pl.when(tb == pl.num_programs(1) - 1)
    def _():
        hf_ref[...] = h_ref[...]
        convf_ref[...] = xbuf_ref[0:tail, :]


def _lru(gate, xin, conv0, h0, cw, cb, wr, br, wi, bi, lam, *, tblock):
    total, width = xin.shape
    n_bg = h0.shape[0] // SUBLANES
    tail = (cw.shape[0] - 1) * SUBLANES
    tb, grid, blk, per_group = _time_grid(total, n_bg, tblock)
    rows = tb * SUBLANES
    h_spec = per_group(SUBLANES, width)
    c_spec = per_group(tail, width)
    consts = (cw, cb, wr, br, wi, bi, lam)
    return pl.pallas_call(
        _lru_body,
        grid=grid,
        in_specs=[blk(width), blk(width), c_spec, h_spec] + [_resident(p) for p in consts],
        out_specs=(blk(width), h_spec, c_spec),
        out_shape=(jax.ShapeDtypeStruct((total, width), BF16), jax.ShapeDtypeStruct((n_bg * SUBLANES, width), F32),
                   jax.ShapeDtypeStruct((n_bg * tail, width), F32)),
        scratch_shapes=[pltpu.VMEM((rows + tail, width), F32), pltpu.VMEM((rows, width), F32),
                        pltpu.VMEM((rows, width), F32), pltpu.VMEM((SUBLANES, width), F32)],
        compiler_params=_params(2),
        name="rg_lru",
    )(gate, xin, conv0, h0, *[_resident_array(p) for p in consts])


def _row(t):
    return t.reshape(1, -1).astype(F32)


def _to_rows(x):
    b, s, d = x.shape
    return jnp.transpose(x.reshape(b // SUBLANES, SUBLANES, s, d), (0, 2, 1, 3)).reshape(b * s, d)


def _from_rows(y, b, s):
    d = y.shape[-1]
    return jnp.transpose(y.reshape(b // SUBLANES, s, SUBLANES, d), (0, 2, 1, 3)).reshape(b, s, d)


def _even_layer(x, nb, gla_states, gla_acc, e, n_even, s_re, s_im, prm, mlp, *, final, batch_major_out):
    (g_norm, w_in, w_u, w_a, wa2, ba2, g_head, ab_re, ab_im, wb_re, wb_im, wc, d_skip, wglu, bglu, w_out, dims) = prm
    nh, dk, dv = dims
    q, k, v, gt, u, la, *reordered = _inproj_even(x, g_norm, w_in, w_u, w_a, wa2, ba2, kw=nh * dk, vw=nh * dv)
    x2d = reordered[0] if reordered else x
    steps = x2d.shape[0] // nb
    og, gla_acc = _gla(q, k, la, v, gt, g_head, gla_states, e, n_even, gla_acc, nb, dims, c=math.gcd(steps, GLA_CHUNK), tblock=128)
    n_state = ab_re.shape[1]
    h0_re = jnp.zeros((nb, n_state), F32) if s_re is None else s_re.astype(F32).reshape(nb, n_state)
    h0_im = jnp.zeros((nb, n_state), F32) if s_im is None else s_im.astype(F32).reshape(nb, n_state)
    zz, hf_re, hf_im = _s5(u, ab_re, ab_im, wb_re, wb_im, wc, d_skip, wglu, bglu, h0_re, h0_im, tblock=64)
    g_mlp, w_up, w_down, g_fin = mlp
    y = _out_mlp(x2d, og, zz, w_out, g_mlp, w_up, w_down, g_fin, final=final, batch_major_out=batch_major_out)
    return y, gla_acc, hf_re, hf_im


def _odd_layer(x2d, nb, s_lru, s_conv, prm, mlp, *, final, batch_major_out):
    (g_norm, w_in, cw, cb, wr, br, wi, bi, lam, w_out) = prm
    n_bg = nb // SUBLANES
    taps = cw.shape[0]
    width = _resident_shape(w_in)[1] // 2
    gate, xin = _inproj_odd(x2d, g_norm, w_in)
    h0 = jnp.zeros((nb, width), F32) if s_lru is None else s_lru.astype(F32)
    if s_conv is None:
        conv0 = jnp.zeros((n_bg * (taps - 1) * SUBLANES, width), F32)
    else:
        conv0 = jnp.transpose(s_conv.astype(F32).reshape(n_bg, SUBLANES, taps - 1, width), (0, 2, 1, 3)).reshape(-1, width)
    gh, hf, convf = _lru(gate, xin, conv0, h0, cw, cb, wr, br, wi, bi, lam, tblock=64)
    g_mlp, w_up, w_down, g_fin = mlp
    y = _out_mlp(x2d, gh, gh, w_out, g_mlp, w_up, w_down, g_fin, final=final, batch_major_out=batch_major_out)
    conv_new = jnp.transpose(convf.reshape(n_bg, taps - 1, SUBLANES, width), (0, 2, 1, 3)).reshape(nb, taps - 1, width)
    return y, hf, conv_new


def kernel(x_prompt, x_sample, state_gla, state_s5_re, state_s5_im, state_lru, state_conv, norm_mix_even, w_in_even, gla_w_a2, gla_b_a2, gla_norm, s5_a_re, s5_a_im, s5_log_dt, s5_b_re, s5_b_im, s5_c_re, s5_c_im, s5_d, s5_w_glu, s5_b_glu, w_out_even, norm_mix_odd, w_in_odd, conv_w, conv_b, lru_w_r, lru_b_r, lru_w_i, lru_b_i, lru_lam, w_out_odd, norm_mlp, w_up, w_down, norm_final):
    depth = norm_mlp.shape[0]
    nh, dk, dv = state_gla.shape[2:]
    rank = gla_w_a2.shape[1]
    n_groups, n_p, ch = s5_b_re.shape[1:]
    kw, vw = nh * dk, nh * dv
    assert rank <= LANES and LANES % ch == 0 and LANES % dk == 0 and dv == LANES
    bp, sp, _ = x_prompt.shape
    bs, ss, _ = x_sample.shape
    n_even = state_gla.shape[0]

    in_kernel_order = bp == SUBLANES and sp % (INPROJ_TILE // SUBLANES) == 0 and depth > 1
    yp = x_prompt.astype(F32) if in_kernel_order else _to_rows(x_prompt.astype(F32))
    ys = _to_rows(x_sample.astype(F32))
    g_fin = _row(norm_final)
    o4 = 2 * kw + 2 * vw
    o5 = o4 + rank
    w_in_main = w_in_even[:, :, :o4].astype(BF16)
    w_in_u = w_in_even[:, :, o5:].astype(BF16)
    w_in_a = jnp.pad(w_in_even[:, :, o4:o5], ((0, 0), (0, 0), (0, LANES - rank))).astype(BF16)
    w_a2 = jnp.pad(gla_w_a2, ((0, 0), (0, LANES - rank), (0, 0))).astype(BF16)
    w_up_b, w_down_b = w_up.astype(BF16), w_down.astype(BF16)
    w_out_even_b, w_out_odd_b, w_in_odd_b = w_out_even.astype(BF16), w_out_odd.astype(BF16), w_in_odd.astype(BF16)
    w_glu_b, w_r_b, w_i_b = s5_w_glu.astype(BF16), lru_w_r.astype(BF16), lru_w_i.astype(BF16)
    keys = ("re", "im", "lru", "conv")
    outs_p = {key: [] for key in keys}
    outs_s = {key: [] for key in keys}
    gla_p = gla_s = None

    def record(outs, sr, si):
        outs["re"].append(sr.reshape(-1, n_groups, n_p))
        outs["im"].append(si.reshape(-1, n_groups, n_p))

    for layer in range(depth):
        final = layer == depth - 1
        bm_out = (bp, sp) if final and in_kernel_order else None
        mlp = (_row(norm_mlp[layer]), (w_up_b, layer), (w_down_b, layer), g_fin)
        if layer % 2 == 0:
            e = layer // 2
            ab_re, ab_im, bb_re, bb_im = _s5_discretise(s5_a_re[e].astype(F32), s5_a_im[e].astype(F32), s5_log_dt[e].astype(F32),
                                                        s5_b_re[e].astype(F32), s5_b_im[e].astype(F32))
            wc = jnp.concatenate([_block_diag_out(s5_c_re[e].astype(F32)), -_block_diag_out(s5_c_im[e].astype(F32))], axis=1)
            prm = (_row(norm_mix_even[e]), (w_in_main, e), (w_in_u, e), (w_in_a, e), (w_a2, e), _row(gla_b_a2[e]), _row(gla_norm[e]),
                   ab_re.reshape(1, -1), ab_im.reshape(1, -1), _block_diag_in(bb_re).astype(BF16), _block_diag_in(bb_im).astype(BF16),
                   wc.astype(BF16), _row(s5_d[e]), (w_glu_b, e), _row(s5_b_glu[e]), (w_out_even_b, e), (nh, dk, dv))
            yp, gla_p, sr, si = _even_layer(yp, bp, None, gla_p, e, n_even, None, None, prm, mlp, final=final, batch_major_out=bm_out)
            record(outs_p, sr, si)
            ys, gla_s, sr, si = _even_layer(ys, bs, state_gla, gla_s, e, n_even, state_s5_re[e], state_s5_im[e], prm, mlp, final=final,
                                            batch_major_out=None)
            record(outs_s, sr, si)
        else:
            o = layer // 2
            prm = (_row(norm_mix_odd[o]), (w_in_odd_b, o), conv_w[o].astype(F32), _row(conv_b[o]),
                   (w_r_b, o), _row(lru_b_r[o]), (w_i_b, o), _row(lru_b_i[o]), _row(lru_lam[o]), (w_out_odd_b, o))
            yp, sl, sc = _odd_layer(yp, bp, None, None, prm, mlp, final=final, batch_major_out=bm_out)
            outs_p["lru"].append(sl)
            outs_p["conv"].append(sc)
            ys, sl, sc = _odd_layer(ys, bs, state_lru[o], state_conv[o], prm, mlp, final=final, batch_major_out=None)
            outs_s["lru"].append(sl)
            outs_s["conv"].append(sc)

    stack = lambda outs: tuple(jnp.stack(outs[key]) for key in keys)
    return (yp if in_kernel_order else _from_rows(yp, bp, sp), _from_rows(ys, bs, ss), gla_p) + stack(outs_p) + (gla_s,) + stack(outs_s)
```

```python
import functools
import math

import jax
import jax.numpy as jnp
from jax import lax
from jax.experimental import pallas as pl
from jax.experimental.pallas import tpu as pltpu

F32 = jnp.float32
BF16 = jnp.bfloat16

NORM_EPS = 1e-6
GLA_TAU = 16.0
GLA_CHUNK = 16
LRU_C = 8.0

SUBLANES = 8
LANES = 128
ROW_TILE = 512
INPROJ_TILE = 1024
FF_CHUNK = 1024
VMEM_LIMIT = 48 * 1024 * 1024
FUSED_VMEM_LIMIT = 56 * 1024 * 1024


def _params(n_axes):
    return pltpu.CompilerParams(dimension_semantics=("arbitrary",) * n_axes, vmem_limit_bytes=VMEM_LIMIT)


def _const_spec(shape):
    zeros = (0,) * len(shape)
    return pl.BlockSpec(shape, lambda *_: zeros, pipeline_mode=pl.Buffered(1))


def _resident(p):
    if not isinstance(p, tuple):
        return _const_spec(p.shape)
    stacked, layer = p
    index = (layer,) + (0,) * (stacked.ndim - 1)
    return pl.BlockSpec((None,) + stacked.shape[1:], lambda *_: index, pipeline_mode=pl.Buffered(1))


def _resident_array(p):
    return p[0] if isinstance(p, tuple) else p


def _resident_shape(p):
    return p[0].shape[1:] if isinstance(p, tuple) else p.shape


def _dot(a, b):
    return jnp.dot(a, b, preferred_element_type=F32)


def _dot_nt(a, b):
    return lax.dot_general(a, b, (((1,), (1,)), ((), ())), preferred_element_type=F32)


def _dot_tn(a, b):
    return lax.dot_general(a, b, (((0,), (0,)), ((), ())), preferred_element_type=F32)


def _rmsnorm(x, g):
    return x * lax.rsqrt(jnp.mean(x * x, axis=-1, keepdims=True) + NORM_EPS) * g


def _log_sigmoid(z):
    return jnp.minimum(z, 0.0) - jnp.log1p(jnp.exp(-jnp.abs(z)))


def _softplus(z):
    return jnp.maximum(z, 0.0) + jnp.log1p(jnp.exp(-jnp.abs(z)))


_GELU_C1 = math.sqrt(2.0 / math.pi)
_GELU_C2 = _GELU_C1 * 0.044715


def _gelu(x):
    half = 0.5 * x
    return half + half * jnp.tanh(x * (_GELU_C1 + _GELU_C2 * (x * x)))


def _sigmoid(x):
    return 0.5 * jnp.tanh(0.5 * x) + 0.5


def _rows(ref, start, size):
    return ref[pl.ds(pl.multiple_of(start, size), size), :]


def _interleave_rows(x_ref, slab_ref):
    nb, ts, d = x_ref.shape
    for b in range(nb):
        for l in range(d // LANES):
            slab_ref[l, pl.ds(b, ts, stride=nb), :] = x_ref[b, :, l * LANES:(l + 1) * LANES]
    return jnp.concatenate([slab_ref[l] for l in range(d // LANES)], axis=1)


def _deinterleave_rows(y, slab_ref, o_ref):
    nb, ts, d = o_ref.shape
    for l in range(d // LANES):
        slab_ref[l] = y[:, l * LANES:(l + 1) * LANES]
    for b in range(nb):
        for l in range(d // LANES):
            o_ref[b, :, l * LANES:(l + 1) * LANES] = slab_ref[l, pl.ds(b, ts, stride=nb), :]


def _row_tiles(rows, tile=ROW_TILE):
    tm = tile if rows % tile == 0 else min(ROW_TILE, rows)
    return tm, (rows // tm,), lambda width, col=0: pl.BlockSpec((tm, width), lambda i: (i, col))


def _inproj_even_body(x_ref, g_ref, w_ref, wu_ref, wa_ref, wa2_ref, ba2_ref, q_ref, k_ref, v_ref, gt_ref, u_ref, la_ref, *rest,
                      kw, vw):
    if rest:
        xt_ref, slab_ref = rest
        x = _interleave_rows(x_ref, slab_ref)
        xt_ref[...] = x
    else:
        x = x_ref[...]
    xn = _rmsnorm(x, g_ref[...]).astype(BF16)
    o1, o2 = kw, 2 * kw
    o3 = o2 + vw
    o4 = o3 + vw
    q_ref[...] = _dot(xn, w_ref[:, 0:o1])
    k_ref[...] = _dot(xn, w_ref[:, o1:o2])
    v_ref[...] = _dot(xn, w_ref[:, o2:o3]).astype(BF16)
    gt_ref[...] = _dot(xn, w_ref[:, o3:o4])
    u_ref[...] = _dot(xn, wu_ref[...])
    a_lr = _dot(xn, wa_ref[...]).astype(BF16)
    z = _dot(a_lr, wa2_ref[...]) + ba2_ref[...]
    la_ref[...] = _log_sigmoid(z) * (1.0 / GLA_TAU)


def _inproj_even(x, g, w, wu, wa, wa2, ba2, *, kw, vw):
    batch_major = x.ndim == 3
    d = x.shape[-1]
    rows = x.size // d
    sw = _resident_shape(wu)[1]
    widths = [kw, kw, vw, vw, sw, kw]
    dtypes = [F32, F32, BF16, F32, F32, F32]
    consts = (g, w, wu, wa, wa2, ba2)
    tm, grid, blk = _row_tiles(rows, INPROJ_TILE)
    x_spec, scratch = blk(d), []
    if batch_major:
        nb = x.shape[0]
        x_spec = pl.BlockSpec((nb, tm // nb, d), lambda i: (0, i, 0))
        widths.append(d)
        dtypes.append(F32)
        scratch = [pltpu.VMEM((d // LANES, tm, LANES), F32)]
    return pl.pallas_call(
        functools.partial(_inproj_even_body, kw=kw, vw=vw),
        grid=grid,
        in_specs=[x_spec] + [_resident(p) for p in consts],
        out_specs=tuple(blk(wd) for wd in widths),
        out_shape=tuple(jax.ShapeDtypeStruct((rows, wd), dt) for wd, dt in zip(widths, dtypes)),
        scratch_shapes=scratch,
        compiler_params=_params(1),
        name="inproj_even",
    )(x, *[_resident_array(p) for p in consts])


def _inproj_odd_body(x_ref, g_ref, w_ref, gate_ref, xin_ref, *, width):
    xn = _rmsnorm(x_ref[...], g_ref[...]).astype(BF16)
    gate_ref[...] = _dot(xn, w_ref[:, 0:width])
    xin_ref[...] = _dot(xn, w_ref[:, width:2 * width])


def _inproj_odd(x2d, g, w):
    rows, d = x2d.shape
    width = _resident_shape(w)[1] // 2
    _, grid, blk = _row_tiles(rows, INPROJ_TILE)
    shape = jax.ShapeDtypeStruct((rows, width), F32)
    return pl.pallas_call(
        functools.partial(_inproj_odd_body, width=width),
        grid=grid,
        in_specs=[blk(d), _resident(g), _resident(w)],
        out_specs=(blk(width), blk(width)),
        out_shape=(shape, shape),
        compiler_params=_params(1),
        name="inproj_odd",
    )(x2d, _resident_array(g), _resident_array(w))


def _out_mlp_body(x_ref, a1_ref, a2_ref, wo_ref, g_ref, wu_ref, wd_ref, gf_ref, o_ref, *slab, final):
    half = a1_ref.shape[1]
    mix = _dot(a1_ref[...], wo_ref[0:half, :]) + _dot(a2_ref[...], wo_ref[half:2 * half, :])
    x1 = x_ref[...] + mix
    xn = _rmsnorm(x1, g_ref[...]).astype(BF16)
    acc = x1
    for c in range(wu_ref.shape[1] // FF_CHUNK):
        h = _dot(xn, wu_ref[:, c * FF_CHUNK:(c + 1) * FF_CHUNK])
        h = jnp.square(jnp.maximum(h, 0.0)).astype(BF16)
        acc = acc + _dot(h, wd_ref[c * FF_CHUNK:(c + 1) * FF_CHUNK, :])
    if final:
        acc = _rmsnorm(acc, gf_ref[...])
    if slab:
        _deinterleave_rows(acc, slab[0], o_ref)
    else:
        o_ref[...] = acc


def _out_mlp(x2d, a1, a2, wo, g, wu, wd, gf, *, final, batch_major_out=None):
    rows, d = x2d.shape
    half = d // 2
    consts = (wo, g, wu, wd, gf)
    tm, grid, blk = _row_tiles(rows)
    out_spec, out_shape, scratch = blk(d), jax.ShapeDtypeStruct((rows, d), F32), []
    if batch_major_out is not None:
        nb, s = batch_major_out
        out_spec = pl.BlockSpec((nb, tm // nb, d), lambda i: (0, i, 0))
        out_shape = jax.ShapeDtypeStruct((nb, s, d), F32)
        scratch = [pltpu.VMEM((d // LANES, tm, LANES), F32)]
    return pl.pallas_call(
        functools.partial(_out_mlp_body, final=final),
        grid=grid,
        in_specs=[blk(d), blk(half, 0), blk(half, a1.shape[1] // half - 1)] + [_resident(p) for p in consts],
        out_specs=out_spec,
        out_shape=out_shape,
        scratch_shapes=scratch,
        compiler_params=_params(1),
        name="out_mlp",
    )(x2d, a1, a2, *[_resident_array(p) for p in consts])


def _time_grid(total_rows, n_bg, tblock):
    steps = total_rows // (n_bg * SUBLANES)
    tb = min(tblock, steps)
    n_t = steps // tb
    blk = lambda width: pl.BlockSpec((tb * SUBLANES, width), lambda bg, t: (bg * n_t + t, 0))
    per_group = lambda rows, width: pl.BlockSpec((rows, width), lambda bg, t: (bg, 0))
    return tb, (n_bg, n_t), blk, per_group


def _gla_body(q_ref, k_ref, la_ref, v_ref, gt_ref, gh_ref, s0_ref, *rest, c, nh, dk, dv, transposed):
    o_ref, sfin_ref, st_ref = rest[-3:]
    tb = pl.program_id(1)
    rows = SUBLANES * c
    n_chunks = q_ref.shape[0] // rows
    dk_shift = int(math.log2(dk))
    heads_per_tile = LANES // dk

    @pl.when(tb == 0)
    def _():
        st_ref[...] = s0_ref[...]

    def iota(shape, axis):
        return lax.broadcasted_iota(jnp.int32, shape, axis)

    causal = ((iota((rows, rows), 0) & 7) == (iota((rows, rows), 1) & 7)) & (iota((rows, rows), 1) <= iota((rows, rows), 0))
    emask = (iota((rows, SUBLANES * dk), 0) & 7) == lax.shift_right_logical(iota((rows, SUBLANES * dk), 1), dk_shift)
    dmask = iota((SUBLANES, SUBLANES * dk), 0) == lax.shift_right_logical(iota((SUBLANES, SUBLANES * dk), 1), dk_shift)
    lo_rows = iota((rows, LANES), 1) < dk
    lo_8 = iota((SUBLANES, LANES), 1) < dk

    def expand(tile, h, lo=lo_rows, mask=emask):
        rolled = pltpu.roll(tile, dk, axis=1)
        d = jnp.where(lo, tile, rolled) if h % heads_per_tile == 0 else jnp.where(lo, rolled, tile)
        return jnp.where(mask, jnp.concatenate([d] * (SUBLANES * dk // LANES), axis=1), 0.0)

    def chunk(n, carry):
        r0 = n * rows
        q = _rows(q_ref, r0, rows)
        k = _rows(k_ref, r0, rows)
        la = _rows(la_ref, r0, rows)
        v = _rows(v_ref, r0, rows)
        gt = _rows(gt_ref, r0, rows)
        acc = jnp.zeros((SUBLANES, nh * dk), F32)
        pieces = []
        for t in range(c):
            acc = acc + la[t * SUBLANES:(t + 1) * SUBLANES, :]
            pieces.append(acc)
        cum = jnp.concatenate(pieces, axis=0)
        last = jnp.concatenate([acc] * c, axis=0)
        q_t = q * jnp.exp(cum) * (dk ** -0.5)
        k_t = k * jnp.exp(-cum)
        k_end = k * jnp.exp(last - cum)
        dec = jnp.exp(acc)
        pad = jnp.zeros((LANES - SUBLANES, LANES), F32)
        outs = []
        for h in range(nh):
            tile = (h // heads_per_tile) * LANES
            lane0 = (h % heads_per_tile) * dk
            sel = lo_rows if lane0 == 0 else jnp.logical_not(lo_rows)
            q_tile = q_t[:, tile:tile + LANES]
            scores = _dot_nt(jnp.where(sel, q_tile, 0.0).astype(BF16), k_t[:, tile:tile + LANES].astype(BF16))
            scores = jnp.where(causal, scores, 0.0).astype(BF16)
            v_h = v[:, h * dv:(h + 1) * dv]
            q_exp = expand(q_tile, h).astype(BF16)
            k_exp = expand(k_end[:, tile:tile + LANES], h).astype(BF16)
            if transposed:
                st = st_ref[h]
                o = _dot(scores, v_h) + _dot_nt(q_exp, st.astype(BF16))
                dec_row = jnp.sum(expand(dec[:, tile:tile + LANES], h, lo_8, dmask), axis=0, keepdims=True)
                st_ref[h] = st * dec_row + _dot_tn(v_h, k_exp)
            else:
                st = jnp.concatenate([st_ref[b, h] for b in range(SUBLANES)], axis=0)
                o = _dot(scores, v_h) + _dot(q_exp, st.astype(BF16))
                upd = _dot_tn(k_exp, v_h)
                dec_t = jnp.concatenate([dec[:, tile:tile + LANES], pad], axis=0).T
                for b in range(SUBLANES):
                    dec_b = jnp.broadcast_to(dec_t[lane0:lane0 + dk, b:b + 1], (dk, dv))
                    st_ref[b, h] = st[b * dk:(b + 1) * dk, :] * dec_b + upd[b * dk:(b + 1) * dk, :]
            o = _rmsnorm(o, gh_ref[:, h * dv:(h + 1) * dv])
            g_h = gt[:, h * dv:(h + 1) * dv]
            outs.append(o * (g_h * _sigmoid(g_h)))
        o_ref[pl.ds(pl.multiple_of(r0, rows), rows), :] = jnp.concatenate(outs, axis=1).astype(BF16)
        return carry

    lax.fori_loop(0, n_chunks, chunk, 0, unroll=min(n_chunks, 4))

    @pl.when(tb == pl.num_programs(1) - 1)
    def _():
        sfin_ref[...] = st_ref[...]


def _gla_state_in(s):
    b, h, dk, dv = s.shape
    return jnp.transpose(s.reshape(b // SUBLANES, SUBLANES, h, dk, dv), (0, 2, 4, 1, 3)).reshape(b // SUBLANES * h, dv, SUBLANES * dk)


def _gla_state_out(st, nh, dk):
    dv = st.shape[1]
    n_bg = st.shape[0] // nh
    return jnp.transpose(st.reshape(n_bg, nh, dv, SUBLANES, dk), (0, 3, 1, 4, 2)).reshape(n_bg * SUBLANES, nh, dk, dv)


def _gla(q, k, la, v, gt, gh, states, layer, n_layers, acc, nb, dims, *, c, tblock):
    nh, dk, dv = dims
    n_bg = nb // SUBLANES
    tb, grid, blk, _ = _time_grid(q.shape[0], n_bg, tblock)
    transposed = q.shape[0] // n_bg > SUBLANES * c
    body = functools.partial(_gla_body, c=c, nh=nh, dk=dk, dv=dv, transposed=transposed)
    row_specs = [blk(nh * dk), blk(nh * dk), blk(nh * dk), blk(nh * dv), blk(nh * dv), _const_spec(gh.shape)]
    og_shape = jax.ShapeDtypeStruct((q.shape[0], nh * dv), BF16)
    if transposed:
        st_block = (nh, dv, SUBLANES * dk)
        st_spec = pl.BlockSpec(st_block, lambda bg, t: (bg, 0, 0))
        s_in = jnp.zeros((n_bg * nh, dv, SUBLANES * dk), F32) if states is None else _gla_state_in(states[layer].astype(F32))
        og, s_fin = pl.pallas_call(
            body, grid=grid, in_specs=row_specs + [st_spec], out_specs=(blk(nh * dv), st_spec),
            out_shape=(og_shape, jax.ShapeDtypeStruct(s_in.shape, F32)), scratch_shapes=[pltpu.VMEM(st_block, F32)],
            compiler_params=_params(2), name="gla",
        )(q, k, la, v, gt, gh, s_in)
        s_new = _gla_state_out(s_fin, nh, dk)[None]
        return og, (jnp.pad(s_new, ((0, n_layers - 1),) + ((0, 0),) * 4) if acc is None else lax.dynamic_update_slice(acc, s_new, (layer, 0, 0, 0, 0)))
    st_block = (SUBLANES, nh, dk, dv)
    stacked_spec = pl.BlockSpec((None,) + st_block, lambda bg, t: (layer, bg, 0, 0, 0))
    if states is None:
        s_in, in_spec = jnp.zeros((nb, nh, dk, dv), F32), pl.BlockSpec(st_block, lambda bg, t: (bg, 0, 0, 0))
    else:
        s_in, in_spec = states.astype(F32), stacked_spec
    if acc is None:
        acc = jnp.zeros((n_layers, nb, nh, dk, dv), F32)
    return pl.pallas_call(
        body, grid=grid, in_specs=row_specs + [in_spec, pl.BlockSpec(memory_space=pl.ANY)], out_specs=(blk(nh * dv), stacked_spec),
        out_shape=(og_shape, jax.ShapeDtypeStruct(acc.shape, F32)), scratch_shapes=[pltpu.VMEM(st_block, F32)],
        input_output_aliases={7: 1}, compiler_params=_params(2), name="gla",
    )(q, k, la, v, gt, gh, s_in, acc)


def _s5_disc_body(a_re_ref, a_im_ref, log_dt_ref, b_re_ref, b_im_ref, ab_re_ref, ab_im_ref, bb_re_ref, bb_im_ref):
    a_re = a_re_ref[...]
    a_im = a_im_ref[...]
    dt = jnp.exp(log_dt_ref[...])
    mag = jnp.exp(dt * a_re)
    ab_re = mag * jnp.cos(dt * a_im)
    ab_im = mag * jnp.sin(dt * a_im)
    den = a_re * a_re + a_im * a_im
    coef_re = ((ab_re - 1.0) * a_re + ab_im * a_im) / den
    coef_im = (ab_im * a_re - (ab_re - 1.0) * a_im) / den
    b_re = b_re_ref[...]
    b_im = b_im_ref[...]
    ab_re_ref[...] = ab_re
    ab_im_ref[...] = ab_im
    bb_re_ref[...] = coef_re * b_re - coef_im * b_im
    bb_im_ref[...] = coef_re * b_im + coef_im * b_re


def _s5_discretise(a_re, a_im, log_dt, b_re, b_im):
    g, p, ch = b_re.shape
    rep = lambda t: jnp.repeat(t, ch, axis=0)
    flat = lambda t: jnp.transpose(t, (0, 2, 1)).reshape(g * ch, p)
    args = (rep(a_re), rep(a_im), rep(jnp.broadcast_to(log_dt[:, None], (g, p))), flat(b_re), flat(b_im))
    spec = _const_spec((g * ch, p))
    shape = jax.ShapeDtypeStruct((g * ch, p), F32)
    ab_re, ab_im, bb_re, bb_im = pl.pallas_call(
        _s5_disc_body, grid=(1,), in_specs=[spec] * 5, out_specs=(spec,) * 4, out_shape=(shape,) * 4,
        compiler_params=_params(1), name="s5_discretise",
    )(*args)
    return ab_re[::ch], ab_im[::ch], bb_re.reshape(g, ch, p), bb_im.reshape(g, ch, p)


def _block_diag_in(w_gcp):
    g, ch, p = w_gcp.shape
    gpt = LANES // ch
    w = w_gcp.reshape(g // gpt, gpt, ch, p)
    eye = jnp.eye(gpt, dtype=w.dtype)
    return jnp.einsum("ab,jacp->jacbp", eye, w).reshape(g // gpt, gpt * ch, gpt * p)


def _block_diag_out(w_gcp):
    g, ch, p = w_gcp.shape
    gpt = LANES // ch
    w = w_gcp.reshape(g // gpt, gpt, ch, p)
    eye = jnp.eye(gpt, dtype=w.dtype)
    return jnp.einsum("ab,jacp->japbc", eye, w).reshape(g // gpt, gpt * p, gpt * ch)


def _s5_body(u_ref, ab_re_ref, ab_im_ref, wb_re_ref, wb_im_ref, wc_ref, d_ref, wglu_ref, bglu_ref, h0_re_ref, h0_im_ref,
             z_ref, hf_re_ref, hf_im_ref, h_re_ref, h_im_ref, bu_re_ref, bu_im_ref, y_ref):
    tb = pl.program_id(1)
    rows = u_ref.shape[0]
    steps = rows // SUBLANES
    n_tiles = wb_re_ref.shape[0]
    sw = wb_re_ref.shape[2]

    @pl.when(tb == 0)
    def _():
        h_re_ref[...] = h0_re_ref[...]
        h_im_ref[...] = h0_im_ref[...]

    for j in range(n_tiles):
        u_j = u_ref[:, j * LANES:(j + 1) * LANES]
        u_b = u_j.astype(BF16)
        bu_re_ref[j] = _dot(u_b, wb_re_ref[j])
        bu_im_ref[j] = _dot(u_b, wb_im_ref[j])
        a_re = jnp.broadcast_to(ab_re_ref[:, j * sw:(j + 1) * sw], (SUBLANES, sw))
        a_im = jnp.broadcast_to(ab_im_ref[:, j * sw:(j + 1) * sw], (SUBLANES, sw))
        h_re = h_re_ref[:, j * sw:(j + 1) * sw]
        h_im = h_im_ref[:, j * sw:(j + 1) * sw]
        for t in range(steps):
            sl = slice(t * SUBLANES, (t + 1) * SUBLANES)
            h_re, h_im = (a_re * h_re - a_im * h_im + bu_re_ref[j, sl, :], a_re * h_im + a_im * h_re + bu_im_ref[j, sl, :])
            bu_re_ref[j, sl, :] = h_re
            bu_im_ref[j, sl, :] = h_im
        h_re_ref[:, j * sw:(j + 1) * sw] = h_re
        h_im_ref[:, j * sw:(j + 1) * sw] = h_im
        h_cat = jnp.concatenate([bu_re_ref[j].astype(BF16), bu_im_ref[j].astype(BF16)], axis=1)
        y_ref[:, j * LANES:(j + 1) * LANES] = _dot(h_cat, wc_ref[j]) + d_ref[:, j * LANES:(j + 1) * LANES] * u_j

    z = _gelu(y_ref[...])
    gate = _sigmoid(_dot(z.astype(BF16), wglu_ref[...]) + bglu_ref[...])
    z_ref[...] = (z * gate).astype(BF16)

    @pl.when(tb == pl.num_programs(1) - 1)
    def _():
        hf_re_ref[...] = h_re_ref[...]
        hf_im_ref[...] = h_im_ref[...]


def _s5(u, ab_re, ab_im, wb_re, wb_im, wc, d_skip, wglu, bglu, h0_re, h0_im, *, tblock):
    total, width = u.shape
    n_state = ab_re.shape[1]
    n_bg = h0_re.shape[0] // SUBLANES
    sw = wb_re.shape[2]
    tb, grid, blk, per_group = _time_grid(total, n_bg, tblock)
    rows = tb * SUBLANES
    st_spec = per_group(SUBLANES, n_state)
    consts = (ab_re, ab_im, wb_re, wb_im, wc, d_skip, wglu, bglu)
    st_shape = jax.ShapeDtypeStruct((n_bg * SUBLANES, n_state), F32)
    return pl.pallas_call(
        _s5_body,
        grid=grid,
        in_specs=[blk(width)] + [_resident(p) for p in consts] + [st_spec, st_spec],
        out_specs=(blk(width), st_spec, st_spec),
        out_shape=(jax.ShapeDtypeStruct((total, width), BF16), st_shape, st_shape),
        scratch_shapes=[pltpu.VMEM((SUBLANES, n_state), F32), pltpu.VMEM((SUBLANES, n_state), F32),
                        pltpu.VMEM((wb_re.shape[0], rows, sw), F32), pltpu.VMEM((wb_re.shape[0], rows, sw), F32),
                        pltpu.VMEM((rows, width), F32)],
        compiler_params=_params(2),
        name="s5",
    )(u, *[_resident_array(p) for p in consts], h0_re, h0_im)


def _lru_block(n, xbuf_ref, gate_ref, params, a_ref, b_ref, h, rows):
    cw_ref, cb_ref, wr_ref, br_ref, wi_ref, bi_ref, lam_ref = params
    bw = wr_ref.shape[1]
    cols = slice(n * bw, (n + 1) * bw)
    a_blk, b_blk = a_ref.at[n % 2], b_ref.at[n % 2]
    xc = cb_ref[:, cols] + xbuf_ref[0:rows, cols] * cw_ref[0:1, cols]
    for j in range(1, cw_ref.shape[0]):
        xc = xc + xbuf_ref[j * SUBLANES:j * SUBLANES + rows, cols] * cw_ref[j:j + 1, cols]
    xb = xc.astype(BF16)
    i = _sigmoid(_dot(xb, wi_ref[n]) + bi_ref[:, cols])
    half_rate = (-0.5 * LRU_C) * _softplus(-lam_ref[:, cols])
    log_a = jnp.tanh(0.5 * (_dot(xb, wr_ref[n]) + br_ref[:, cols])) * half_rate + half_rate
    a_blk[...] = jnp.exp(log_a)
    th = jnp.tanh(log_a)
    y = -2.0 * th / (1.0 - th)
    root = jnp.where(y > 0.0, y * lax.rsqrt(y), 0.0)
    b_blk[...] = root * (i * xc)
    for t in range(rows // SUBLANES):
        sl = slice(t * SUBLANES, (t + 1) * SUBLANES)
        h = a_blk[sl, :] * h + b_blk[sl, :]
        b_blk[sl, :] = h
    return (_gelu(gate_ref[:, cols]) * b_blk[...]).astype(BF16), h


def _lru_body(gate_ref, xin_ref, conv0_ref, h0_ref, cw_ref, cb_ref, wr_ref, br_ref, wi_ref, bi_ref, lam_ref,
              o_ref, hf_ref, convf_ref, xbuf_ref, a_ref, b_ref, h_ref):
    tb = pl.program_id(1)
    rows, width = xin_ref.shape
    taps = cw_ref.shape[0]
    tail = (taps - 1) * SUBLANES
    n_blocks = wr_ref.shape[0]
    bw = wr_ref.shape[1]

    @pl.when(tb == 0)
    def _():
        xbuf_ref[0:tail, :] = conv0_ref[...]
        h_ref[...] = h0_ref[...]

    xbuf_ref[tail:tail + rows, :] = xin_ref[...]
    params = (cw_ref, cb_ref, wr_ref, br_ref, wi_ref, bi_ref, lam_ref)
    for n in range(n_blocks):
        cols = slice(n * bw, (n + 1) * bw)
        o_ref[:, cols], h_ref[:, cols] = _lru_block(n, xbuf_ref, gate_ref, params, a_ref, b_ref, h_ref[:, cols], rows)
    xbuf_ref[0:tail, :] = xbuf_ref[rows:rows + tail, :]

    @pl.when(tb == pl.num_programs(1) - 1)
    def _():
        hf_ref[...] = h_ref[...]
        convf_ref[...] = xbuf_ref[0:tail, :]


def _lru(gate, xin, conv0, h0, cw, cb, wr, br, wi, bi, lam, *, tblock):
    total, width = xin.shape
    n_bg = h0.shape[0] // SUBLANES
    tail = (cw.shape[0] - 1) * SUBLANES
    bw = _resident_shape(wr)[1]
    tb, grid, blk, per_group = _time_grid(total, n_bg, tblock)
    rows = tb * SUBLANES
    h_spec = per_group(SUBLANES, width)
    c_spec = per_group(tail, width)
    consts = (cw, cb, wr, br, wi, bi, lam)
    return pl.pallas_call(
        _lru_body,
        grid=grid,
        in_specs=[blk(width), blk(width), c_spec, h_spec] + [_resident(p) for p in consts],
        out_specs=(blk(width), h_spec, c_spec),
        out_shape=(jax.ShapeDtypeStruct((total, width), BF16), jax.ShapeDtypeStruct((n_bg * SUBLANES, width), F32),
                   jax.ShapeDtypeStruct((n_bg * tail, width), F32)),
        scratch_shapes=[pltpu.VMEM((rows + tail, width), F32), pltpu.VMEM((2, rows, bw), F32),
                        pltpu.VMEM((2, rows, bw), F32), pltpu.VMEM((SUBLANES, width), F32)],
        compiler_params=_params(2),
        name="rg_lru",
    )(gate, xin, conv0, h0, *[_resident_array(p) for p in consts])


def _lru_mlp_body(gate_ref, xin_ref, conv0_ref, h0_ref, cw_ref, cb_ref, wr_ref, br_ref, wi_ref, bi_ref, lam_ref,
                  x_ref, wo_ref, g_ref, wu_ref, wd_ref, gf_ref, o_ref, hf_ref, convf_ref,
                  xbuf_ref, a_ref, b_ref, h_ref, hn_ref, gh_a, gh_b, *slab, final):
    s = pl.program_id(1)
    last = pl.num_programs(1) - 1
    rows, width = xin_ref.shape
    half = width // 2
    tail = (cw_ref.shape[0] - 1) * SUBLANES
    n_blocks = wr_ref.shape[0]
    bw = wr_ref.shape[1]
    n_ff = wu_ref.shape[1] // FF_CHUNK
    params = (cw_ref, cb_ref, wr_ref, br_ref, wi_ref, bi_ref, lam_ref)

    @pl.when(s == 0)
    def _():
        xbuf_ref[0:tail, :] = conv0_ref[...]
        h_ref[...] = h0_ref[...]
        gh_b[...] = jnp.zeros(gh_b.shape, BF16)

    xbuf_ref[tail:tail + rows, :] = xin_ref[...]

    def run(gh_in, gh_rec):
        mlp = {}

        def mlp_piece(i):
            if i == 0:
                mix = _dot(gh_rec[:, 0:half], wo_ref[0:half, :]) + _dot(gh_rec[:, half:width], wo_ref[half:width, :])
                mlp["acc"] = x_ref[...] + mix
                mlp["xn"] = _rmsnorm(mlp["acc"], g_ref[...]).astype(BF16)
            elif i % 2 == 1:
                c = (i - 1) // 2
                up = _dot(mlp["xn"], wu_ref[:, c * FF_CHUNK:(c + 1) * FF_CHUNK])
                mlp["h"] = jnp.square(jnp.maximum(up, 0.0)).astype(BF16)
            else:
                c = (i - 2) // 2
                mlp["acc"] = mlp["acc"] + _dot(mlp["h"], wd_ref[c * FF_CHUNK:(c + 1) * FF_CHUNK, :])

        n_pieces = 1 + 2 * n_ff
        for n in range(max(n_blocks, n_pieces)):
            if n < n_blocks:
                cols = slice(n * bw, (n + 1) * bw)
                gh_in[:, cols], hn_ref[:, cols] = _lru_block(n, xbuf_ref, gate_ref, params, a_ref, b_ref, h_ref[:, cols], rows)
            if n < n_pieces:
                mlp_piece(n)
        y = _rmsnorm(mlp["acc"], gf_ref[...]) if final else mlp["acc"]
        if slab:
            _deinterleave_rows(y, slab[0], o_ref)
        else:
            o_ref[...] = y

    pl.when(lax.rem(s, 2) == 0)(lambda: run(gh_a, gh_b))
    pl.when(lax.rem(s, 2) == 1)(lambda: run(gh_b, gh_a))

    @pl.when(s == last - 1)
    def _():
        hf_ref[...] = hn_ref[...]
        convf_ref[...] = xbuf_ref[rows:rows + tail, :]

    @pl.when(s < last)
    def _():
        h_ref[...] = hn_ref[...]
        xbuf_ref[0:tail, :] = xbuf_ref[rows:rows + tail, :]


def _lru_mlp(gate, xin, conv0, h0, lru_consts, x2d, mlp_consts, *, final, batch_major_out):
    total, width = xin.shape
    d = x2d.shape[1]
    cw, wr = lru_consts[0], lru_consts[2]
    tail = (cw.shape[0] - 1) * SUBLANES
    bw = _resident_shape(wr)[1]
    rows = ROW_TILE
    n_t = total // rows
    lead = lambda width_: pl.BlockSpec((rows, width_), lambda bg, s: (jnp.minimum(s, n_t - 1), 0))
    lag = lambda width_: pl.BlockSpec((rows, width_), lambda bg, s: (jnp.maximum(s - 1, 0), 0))
    whole = lambda r, width_: pl.BlockSpec((r, width_), lambda bg, s: (0, 0))
    out_spec, out_shape, scratch = lag(d), jax.ShapeDtypeStruct((total, d), F32), []
    if batch_major_out is not None:
        nb, steps = batch_major_out
        out_spec = pl.BlockSpec((nb, rows // nb, d), lambda bg, s: (0, jnp.maximum(s - 1, 0), 0))
        out_shape = jax.ShapeDtypeStruct((nb, steps, d), F32)
        scratch = [pltpu.VMEM((d // LANES, rows, LANES), F32)]
    in_specs = [lead(width), lead(width), whole(tail, width), whole(SUBLANES, width)]
    in_specs += [_resident(p) for p in lru_consts] + [lag(d)] + [_resident(p) for p in mlp_consts]
    return pl.pallas_call(
        functools.partial(_lru_mlp_body, final=final),
        grid=(1, n_t + 1),
        in_specs=in_specs,
        out_specs=(out_spec, whole(SUBLANES, width), whole(tail, width)),
        out_shape=(out_shape, jax.ShapeDtypeStruct((SUBLANES, width), F32), jax.ShapeDtypeStruct((tail, width), F32)),
        scratch_shapes=[pltpu.VMEM((rows + tail, width), F32), pltpu.VMEM((2, rows, bw), F32), pltpu.VMEM((2, rows, bw), F32),
                        pltpu.VMEM((SUBLANES, width), F32), pltpu.VMEM((SUBLANES, width), F32),
                        pltpu.VMEM((rows, width), BF16), pltpu.VMEM((rows, width), BF16)] + scratch,
        compiler_params=pltpu.CompilerParams(dimension_semantics=("arbitrary", "arbitrary"), vmem_limit_bytes=FUSED_VMEM_LIMIT),
        name="lru_mlp",
    )(gate, xin, conv0, h0, *[_resident_array(p) for p in lru_consts], x2d, *[_resident_array(p) for p in mlp_consts])


def _row(t):
    return t.reshape(1, -1).astype(F32)


def _to_rows(x):
    b, s, d = x.shape
    return jnp.transpose(x.reshape(b // SUBLANES, SUBLANES, s, d), (0, 2, 1, 3)).reshape(b * s, d)


def _from_rows(y, b, s):
    d = y.shape[-1]
    return jnp.transpose(y.reshape(b // SUBLANES, s, SUBLANES, d), (0, 2, 1, 3)).reshape(b, s, d)


def _even_layer(x, nb, gla_states, gla_acc, e, n_even, s_re, s_im, prm, mlp, *, final, batch_major_out):
    (g_norm, w_in, w_u, w_a, wa2, ba2, g_head, ab_re, ab_im, wb_re, wb_im, wc, d_skip, wglu, bglu, w_out, dims) = prm
    nh, dk, dv = dims
    q, k, v, gt, u, la, *reordered = _inproj_even(x, g_norm, w_in, w_u, w_a, wa2, ba2, kw=nh * dk, vw=nh * dv)
    x2d = reordered[0] if reordered else x
    steps = x2d.shape[0] // nb
    og, gla_acc = _gla(q, k, la, v, gt, g_head, gla_states, e, n_even, gla_acc, nb, dims, c=math.gcd(steps, GLA_CHUNK), tblock=128)
    n_state = ab_re.shape[1]
    h0_re = jnp.zeros((nb, n_state), F32) if s_re is None else s_re.astype(F32).reshape(nb, n_state)
    h0_im = jnp.zeros((nb, n_state), F32) if s_im is None else s_im.astype(F32).reshape(nb, n_state)
    zz, hf_re, hf_im = _s5(u, ab_re, ab_im, wb_re, wb_im, wc, d_skip, wglu, bglu, h0_re, h0_im, tblock=64)
    g_mlp, w_up, w_down, g_fin = mlp
    y = _out_mlp(x2d, og, zz, w_out, g_mlp, w_up, w_down, g_fin, final=final, batch_major_out=batch_major_out)
    return y, gla_acc, hf_re, hf_im


def _odd_layer(x2d, nb, s_lru, s_conv, prm, mlp, *, final, batch_major_out):
    (g_norm, w_in, cw, cb, wr, br, wi, bi, lam, w_out) = prm
    n_bg = nb // SUBLANES
    taps = cw.shape[0]
    width = _resident_shape(w_in)[1] // 2
    gate, xin = _inproj_odd(x2d, g_norm, w_in)
    h0 = jnp.zeros((nb, width), F32) if s_lru is None else s_lru.astype(F32)
    if s_conv is None:
        conv0 = jnp.zeros((n_bg * (taps - 1) * SUBLANES, width), F32)
    else:
        conv0 = jnp.transpose(s_conv.astype(F32).reshape(n_bg, SUBLANES, taps - 1, width), (0, 2, 1, 3)).reshape(-1, width)
    g_mlp, w_up, w_down, g_fin = mlp
    if n_bg == 1 and x2d.shape[0] > ROW_TILE and x2d.shape[0] % ROW_TILE == 0:
        y, hf, convf = _lru_mlp(gate, xin, conv0, h0, (cw, cb, wr, br, wi, bi, lam), x2d, (w_out, g_mlp, w_up, w_down, g_fin),
                                final=final, batch_major_out=batch_major_out)
    else:
        gh, hf, convf = _lru(gate, xin, conv0, h0, cw, cb, wr, br, wi, bi, lam, tblock=64)
        y = _out_mlp(x2d, gh, gh, w_out, g_mlp, w_up, w_down, g_fin, final=final, batch_major_out=batch_major_out)
    conv_new = jnp.transpose(convf.reshape(n_bg, taps - 1, SUBLANES, width), (0, 2, 1, 3)).reshape(nb, taps - 1, width)
    return y, hf, conv_new


def kernel(x_prompt, x_sample, state_gla, state_s5_re, state_s5_im, state_lru, state_conv, norm_mix_even, w_in_even, gla_w_a2, gla_b_a2, gla_norm, s5_a_re, s5_a_im, s5_log_dt, s5_b_re, s5_b_im, s5_c_re, s5_c_im, s5_d, s5_w_glu, s5_b_glu, w_out_even, norm_mix_odd, w_in_odd, conv_w, conv_b, lru_w_r, lru_b_r, lru_w_i, lru_b_i, lru_lam, w_out_odd, norm_mlp, w_up, w_down, norm_final):
    depth = norm_mlp.shape[0]
    nh, dk, dv = state_gla.shape[2:]
    rank = gla_w_a2.shape[1]
    n_groups, n_p, ch = s5_b_re.shape[1:]
    kw, vw = nh * dk, nh * dv
    assert rank <= LANES and LANES % ch == 0 and LANES % dk == 0 and dv == LANES
    bp, sp, _ = x_prompt.shape
    bs, ss, _ = x_sample.shape
    n_even = state_gla.shape[0]

    in_kernel_order = bp == SUBLANES and sp % (INPROJ_TILE // SUBLANES) == 0 and depth > 1
    yp = x_prompt.astype(F32) if in_kernel_order else _to_rows(x_prompt.astype(F32))
    ys = _to_rows(x_sample.astype(F32))
    g_fin = _row(norm_final)
    o4 = 2 * kw + 2 * vw
    o5 = o4 + rank
    w_in_main = w_in_even[:, :, :o4].astype(BF16)
    w_in_u = w_in_even[:, :, o5:].astype(BF16)
    w_in_a = jnp.pad(w_in_even[:, :, o4:o5], ((0, 0), (0, 0), (0, LANES - rank))).astype(BF16)
    w_a2 = jnp.pad(gla_w_a2, ((0, 0), (0, LANES - rank), (0, 0))).astype(BF16)
    w_up_b, w_down_b = w_up.astype(BF16), w_down.astype(BF16)
    w_out_even_b, w_out_odd_b, w_in_odd_b = w_out_even.astype(BF16), w_out_odd.astype(BF16), w_in_odd.astype(BF16)
    w_glu_b, w_r_b, w_i_b = s5_w_glu.astype(BF16), lru_w_r.astype(BF16), lru_w_i.astype(BF16)
    keys = ("re", "im", "lru", "conv")
    outs_p = {key: [] for key in keys}
    outs_s = {key: [] for key in keys}
    gla_p = gla_s = None

    def record(outs, sr, si):
        outs["re"].append(sr.reshape(-1, n_groups, n_p))
        outs["im"].append(si.reshape(-1, n_groups, n_p))

    for layer in range(depth):
        final = layer == depth - 1
        bm_out = (bp, sp) if final and in_kernel_order else None
        mlp = (_row(norm_mlp[layer]), (w_up_b, layer), (w_down_b, layer), g_fin)
        if layer % 2 == 0:
            e = layer // 2
            ab_re, ab_im, bb_re, bb_im = _s5_discretise(s5_a_re[e].astype(F32), s5_a_im[e].astype(F32), s5_log_dt[e].astype(F32),
                                                        s5_b_re[e].astype(F32), s5_b_im[e].astype(F32))
            wc = jnp.concatenate([_block_diag_out(s5_c_re[e].astype(F32)), -_block_diag_out(s5_c_im[e].astype(F32))], axis=1)
            prm = (_row(norm_mix_even[e]), (w_in_main, e), (w_in_u, e), (w_in_a, e), (w_a2, e), _row(gla_b_a2[e]), _row(gla_norm[e]),
                   ab_re.reshape(1, -1), ab_im.reshape(1, -1), _block_diag_in(bb_re).astype(BF16), _block_diag_in(bb_im).astype(BF16),
                   wc.astype(BF16), _row(s5_d[e]), (w_glu_b, e), _row(s5_b_glu[e]), (w_out_even_b, e), (nh, dk, dv))
            yp, gla_p, sr, si = _even_layer(yp, bp, None, gla_p, e, n_even, None, None, prm, mlp, final=final, batch_major_out=bm_out)
            record(outs_p, sr, si)
            ys, gla_s, sr, si = _even_layer(ys, bs, state_gla, gla_s, e, n_even, state_s5_re[e], state_s5_im[e], prm, mlp, final=final,
                                            batch_major_out=None)
            record(outs_s, sr, si)
        else:
            o = layer // 2
            prm = (_row(norm_mix_odd[o]), (w_in_odd_b, o), conv_w[o].astype(F32), _row(conv_b[o]),
                   (w_r_b, o), _row(lru_b_r[o]), (w_i_b, o), _row(lru_b_i[o]), _row(lru_lam[o]), (w_out_odd_b, o))
            yp, sl, sc = _odd_layer(yp, bp, None, None, prm, mlp, final=final, batch_major_out=bm_out)
            outs_p["lru"].append(sl)
            outs_p["conv"].append(sc)
            ys, sl, sc = _odd_layer(ys, bs, state_lru[o], state_conv[o], prm, mlp, final=final, batch_major_out=None)
            outs_s["lru"].append(sl)
            outs_s["conv"].append(sc)

    stack = lambda outs: tuple(jnp.stack(outs[key]) for key in keys)
    return (yp if in_kernel_order else _from_rows(yp, bp, sp), _from_rows(ys, bs, ss), gla_p) + stack(outs_p) + (gla_s,) + stack(outs_s)
```

```python
import functools
import math

import jax
import jax.numpy as jnp
from jax import lax
from jax.experimental import pallas as pl
from jax.experimental.pallas import tpu as pltpu

F32 = jnp.float32
BF16 = jnp.bfloat16

NORM_EPS = 1e-6
GLA_TAU = 16.0
GLA_CHUNK = 16
LRU_C = 8.0

SUBLANES = 8
LANES = 128
ROW_TILE = 512
INPROJ_TILE = 1024
FF_CHUNK = 1024
VMEM_LIMIT = 48 * 1024 * 1024


def _params(n_axes):
    return pltpu.CompilerParams(dimension_semantics=("arbitrary",) * n_axes, vmem_limit_bytes=VMEM_LIMIT)


def _const_spec(shape):
    zeros = (0,) * len(shape)
    return pl.BlockSpec(shape, lambda *_: zeros, pipeline_mode=pl.Buffered(1))


def _resident(p):
    if not isinstance(p, tuple):
        return _const_spec(p.shape)
    stacked, layer = p
    index = (layer,) + (0,) * (stacked.ndim - 1)
    return pl.BlockSpec((None,) + stacked.shape[1:], lambda *_: index, pipeline_mode=pl.Buffered(1))


def _resident_array(p):
    return p[0] if isinstance(p, tuple) else p


def _resident_shape(p):
    return p[0].shape[1:] if isinstance(p, tuple) else p.shape


def _dot(a, b):
    return jnp.dot(a, b, preferred_element_type=F32)


def _dot_nt(a, b):
    return lax.dot_general(a, b, (((1,), (1,)), ((), ())), preferred_element_type=F32)


def _dot_tn(a, b):
    return lax.dot_general(a, b, (((0,), (0,)), ((), ())), preferred_element_type=F32)


def _rmsnorm(x, g):
    return x * lax.rsqrt(jnp.mean(x * x, axis=-1, keepdims=True) + NORM_EPS) * g


def _log_sigmoid(z):
    return jnp.minimum(z, 0.0) - jnp.log1p(jnp.exp(-jnp.abs(z)))


def _softplus(z):
    return jnp.maximum(z, 0.0) + jnp.log1p(jnp.exp(-jnp.abs(z)))


_GELU_C1 = math.sqrt(2.0 / math.pi)
_GELU_C2 = _GELU_C1 * 0.044715


def _gelu(x):
    half = 0.5 * x
    return half + half * jnp.tanh(x * (_GELU_C1 + _GELU_C2 * (x * x)))


def _sigmoid(x):
    return 0.5 * jnp.tanh(0.5 * x) + 0.5


def _rows(ref, start, size):
    return ref[pl.ds(pl.multiple_of(start, size), size), :]


def _interleave_rows(x_ref, slab_ref):
    nb, ts, d = x_ref.shape
    for b in range(nb):
        for l in range(d // LANES):
            slab_ref[l, pl.ds(b, ts, stride=nb), :] = x_ref[b, :, l * LANES:(l + 1) * LANES]
    return jnp.concatenate([slab_ref[l] for l in range(d // LANES)], axis=1)


def _deinterleave_rows(y, slab_ref, o_ref):
    nb, ts, d = o_ref.shape
    for l in range(d // LANES):
        slab_ref[l] = y[:, l * LANES:(l + 1) * LANES]
    for b in range(nb):
        for l in range(d // LANES):
            o_ref[b, :, l * LANES:(l + 1) * LANES] = slab_ref[l, pl.ds(b, ts, stride=nb), :]


def _row_tiles(rows, tile=ROW_TILE):
    tm = tile if rows % tile == 0 else min(ROW_TILE, rows)
    return tm, (rows // tm,), lambda width, col=0: pl.BlockSpec((tm, width), lambda i: (i, col))


def _inproj_even_body(x_ref, g_ref, w_ref, wu_ref, wa_ref, wa2_ref, ba2_ref, q_ref, k_ref, v_ref, gt_ref, u_ref, la_ref, *rest,
                      kw, vw):
    if rest:
        xt_ref, slab_ref = rest
        x = _interleave_rows(x_ref, slab_ref)
        xt_ref[...] = x
    else:
        x = x_ref[...]
    xn = _rmsnorm(x, g_ref[...]).astype(BF16)
    o1, o2 = kw, 2 * kw
    o3 = o2 + vw
    o4 = o3 + vw
    q_ref[...] = _dot(xn, w_ref[:, 0:o1])
    k_ref[...] = _dot(xn, w_ref[:, o1:o2])
    v_ref[...] = _dot(xn, w_ref[:, o2:o3]).astype(BF16)
    gt_ref[...] = _dot(xn, w_ref[:, o3:o4])
    u_ref[...] = _dot(xn, wu_ref[...])
    a_lr = _dot(xn, wa_ref[...]).astype(BF16)
    z = _dot(a_lr, wa2_ref[...]) + ba2_ref[...]
    la_ref[...] = _log_sigmoid(z) * (1.0 / GLA_TAU)


def _inproj_even(x, g, w, wu, wa, wa2, ba2, *, kw, vw):
    batch_major = x.ndim == 3
    d = x.shape[-1]
    rows = x.size // d
    sw = _resident_shape(wu)[1]
    widths = [kw, kw, vw, vw, sw, kw]
    dtypes = [F32, F32, BF16, F32, F32, F32]
    consts = (g, w, wu, wa, wa2, ba2)
    tm, grid, blk = _row_tiles(rows, INPROJ_TILE)
    x_spec, scratch = blk(d), []
    if batch_major:
        nb = x.shape[0]
        x_spec = pl.BlockSpec((nb, tm // nb, d), lambda i: (0, i, 0))
        widths.append(d)
        dtypes.append(F32)
        scratch = [pltpu.VMEM((d // LANES, tm, LANES), F32)]
    return pl.pallas_call(
        functools.partial(_inproj_even_body, kw=kw, vw=vw),
        grid=grid,
        in_specs=[x_spec] + [_resident(p) for p in consts],
        out_specs=tuple(blk(wd) for wd in widths),
        out_shape=tuple(jax.ShapeDtypeStruct((rows, wd), dt) for wd, dt in zip(widths, dtypes)),
        scratch_shapes=scratch,
        compiler_params=_params(1),
        name="inproj_even",
    )(x, *[_resident_array(p) for p in consts])


def _inproj_odd_body(x_ref, g_ref, w_ref, gate_ref, xin_ref, *, width):
    xn = _rmsnorm(x_ref[...], g_ref[...]).astype(BF16)
    gate_ref[...] = _dot(xn, w_ref[:, 0:width])
    xin_ref[...] = _dot(xn, w_ref[:, width:2 * width])


def _inproj_odd(x2d, g, w):
    rows, d = x2d.shape
    width = _resident_shape(w)[1] // 2
    _, grid, blk = _row_tiles(rows, INPROJ_TILE)
    shape = jax.ShapeDtypeStruct((rows, width), F32)
    return pl.pallas_call(
        functools.partial(_inproj_odd_body, width=width),
        grid=grid,
        in_specs=[blk(d), _resident(g), _resident(w)],
        out_specs=(blk(width), blk(width)),
        out_shape=(shape, shape),
        compiler_params=_params(1),
        name="inproj_odd",
    )(x2d, _resident_array(g), _resident_array(w))


def _out_mlp_body(x_ref, a1_ref, a2_ref, wo_ref, g_ref, wu_ref, wd_ref, gf_ref, o_ref, *slab, final):
    half = a1_ref.shape[1]
    mix = _dot(a1_ref[...], wo_ref[0:half, :]) + _dot(a2_ref[...], wo_ref[half:2 * half, :])
    x1 = x_ref[...] + mix
    xn = _rmsnorm(x1, g_ref[...]).astype(BF16)
    acc = x1
    for c in range(wu_ref.shape[1] // FF_CHUNK):
        h = _dot(xn, wu_ref[:, c * FF_CHUNK:(c + 1) * FF_CHUNK])
        h = jnp.square(jnp.maximum(h, 0.0)).astype(BF16)
        acc = acc + _dot(h, wd_ref[c * FF_CHUNK:(c + 1) * FF_CHUNK, :])
    if final:
        acc = _rmsnorm(acc, gf_ref[...])
    if slab:
        _deinterleave_rows(acc, slab[0], o_ref)
    else:
        o_ref[...] = acc


def _out_mlp(x2d, a1, a2, wo, g, wu, wd, gf, *, final, batch_major_out=None):
    rows, d = x2d.shape
    half = d // 2
    consts = (wo, g, wu, wd, gf)
    tm, grid, blk = _row_tiles(rows)
    out_spec, out_shape, scratch = blk(d), jax.ShapeDtypeStruct((rows, d), F32), []
    if batch_major_out is not None:
        nb, s = batch_major_out
        out_spec = pl.BlockSpec((nb, tm // nb, d), lambda i: (0, i, 0))
        out_shape = jax.ShapeDtypeStruct((nb, s, d), F32)
        scratch = [pltpu.VMEM((d // LANES, tm, LANES), F32)]
    return pl.pallas_call(
        functools.partial(_out_mlp_body, final=final),
        grid=grid,
        in_specs=[blk(d), blk(half, 0), blk(half, a1.shape[1] // half - 1)] + [_resident(p) for p in consts],
        out_specs=out_spec,
        out_shape=out_shape,
        scratch_shapes=scratch,
        compiler_params=_params(1),
        name="out_mlp",
    )(x2d, a1, a2, *[_resident_array(p) for p in consts])


def _time_grid(total_rows, n_bg, tblock, pack_groups=False):
    steps = total_rows // (n_bg * SUBLANES)
    tb = min(tblock, steps)
    n_t = steps // tb
    groups = 1
    if pack_groups and n_t == 1:
        groups = max(1, min(n_bg, ROW_TILE // (steps * SUBLANES)))
        while n_bg % groups:
            groups -= 1
    blk = lambda width: pl.BlockSpec((groups * tb * SUBLANES, width), lambda bg, t: (bg * n_t + t, 0))
    per_group = lambda rows, width: pl.BlockSpec((groups * rows, width), lambda bg, t: (bg, 0))
    return tb, groups, (n_bg // groups, n_t), blk, per_group


def _gla_body(q_ref, k_ref, la_ref, v_ref, gt_ref, gh_ref, s0_ref, *rest, c, nh, dk, dv, transposed):
    o_ref, sfin_ref, st_ref = rest[-3:]
    tb = pl.program_id(1)
    rows = SUBLANES * c
    n_chunks = q_ref.shape[0] // rows
    dk_shift = int(math.log2(dk))
    heads_per_tile = LANES // dk

    @pl.when(tb == 0)
    def _():
        st_ref[...] = s0_ref[...]

    def iota(shape, axis):
        return lax.broadcasted_iota(jnp.int32, shape, axis)

    causal = ((iota((rows, rows), 0) & 7) == (iota((rows, rows), 1) & 7)) & (iota((rows, rows), 1) <= iota((rows, rows), 0))
    emask = (iota((rows, SUBLANES * dk), 0) & 7) == lax.shift_right_logical(iota((rows, SUBLANES * dk), 1), dk_shift)
    dmask = iota((SUBLANES, SUBLANES * dk), 0) == lax.shift_right_logical(iota((SUBLANES, SUBLANES * dk), 1), dk_shift)
    lo_rows = iota((rows, LANES), 1) < dk
    lo_8 = iota((SUBLANES, LANES), 1) < dk

    def expand(tile, h, lo=lo_rows, mask=emask):
        rolled = pltpu.roll(tile, dk, axis=1)
        d = jnp.where(lo, tile, rolled) if h % heads_per_tile == 0 else jnp.where(lo, rolled, tile)
        return jnp.where(mask, jnp.concatenate([d] * (SUBLANES * dk // LANES), axis=1), 0.0)

    def chunk(n, carry):
        r0 = n * rows
        q = _rows(q_ref, r0, rows)
        k = _rows(k_ref, r0, rows)
        la = _rows(la_ref, r0, rows)
        v = _rows(v_ref, r0, rows)
        gt = _rows(gt_ref, r0, rows)
        acc = jnp.zeros((SUBLANES, nh * dk), F32)
        pieces = []
        for t in range(c):
            acc = acc + la[t * SUBLANES:(t + 1) * SUBLANES, :]
            pieces.append(acc)
        cum = jnp.concatenate(pieces, axis=0)
        last = jnp.concatenate([acc] * c, axis=0)
        q_t = q * jnp.exp(cum) * (dk ** -0.5)
        k_t = k * jnp.exp(-cum)
        k_end = k * jnp.exp(last - cum)
        dec = jnp.exp(acc)
        pad = jnp.zeros((LANES - SUBLANES, LANES), F32)
        outs = []
        for h in range(nh):
            tile = (h // heads_per_tile) * LANES
            lane0 = (h % heads_per_tile) * dk
            sel = lo_rows if lane0 == 0 else jnp.logical_not(lo_rows)
            q_tile = q_t[:, tile:tile + LANES]
            scores = _dot_nt(jnp.where(sel, q_tile, 0.0).astype(BF16), k_t[:, tile:tile + LANES].astype(BF16))
            scores = jnp.where(causal, scores, 0.0).astype(BF16)
            v_h = v[:, h * dv:(h + 1) * dv]
            q_exp = expand(q_tile, h).astype(BF16)
            k_exp = expand(k_end[:, tile:tile + LANES], h).astype(BF16)
            if transposed:
                st = st_ref[h]
                o = _dot(scores, v_h) + _dot_nt(q_exp, st.astype(BF16))
                dec_row = jnp.sum(expand(dec[:, tile:tile + LANES], h, lo_8, dmask), axis=0, keepdims=True)
                st_ref[h] = st * dec_row + _dot_tn(v_h, k_exp)
            else:
                st = jnp.concatenate([st_ref[b, h] for b in range(SUBLANES)], axis=0)
                o = _dot(scores, v_h) + _dot(q_exp, st.astype(BF16))
                upd = _dot_tn(k_exp, v_h)
                dec_t = jnp.concatenate([dec[:, tile:tile + LANES], pad], axis=0).T
                for b in range(SUBLANES):
                    dec_b = jnp.broadcast_to(dec_t[lane0:lane0 + dk, b:b + 1], (dk, dv))
                    st_ref[b, h] = st[b * dk:(b + 1) * dk, :] * dec_b + upd[b * dk:(b + 1) * dk, :]
            o = _rmsnorm(o, gh_ref[:, h * dv:(h + 1) * dv])
            g_h = gt[:, h * dv:(h + 1) * dv]
            outs.append(o * (g_h * _sigmoid(g_h)))
        o_ref[pl.ds(pl.multiple_of(r0, rows), rows), :] = jnp.concatenate(outs, axis=1).astype(BF16)
        return carry

    lax.fori_loop(0, n_chunks, chunk, 0, unroll=min(n_chunks, 4))

    @pl.when(tb == pl.num_programs(1) - 1)
    def _():
        sfin_ref[...] = st_ref[...]


def _gla_state_in(s):
    b, h, dk, dv = s.shape
    return jnp.transpose(s.reshape(b // SUBLANES, SUBLANES, h, dk, dv), (0, 2, 4, 1, 3)).reshape(b // SUBLANES * h, dv, SUBLANES * dk)


def _gla_state_out(st, nh, dk):
    dv = st.shape[1]
    n_bg = st.shape[0] // nh
    return jnp.transpose(st.reshape(n_bg, nh, dv, SUBLANES, dk), (0, 3, 1, 4, 2)).reshape(n_bg * SUBLANES, nh, dk, dv)


def _gla(q, k, la, v, gt, gh, states, layer, n_layers, acc, nb, dims, *, c, tblock):
    nh, dk, dv = dims
    n_bg = nb // SUBLANES
    tb, _, grid, blk, _ = _time_grid(q.shape[0], n_bg, tblock)
    transposed = q.shape[0] // n_bg > SUBLANES * c
    body = functools.partial(_gla_body, c=c, nh=nh, dk=dk, dv=dv, transposed=transposed)
    row_specs = [blk(nh * dk), blk(nh * dk), blk(nh * dk), blk(nh * dv), blk(nh * dv), _const_spec(gh.shape)]
    og_shape = jax.ShapeDtypeStruct((q.shape[0], nh * dv), BF16)
    if transposed:
        st_block = (nh, dv, SUBLANES * dk)
        st_spec = pl.BlockSpec(st_block, lambda bg, t: (bg, 0, 0))
        s_in = jnp.zeros((n_bg * nh, dv, SUBLANES * dk), F32) if states is None else _gla_state_in(states[layer].astype(F32))
        og, s_fin = pl.pallas_call(
            body, grid=grid, in_specs=row_specs + [st_spec], out_specs=(blk(nh * dv), st_spec),
            out_shape=(og_shape, jax.ShapeDtypeStruct(s_in.shape, F32)), scratch_shapes=[pltpu.VMEM(st_block, F32)],
            compiler_params=_params(2), name="gla",
        )(q, k, la, v, gt, gh, s_in)
        s_new = _gla_state_out(s_fin, nh, dk)[None]
        return og, (jnp.pad(s_new, ((0, n_layers - 1),) + ((0, 0),) * 4) if acc is None else lax.dynamic_update_slice(acc, s_new, (layer, 0, 0, 0, 0)))
    st_block = (SUBLANES, nh, dk, dv)
    stacked_spec = pl.BlockSpec((None,) + st_block, lambda bg, t: (layer, bg, 0, 0, 0))
    if states is None:
        s_in, in_spec = jnp.zeros((nb, nh, dk, dv), F32), pl.BlockSpec(st_block, lambda bg, t: (bg, 0, 0, 0))
    else:
        s_in, in_spec = states.astype(F32), stacked_spec
    if acc is None:
        acc = jnp.zeros((n_layers, nb, nh, dk, dv), F32)
    return pl.pallas_call(
        body, grid=grid, in_specs=row_specs + [in_spec, pl.BlockSpec(memory_space=pl.ANY)], out_specs=(blk(nh * dv), stacked_spec),
        out_shape=(og_shape, jax.ShapeDtypeStruct(acc.shape, F32)), scratch_shapes=[pltpu.VMEM(st_block, F32)],
        input_output_aliases={7: 1}, compiler_params=_params(2), name="gla",
    )(q, k, la, v, gt, gh, s_in, acc)


def _s5_disc_body(a_re_ref, a_im_ref, log_dt_ref, b_re_ref, b_im_ref, ab_re_ref, ab_im_ref, bb_re_ref, bb_im_ref):
    a_re = a_re_ref[...]
    a_im = a_im_ref[...]
    dt = jnp.exp(log_dt_ref[...])
    mag = jnp.exp(dt * a_re)
    ab_re = mag * jnp.cos(dt * a_im)
    ab_im = mag * jnp.sin(dt * a_im)
    den = a_re * a_re + a_im * a_im
    coef_re = ((ab_re - 1.0) * a_re + ab_im * a_im) / den
    coef_im = (ab_im * a_re - (ab_re - 1.0) * a_im) / den
    b_re = b_re_ref[...]
    b_im = b_im_ref[...]
    ab_re_ref[...] = ab_re
    ab_im_ref[...] = ab_im
    bb_re_ref[...] = coef_re * b_re - coef_im * b_im
    bb_im_ref[...] = coef_re * b_im + coef_im * b_re


def _s5_discretise(a_re, a_im, log_dt, b_re, b_im):
    g, p, ch = b_re.shape
    rep = lambda t: jnp.repeat(t, ch, axis=0)
    flat = lambda t: jnp.transpose(t, (0, 2, 1)).reshape(g * ch, p)
    args = (rep(a_re), rep(a_im), rep(jnp.broadcast_to(log_dt[:, None], (g, p))), flat(b_re), flat(b_im))
    spec = _const_spec((g * ch, p))
    shape = jax.ShapeDtypeStruct((g * ch, p), F32)
    ab_re, ab_im, bb_re, bb_im = pl.pallas_call(
        _s5_disc_body, grid=(1,), in_specs=[spec] * 5, out_specs=(spec,) * 4, out_shape=(shape,) * 4,
        compiler_params=_params(1), name="s5_discretise",
    )(*args)
    return ab_re[::ch], ab_im[::ch], bb_re.reshape(g, ch, p), bb_im.reshape(g, ch, p)


def _block_diag_in(w_gcp):
    g, ch, p = w_gcp.shape
    gpt = LANES // ch
    w = w_gcp.reshape(g // gpt, gpt, ch, p)
    eye = jnp.eye(gpt, dtype=w.dtype)
    return jnp.einsum("ab,jacp->jacbp", eye, w).reshape(g // gpt, gpt * ch, gpt * p)


def _block_diag_out(w_gcp):
    g, ch, p = w_gcp.shape
    gpt = LANES // ch
    w = w_gcp.reshape(g // gpt, gpt, ch, p)
    eye = jnp.eye(gpt, dtype=w.dtype)
    return jnp.einsum("ab,jacp->japbc", eye, w).reshape(g // gpt, gpt * p, gpt * ch)


def _s5_body(u_ref, ab_re_ref, ab_im_ref, wb_re_ref, wb_im_ref, wc_ref, d_ref, wglu_ref, bglu_ref, h0_re_ref, h0_im_ref,
             z_ref, hf_re_ref, hf_im_ref, h_re_ref, h_im_ref, bu_re_ref, bu_im_ref, y_ref):
    step = pl.program_id(1)
    rows = u_ref.shape[0]
    groups = h_re_ref.shape[0] // SUBLANES
    tb = rows // SUBLANES // groups
    n_tiles = wb_re_ref.shape[0]
    sw = wb_re_ref.shape[2]

    @pl.when(step == 0)
    def _():
        h_re_ref[...] = h0_re_ref[...]
        h_im_ref[...] = h0_im_ref[...]

    for j in range(n_tiles):
        u_j = u_ref[:, j * LANES:(j + 1) * LANES]
        u_b = u_j.astype(BF16)
        bu_re_ref[j] = _dot(u_b, wb_re_ref[j])
        bu_im_ref[j] = _dot(u_b, wb_im_ref[j])
        a_re = jnp.broadcast_to(ab_re_ref[:, j * sw:(j + 1) * sw], (SUBLANES, sw))
        a_im = jnp.broadcast_to(ab_im_ref[:, j * sw:(j + 1) * sw], (SUBLANES, sw))
        for grp in range(groups):
            state_rows = slice(grp * SUBLANES, (grp + 1) * SUBLANES)
            h_re = h_re_ref[state_rows, j * sw:(j + 1) * sw]
            h_im = h_im_ref[state_rows, j * sw:(j + 1) * sw]
            for t in range(grp * tb, (grp + 1) * tb):
                sl = slice(t * SUBLANES, (t + 1) * SUBLANES)
                h_re, h_im = (a_re * h_re - a_im * h_im + bu_re_ref[j, sl, :], a_re * h_im + a_im * h_re + bu_im_ref[j, sl, :])
                bu_re_ref[j, sl, :] = h_re
                bu_im_ref[j, sl, :] = h_im
            h_re_ref[state_rows, j * sw:(j + 1) * sw] = h_re
            h_im_ref[state_rows, j * sw:(j + 1) * sw] = h_im
        h_cat = jnp.concatenate([bu_re_ref[j].astype(BF16), bu_im_ref[j].astype(BF16)], axis=1)
        y_ref[:, j * LANES:(j + 1) * LANES] = _dot(h_cat, wc_ref[j]) + d_ref[:, j * LANES:(j + 1) * LANES] * u_j

    z = _gelu(y_ref[...])
    gate = _sigmoid(_dot(z.astype(BF16), wglu_ref[...]) + bglu_ref[...])
    z_ref[...] = (z * gate).astype(BF16)

    @pl.when(step == pl.num_programs(1) - 1)
    def _():
        hf_re_ref[...] = h_re_ref[...]
        hf_im_ref[...] = h_im_ref[...]


def _s5(u, ab_re, ab_im, wb_re, wb_im, wc, d_skip, wglu, bglu, h0_re, h0_im, *, tblock):
    total, width = u.shape
    n_state = ab_re.shape[1]
    n_bg = h0_re.shape[0] // SUBLANES
    sw = wb_re.shape[2]
    tb, groups, grid, blk, per_group = _time_grid(total, n_bg, tblock, pack_groups=True)
    rows = groups * tb * SUBLANES
    st_spec = per_group(SUBLANES, n_state)
    consts = (ab_re, ab_im, wb_re, wb_im, wc, d_skip, wglu, bglu)
    st_shape = jax.ShapeDtypeStruct((n_bg * SUBLANES, n_state), F32)
    return pl.pallas_call(
        _s5_body,
        grid=grid,
        in_specs=[blk(width)] + [_resident(p) for p in consts] + [st_spec, st_spec],
        out_specs=(blk(width), st_spec, st_spec),
        out_shape=(jax.ShapeDtypeStruct((total, width), BF16), st_shape, st_shape),
        scratch_shapes=[pltpu.VMEM((groups * SUBLANES, n_state), F32), pltpu.VMEM((groups * SUBLANES, n_state), F32),
                        pltpu.VMEM((wb_re.shape[0], rows, sw), F32), pltpu.VMEM((wb_re.shape[0], rows, sw), F32),
                        pltpu.VMEM((rows, width), F32)],
        compiler_params=_params(2),
        name="s5",
    )(u, *[_resident_array(p) for p in consts], h0_re, h0_im)


def _lru_block(n, xbuf_ref, gate_ref, params, a_ref, b_ref, h, rows):
    cw_ref, cb_ref, wr_ref, br_ref, wi_ref, bi_ref, lam_ref = params
    bw = wr_ref.shape[1]
    cols = slice(n * bw, (n + 1) * bw)
    a_blk, b_blk = a_ref.at[n % 2], b_ref.at[n % 2]
    taps = cw_ref.shape[0]
    groups = h.shape[0] // SUBLANES
    seg = rows // groups
    pitch = seg + (taps - 1) * SUBLANES
    pieces = []
    for grp in range(groups):
        xc = cb_ref[:, cols] + xbuf_ref[grp * pitch:grp * pitch + seg, cols] * cw_ref[0:1, cols]
        for j in range(1, taps):
            xc = xc + xbuf_ref[grp * pitch + j * SUBLANES:grp * pitch + j * SUBLANES + seg, cols] * cw_ref[j:j + 1, cols]
        pieces.append(xc)
    xc = jnp.concatenate(pieces, axis=0) if groups > 1 else pieces[0]
    xb = xc.astype(BF16)
    i = _sigmoid(_dot(xb, wi_ref[n]) + bi_ref[:, cols])
    half_rate = (-0.5 * LRU_C) * _softplus(-lam_ref[:, cols])
    log_a = jnp.tanh(0.5 * (_dot(xb, wr_ref[n]) + br_ref[:, cols])) * half_rate + half_rate
    a_blk[...] = jnp.exp(log_a)
    th = jnp.tanh(log_a)
    y = -2.0 * th / (1.0 - th)
    root = jnp.where(y > 0.0, y * lax.rsqrt(y), 0.0)
    b_blk[...] = root * (i * xc)
    finals = []
    for grp in range(groups):
        h_g = h[grp * SUBLANES:(grp + 1) * SUBLANES, :]
        for t in range(grp * seg // SUBLANES, (grp + 1) * seg // SUBLANES):
            sl = slice(t * SUBLANES, (t + 1) * SUBLANES)
            h_g = a_blk[sl, :] * h_g + b_blk[sl, :]
            b_blk[sl, :] = h_g
        finals.append(h_g)
    h = jnp.concatenate(finals, axis=0) if groups > 1 else finals[0]
    return (_gelu(gate_ref[:, cols]) * b_blk[...]).astype(BF16), h


def _lru_body(gate_ref, xin_ref, conv0_ref, h0_ref, cw_ref, cb_ref, wr_ref, br_ref, wi_ref, bi_ref, lam_ref,
              o_ref, hf_ref, convf_ref, xbuf_ref, a_ref, b_ref, h_ref):
    step = pl.program_id(1)
    rows, width = xin_ref.shape
    tail = (cw_ref.shape[0] - 1) * SUBLANES
    n_blocks = wr_ref.shape[0]
    bw = wr_ref.shape[1]
    groups = h_ref.shape[0] // SUBLANES
    seg = rows // groups
    pitch = seg + tail

    @pl.when(step == 0)
    def _():
        for grp in range(groups):
            xbuf_ref[grp * pitch:grp * pitch + tail, :] = conv0_ref[grp * tail:(grp + 1) * tail, :]
        h_ref[...] = h0_ref[...]

    for grp in range(groups):
        xbuf_ref[grp * pitch + tail:(grp + 1) * pitch, :] = xin_ref[grp * seg:(grp + 1) * seg, :]
    params = (cw_ref, cb_ref, wr_ref, br_ref, wi_ref, bi_ref, lam_ref)
    for n in range(n_blocks):
        cols = slice(n * bw, (n + 1) * bw)
        o_ref[:, cols], h_ref[:, cols] = _lru_block(n, xbuf_ref, gate_ref, params, a_ref, b_ref, h_ref[:, cols], rows)
    for grp in range(groups):
        xbuf_ref[grp * pitch:grp * pitch + tail, :] = xbuf_ref[grp * pitch + seg:(grp + 1) * pitch, :]

    @pl.when(step == pl.num_programs(1) - 1)
    def _():
        hf_ref[...] = h_ref[...]
        for grp in range(groups):
            convf_ref[grp * tail:(grp + 1) * tail, :] = xbuf_ref[grp * pitch:grp * pitch + tail, :]


def _lru(gate, xin, conv0, h0, cw, cb, wr, br, wi, bi, lam, *, tblock):
    total, width = xin.shape
    n_bg = h0.shape[0] // SUBLANES
    tail = (cw.shape[0] - 1) * SUBLANES
    bw = _resident_shape(wr)[1]
    tb, groups, grid, blk, per_group = _time_grid(total, n_bg, tblock, pack_groups=True)
    rows = groups * tb * SUBLANES
    h_spec = per_group(SUBLANES, width)
    c_spec = per_group(tail, width)
    consts = (cw, cb, wr, br, wi, bi, lam)
    return pl.pallas_call(
        _lru_body,
        grid=grid,
        in_specs=[blk(width), blk(width), c_spec, h_spec] + [_resident(p) for p in consts],
        out_specs=(blk(width), h_spec, c_spec),
        out_shape=(jax.ShapeDtypeStruct((total, width), BF16), jax.ShapeDtypeStruct((n_bg * SUBLANES, width), F32),
                   jax.ShapeDtypeStruct((n_bg * tail, width), F32)),
        scratch_shapes=[pltpu.VMEM((rows + groups * tail, width), F32), pltpu.VMEM((2, rows, bw), F32),
                        pltpu.VMEM((2, rows, bw), F32), pltpu.VMEM((groups * SUBLANES, width), F32)],
        compiler_params=_params(2),
        name="rg_lru",
    )(gate, xin, conv0, h0, *[_resident_array(p) for p in consts])


def _row(t):
    return t.reshape(1, -1).astype(F32)


def _to_rows(x):
    b, s, d = x.shape
    return jnp.transpose(x.reshape(b // SUBLANES, SUBLANES, s, d), (0, 2, 1, 3)).reshape(b * s, d)


def _from_rows(y, b, s):
    d = y.shape[-1]
    return jnp.transpose(y.reshape(b // SUBLANES, s, SUBLANES, d), (0, 2, 1, 3)).reshape(b, s, d)


def _even_layer(x, nb, gla_states, gla_acc, e, n_even, s_re, s_im, prm, mlp, *, final, batch_major_out):
    (g_norm, w_in, w_u, w_a, wa2, ba2, g_head, ab_re, ab_im, wb_re, wb_im, wc, d_skip, wglu, bglu, w_out, dims) = prm
    nh, dk, dv = dims
    q, k, v, gt, u, la, *reordered = _inproj_even(x, g_norm, w_in, w_u, w_a, wa2, ba2, kw=nh * dk, vw=nh * dv)
    x2d = reordered[0] if reordered else x
    steps = x2d.shape[0] // nb
    og, gla_acc = _gla(q, k, la, v, gt, g_head, gla_states, e, n_even, gla_acc, nb, dims, c=math.gcd(steps, GLA_CHUNK), tblock=128)
    n_state = ab_re.shape[1]
    h0_re = jnp.zeros((nb, n_state), F32) if s_re is None else s_re.astype(F32).reshape(nb, n_state)
    h0_im = jnp.zeros((nb, n_state), F32) if s_im is None else s_im.astype(F32).reshape(nb, n_state)
    zz, hf_re, hf_im = _s5(u, ab_re, ab_im, wb_re, wb_im, wc, d_skip, wglu, bglu, h0_re, h0_im, tblock=64)
    g_mlp, w_up, w_down, g_fin = mlp
    y = _out_mlp(x2d, og, zz, w_out, g_mlp, w_up, w_down, g_fin, final=final, batch_major_out=batch_major_out)
    return y, gla_acc, hf_re, hf_im


def _odd_layer(x2d, nb, s_lru, s_conv, prm, mlp, *, final, batch_major_out):
    (g_norm, w_in, cw, cb, wr, br, wi, bi, lam, w_out) = prm
    n_bg = nb // SUBLANES
    taps = cw.shape[0]
    width = _resident_shape(w_in)[1] // 2
    gate, xin = _inproj_odd(x2d, g_norm, w_in)
    h0 = jnp.zeros((nb, width), F32) if s_lru is None else s_lru.astype(F32)
    if s_conv is None:
        conv0 = jnp.zeros((n_bg * (taps - 1) * SUBLANES, width), F32)
    else:
        conv0 = jnp.transpose(s_conv.astype(F32).reshape(n_bg, SUBLANES, taps - 1, width), (0, 2, 1, 3)).reshape(-1, width)
    gh, hf, convf = _lru(gate, xin, conv0, h0, cw, cb, wr, br, wi, bi, lam, tblock=64)
    g_mlp, w_up, w_down, g_fin = mlp
    y = _out_mlp(x2d, gh, gh, w_out, g_mlp, w_up, w_down, g_fin, final=final, batch_major_out=batch_major_out)
    conv_new = jnp.transpose(convf.reshape(n_bg, taps - 1, SUBLANES, width), (0, 2, 1, 3)).reshape(nb, taps - 1, width)
    return y, hf, conv_new


def kernel(x_prompt, x_sample, state_gla, state_s5_re, state_s5_im, state_lru, state_conv, norm_mix_even, w_in_even, gla_w_a2, gla_b_a2, gla_norm, s5_a_re, s5_a_im, s5_log_dt, s5_b_re, s5_b_im, s5_c_re, s5_c_im, s5_d, s5_w_glu, s5_b_glu, w_out_even, norm_mix_odd, w_in_odd, conv_w, conv_b, lru_w_r, lru_b_r, lru_w_i, lru_b_i, lru_lam, w_out_odd, norm_mlp, w_up, w_down, norm_final):
    depth = norm_mlp.shape[0]
    nh, dk, dv = state_gla.shape[2:]
    rank = gla_w_a2.shape[1]
    n_groups, n_p, ch = s5_b_re.shape[1:]
    kw, vw = nh * dk, nh * dv
    assert rank <= LANES and LANES % ch == 0 and LANES % dk == 0 and dv == LANES
    bp, sp, _ = x_prompt.shape
    bs, ss, _ = x_sample.shape
    n_even = state_gla.shape[0]

    in_kernel_order = bp == SUBLANES and sp % (INPROJ_TILE // SUBLANES) == 0 and depth > 1
    yp = x_prompt.astype(F32) if in_kernel_order else _to_rows(x_prompt.astype(F32))
    ys = _to_rows(x_sample.astype(F32))
    g_fin = _row(norm_final)
    o4 = 2 * kw + 2 * vw
    o5 = o4 + rank
    w_in_main = w_in_even[:, :, :o4].astype(BF16)
    w_in_u = w_in_even[:, :, o5:].astype(BF16)
    w_in_a = jnp.pad(w_in_even[:, :, o4:o5], ((0, 0), (0, 0), (0, LANES - rank))).astype(BF16)
    w_a2 = jnp.pad(gla_w_a2, ((0, 0), (0, LANES - rank), (0, 0))).astype(BF16)
    w_up_b, w_down_b = w_up.astype(BF16), w_down.astype(BF16)
    w_out_even_b, w_out_odd_b, w_in_odd_b = w_out_even.astype(BF16), w_out_odd.astype(BF16), w_in_odd.astype(BF16)
    w_glu_b, w_r_b, w_i_b = s5_w_glu.astype(BF16), lru_w_r.astype(BF16), lru_w_i.astype(BF16)
    keys = ("re", "im", "lru", "conv")
    outs_p = {key: [] for key in keys}
    outs_s = {key: [] for key in keys}
    gla_p = gla_s = None

    def record(outs, sr, si):
        outs["re"].append(sr.reshape(-1, n_groups, n_p))
        outs["im"].append(si.reshape(-1, n_groups, n_p))

    for layer in range(depth):
        final = layer == depth - 1
        bm_out = (bp, sp) if final and in_kernel_order else None
        mlp = (_row(norm_mlp[layer]), (w_up_b, layer), (w_down_b, layer), g_fin)
        if layer % 2 == 0:
            e = layer // 2
            ab_re, ab_im, bb_re, bb_im = _s5_discretise(s5_a_re[e].astype(F32), s5_a_im[e].astype(F32), s5_log_dt[e].astype(F32),
                                                        s5_b_re[e].astype(F32), s5_b_im[e].astype(F32))
            wc = jnp.concatenate([_block_diag_out(s5_c_re[e].astype(F32)), -_block_diag_out(s5_c_im[e].astype(F32))], axis=1)
            prm = (_row(norm_mix_even[e]), (w_in_main, e), (w_in_u, e), (w_in_a, e), (w_a2, e), _row(gla_b_a2[e]), _row(gla_norm[e]),
                   ab_re.reshape(1, -1), ab_im.reshape(1, -1), _block_diag_in(bb_re).astype(BF16), _block_diag_in(bb_im).astype(BF16),
                   wc.astype(BF16), _row(s5_d[e]), (w_glu_b, e), _row(s5_b_glu[e]), (w_out_even_b, e), (nh, dk, dv))
            yp, gla_p, sr, si = _even_layer(yp, bp, None, gla_p, e, n_even, None, None, prm, mlp, final=final, batch_major_out=bm_out)
            record(outs_p, sr, si)
            ys, gla_s, sr, si = _even_layer(ys, bs, state_gla, gla_s, e, n_even, state_s5_re[e], state_s5_im[e], prm, mlp, final=final,
                                            batch_major_out=None)
            record(outs_s, sr, si)
        else:
            o = layer // 2
            prm = (_row(norm_mix_odd[o]), (w_in_odd_b, o), conv_w[o].astype(F32), _row(conv_b[o]),
                   (w_r_b, o), _row(lru_b_r[o]), (w_i_b, o), _row(lru_b_i[o]), _row(lru_lam[o]), (w_out_odd_b, o))
            yp, sl, sc = _odd_layer(yp, bp, None, None, prm, mlp, final=final, batch_major_out=bm_out)
            outs_p["lru"].append(sl)
            outs_p["conv"].append(sc)
            ys, sl, sc = _odd_layer(ys, bs, state_lru[o], state_conv[o], prm, mlp, final=final, batch_major_out=None)
            outs_s["lru"].append(sl)
            outs_s["conv"].append(sc)

    stack = lambda outs: tuple(jnp.stack(outs[key]) for key in keys)
    return (yp if in_kernel_order else _from_rows(yp, bp, sp), _from_rows(ys, bs, ss), gla_p) + stack(outs_p) + (gla_s,) + stack(outs_s)
```

```python
import functools
import math

import jax
import jax.numpy as jnp
from jax import lax
from jax.experimental import pallas as pl
from jax.experimental.pallas import tpu as pltpu

F32 = jnp.float32
BF16 = jnp.bfloat16

NORM_EPS = 1e-6
GLA_TAU = 16.0
GLA_CHUNK = 16
LRU_C = 8.0

SUBLANES = 8
LANES = 128
ROW_TILE = 512
INPROJ_TILE = 1024
FF_CHUNK = 1024
VMEM_LIMIT = 48 * 1024 * 1024


def _params(n_axes):
    return pltpu.CompilerParams(dimension_semantics=("arbitrary",) * n_axes, vmem_limit_bytes=VMEM_LIMIT)


def _const_spec(shape):
    zeros = (0,) * len(shape)
    return pl.BlockSpec(shape, lambda *_: zeros, pipeline_mode=pl.Buffered(1))


def _resident(p):
    if not isinstance(p, tuple):
        return _const_spec(p.shape)
    stacked, layer = p
    index = (layer,) + (0,) * (stacked.ndim - 1)
    return pl.BlockSpec((None,) + stacked.shape[1:], lambda *_: index, pipeline_mode=pl.Buffered(1))


def _resident_array(p):
    return p[0] if isinstance(p, tuple) else p


def _resident_shape(p):
    return p[0].shape[1:] if isinstance(p, tuple) else p.shape


def _dot(a, b):
    return jnp.dot(a, b, preferred_element_type=F32)


def _dot_nt(a, b):
    return lax.dot_general(a, b, (((1,), (1,)), ((), ())), preferred_element_type=F32)


def _dot_tn(a, b):
    return lax.dot_general(a, b, (((0,), (0,)), ((), ())), preferred_element_type=F32)


def _rmsnorm(x, g):
    return x * lax.rsqrt(jnp.mean(x * x, axis=-1, keepdims=True) + NORM_EPS) * g


def _log_sigmoid(z):
    return jnp.minimum(z, 0.0) - jnp.log1p(jnp.exp(-jnp.abs(z)))


def _softplus(z):
    return jnp.maximum(z, 0.0) + jnp.log1p(jnp.exp(-jnp.abs(z)))


_GELU_C1 = math.sqrt(2.0 / math.pi)
_GELU_C2 = _GELU_C1 * 0.044715


def _gelu(x):
    half = 0.5 * x
    return half + half * jnp.tanh(x * (_GELU_C1 + _GELU_C2 * (x * x)))


def _sigmoid(x):
    return 0.5 * jnp.tanh(0.5 * x) + 0.5


def _rows(ref, start, size):
    return ref[pl.ds(pl.multiple_of(start, size), size), :]


def _interleave_rows(x_ref, slab_ref):
    nb, ts, d = x_ref.shape
    for b in range(nb):
        for l in range(d // LANES):
            slab_ref[l, pl.ds(b, ts, stride=nb), :] = x_ref[b, :, l * LANES:(l + 1) * LANES]
    return jnp.concatenate([slab_ref[l] for l in range(d // LANES)], axis=1)


def _deinterleave_rows(y, slab_ref, o_ref):
    nb, ts, d = o_ref.shape
    for l in range(d // LANES):
        slab_ref[l] = y[:, l * LANES:(l + 1) * LANES]
    for b in range(nb):
        for l in range(d // LANES):
            o_ref[b, :, l * LANES:(l + 1) * LANES] = slab_ref[l, pl.ds(b, ts, stride=nb), :]


def _row_tiles(rows, tile=ROW_TILE):
    tm = tile if rows % tile == 0 else min(ROW_TILE, rows)
    return tm, (rows // tm,), lambda width, col=0: pl.BlockSpec((tm, width), lambda i: (i, col))


def _inproj_even_body(x_ref, g_ref, w_ref, wu_ref, wa_ref, wa2_ref, ba2_ref, q_ref, k_ref, v_ref, gt_ref, u_ref, la_ref, *rest,
                      kw, vw):
    if rest:
        xt_ref, slab_ref = rest
        x = _interleave_rows(x_ref, slab_ref)
        xt_ref[...] = x
    else:
        x = x_ref[...]
    xn = _rmsnorm(x, g_ref[...]).astype(BF16)
    o1, o2 = kw, 2 * kw
    o3 = o2 + vw
    o4 = o3 + vw
    q_ref[...] = _dot(xn, w_ref[:, 0:o1])
    k_ref[...] = _dot(xn, w_ref[:, o1:o2])
    v_ref[...] = _dot(xn, w_ref[:, o2:o3]).astype(BF16)
    gt_ref[...] = _dot(xn, w_ref[:, o3:o4])
    u_ref[...] = _dot(xn, wu_ref[...])
    a_lr = _dot(xn, wa_ref[...]).astype(BF16)
    z = _dot(a_lr, wa2_ref[...]) + ba2_ref[...]
    la_ref[...] = _log_sigmoid(z) * (1.0 / GLA_TAU)


def _inproj_even(x, g, w, wu, wa, wa2, ba2, *, kw, vw):
    batch_major = x.ndim == 3
    d = x.shape[-1]
    rows = x.size // d
    sw = _resident_shape(wu)[1]
    widths = [kw, kw, vw, vw, sw, kw]
    dtypes = [F32, F32, BF16, F32, F32, F32]
    consts = (g, w, wu, wa, wa2, ba2)
    tm, grid, blk = _row_tiles(rows, INPROJ_TILE)
    x_spec, scratch = blk(d), []
    if batch_major:
        nb = x.shape[0]
        x_spec = pl.BlockSpec((nb, tm // nb, d), lambda i: (0, i, 0))
        widths.append(d)
        dtypes.append(F32)
        scratch = [pltpu.VMEM((d // LANES, tm, LANES), F32)]
    return pl.pallas_call(
        functools.partial(_inproj_even_body, kw=kw, vw=vw),
        grid=grid,
        in_specs=[x_spec] + [_resident(p) for p in consts],
        out_specs=tuple(blk(wd) for wd in widths),
        out_shape=tuple(jax.ShapeDtypeStruct((rows, wd), dt) for wd, dt in zip(widths, dtypes)),
        scratch_shapes=scratch,
        compiler_params=_params(1),
        name="inproj_even",
    )(x, *[_resident_array(p) for p in consts])


def _inproj_odd_body(x_ref, g_ref, w_ref, gate_ref, xin_ref, *, width):
    xn = _rmsnorm(x_ref[...], g_ref[...]).astype(BF16)
    gate_ref[...] = _dot(xn, w_ref[:, 0:width])
    xin_ref[...] = _dot(xn, w_ref[:, width:2 * width])


def _inproj_odd(x2d, g, w):
    rows, d = x2d.shape
    width = _resident_shape(w)[1] // 2
    _, grid, blk = _row_tiles(rows, INPROJ_TILE)
    shape = jax.ShapeDtypeStruct((rows, width), F32)
    return pl.pallas_call(
        functools.partial(_inproj_odd_body, width=width),
        grid=grid,
        in_specs=[blk(d), _resident(g), _resident(w)],
        out_specs=(blk(width), blk(width)),
        out_shape=(shape, shape),
        compiler_params=_params(1),
        name="inproj_odd",
    )(x2d, _resident_array(g), _resident_array(w))


def _out_mlp_cast_body(x_ref, a1_ref, a2_ref, wo_ref, g_ref, gf_ref, wu_hbm, wd_hbm, o_ref, wu_out, wd_out,
                       wu_bf, wd_bf, stage, sem_in, sem_out, *slab, final, layer):
    i = pl.program_id(0)
    d, d_ff = wu_bf.shape
    n_up = d_ff // FF_CHUNK
    n_chunks = 2 * n_up

    def fetch(k):
        if k < n_up:
            src = wu_hbm.at[layer, :, pl.ds(k * FF_CHUNK, FF_CHUNK)]
        else:
            src = wd_hbm.at[layer, pl.ds((k - n_up) * FF_CHUNK, FF_CHUNK), :]
        return pltpu.make_async_copy(src, stage.at[k % 2], sem_in.at[k % 2])

    def publish():
        return (pltpu.make_async_copy(wu_bf, wu_out, sem_out.at[0]), pltpu.make_async_copy(wd_bf, wd_out, sem_out.at[1]))

    @pl.when(i == 0)
    def _():
        fetch(0).start()
        for k in range(n_chunks):
            if k + 1 < n_chunks:
                fetch(k + 1).start()
            fetch(k).wait()
            w = stage[k % 2].astype(BF16)
            if k < n_up:
                wu_bf[:, k * FF_CHUNK:(k + 1) * FF_CHUNK] = w
            else:
                wd_bf[(k - n_up) * FF_CHUNK:(k - n_up + 1) * FF_CHUNK, :] = w
        for copy in publish():
            copy.start()

    _out_mlp_body(x_ref, a1_ref, a2_ref, wo_ref, g_ref, wu_bf, wd_bf, gf_ref, o_ref, *slab, final=final)

    @pl.when(i == pl.num_programs(0) - 1)
    def _():
        for copy in publish():
            copy.wait()


def _out_mlp_body(x_ref, a1_ref, a2_ref, wo_ref, g_ref, wu_ref, wd_ref, gf_ref, o_ref, *slab, final):
    half = a1_ref.shape[1]
    mix = _dot(a1_ref[...], wo_ref[0:half, :]) + _dot(a2_ref[...], wo_ref[half:2 * half, :])
    x1 = x_ref[...] + mix
    xn = _rmsnorm(x1, g_ref[...]).astype(BF16)
    acc = x1
    for c in range(wu_ref.shape[1] // FF_CHUNK):
        h = _dot(xn, wu_ref[:, c * FF_CHUNK:(c + 1) * FF_CHUNK])
        h = jnp.square(jnp.maximum(h, 0.0)).astype(BF16)
        acc = acc + _dot(h, wd_ref[c * FF_CHUNK:(c + 1) * FF_CHUNK, :])
    if final:
        acc = _rmsnorm(acc, gf_ref[...])
    if slab:
        _deinterleave_rows(acc, slab[0], o_ref)
    else:
        o_ref[...] = acc


def _out_mlp(x2d, a1, a2, wo, g, wu, wd, gf, *, final, batch_major_out=None):
    rows, d = x2d.shape
    half = d // 2
    consts = (wo, g, wu, wd, gf)
    tm, grid, blk = _row_tiles(rows)
    out_spec, out_shape, scratch = blk(d), jax.ShapeDtypeStruct((rows, d), F32), []
    if batch_major_out is not None:
        nb, s = batch_major_out
        out_spec = pl.BlockSpec((nb, tm // nb, d), lambda i: (0, i, 0))
        out_shape = jax.ShapeDtypeStruct((nb, s, d), F32)
        scratch = [pltpu.VMEM((d // LANES, tm, LANES), F32)]
    if _resident_array(wu).dtype == F32:
        (w_up, layer), (w_down, _) = wu, wd
        d_ff = w_up.shape[2]
        assert d == FF_CHUNK and d_ff % FF_CHUNK == 0
        any_spec = pl.BlockSpec(memory_space=pl.ANY)
        row_consts = (wo, g, gf)
        return pl.pallas_call(
            functools.partial(_out_mlp_cast_body, final=final, layer=layer),
            grid=grid,
            in_specs=[blk(d), blk(half, 0), blk(half, a1.shape[1] // half - 1)] + [_resident(p) for p in row_consts] + [any_spec, any_spec],
            out_specs=(out_spec, any_spec, any_spec),
            out_shape=(out_shape, jax.ShapeDtypeStruct((d, d_ff), BF16), jax.ShapeDtypeStruct((d_ff, d), BF16)),
            scratch_shapes=[pltpu.VMEM((d, d_ff), BF16), pltpu.VMEM((d_ff, d), BF16), pltpu.VMEM((2, FF_CHUNK, FF_CHUNK), F32),
                            pltpu.SemaphoreType.DMA((2,)), pltpu.SemaphoreType.DMA((2,))] + scratch,
            compiler_params=_params(1),
            name="out_mlp_cast",
        )(x2d, a1, a2, *[_resident_array(p) for p in row_consts], w_up, w_down)
    return pl.pallas_call(
        functools.partial(_out_mlp_body, final=final),
        grid=grid,
        in_specs=[blk(d), blk(half, 0), blk(half, a1.shape[1] // half - 1)] + [_resident(p) for p in consts],
        out_specs=out_spec,
        out_shape=out_shape,
        scratch_shapes=scratch,
        compiler_params=_params(1),
        name="out_mlp",
    )(x2d, a1, a2, *[_resident_array(p) for p in consts])


def _time_grid(total_rows, n_bg, tblock, pack_groups=False):
    steps = total_rows // (n_bg * SUBLANES)
    tb = min(tblock, steps)
    n_t = steps // tb
    groups = 1
    if pack_groups and n_t == 1:
        groups = max(1, min(n_bg, ROW_TILE // (steps * SUBLANES)))
        while n_bg % groups:
            groups -= 1
    blk = lambda width: pl.BlockSpec((groups * tb * SUBLANES, width), lambda bg, t: (bg * n_t + t, 0))
    per_group = lambda rows, width: pl.BlockSpec((groups * rows, width), lambda bg, t: (bg, 0))
    return tb, groups, (n_bg // groups, n_t), blk, per_group


def _gla_body(q_ref, k_ref, la_ref, v_ref, gt_ref, gh_ref, s0_ref, *rest, c, nh, dk, dv, transposed):
    o_ref, sfin_ref, st_ref = rest[-3:]
    tb = pl.program_id(1)
    rows = SUBLANES * c
    n_chunks = q_ref.shape[0] // rows
    dk_shift = int(math.log2(dk))
    heads_per_tile = LANES // dk

    @pl.when(tb == 0)
    def _():
        st_ref[...] = s0_ref[...]

    def iota(shape, axis):
        return lax.broadcasted_iota(jnp.int32, shape, axis)

    causal = ((iota((rows, rows), 0) & 7) == (iota((rows, rows), 1) & 7)) & (iota((rows, rows), 1) <= iota((rows, rows), 0))
    emask = (iota((rows, SUBLANES * dk), 0) & 7) == lax.shift_right_logical(iota((rows, SUBLANES * dk), 1), dk_shift)
    dmask = iota((SUBLANES, SUBLANES * dk), 0) == lax.shift_right_logical(iota((SUBLANES, SUBLANES * dk), 1), dk_shift)
    lo_rows = iota((rows, LANES), 1) < dk
    lo_8 = iota((SUBLANES, LANES), 1) < dk

    def expand(tile, h, lo=lo_rows, mask=emask):
        rolled = pltpu.roll(tile, dk, axis=1)
        d = jnp.where(lo, tile, rolled) if h % heads_per_tile == 0 else jnp.where(lo, rolled, tile)
        return jnp.where(mask, jnp.concatenate([d] * (SUBLANES * dk // LANES), axis=1), 0.0)

    def chunk(n, carry):
        r0 = n * rows
        q = _rows(q_ref, r0, rows)
        k = _rows(k_ref, r0, rows)
        la = _rows(la_ref, r0, rows)
        v = _rows(v_ref, r0, rows)
        gt = _rows(gt_ref, r0, rows)
        acc = jnp.zeros((SUBLANES, nh * dk), F32)
        pieces = []
        for t in range(c):
            acc = acc + la[t * SUBLANES:(t + 1) * SUBLANES, :]
            pieces.append(acc)
        cum = jnp.concatenate(pieces, axis=0)
        last = jnp.concatenate([acc] * c, axis=0)
        q_t = q * jnp.exp(cum) * (dk ** -0.5)
        k_t = k * jnp.exp(-cum)
        k_end = k * jnp.exp(last - cum)
        dec = jnp.exp(acc)
        pad = jnp.zeros((LANES - SUBLANES, LANES), F32)
        outs = []
        for h in range(nh):
            tile = (h // heads_per_tile) * LANES
            lane0 = (h % heads_per_tile) * dk
            sel = lo_rows if lane0 == 0 else jnp.logical_not(lo_rows)
            q_tile = q_t[:, tile:tile + LANES]
            scores = _dot_nt(jnp.where(sel, q_tile, 0.0).astype(BF16), k_t[:, tile:tile + LANES].astype(BF16))
            scores = jnp.where(causal, scores, 0.0).astype(BF16)
            v_h = v[:, h * dv:(h + 1) * dv]
            q_exp = expand(q_tile, h).astype(BF16)
            k_exp = expand(k_end[:, tile:tile + LANES], h).astype(BF16)
            if transposed:
                st = st_ref[h]
                o = _dot(scores, v_h) + _dot_nt(q_exp, st.astype(BF16))
                dec_row = jnp.sum(expand(dec[:, tile:tile + LANES], h, lo_8, dmask), axis=0, keepdims=True)
                st_ref[h] = st * dec_row + _dot_tn(v_h, k_exp)
            else:
                st = jnp.concatenate([st_ref[b, h] for b in range(SUBLANES)], axis=0)
                o = _dot(scores, v_h) + _dot(q_exp, st.astype(BF16))
                upd = _dot_tn(k_exp, v_h)
                dec_t = jnp.concatenate([dec[:, tile:tile + LANES], pad], axis=0).T
                for b in range(SUBLANES):
                    dec_b = jnp.broadcast_to(dec_t[lane0:lane0 + dk, b:b + 1], (dk, dv))
                    st_ref[b, h] = st[b * dk:(b + 1) * dk, :] * dec_b + upd[b * dk:(b + 1) * dk, :]
            o = _rmsnorm(o, gh_ref[:, h * dv:(h + 1) * dv])
            g_h = gt[:, h * dv:(h + 1) * dv]
            outs.append(o * (g_h * _sigmoid(g_h)))
        o_ref[pl.ds(pl.multiple_of(r0, rows), rows), :] = jnp.concatenate(outs, axis=1).astype(BF16)
        return carry

    lax.fori_loop(0, n_chunks, chunk, 0, unroll=min(n_chunks, 4))

    @pl.when(tb == pl.num_programs(1) - 1)
    def _():
        sfin_ref[...] = st_ref[...]


def _gla_state_in(s):
    b, h, dk, dv = s.shape
    return jnp.transpose(s.reshape(b // SUBLANES, SUBLANES, h, dk, dv), (0, 2, 4, 1, 3)).reshape(b // SUBLANES * h, dv, SUBLANES * dk)


def _gla_state_out(st, nh, dk):
    dv = st.shape[1]
    n_bg = st.shape[0] // nh
    return jnp.transpose(st.reshape(n_bg, nh, dv, SUBLANES, dk), (0, 3, 1, 4, 2)).reshape(n_bg * SUBLANES, nh, dk, dv)


def _gla(q, k, la, v, gt, gh, states, layer, n_layers, acc, nb, dims, *, c, tblock):
    nh, dk, dv = dims
    n_bg = nb // SUBLANES
    tb, _, grid, blk, _ = _time_grid(q.shape[0], n_bg, tblock)
    transposed = q.shape[0] // n_bg > SUBLANES * c
    body = functools.partial(_gla_body, c=c, nh=nh, dk=dk, dv=dv, transposed=transposed)
    row_specs = [blk(nh * dk), blk(nh * dk), blk(nh * dk), blk(nh * dv), blk(nh * dv), _const_spec(gh.shape)]
    og_shape = jax.ShapeDtypeStruct((q.shape[0], nh * dv), BF16)
    if transposed:
        st_block = (nh, dv, SUBLANES * dk)
        st_spec = pl.BlockSpec(st_block, lambda bg, t: (bg, 0, 0))
        s_in = jnp.zeros((n_bg * nh, dv, SUBLANES * dk), F32) if states is None else _gla_state_in(states[layer].astype(F32))
        og, s_fin = pl.pallas_call(
            body, grid=grid, in_specs=row_specs + [st_spec], out_specs=(blk(nh * dv), st_spec),
            out_shape=(og_shape, jax.ShapeDtypeStruct(s_in.shape, F32)), scratch_shapes=[pltpu.VMEM(st_block, F32)],
            compiler_params=_params(2), name="gla",
        )(q, k, la, v, gt, gh, s_in)
        s_new = _gla_state_out(s_fin, nh, dk)[None]
        return og, (jnp.pad(s_new, ((0, n_layers - 1),) + ((0, 0),) * 4) if acc is None else lax.dynamic_update_slice(acc, s_new, (layer, 0, 0, 0, 0)))
    st_block = (SUBLANES, nh, dk, dv)
    stacked_spec = pl.BlockSpec((None,) + st_block, lambda bg, t: (layer, bg, 0, 0, 0))
    if states is None:
        s_in, in_spec = jnp.zeros((nb, nh, dk, dv), F32), pl.BlockSpec(st_block, lambda bg, t: (bg, 0, 0, 0))
    else:
        s_in, in_spec = states.astype(F32), stacked_spec
    if acc is None:
        acc = jnp.zeros((n_layers, nb, nh, dk, dv), F32)
    return pl.pallas_call(
        body, grid=grid, in_specs=row_specs + [in_spec, pl.BlockSpec(memory_space=pl.ANY)], out_specs=(blk(nh * dv), stacked_spec),
        out_shape=(og_shape, jax.ShapeDtypeStruct(acc.shape, F32)), scratch_shapes=[pltpu.VMEM(st_block, F32)],
        input_output_aliases={7: 1}, compiler_params=_params(2), name="gla",
    )(q, k, la, v, gt, gh, s_in, acc)


def _s5_disc_body(a_re_ref, a_im_ref, log_dt_ref, b_re_ref, b_im_ref, ab_re_ref, ab_im_ref, bb_re_ref, bb_im_ref):
    a_re = a_re_ref[...]
    a_im = a_im_ref[...]
    dt = jnp.exp(log_dt_ref[...])
    mag = jnp.exp(dt * a_re)
    ab_re = mag * jnp.cos(dt * a_im)
    ab_im = mag * jnp.sin(dt * a_im)
    den = a_re * a_re + a_im * a_im
    coef_re = ((ab_re - 1.0) * a_re + ab_im * a_im) / den
    coef_im = (ab_im * a_re - (ab_re - 1.0) * a_im) / den
    b_re = b_re_ref[...]
    b_im = b_im_ref[...]
    ab_re_ref[...] = ab_re
    ab_im_ref[...] = ab_im
    bb_re_ref[...] = coef_re * b_re - coef_im * b_im
    bb_im_ref[...] = coef_re * b_im + coef_im * b_re


def _s5_discretise(a_re, a_im, log_dt, b_re, b_im):
    g, p, ch = b_re.shape
    rep = lambda t: jnp.repeat(t, ch, axis=0)
    flat = lambda t: jnp.transpose(t, (0, 2, 1)).reshape(g * ch, p)
    args = (rep(a_re), rep(a_im), rep(jnp.broadcast_to(log_dt[:, None], (g, p))), flat(b_re), flat(b_im))
    spec = _const_spec((g * ch, p))
    shape = jax.ShapeDtypeStruct((g * ch, p), F32)
    ab_re, ab_im, bb_re, bb_im = pl.pallas_call(
        _s5_disc_body, grid=(1,), in_specs=[spec] * 5, out_specs=(spec,) * 4, out_shape=(shape,) * 4,
        compiler_params=_params(1), name="s5_discretise",
    )(*args)
    return ab_re[::ch], ab_im[::ch], bb_re.reshape(g, ch, p), bb_im.reshape(g, ch, p)


def _block_diag_in(w_gcp):
    g, ch, p = w_gcp.shape
    gpt = LANES // ch
    w = w_gcp.reshape(g // gpt, gpt, ch, p)
    eye = jnp.eye(gpt, dtype=w.dtype)
    return jnp.einsum("ab,jacp->jacbp", eye, w).reshape(g // gpt, gpt * ch, gpt * p)


def _block_diag_out(w_gcp):
    g, ch, p = w_gcp.shape
    gpt = LANES // ch
    w = w_gcp.reshape(g // gpt, gpt, ch, p)
    eye = jnp.eye(gpt, dtype=w.dtype)
    return jnp.einsum("ab,jacp->japbc", eye, w).reshape(g // gpt, gpt * p, gpt * ch)


def _s5_body(u_ref, ab_re_ref, ab_im_ref, wb_re_ref, wb_im_ref, wc_ref, d_ref, wglu_ref, bglu_ref, h0_re_ref, h0_im_ref,
             z_ref, hf_re_ref, hf_im_ref, h_re_ref, h_im_ref, bu_re_ref, bu_im_ref, y_ref):
    step = pl.program_id(1)
    rows = u_ref.shape[0]
    groups = h_re_ref.shape[0] // SUBLANES
    tb = rows // SUBLANES // groups
    n_tiles = wb_re_ref.shape[0]
    sw = wb_re_ref.shape[2]

    @pl.when(step == 0)
    def _():
        h_re_ref[...] = h0_re_ref[...]
        h_im_ref[...] = h0_im_ref[...]

    for j in range(n_tiles):
        u_j = u_ref[:, j * LANES:(j + 1) * LANES]
        u_b = u_j.astype(BF16)
        bu_re_ref[j] = _dot(u_b, wb_re_ref[j])
        bu_im_ref[j] = _dot(u_b, wb_im_ref[j])
        a_re = jnp.broadcast_to(ab_re_ref[:, j * sw:(j + 1) * sw], (SUBLANES, sw))
        a_im = jnp.broadcast_to(ab_im_ref[:, j * sw:(j + 1) * sw], (SUBLANES, sw))
        for grp in range(groups):
            state_rows = slice(grp * SUBLANES, (grp + 1) * SUBLANES)
            h_re = h_re_ref[state_rows, j * sw:(j + 1) * sw]
            h_im = h_im_ref[state_rows, j * sw:(j + 1) * sw]
            for t in range(grp * tb, (grp + 1) * tb):
                sl = slice(t * SUBLANES, (t + 1) * SUBLANES)
                h_re, h_im = (a_re * h_re - a_im * h_im + bu_re_ref[j, sl, :], a_re * h_im + a_im * h_re + bu_im_ref[j, sl, :])
                bu_re_ref[j, sl, :] = h_re
                bu_im_ref[j, sl, :] = h_im
            h_re_ref[state_rows, j * sw:(j + 1) * sw] = h_re
            h_im_ref[state_rows, j * sw:(j + 1) * sw] = h_im
        h_cat = jnp.concatenate([bu_re_ref[j].astype(BF16), bu_im_ref[j].astype(BF16)], axis=1)
        y_ref[:, j * LANES:(j + 1) * LANES] = _dot(h_cat, wc_ref[j]) + d_ref[:, j * LANES:(j + 1) * LANES] * u_j

    z = _gelu(y_ref[...])
    gate = _sigmoid(_dot(z.astype(BF16), wglu_ref[...]) + bglu_ref[...])
    z_ref[...] = (z * gate).astype(BF16)

    @pl.when(step == pl.num_programs(1) - 1)
    def _():
        hf_re_ref[...] = h_re_ref[...]
        hf_im_ref[...] = h_im_ref[...]


def _s5(u, ab_re, ab_im, wb_re, wb_im, wc, d_skip, wglu, bglu, h0_re, h0_im, *, tblock):
    total, width = u.shape
    n_state = ab_re.shape[1]
    n_bg = h0_re.shape[0] // SUBLANES
    sw = wb_re.shape[2]
    tb, groups, grid, blk, per_group = _time_grid(total, n_bg, tblock, pack_groups=True)
    rows = groups * tb * SUBLANES
    st_spec = per_group(SUBLANES, n_state)
    consts = (ab_re, ab_im, wb_re, wb_im, wc, d_skip, wglu, bglu)
    st_shape = jax.ShapeDtypeStruct((n_bg * SUBLANES, n_state), F32)
    return pl.pallas_call(
        _s5_body,
        grid=grid,
        in_specs=[blk(width)] + [_resident(p) for p in consts] + [st_spec, st_spec],
        out_specs=(blk(width), st_spec, st_spec),
        out_shape=(jax.ShapeDtypeStruct((total, width), BF16), st_shape, st_shape),
        scratch_shapes=[pltpu.VMEM((groups * SUBLANES, n_state), F32), pltpu.VMEM((groups * SUBLANES, n_state), F32),
                        pltpu.VMEM((wb_re.shape[0], rows, sw), F32), pltpu.VMEM((wb_re.shape[0], rows, sw), F32),
                        pltpu.VMEM((rows, width), F32)],
        compiler_params=_params(2),
        name="s5",
    )(u, *[_resident_array(p) for p in consts], h0_re, h0_im)


def _lru_block(n, xbuf_ref, gate_ref, params, a_ref, b_ref, h, rows):
    cw_ref, cb_ref, wr_ref, br_ref, wi_ref, bi_ref, lam_ref = params
    bw = wr_ref.shape[1]
    cols = slice(n * bw, (n + 1) * bw)
    a_blk, b_blk = a_ref.at[n % 2], b_ref.at[n % 2]
    taps = cw_ref.shape[0]
    groups = h.shape[0] // SUBLANES
    seg = rows // groups
    pitch = seg + (taps - 1) * SUBLANES
    pieces = []
    for grp in range(groups):
        xc = cb_ref[:, cols] + xbuf_ref[grp * pitch:grp * pitch + seg, cols] * cw_ref[0:1, cols]
        for j in range(1, taps):
            xc = xc + xbuf_ref[grp * pitch + j * SUBLANES:grp * pitch + j * SUBLANES + seg, cols] * cw_ref[j:j + 1, cols]
        pieces.append(xc)
    xc = jnp.concatenate(pieces, axis=0) if groups > 1 else pieces[0]
    xb = xc.astype(BF16)
    i = _sigmoid(_dot(xb, wi_ref[n]) + bi_ref[:, cols])
    half_rate = (-0.5 * LRU_C) * _softplus(-lam_ref[:, cols])
    log_a = jnp.tanh(0.5 * (_dot(xb, wr_ref[n]) + br_ref[:, cols])) * half_rate + half_rate
    a_blk[...] = jnp.exp(log_a)
    th = jnp.tanh(log_a)
    y = -2.0 * th / (1.0 - th)
    root = jnp.where(y > 0.0, y * lax.rsqrt(y), 0.0)
    b_blk[...] = root * (i * xc)
    finals = []
    for grp in range(groups):
        h_g = h[grp * SUBLANES:(grp + 1) * SUBLANES, :]
        for t in range(grp * seg // SUBLANES, (grp + 1) * seg // SUBLANES):
            sl = slice(t * SUBLANES, (t + 1) * SUBLANES)
            h_g = a_blk[sl, :] * h_g + b_blk[sl, :]
            b_blk[sl, :] = h_g
        finals.append(h_g)
    h = jnp.concatenate(finals, axis=0) if groups > 1 else finals[0]
    return (_gelu(gate_ref[:, cols]) * b_blk[...]).astype(BF16), h


def _lru_body(gate_ref, xin_ref, conv0_ref, h0_ref, cw_ref, cb_ref, wr_ref, br_ref, wi_ref, bi_ref, lam_ref,
              o_ref, hf_ref, convf_ref, xbuf_ref, a_ref, b_ref, h_ref):
    step = pl.program_id(1)
    rows, width = xin_ref.shape
    tail = (cw_ref.shape[0] - 1) * SUBLANES
    n_blocks = wr_ref.shape[0]
    bw = wr_ref.shape[1]
    groups = h_ref.shape[0] // SUBLANES
    seg = rows // groups
    pitch = seg + tail

    @pl.when(step == 0)
    def _():
        for grp in range(groups):
            xbuf_ref[grp * pitch:grp * pitch + tail, :] = conv0_ref[grp * tail:(grp + 1) * tail, :]
        h_ref[...] = h0_ref[...]

    for grp in range(groups):
        xbuf_ref[grp * pitch + tail:(grp + 1) * pitch, :] = xin_ref[grp * seg:(grp + 1) * seg, :]
    params = (cw_ref, cb_ref, wr_ref, br_ref, wi_ref, bi_ref, lam_ref)
    for n in range(n_blocks):
        cols = slice(n * bw, (n + 1) * bw)
        o_ref[:, cols], h_ref[:, cols] = _lru_block(n, xbuf_ref, gate_ref, params, a_ref, b_ref, h_ref[:, cols], rows)
    for grp in range(groups):
        xbuf_ref[grp * pitch:grp * pitch + tail, :] = xbuf_ref[grp * pitch + seg:(grp + 1) * pitch, :]

    @pl.when(step == pl.num_programs(1) - 1)
    def _():
        hf_ref[...] = h_ref[...]
        for grp in range(groups):
            convf_ref[grp * tail:(grp + 1) * tail, :] = xbuf_ref[grp * pitch:grp * pitch + tail, :]


def _lru(gate, xin, conv0, h0, cw, cb, wr, br, wi, bi, lam, *, tblock):
    total, width = xin.shape
    n_bg = h0.shape[0] // SUBLANES
    tail = (cw.shape[0] - 1) * SUBLANES
    bw = _resident_shape(wr)[1]
    tb, groups, grid, blk, per_group = _time_grid(total, n_bg, tblock, pack_groups=True)
    rows = groups * tb * SUBLANES
    h_spec = per_group(SUBLANES, width)
    c_spec = per_group(tail, width)
    consts = (cw, cb, wr, br, wi, bi, lam)
    return pl.pallas_call(
        _lru_body,
        grid=grid,
        in_specs=[blk(width), blk(width), c_spec, h_spec] + [_resident(p) for p in consts],
        out_specs=(blk(width), h_spec, c_spec),
        out_shape=(jax.ShapeDtypeStruct((total, width), BF16), jax.ShapeDtypeStruct((n_bg * SUBLANES, width), F32),
                   jax.ShapeDtypeStruct((n_bg * tail, width), F32)),
        scratch_shapes=[pltpu.VMEM((rows + groups * tail, width), F32), pltpu.VMEM((2, rows, bw), F32),
                        pltpu.VMEM((2, rows, bw), F32), pltpu.VMEM((groups * SUBLANES, width), F32)],
        compiler_params=_params(2),
        name="rg_lru",
    )(gate, xin, conv0, h0, *[_resident_array(p) for p in consts])


def _row(t):
    return t.reshape(1, -1).astype(F32)


def _to_rows(x):
    b, s, d = x.shape
    return jnp.transpose(x.reshape(b // SUBLANES, SUBLANES, s, d), (0, 2, 1, 3)).reshape(b * s, d)


def _from_rows(y, b, s):
    d = y.shape[-1]
    return jnp.transpose(y.reshape(b // SUBLANES, s, SUBLANES, d), (0, 2, 1, 3)).reshape(b, s, d)


def _even_layer(x, nb, gla_states, gla_acc, e, n_even, s_re, s_im, prm, mlp, *, final, batch_major_out):
    (g_norm, w_in, w_u, w_a, wa2, ba2, g_head, ab_re, ab_im, wb_re, wb_im, wc, d_skip, wglu, bglu, w_out, dims) = prm
    nh, dk, dv = dims
    q, k, v, gt, u, la, *reordered = _inproj_even(x, g_norm, w_in, w_u, w_a, wa2, ba2, kw=nh * dk, vw=nh * dv)
    x2d = reordered[0] if reordered else x
    steps = x2d.shape[0] // nb
    og, gla_acc = _gla(q, k, la, v, gt, g_head, gla_states, e, n_even, gla_acc, nb, dims, c=math.gcd(steps, GLA_CHUNK), tblock=128)
    n_state = ab_re.shape[1]
    h0_re = jnp.zeros((nb, n_state), F32) if s_re is None else s_re.astype(F32).reshape(nb, n_state)
    h0_im = jnp.zeros((nb, n_state), F32) if s_im is None else s_im.astype(F32).reshape(nb, n_state)
    zz, hf_re, hf_im = _s5(u, ab_re, ab_im, wb_re, wb_im, wc, d_skip, wglu, bglu, h0_re, h0_im, tblock=64)
    g_mlp, w_up, w_down, g_fin = mlp
    y = _out_mlp(x2d, og, zz, w_out, g_mlp, w_up, w_down, g_fin, final=final, batch_major_out=batch_major_out)
    return y, gla_acc, hf_re, hf_im


def _odd_layer(x2d, nb, s_lru, s_conv, prm, mlp, *, final, batch_major_out):
    (g_norm, w_in, cw, cb, wr, br, wi, bi, lam, w_out) = prm
    n_bg = nb // SUBLANES
    taps = cw.shape[0]
    width = _resident_shape(w_in)[1] // 2
    gate, xin = _inproj_odd(x2d, g_norm, w_in)
    h0 = jnp.zeros((nb, width), F32) if s_lru is None else s_lru.astype(F32)
    if s_conv is None:
        conv0 = jnp.zeros((n_bg * (taps - 1) * SUBLANES, width), F32)
    else:
        conv0 = jnp.transpose(s_conv.astype(F32).reshape(n_bg, SUBLANES, taps - 1, width), (0, 2, 1, 3)).reshape(-1, width)
    gh, hf, convf = _lru(gate, xin, conv0, h0, cw, cb, wr, br, wi, bi, lam, tblock=64)
    g_mlp, w_up, w_down, g_fin = mlp
    y = _out_mlp(x2d, gh, gh, w_out, g_mlp, w_up, w_down, g_fin, final=final, batch_major_out=batch_major_out)
    conv_new = jnp.transpose(convf.reshape(n_bg, taps - 1, SUBLANES, width), (0, 2, 1, 3)).reshape(nb, taps - 1, width)
    return y, hf, conv_new


def kernel(x_prompt, x_sample, state_gla, state_s5_re, state_s5_im, state_lru, state_conv, norm_mix_even, w_in_even, gla_w_a2, gla_b_a2, gla_norm, s5_a_re, s5_a_im, s5_log_dt, s5_b_re, s5_b_im, s5_c_re, s5_c_im, s5_d, s5_w_glu, s5_b_glu, w_out_even, norm_mix_odd, w_in_odd, conv_w, conv_b, lru_w_r, lru_b_r, lru_w_i, lru_b_i, lru_lam, w_out_odd, norm_mlp, w_up, w_down, norm_final):
    depth = norm_mlp.shape[0]
    nh, dk, dv = state_gla.shape[2:]
    rank = gla_w_a2.shape[1]
    n_groups, n_p, ch = s5_b_re.shape[1:]
    kw, vw = nh * dk, nh * dv
    assert rank <= LANES and LANES % ch == 0 and LANES % dk == 0 and dv == LANES
    bp, sp, _ = x_prompt.shape
    bs, ss, _ = x_sample.shape
    n_even = state_gla.shape[0]

    in_kernel_order = bp == SUBLANES and sp % (INPROJ_TILE // SUBLANES) == 0 and depth > 1
    yp = x_prompt.astype(F32) if in_kernel_order else _to_rows(x_prompt.astype(F32))
    ys = _to_rows(x_sample.astype(F32))
    g_fin = _row(norm_final)
    o4 = 2 * kw + 2 * vw
    o5 = o4 + rank
    w_in_main = w_in_even[:, :, :o4].astype(BF16)
    w_in_u = w_in_even[:, :, o5:].astype(BF16)
    w_in_a = jnp.pad(w_in_even[:, :, o4:o5], ((0, 0), (0, 0), (0, LANES - rank))).astype(BF16)
    w_a2 = jnp.pad(gla_w_a2, ((0, 0), (0, LANES - rank), (0, 0))).astype(BF16)
    w_up_f, w_down_f = w_up.astype(F32), w_down.astype(F32)
    w_out_even_b, w_out_odd_b, w_in_odd_b = w_out_even.astype(BF16), w_out_odd.astype(BF16), w_in_odd.astype(BF16)
    w_glu_b, w_r_b, w_i_b = s5_w_glu.astype(BF16), lru_w_r.astype(BF16), lru_w_i.astype(BF16)
    keys = ("re", "im", "lru", "conv")
    outs_p = {key: [] for key in keys}
    outs_s = {key: [] for key in keys}
    gla_p = gla_s = None

    def record(outs, sr, si):
        outs["re"].append(sr.reshape(-1, n_groups, n_p))
        outs["im"].append(si.reshape(-1, n_groups, n_p))

    for layer in range(depth):
        final = layer == depth - 1
        bm_out = (bp, sp) if final and in_kernel_order else None
        mlp = (_row(norm_mlp[layer]), (w_up_f, layer), (w_down_f, layer), g_fin)
        if layer % 2 == 0:
            e = layer // 2
            ab_re, ab_im, bb_re, bb_im = _s5_discretise(s5_a_re[e].astype(F32), s5_a_im[e].astype(F32), s5_log_dt[e].astype(F32),
                                                        s5_b_re[e].astype(F32), s5_b_im[e].astype(F32))
            wc = jnp.concatenate([_block_diag_out(s5_c_re[e].astype(F32)), -_block_diag_out(s5_c_im[e].astype(F32))], axis=1)
            prm = (_row(norm_mix_even[e]), (w_in_main, e), (w_in_u, e), (w_in_a, e), (w_a2, e), _row(gla_b_a2[e]), _row(gla_norm[e]),
                   ab_re.reshape(1, -1), ab_im.reshape(1, -1), _block_diag_in(bb_re).astype(BF16), _block_diag_in(bb_im).astype(BF16),
                   wc.astype(BF16), _row(s5_d[e]), (w_glu_b, e), _row(s5_b_glu[e]), (w_out_even_b, e), (nh, dk, dv))
            (yp, w_up_l, w_down_l), gla_p, sr, si = _even_layer(yp, bp, None, gla_p, e, n_even, None, None, prm, mlp, final=final,
                                                                batch_major_out=bm_out)
            record(outs_p, sr, si)
            mlp = (mlp[0], w_up_l, w_down_l, g_fin)
            ys, gla_s, sr, si = _even_layer(ys, bs, state_gla, gla_s, e, n_even, state_s5_re[e], state_s5_im[e], prm, mlp, final=final,
                                            batch_major_out=None)
            record(outs_s, sr, si)
        else:
            o = layer // 2
            prm = (_row(norm_mix_odd[o]), (w_in_odd_b, o), conv_w[o].astype(F32), _row(conv_b[o]),
                   (w_r_b, o), _row(lru_b_r[o]), (w_i_b, o), _row(lru_b_i[o]), _row(lru_lam[o]), (w_out_odd_b, o))
            (yp, w_up_l, w_down_l), sl, sc = _odd_layer(yp, bp, None, None, prm, mlp, final=final, batch_major_out=bm_out)
            outs_p["lru"].append(sl)
            outs_p["conv"].append(sc)
            mlp = (mlp[0], w_up_l, w_down_l, g_fin)
            ys, sl, sc = _odd_layer(ys, bs, state_lru[o], state_conv[o], prm, mlp, final=final, batch_major_out=None)
            outs_s["lru"].append(sl)
            outs_s["conv"].append(sc)

    stack = lambda outs: tuple(jnp.stack(outs[key]) for key in keys)
    return (yp if in_kernel_order else _from_rows(yp, bp, sp), _from_rows(ys, bs, ss), gla_p) + stack(outs_p) + (gla_s,) + stack(outs_s)
```

```python
import functools
import math

import jax
import jax.numpy as jnp
from jax import lax
from jax.experimental import pallas as pl
from jax.experimental.pallas import tpu as pltpu

F32 = jnp.float32
BF16 = jnp.bfloat16

NORM_EPS = 1e-6
GLA_TAU = 16.0
GLA_CHUNK = 16
LRU_C = 8.0

SUBLANES = 8
LANES = 128
ROW_TILE = 512
INPROJ_TILE = 1024
FF_CHUNK = 1024
VMEM_LIMIT = 48 * 1024 * 1024


def _params(n_axes):
    return pltpu.CompilerParams(dimension_semantics=("arbitrary",) * n_axes, vmem_limit_bytes=VMEM_LIMIT)


def _const_spec(shape):
    zeros = (0,) * len(shape)
    return pl.BlockSpec(shape, lambda *_: zeros, pipeline_mode=pl.Buffered(1))


def _resident(p):
    if not isinstance(p, tuple):
        return _const_spec(p.shape)
    stacked, layer = p
    index = (layer,) + (0,) * (stacked.ndim - 1)
    return pl.BlockSpec((None,) + stacked.shape[1:], lambda *_: index, pipeline_mode=pl.Buffered(1))


def _resident_array(p):
    return p[0] if isinstance(p, tuple) else p


def _resident_shape(p):
    return p[0].shape[1:] if isinstance(p, tuple) else p.shape


def _dot(a, b):
    return jnp.dot(a, b, preferred_element_type=F32)


def _dot_nt(a, b):
    return lax.dot_general(a, b, (((1,), (1,)), ((), ())), preferred_element_type=F32)


def _dot_tn(a, b):
    return lax.dot_general(a, b, (((0,), (0,)), ((), ())), preferred_element_type=F32)


def _rmsnorm(x, g):
    return x * lax.rsqrt(jnp.mean(x * x, axis=-1, keepdims=True) + NORM_EPS) * g


def _log_sigmoid(z):
    return jnp.minimum(z, 0.0) - jnp.log1p(jnp.exp(-jnp.abs(z)))


def _softplus(z):
    return jnp.maximum(z, 0.0) + jnp.log1p(jnp.exp(-jnp.abs(z)))


_GELU_C1 = math.sqrt(2.0 / math.pi)
_GELU_C2 = _GELU_C1 * 0.044715


def _gelu(x):
    half = 0.5 * x
    return half + half * jnp.tanh(x * (_GELU_C1 + _GELU_C2 * (x * x)))


def _sigmoid(x):
    return 0.5 * jnp.tanh(0.5 * x) + 0.5


def _rows(ref, start, size):
    return ref[pl.ds(pl.multiple_of(start, size), size), :]


def _interleave_rows(x_ref, slab_ref):
    nb, ts, d = x_ref.shape
    for b in range(nb):
        for l in range(d // LANES):
            slab_ref[l, pl.ds(b, ts, stride=nb), :] = x_ref[b, :, l * LANES:(l + 1) * LANES]
    return jnp.concatenate([slab_ref[l] for l in range(d // LANES)], axis=1)


def _deinterleave_rows(y, slab_ref, o_ref):
    nb, ts, d = o_ref.shape
    for l in range(d // LANES):
        slab_ref[l] = y[:, l * LANES:(l + 1) * LANES]
    for b in range(nb):
        for l in range(d // LANES):
            o_ref[b, :, l * LANES:(l + 1) * LANES] = slab_ref[l, pl.ds(b, ts, stride=nb), :]


def _row_tiles(rows, tile=ROW_TILE):
    tm = tile if rows % tile == 0 else min(ROW_TILE, rows)
    return tm, (rows // tm,), lambda width, col=0: pl.BlockSpec((tm, width), lambda i: (i, col))


def _inproj_even_body(x_ref, g_ref, w_ref, wu_ref, wa_ref, wa2_ref, ba2_ref, q_ref, k_ref, v_ref, gt_ref, u_ref, la_ref, *rest,
                      kw, vw):
    if rest:
        xt_ref, slab_ref = rest
        x = _interleave_rows(x_ref, slab_ref)
        xt_ref[...] = x
    else:
        x = x_ref[...]
    xn = _rmsnorm(x, g_ref[...]).astype(BF16)
    o1, o2 = kw, 2 * kw
    o3 = o2 + vw
    o4 = o3 + vw
    q_ref[...] = _dot(xn, w_ref[:, 0:o1])
    k_ref[...] = _dot(xn, w_ref[:, o1:o2])
    v_ref[...] = _dot(xn, w_ref[:, o2:o3]).astype(BF16)
    gt_ref[...] = _dot(xn, w_ref[:, o3:o4])
    u_ref[...] = _dot(xn, wu_ref[...])
    a_lr = _dot(xn, wa_ref[...]).astype(BF16)
    z = _dot(a_lr, wa2_ref[...]) + ba2_ref[...]
    la_ref[...] = _log_sigmoid(z) * (1.0 / GLA_TAU)


def _inproj_even(x, g, w, wu, wa, wa2, ba2, *, kw, vw):
    batch_major = x.ndim == 3
    d = x.shape[-1]
    rows = x.size // d
    sw = _resident_shape(wu)[1]
    widths = [kw, kw, vw, vw, sw, kw]
    dtypes = [F32, F32, BF16, F32, F32, F32]
    consts = (g, w, wu, wa, wa2, ba2)
    tm, grid, blk = _row_tiles(rows, INPROJ_TILE)
    x_spec, scratch = blk(d), []
    if batch_major:
        nb = x.shape[0]
        x_spec = pl.BlockSpec((nb, tm // nb, d), lambda i: (0, i, 0))
        widths.append(d)
        dtypes.append(F32)
        scratch = [pltpu.VMEM((d // LANES, tm, LANES), F32)]
    return pl.pallas_call(
        functools.partial(_inproj_even_body, kw=kw, vw=vw),
        grid=grid,
        in_specs=[x_spec] + [_resident(p) for p in consts],
        out_specs=tuple(blk(wd) for wd in widths),
        out_shape=tuple(jax.ShapeDtypeStruct((rows, wd), dt) for wd, dt in zip(widths, dtypes)),
        scratch_shapes=scratch,
        compiler_params=_params(1),
        name="inproj_even",
    )(x, *[_resident_array(p) for p in consts])


def _inproj_odd_body(x_ref, g_ref, w_ref, gate_ref, xin_ref, *, width):
    xn = _rmsnorm(x_ref[...], g_ref[...]).astype(BF16)
    gate_ref[...] = _dot(xn, w_ref[:, 0:width])
    xin_ref[...] = _dot(xn, w_ref[:, width:2 * width])


def _inproj_odd(x2d, g, w):
    rows, d = x2d.shape
    width = _resident_shape(w)[1] // 2
    _, grid, blk = _row_tiles(rows, INPROJ_TILE)
    shape = jax.ShapeDtypeStruct((rows, width), F32)
    return pl.pallas_call(
        functools.partial(_inproj_odd_body, width=width),
        grid=grid,
        in_specs=[blk(d), _resident(g), _resident(w)],
        out_specs=(blk(width), blk(width)),
        out_shape=(shape, shape),
        compiler_params=_params(1),
        name="inproj_odd",
    )(x2d, _resident_array(g), _resident_array(w))


def _out_mlp_cast_body(x_ref, a1_ref, a2_ref, wo_ref, g_ref, gf_ref, wu_hbm, wd_hbm, o_ref, wu_out, wd_out,
                       wu_bf, wd_bf, stage, sem_in, sem_out, *slab, final, layer):
    i = pl.program_id(0)
    d, d_ff = wu_bf.shape
    n_up = d_ff // FF_CHUNK
    n_chunks = 2 * n_up

    def fetch(k):
        if k < n_up:
            src = wu_hbm.at[layer, :, pl.ds(k * FF_CHUNK, FF_CHUNK)]
        else:
            src = wd_hbm.at[layer, pl.ds((k - n_up) * FF_CHUNK, FF_CHUNK), :]
        return pltpu.make_async_copy(src, stage.at[k % 2], sem_in.at[k % 2])

    def publish():
        return (pltpu.make_async_copy(wu_bf, wu_out, sem_out.at[0]), pltpu.make_async_copy(wd_bf, wd_out, sem_out.at[1]))

    @pl.when(i == 0)
    def _():
        fetch(0).start()
        for k in range(n_chunks):
            if k + 1 < n_chunks:
                fetch(k + 1).start()
            fetch(k).wait()
            w = stage[k % 2].astype(BF16)
            if k < n_up:
                wu_bf[:, k * FF_CHUNK:(k + 1) * FF_CHUNK] = w
            else:
                wd_bf[(k - n_up) * FF_CHUNK:(k - n_up + 1) * FF_CHUNK, :] = w
        for copy in publish():
            copy.start()

    _out_mlp_body(x_ref, a1_ref, a2_ref, wo_ref, g_ref, wu_bf, wd_bf, gf_ref, o_ref, *slab, final=final)

    @pl.when(i == pl.num_programs(0) - 1)
    def _():
        for copy in publish():
            copy.wait()


def _out_mlp_body(x_ref, a1_ref, a2_ref, wo_ref, g_ref, wu_ref, wd_ref, gf_ref, o_ref, *slab, final):
    half = a1_ref.shape[1]
    mix = _dot(a1_ref[...], wo_ref[0:half, :]) + _dot(a2_ref[...], wo_ref[half:2 * half, :])
    x1 = x_ref[...] + mix
    xn = _rmsnorm(x1, g_ref[...]).astype(BF16)
    acc = x1
    for c in range(wu_ref.shape[1] // FF_CHUNK):
        h = _dot(xn, wu_ref[:, c * FF_CHUNK:(c + 1) * FF_CHUNK])
        h = jnp.square(jnp.maximum(h, 0.0)).astype(BF16)
        acc = acc + _dot(h, wd_ref[c * FF_CHUNK:(c + 1) * FF_CHUNK, :])
    if final:
        acc = _rmsnorm(acc, gf_ref[...])
    if slab:
        _deinterleave_rows(acc, slab[0], o_ref)
    else:
        o_ref[...] = acc


def _out_mlp(x2d, a1, a2, wo, g, wu, wd, gf, *, final, batch_major_out=None):
    rows, d = x2d.shape
    half = d // 2
    consts = (wo, g, wu, wd, gf)
    tm, grid, blk = _row_tiles(rows)
    out_spec, out_shape, scratch = blk(d), jax.ShapeDtypeStruct((rows, d), F32), []
    if batch_major_out is not None:
        nb, s = batch_major_out
        out_spec = pl.BlockSpec((nb, tm // nb, d), lambda i: (0, i, 0))
        out_shape = jax.ShapeDtypeStruct((nb, s, d), F32)
        scratch = [pltpu.VMEM((d // LANES, tm, LANES), F32)]
    if _resident_array(wu).dtype == F32:
        (w_up, layer), (w_down, _) = wu, wd
        d_ff = w_up.shape[2]
        assert d == FF_CHUNK and d_ff % FF_CHUNK == 0
        any_spec = pl.BlockSpec(memory_space=pl.ANY)
        row_consts = (wo, g, gf)
        return pl.pallas_call(
            functools.partial(_out_mlp_cast_body, final=final, layer=layer),
            grid=grid,
            in_specs=[blk(d), blk(half, 0), blk(half, a1.shape[1] // half - 1)] + [_resident(p) for p in row_consts] + [any_spec, any_spec],
            out_specs=(out_spec, any_spec, any_spec),
            out_shape=(out_shape, jax.ShapeDtypeStruct((d, d_ff), BF16), jax.ShapeDtypeStruct((d_ff, d), BF16)),
            scratch_shapes=[pltpu.VMEM((d, d_ff), BF16), pltpu.VMEM((d_ff, d), BF16), pltpu.VMEM((2, FF_CHUNK, FF_CHUNK), F32),
                            pltpu.SemaphoreType.DMA((2,)), pltpu.SemaphoreType.DMA((2,))] + scratch,
            compiler_params=_params(1),
            name="out_mlp_cast",
        )(x2d, a1, a2, *[_resident_array(p) for p in row_consts], w_up, w_down)
    return pl.pallas_call(
        functools.partial(_out_mlp_body, final=final),
        grid=grid,
        in_specs=[blk(d), blk(half, 0), blk(half, a1.shape[1] // half - 1)] + [_resident(p) for p in consts],
        out_specs=out_spec,
        out_shape=out_shape,
        scratch_shapes=scratch,
        compiler_params=_params(1),
        name="out_mlp",
    )(x2d, a1, a2, *[_resident_array(p) for p in consts])


def _time_grid(total_rows, n_bg, tblock, pack_groups=False):
    steps = total_rows // (n_bg * SUBLANES)
    tb = min(tblock, steps)
    n_t = steps // tb
    groups = 1
    if pack_groups and n_t == 1:
        groups = max(1, min(n_bg, ROW_TILE // (steps * SUBLANES)))
        while n_bg % groups:
            groups -= 1
    blk = lambda width: pl.BlockSpec((groups * tb * SUBLANES, width), lambda bg, t: (bg * n_t + t, 0))
    per_group = lambda rows, width: pl.BlockSpec((groups * rows, width), lambda bg, t: (bg, 0))
    return tb, groups, (n_bg // groups, n_t), blk, per_group


def _gla_body(q_ref, k_ref, la_ref, v_ref, gt_ref, gh_ref, s0_ref, *rest, c, nh, dk, dv, transposed):
    o_ref, sfin_ref, st_ref = rest[-3:]
    tb = pl.program_id(1)
    rows = SUBLANES * c
    n_chunks = q_ref.shape[0] // rows
    dk_shift = int(math.log2(dk))
    heads_per_tile = LANES // dk

    @pl.when(tb == 0)
    def _():
        st_ref[...] = s0_ref[...]

    def iota(shape, axis):
        return lax.broadcasted_iota(jnp.int32, shape, axis)

    causal = ((iota((rows, rows), 0) & 7) == (iota((rows, rows), 1) & 7)) & (iota((rows, rows), 1) <= iota((rows, rows), 0))
    emask = (iota((rows, SUBLANES * dk), 0) & 7) == lax.shift_right_logical(iota((rows, SUBLANES * dk), 1), dk_shift)
    dmask = iota((SUBLANES, SUBLANES * dk), 0) == lax.shift_right_logical(iota((SUBLANES, SUBLANES * dk), 1), dk_shift)
    lo_rows = iota((rows, LANES), 1) < dk
    lo_8 = iota((SUBLANES, LANES), 1) < dk

    def expand(tile, h, lo=lo_rows, mask=emask):
        rolled = pltpu.roll(tile, dk, axis=1)
        d = jnp.where(lo, tile, rolled) if h % heads_per_tile == 0 else jnp.where(lo, rolled, tile)
        return jnp.where(mask, jnp.concatenate([d] * (SUBLANES * dk // LANES), axis=1), 0.0)

    def chunk(n, carry):
        r0 = n * rows
        q = _rows(q_ref, r0, rows)
        k = _rows(k_ref, r0, rows)
        la = _rows(la_ref, r0, rows)
        v = _rows(v_ref, r0, rows)
        gt = _rows(gt_ref, r0, rows)
        acc = jnp.zeros((SUBLANES, nh * dk), F32)
        pieces = []
        for t in range(c):
            acc = acc + la[t * SUBLANES:(t + 1) * SUBLANES, :]
            pieces.append(acc)
        cum = jnp.concatenate(pieces, axis=0)
        last = jnp.concatenate([acc] * c, axis=0)
        q_t = q * jnp.exp(cum) * (dk ** -0.5)
        k_t = k * jnp.exp(-cum)
        k_end = k * jnp.exp(last - cum)
        dec = jnp.exp(acc)
        pad = jnp.zeros((LANES - SUBLANES, LANES), F32)
        outs = []
        for h in range(nh):
            tile = (h // heads_per_tile) * LANES
            lane0 = (h % heads_per_tile) * dk
            sel = lo_rows if lane0 == 0 else jnp.logical_not(lo_rows)
            q_tile = q_t[:, tile:tile + LANES]
            scores = _dot_nt(jnp.where(sel, q_tile, 0.0).astype(BF16), k_t[:, tile:tile + LANES].astype(BF16))
            scores = jnp.where(causal, scores, 0.0).astype(BF16)
            v_h = v[:, h * dv:(h + 1) * dv]
            q_exp = expand(q_tile, h).astype(BF16)
            k_exp = expand(k_end[:, tile:tile + LANES], h).astype(BF16)
            if transposed:
                st = st_ref[h]
                o = _dot(scores, v_h) + _dot_nt(q_exp, st.astype(BF16))
                dec_row = jnp.sum(expand(dec[:, tile:tile + LANES], h, lo_8, dmask), axis=0, keepdims=True)
                st_ref[h] = st * dec_row + _dot_tn(v_h, k_exp)
            else:
                st = jnp.concatenate([st_ref[b, h] for b in range(SUBLANES)], axis=0)
                o = _dot(scores, v_h) + _dot(q_exp, st.astype(BF16))
                upd = _dot_tn(k_exp, v_h)
                dec_t = jnp.concatenate([dec[:, tile:tile + LANES], pad], axis=0).T
                for b in range(SUBLANES):
                    dec_b = jnp.broadcast_to(dec_t[lane0:lane0 + dk, b:b + 1], (dk, dv))
                    st_ref[b, h] = st[b * dk:(b + 1) * dk, :] * dec_b + upd[b * dk:(b + 1) * dk, :]
            o = _rmsnorm(o, gh_ref[:, h * dv:(h + 1) * dv])
            g_h = gt[:, h * dv:(h + 1) * dv]
            outs.append(o * (g_h * _sigmoid(g_h)))
        o_ref[pl.ds(pl.multiple_of(r0, rows), rows), :] = jnp.concatenate(outs, axis=1).astype(BF16)
        return carry

    lax.fori_loop(0, n_chunks, chunk, 0, unroll=min(n_chunks, 4))

    @pl.when(tb == pl.num_programs(1) - 1)
    def _():
        sfin_ref[...] = st_ref[...]


def _gla_state_in(s):
    b, h, dk, dv = s.shape
    return jnp.transpose(s.reshape(b // SUBLANES, SUBLANES, h, dk, dv), (0, 2, 4, 1, 3)).reshape(b // SUBLANES * h, dv, SUBLANES * dk)


def _gla_state_out(st, nh, dk):
    dv = st.shape[1]
    n_bg = st.shape[0] // nh
    return jnp.transpose(st.reshape(n_bg, nh, dv, SUBLANES, dk), (0, 3, 1, 4, 2)).reshape(n_bg * SUBLANES, nh, dk, dv)


def _gla(q, k, la, v, gt, gh, states, layer, n_layers, acc, nb, dims, *, c, tblock):
    nh, dk, dv = dims
    n_bg = nb // SUBLANES
    tb, _, grid, blk, _ = _time_grid(q.shape[0], n_bg, tblock)
    transposed = q.shape[0] // n_bg > SUBLANES * c
    body = functools.partial(_gla_body, c=c, nh=nh, dk=dk, dv=dv, transposed=transposed)
    row_specs = [blk(nh * dk), blk(nh * dk), blk(nh * dk), blk(nh * dv), blk(nh * dv), _const_spec(gh.shape)]
    og_shape = jax.ShapeDtypeStruct((q.shape[0], nh * dv), BF16)
    if transposed:
        st_block = (nh, dv, SUBLANES * dk)
        st_spec = pl.BlockSpec(st_block, lambda bg, t: (bg, 0, 0))
        s_in = jnp.zeros((n_bg * nh, dv, SUBLANES * dk), F32) if states is None else _gla_state_in(states[layer].astype(F32))
        og, s_fin = pl.pallas_call(
            body, grid=grid, in_specs=row_specs + [st_spec], out_specs=(blk(nh * dv), st_spec),
            out_shape=(og_shape, jax.ShapeDtypeStruct(s_in.shape, F32)), scratch_shapes=[pltpu.VMEM(st_block, F32)],
            compiler_params=_params(2), name="gla",
        )(q, k, la, v, gt, gh, s_in)
        s_new = _gla_state_out(s_fin, nh, dk)[None]
        return og, (jnp.pad(s_new, ((0, n_layers - 1),) + ((0, 0),) * 4) if acc is None else lax.dynamic_update_slice(acc, s_new, (layer, 0, 0, 0, 0)))
    st_block = (SUBLANES, nh, dk, dv)
    stacked_spec = pl.BlockSpec((None,) + st_block, lambda bg, t: (layer, bg, 0, 0, 0))
    if states is None:
        s_in, in_spec = jnp.zeros((nb, nh, dk, dv), F32), pl.BlockSpec(st_block, lambda bg, t: (bg, 0, 0, 0))
    else:
        s_in, in_spec = states.astype(F32), stacked_spec
    if acc is None:
        acc = jnp.zeros((n_layers, nb, nh, dk, dv), F32)
    return pl.pallas_call(
        body, grid=grid, in_specs=row_specs + [in_spec, pl.BlockSpec(memory_space=pl.ANY)], out_specs=(blk(nh * dv), stacked_spec),
        out_shape=(og_shape, jax.ShapeDtypeStruct(acc.shape, F32)), scratch_shapes=[pltpu.VMEM(st_block, F32)],
        input_output_aliases={7: 1}, compiler_params=_params(2), name="gla",
    )(q, k, la, v, gt, gh, s_in, acc)


def _s5_disc_body(a_re_ref, a_im_ref, log_dt_ref, b_re_ref, b_im_ref, ab_re_ref, ab_im_ref, bb_re_ref, bb_im_ref):
    a_re = a_re_ref[...]
    a_im = a_im_ref[...]
    dt = jnp.exp(log_dt_ref[...])
    mag = jnp.exp(dt * a_re)
    ab_re = mag * jnp.cos(dt * a_im)
    ab_im = mag * jnp.sin(dt * a_im)
    den = a_re * a_re + a_im * a_im
    coef_re = ((ab_re - 1.0) * a_re + ab_im * a_im) / den
    coef_im = (ab_im * a_re - (ab_re - 1.0) * a_im) / den
    b_re = b_re_ref[...]
    b_im = b_im_ref[...]
    ab_re_ref[...] = ab_re
    ab_im_ref[...] = ab_im
    bb_re_ref[...] = coef_re * b_re - coef_im * b_im
    bb_im_ref[...] = coef_re * b_im + coef_im * b_re


def _s5_discretise(a_re, a_im, log_dt, b_re, b_im):
    g, p, ch = b_re.shape
    rep = lambda t: jnp.repeat(t, ch, axis=0)
    flat = lambda t: jnp.transpose(t, (0, 2, 1)).reshape(g * ch, p)
    args = (rep(a_re), rep(a_im), rep(jnp.broadcast_to(log_dt[:, None], (g, p))), flat(b_re), flat(b_im))
    spec = _const_spec((g * ch, p))
    shape = jax.ShapeDtypeStruct((g * ch, p), F32)
    ab_re, ab_im, bb_re, bb_im = pl.pallas_call(
        _s5_disc_body, grid=(1,), in_specs=[spec] * 5, out_specs=(spec,) * 4, out_shape=(shape,) * 4,
        compiler_params=_params(1), name="s5_discretise",
    )(*args)
    return ab_re[::ch], ab_im[::ch], bb_re.reshape(g, ch, p), bb_im.reshape(g, ch, p)


def _block_diag_in(w_gcp):
    g, ch, p = w_gcp.shape
    gpt = LANES // ch
    w = w_gcp.reshape(g // gpt, gpt, ch, p)
    eye = jnp.eye(gpt, dtype=w.dtype)
    return jnp.einsum("ab,jacp->jacbp", eye, w).reshape(g // gpt, gpt * ch, gpt * p)


def _block_diag_out(w_gcp):
    g, ch, p = w_gcp.shape
    gpt = LANES // ch
    w = w_gcp.reshape(g // gpt, gpt, ch, p)
    eye = jnp.eye(gpt, dtype=w.dtype)
    return jnp.einsum("ab,jacp->japbc", eye, w).reshape(g // gpt, gpt * p, gpt * ch)


def _s5_body(u_ref, ab_re_ref, ab_im_ref, wb_re_ref, wb_im_ref, wc_ref, d_ref, wglu_ref, bglu_ref, h0_re_ref, h0_im_ref,
             z_ref, hf_re_ref, hf_im_ref, h_re_ref, h_im_ref, bu_re_ref, bu_im_ref, y_ref):
    step = pl.program_id(1)
    rows = u_ref.shape[0]
    groups = h_re_ref.shape[0] // SUBLANES
    tb = rows // SUBLANES // groups
    n_tiles = wb_re_ref.shape[0]
    sw = wb_re_ref.shape[2]

    @pl.when(step == 0)
    def _():
        h_re_ref[...] = h0_re_ref[...]
        h_im_ref[...] = h0_im_ref[...]

    for j in range(n_tiles):
        u_j = u_ref[:, j * LANES:(j + 1) * LANES]
        u_b = u_j.astype(BF16)
        bu_re_ref[j] = _dot(u_b, wb_re_ref[j])
        bu_im_ref[j] = _dot(u_b, wb_im_ref[j])
        a_re = jnp.broadcast_to(ab_re_ref[:, j * sw:(j + 1) * sw], (SUBLANES, sw))
        a_im = jnp.broadcast_to(ab_im_ref[:, j * sw:(j + 1) * sw], (SUBLANES, sw))
        for grp in range(groups):
            state_rows = slice(grp * SUBLANES, (grp + 1) * SUBLANES)
            h_re = h_re_ref[state_rows, j * sw:(j + 1) * sw]
            h_im = h_im_ref[state_rows, j * sw:(j + 1) * sw]
            for t in range(grp * tb, (grp + 1) * tb):
                sl = slice(t * SUBLANES, (t + 1) * SUBLANES)
                h_re, h_im = (a_re * h_re - a_im * h_im + bu_re_ref[j, sl, :], a_re * h_im + a_im * h_re + bu_im_ref[j, sl, :])
                bu_re_ref[j, sl, :] = h_re
                bu_im_ref[j, sl, :] = h_im
            h_re_ref[state_rows, j * sw:(j + 1) * sw] = h_re
            h_im_ref[state_rows, j * sw:(j + 1) * sw] = h_im
        h_cat = jnp.concatenate([bu_re_ref[j].astype(BF16), bu_im_ref[j].astype(BF16)], axis=1)
        y_ref[:, j * LANES:(j + 1) * LANES] = _dot(h_cat, wc_ref[j]) + d_ref[:, j * LANES:(j + 1) * LANES] * u_j

    z = _gelu(y_ref[...])
    gate = _sigmoid(_dot(z.astype(BF16), wglu_ref[...]) + bglu_ref[...])
    z_ref[...] = (z * gate).astype(BF16)

    @pl.when(step == pl.num_programs(1) - 1)
    def _():
        hf_re_ref[...] = h_re_ref[...]
        hf_im_ref[...] = h_im_ref[...]


def _s5(u, ab_re, ab_im, wb_re, wb_im, wc, d_skip, wglu, bglu, h0_re, h0_im, *, tblock):
    total, width = u.shape
    n_state = ab_re.shape[1]
    n_bg = h0_re.shape[0] // SUBLANES
    sw = wb_re.shape[2]
    tb, groups, grid, blk, per_group = _time_grid(total, n_bg, tblock, pack_groups=True)
    rows = groups * tb * SUBLANES
    st_spec = per_group(SUBLANES, n_state)
    consts = (ab_re, ab_im, wb_re, wb_im, wc, d_skip, wglu, bglu)
    st_shape = jax.ShapeDtypeStruct((n_bg * SUBLANES, n_state), F32)
    return pl.pallas_call(
        _s5_body,
        grid=grid,
        in_specs=[blk(width)] + [_resident(p) for p in consts] + [st_spec, st_spec],
        out_specs=(blk(width), st_spec, st_spec),
        out_shape=(jax.ShapeDtypeStruct((total, width), BF16), st_shape, st_shape),
        scratch_shapes=[pltpu.VMEM((groups * SUBLANES, n_state), F32), pltpu.VMEM((groups * SUBLANES, n_state), F32),
                        pltpu.VMEM((wb_re.shape[0], rows, sw), F32), pltpu.VMEM((wb_re.shape[0], rows, sw), F32),
                        pltpu.VMEM((rows, width), F32)],
        compiler_params=_params(2),
        name="s5",
    )(u, *[_resident_array(p) for p in consts], h0_re, h0_im)


def _lru_block(n, xbuf_ref, gate_ref, params, a_ref, b_ref, h, rows):
    cw_ref, cb_ref, wr_ref, br_ref, wi_ref, bi_ref, lam_ref = params
    bw = wr_ref.shape[1]
    cols = slice(n * bw, (n + 1) * bw)
    a_blk, b_blk = a_ref.at[n % 2], b_ref.at[n % 2]
    taps = cw_ref.shape[0]
    groups = h.shape[0] // SUBLANES
    seg = rows // groups
    pitch = seg + (taps - 1) * SUBLANES
    pieces = []
    for grp in range(groups):
        xc = cb_ref[:, cols] + xbuf_ref[grp * pitch:grp * pitch + seg, cols] * cw_ref[0:1, cols]
        for j in range(1, taps):
            xc = xc + xbuf_ref[grp * pitch + j * SUBLANES:grp * pitch + j * SUBLANES + seg, cols] * cw_ref[j:j + 1, cols]
        pieces.append(xc)
    xc = jnp.concatenate(pieces, axis=0) if groups > 1 else pieces[0]
    xb = xc.astype(BF16)
    t_i = jnp.tanh(_dot(xb, wi_ref[n]) + bi_ref[:, cols])
    t_r = jnp.tanh(_dot(xb, wr_ref[n]) + br_ref[:, cols])
    half_rate = (-0.5 * LRU_C) * _softplus(-lam_ref[:, cols])
    log_a = t_r * half_rate + half_rate
    a_blk[...] = jnp.exp(log_a)
    th = jnp.tanh(log_a)
    w = th / (th - 1.0)
    root = jnp.where(w > 0.0, w * lax.rsqrt(w), 0.0)
    scaled = xc * (0.5 * math.sqrt(2.0))
    b_blk[...] = root * (t_i * scaled + scaled)
    finals = []
    for grp in range(groups):
        h_g = h[grp * SUBLANES:(grp + 1) * SUBLANES, :]
        for t in range(grp * seg // SUBLANES, (grp + 1) * seg // SUBLANES):
            sl = slice(t * SUBLANES, (t + 1) * SUBLANES)
            h_g = a_blk[sl, :] * h_g + b_blk[sl, :]
            b_blk[sl, :] = h_g
        finals.append(h_g)
    h = jnp.concatenate(finals, axis=0) if groups > 1 else finals[0]
    return (_gelu(gate_ref[:, cols]) * b_blk[...]).astype(BF16), h


def _lru_body(gate_ref, xin_ref, conv0_ref, h0_ref, cw_ref, cb_ref, wr_ref, br_ref, wi_ref, bi_ref, lam_ref,
              o_ref, hf_ref, convf_ref, xbuf_ref, a_ref, b_ref, h_ref):
    step = pl.program_id(1)
    rows, width = xin_ref.shape
    tail = (cw_ref.shape[0] - 1) * SUBLANES
    n_blocks = wr_ref.shape[0]
    bw = wr_ref.shape[1]
    groups = h_ref.shape[0] // SUBLANES
    seg = rows // groups
    pitch = seg + tail

    @pl.when(step == 0)
    def _():
        for grp in range(groups):
            xbuf_ref[grp * pitch:grp * pitch + tail, :] = conv0_ref[grp * tail:(grp + 1) * tail, :]
        h_ref[...] = h0_ref[...]

    for grp in range(groups):
        xbuf_ref[grp * pitch + tail:(grp + 1) * pitch, :] = xin_ref[grp * seg:(grp + 1) * seg, :]
    params = (cw_ref, cb_ref, wr_ref, br_ref, wi_ref, bi_ref, lam_ref)
    for n in range(n_blocks):
        cols = slice(n * bw, (n + 1) * bw)
        o_ref[:, cols], h_ref[:, cols] = _lru_block(n, xbuf_ref, gate_ref, params, a_ref, b_ref, h_ref[:, cols], rows)
    for grp in range(groups):
        xbuf_ref[grp * pitch:grp * pitch + tail, :] = xbuf_ref[grp * pitch + seg:(grp + 1) * pitch, :]

    @pl.when(step == pl.num_programs(1) - 1)
    def _():
        hf_ref[...] = h_ref[...]
        for grp in range(groups):
            convf_ref[grp * tail:(grp + 1) * tail, :] = xbuf_ref[grp * pitch:grp * pitch + tail, :]


def _lru(gate, xin, conv0, h0, cw, cb, wr, br, wi, bi, lam, *, tblock):
    total, width = xin.shape
    n_bg = h0.shape[0] // SUBLANES
    tail = (cw.shape[0] - 1) * SUBLANES
    bw = _resident_shape(wr)[1]
    tb, groups, grid, blk, per_group = _time_grid(total, n_bg, tblock, pack_groups=True)
    rows = groups * tb * SUBLANES
    h_spec = per_group(SUBLANES, width)
    c_spec = per_group(tail, width)
    consts = (cw, cb, wr, br, wi, bi, lam)
    return pl.pallas_call(
        _lru_body,
        grid=grid,
        in_specs=[blk(width), blk(width), c_spec, h_spec] + [_resident(p) for p in consts],
        out_specs=(blk(width), h_spec, c_spec),
        out_shape=(jax.ShapeDtypeStruct((total, width), BF16), jax.ShapeDtypeStruct((n_bg * SUBLANES, width), F32),
                   jax.ShapeDtypeStruct((n_bg * tail, width), F32)),
        scratch_shapes=[pltpu.VMEM((rows + groups * tail, width), F32), pltpu.VMEM((2, rows, bw), F32),
                        pltpu.VMEM((2, rows, bw), F32), pltpu.VMEM((groups * SUBLANES, width), F32)],
        compiler_params=_params(2),
        name="rg_lru",
    )(gate, xin, conv0, h0, *[_resident_array(p) for p in consts])


def _row(t):
    return t.reshape(1, -1).astype(F32)


def _to_rows(x):
    b, s, d = x.shape
    return jnp.transpose(x.reshape(b // SUBLANES, SUBLANES, s, d), (0, 2, 1, 3)).reshape(b * s, d)


def _from_rows(y, b, s):
    d = y.shape[-1]
    return jnp.transpose(y.reshape(b // SUBLANES, s, SUBLANES, d), (0, 2, 1, 3)).reshape(b, s, d)


def _even_layer(x, nb, gla_states, gla_acc, e, n_even, s_re, s_im, prm, mlp, *, final, batch_major_out):
    (g_norm, w_in, w_u, w_a, wa2, ba2, g_head, ab_re, ab_im, wb_re, wb_im, wc, d_skip, wglu, bglu, w_out, dims) = prm
    nh, dk, dv = dims
    q, k, v, gt, u, la, *reordered = _inproj_even(x, g_norm, w_in, w_u, w_a, wa2, ba2, kw=nh * dk, vw=nh * dv)
    x2d = reordered[0] if reordered else x
    steps = x2d.shape[0] // nb
    og, gla_acc = _gla(q, k, la, v, gt, g_head, gla_states, e, n_even, gla_acc, nb, dims, c=math.gcd(steps, GLA_CHUNK), tblock=128)
    n_state = ab_re.shape[1]
    h0_re = jnp.zeros((nb, n_state), F32) if s_re is None else s_re.astype(F32).reshape(nb, n_state)
    h0_im = jnp.zeros((nb, n_state), F32) if s_im is None else s_im.astype(F32).reshape(nb, n_state)
    zz, hf_re, hf_im = _s5(u, ab_re, ab_im, wb_re, wb_im, wc, d_skip, wglu, bglu, h0_re, h0_im, tblock=128)
    g_mlp, w_up, w_down, g_fin = mlp
    y = _out_mlp(x2d, og, zz, w_out, g_mlp, w_up, w_down, g_fin, final=final, batch_major_out=batch_major_out)
    return y, gla_acc, hf_re, hf_im


def _odd_layer(x2d, nb, s_lru, s_conv, prm, mlp, *, final, batch_major_out):
    (g_norm, w_in, cw, cb, wr, br, wi, bi, lam, w_out) = prm
    n_bg = nb // SUBLANES
    taps = cw.shape[0]
    width = _resident_shape(w_in)[1] // 2
    gate, xin = _inproj_odd(x2d, g_norm, w_in)
    h0 = jnp.zeros((nb, width), F32) if s_lru is None else s_lru.astype(F32)
    if s_conv is None:
        conv0 = jnp.zeros((n_bg * (taps - 1) * SUBLANES, width), F32)
    else:
        conv0 = jnp.transpose(s_conv.astype(F32).reshape(n_bg, SUBLANES, taps - 1, width), (0, 2, 1, 3)).reshape(-1, width)
    gh, hf, convf = _lru(gate, xin, conv0, h0, cw, cb, wr, br, wi, bi, lam, tblock=128)
    g_mlp, w_up, w_down, g_fin = mlp
    y = _out_mlp(x2d, gh, gh, w_out, g_mlp, w_up, w_down, g_fin, final=final, batch_major_out=batch_major_out)
    conv_new = jnp.transpose(convf.reshape(n_bg, taps - 1, SUBLANES, width), (0, 2, 1, 3)).reshape(nb, taps - 1, width)
    return y, hf, conv_new


def kernel(x_prompt, x_sample, state_gla, state_s5_re, state_s5_im, state_lru, state_conv, norm_mix_even, w_in_even, gla_w_a2, gla_b_a2, gla_norm, s5_a_re, s5_a_im, s5_log_dt, s5_b_re, s5_b_im, s5_c_re, s5_c_im, s5_d, s5_w_glu, s5_b_glu, w_out_even, norm_mix_odd, w_in_odd, conv_w, conv_b, lru_w_r, lru_b_r, lru_w_i, lru_b_i, lru_lam, w_out_odd, norm_mlp, w_up, w_down, norm_final):
    depth = norm_mlp.shape[0]
    nh, dk, dv = state_gla.shape[2:]
    rank = gla_w_a2.shape[1]
    n_groups, n_p, ch = s5_b_re.shape[1:]
    kw, vw = nh * dk, nh * dv
    assert rank <= LANES and LANES % ch == 0 and LANES % dk == 0 and dv == LANES
    bp, sp, _ = x_prompt.shape
    bs, ss, _ = x_sample.shape
    n_even = state_gla.shape[0]

    in_kernel_order = bp == SUBLANES and sp % (INPROJ_TILE // SUBLANES) == 0 and depth > 1
    yp = x_prompt.astype(F32) if in_kernel_order else _to_rows(x_prompt.astype(F32))
    ys = _to_rows(x_sample.astype(F32))
    g_fin = _row(norm_final)
    o4 = 2 * kw + 2 * vw
    o5 = o4 + rank
    w_in_main = w_in_even[:, :, :o4].astype(BF16)
    w_in_u = w_in_even[:, :, o5:].astype(BF16)
    w_in_a = jnp.pad(w_in_even[:, :, o4:o5], ((0, 0), (0, 0), (0, LANES - rank))).astype(BF16)
    w_a2 = jnp.pad(gla_w_a2, ((0, 0), (0, LANES - rank), (0, 0))).astype(BF16)
    w_up_f, w_down_f = w_up.astype(F32), w_down.astype(F32)
    w_out_even_b, w_out_odd_b, w_in_odd_b = w_out_even.astype(BF16), w_out_odd.astype(BF16), w_in_odd.astype(BF16)
    w_glu_b, w_r_b, w_i_b = s5_w_glu.astype(BF16), (0.5 * lru_w_r).astype(BF16), (0.5 * lru_w_i).astype(BF16)
    keys = ("re", "im", "lru", "conv")
    outs_p = {key: [] for key in keys}
    outs_s = {key: [] for key in keys}
    gla_p = gla_s = None

    def record(outs, sr, si):
        outs["re"].append(sr.reshape(-1, n_groups, n_p))
        outs["im"].append(si.reshape(-1, n_groups, n_p))

    for layer in range(depth):
        final = layer == depth - 1
        bm_out = (bp, sp) if final and in_kernel_order else None
        mlp = (_row(norm_mlp[layer]), (w_up_f, layer), (w_down_f, layer), g_fin)
        if layer % 2 == 0:
            e = layer // 2
            ab_re, ab_im, bb_re, bb_im = _s5_discretise(s5_a_re[e].astype(F32), s5_a_im[e].astype(F32), s5_log_dt[e].astype(F32),
                                                        s5_b_re[e].astype(F32), s5_b_im[e].astype(F32))
            wc = jnp.concatenate([_block_diag_out(s5_c_re[e].astype(F32)), -_block_diag_out(s5_c_im[e].astype(F32))], axis=1)
            prm = (_row(norm_mix_even[e]), (w_in_main, e), (w_in_u, e), (w_in_a, e), (w_a2, e), _row(gla_b_a2[e]), _row(gla_norm[e]),
                   ab_re.reshape(1, -1), ab_im.reshape(1, -1), _block_diag_in(bb_re).astype(BF16), _block_diag_in(bb_im).astype(BF16),
                   wc.astype(BF16), _row(s5_d[e]), (w_glu_b, e), _row(s5_b_glu[e]), (w_out_even_b, e), (nh, dk, dv))
            (yp, w_up_l, w_down_l), gla_p, sr, si = _even_layer(yp, bp, None, gla_p, e, n_even, None, None, prm, mlp, final=final,
                                                                batch_major_out=bm_out)
            record(outs_p, sr, si)
            mlp = (mlp[0], w_up_l, w_down_l, g_fin)
            ys, gla_s, sr, si = _even_layer(ys, bs, state_gla, gla_s, e, n_even, state_s5_re[e], state_s5_im[e], prm, mlp, final=final,
                                            batch_major_out=None)
            record(outs_s, sr, si)
        else:
            o = layer // 2
            prm = (_row(norm_mix_odd[o]), (w_in_odd_b, o), conv_w[o].astype(F32), _row(conv_b[o]),
                   (w_r_b, o), _row(0.5 * lru_b_r[o]), (w_i_b, o), _row(0.5 * lru_b_i[o]), _row(lru_lam[o]), (w_out_odd_b, o))
            (yp, w_up_l, w_down_l), sl, sc = _odd_layer(yp, bp, None, None, prm, mlp, final=final, batch_major_out=bm_out)
            outs_p["lru"].append(sl)
            outs_p["conv"].append(sc)
            mlp = (mlp[0], w_up_l, w_down_l, g_fin)
            ys, sl, sc = _odd_layer(ys, bs, state_lru[o], state_conv[o], prm, mlp, final=final, batch_major_out=None)
            outs_s["lru"].append(sl)
            outs_s["conv"].append(sc)

    stack = lambda outs: tuple(jnp.stack(outs[key]) for key in keys)
    return (yp if in_kernel_order else _from_rows(yp, bp, sp), _from_rows(ys, bs, ss), gla_p) + stack(outs_p) + (gla_s,) + stack(outs_s)
```

```python
import functools
import math

import jax
import jax.numpy as jnp
from jax import lax
from jax.experimental import pallas as pl
from jax.experimental.pallas import tpu as pltpu

F32 = jnp.float32
BF16 = jnp.bfloat16

NORM_EPS = 1e-6
GLA_TAU = 16.0
GLA_CHUNK = 16
LRU_C = 8.0

SUBLANES = 8
LANES = 128
ROW_TILE = 512
INPROJ_TILE = 1024
FF_CHUNK = 1024
VMEM_LIMIT = 48 * 1024 * 1024


def _params(n_axes):
    return pltpu.CompilerParams(dimension_semantics=("arbitrary",) * n_axes, vmem_limit_bytes=VMEM_LIMIT)


def _const_spec(shape):
    zeros = (0,) * len(shape)
    return pl.BlockSpec(shape, lambda *_: zeros, pipeline_mode=pl.Buffered(1))


def _resident(p):
    if not isinstance(p, tuple):
        return _const_spec(p.shape)
    stacked, layer = p
    index = (layer,) + (0,) * (stacked.ndim - 1)
    return pl.BlockSpec((None,) + stacked.shape[1:], lambda *_: index, pipeline_mode=pl.Buffered(1))


def _resident_array(p):
    return p[0] if isinstance(p, tuple) else p


def _resident_shape(p):
    return p[0].shape[1:] if isinstance(p, tuple) else p.shape


def _dot(a, b):
    return jnp.dot(a, b, preferred_element_type=F32)


def _dot_nt(a, b):
    return lax.dot_general(a, b, (((1,), (1,)), ((), ())), preferred_element_type=F32)


def _dot_tn(a, b):
    return lax.dot_general(a, b, (((0,), (0,)), ((), ())), preferred_element_type=F32)


def _rmsnorm(x, g):
    return x * lax.rsqrt(jnp.mean(x * x, axis=-1, keepdims=True) + NORM_EPS) * g


def _log_sigmoid(z):
    return jnp.minimum(z, 0.0) - jnp.log1p(jnp.exp(-jnp.abs(z)))


def _softplus(z):
    return jnp.maximum(z, 0.0) + jnp.log1p(jnp.exp(-jnp.abs(z)))


_GELU_C1 = math.sqrt(2.0 / math.pi)
_GELU_C2 = _GELU_C1 * 0.044715


def _gelu(x):
    half = 0.5 * x
    return half + half * jnp.tanh(x * (_GELU_C1 + _GELU_C2 * (x * x)))


def _sigmoid(x):
    return 0.5 * jnp.tanh(0.5 * x) + 0.5


def _rows(ref, start, size):
    return ref[pl.ds(pl.multiple_of(start, size), size), :]


def _interleave_rows(x_ref, slab_ref):
    nb, ts, d = x_ref.shape
    for b in range(nb):
        for l in range(d // LANES):
            slab_ref[l, pl.ds(b, ts, stride=nb), :] = x_ref[b, :, l * LANES:(l + 1) * LANES]
    return jnp.concatenate([slab_ref[l] for l in range(d // LANES)], axis=1)


def _deinterleave_rows(y, slab_ref, o_ref):
    nb, ts, d = o_ref.shape
    for l in range(d // LANES):
        slab_ref[l] = y[:, l * LANES:(l + 1) * LANES]
    for b in range(nb):
        for l in range(d // LANES):
            o_ref[b, :, l * LANES:(l + 1) * LANES] = slab_ref[l, pl.ds(b, ts, stride=nb), :]


def _row_tiles(rows, tile=ROW_TILE):
    tm = tile if rows % tile == 0 else min(ROW_TILE, rows)
    return tm, (rows // tm,), lambda width, col=0: pl.BlockSpec((tm, width), lambda i: (i, col))


def _inproj_even_body(x_ref, g_ref, w_ref, wu_ref, wa_ref, wa2_ref, ba2_ref, q_ref, k_ref, v_ref, gt_ref, u_ref, la_ref, *rest,
                      kw, vw):
    if rest:
        xt_ref, slab_ref = rest
        x = _interleave_rows(x_ref, slab_ref)
        xt_ref[...] = x
    else:
        x = x_ref[...]
    xn = _rmsnorm(x, g_ref[...]).astype(BF16)
    o1, o2 = kw, 2 * kw
    o3 = o2 + vw
    o4 = o3 + vw
    q_ref[...] = _dot(xn, w_ref[:, 0:o1])
    k_ref[...] = _dot(xn, w_ref[:, o1:o2])
    v_ref[...] = _dot(xn, w_ref[:, o2:o3]).astype(BF16)
    gt_ref[...] = _dot(xn, w_ref[:, o3:o4])
    u_ref[...] = _dot(xn, wu_ref[...])
    a_lr = _dot(xn, wa_ref[...]).astype(BF16)
    z = _dot(a_lr, wa2_ref[...]) + ba2_ref[...]
    la_ref[...] = _log_sigmoid(z) * (1.0 / GLA_TAU)


def _inproj_even(x, g, w, wu, wa, wa2, ba2, *, kw, vw):
    batch_major = x.ndim == 3
    d = x.shape[-1]
    rows = x.size // d
    sw = _resident_shape(wu)[1]
    widths = [kw, kw, vw, vw, sw, kw]
    dtypes = [F32, F32, BF16, F32, F32, F32]
    consts = (g, w, wu, wa, wa2, ba2)
    tm, grid, blk = _row_tiles(rows, INPROJ_TILE)
    x_spec, scratch = blk(d), []
    if batch_major:
        nb = x.shape[0]
        x_spec = pl.BlockSpec((nb, tm // nb, d), lambda i: (0, i, 0))
        widths.append(d)
        dtypes.append(F32)
        scratch = [pltpu.VMEM((d // LANES, tm, LANES), F32)]
    return pl.pallas_call(
        functools.partial(_inproj_even_body, kw=kw, vw=vw),
        grid=grid,
        in_specs=[x_spec] + [_resident(p) for p in consts],
        out_specs=tuple(blk(wd) for wd in widths),
        out_shape=tuple(jax.ShapeDtypeStruct((rows, wd), dt) for wd, dt in zip(widths, dtypes)),
        scratch_shapes=scratch,
        compiler_params=_params(1),
        name="inproj_even",
    )(x, *[_resident_array(p) for p in consts])


def _inproj_odd_body(x_ref, g_ref, w_ref, gate_ref, xin_ref, *, width):
    xn = _rmsnorm(x_ref[...], g_ref[...]).astype(BF16)
    gate_ref[...] = _dot(xn, w_ref[:, 0:width])
    xin_ref[...] = _dot(xn, w_ref[:, width:2 * width])


def _inproj_odd(x2d, g, w):
    rows, d = x2d.shape
    width = _resident_shape(w)[1] // 2
    _, grid, blk = _row_tiles(rows, INPROJ_TILE)
    shape = jax.ShapeDtypeStruct((rows, width), F32)
    return pl.pallas_call(
        functools.partial(_inproj_odd_body, width=width),
        grid=grid,
        in_specs=[blk(d), _resident(g), _resident(w)],
        out_specs=(blk(width), blk(width)),
        out_shape=(shape, shape),
        compiler_params=_params(1),
        name="inproj_odd",
    )(x2d, _resident_array(g), _resident_array(w))


def _out_mlp_cast_body(x_ref, a1_ref, a2_ref, wo_ref, g_ref, gf_ref, wu_hbm, wd_hbm, o_ref, wu_out, wd_out,
                       wu_bf, wd_bf, stage, sem_in, sem_out, *slab, final, layer):
    i = pl.program_id(0)
    d, d_ff = wu_bf.shape
    n_up = d_ff // FF_CHUNK
    n_chunks = 2 * n_up

    def fetch(k):
        if k < n_up:
            src = wu_hbm.at[layer, :, pl.ds(k * FF_CHUNK, FF_CHUNK)]
        else:
            src = wd_hbm.at[layer, pl.ds((k - n_up) * FF_CHUNK, FF_CHUNK), :]
        return pltpu.make_async_copy(src, stage.at[k % 2], sem_in.at[k % 2])

    def publish():
        return (pltpu.make_async_copy(wu_bf, wu_out, sem_out.at[0]), pltpu.make_async_copy(wd_bf, wd_out, sem_out.at[1]))

    @pl.when(i == 0)
    def _():
        fetch(0).start()
        for k in range(n_chunks):
            if k + 1 < n_chunks:
                fetch(k + 1).start()
            fetch(k).wait()
            w = stage[k % 2].astype(BF16)
            if k < n_up:
                wu_bf[:, k * FF_CHUNK:(k + 1) * FF_CHUNK] = w
            else:
                wd_bf[(k - n_up) * FF_CHUNK:(k - n_up + 1) * FF_CHUNK, :] = w
        for copy in publish():
            copy.start()

    _out_mlp_body(x_ref, a1_ref, a2_ref, wo_ref, g_ref, wu_bf, wd_bf, gf_ref, o_ref, *slab, final=final)

    @pl.when(i == pl.num_programs(0) - 1)
    def _():
        for copy in publish():
            copy.wait()


def _out_mlp_stream_body(x_ref, a1_ref, a2_ref, wo_ref, g_ref, wu_ref, wd_ref, gf_ref, o_ref, xn_ref, acc_ref, *, final):
    c = pl.program_id(0)
    half = a1_ref.shape[1]

    @pl.when(c == 0)
    def _():
        mix = _dot(a1_ref[...], wo_ref[0:half, :]) + _dot(a2_ref[...], wo_ref[half:2 * half, :])
        x1 = x_ref[...] + mix
        xn_ref[...] = _rmsnorm(x1, g_ref[...]).astype(BF16)
        acc_ref[...] = x1

    h = jnp.square(jnp.maximum(_dot(xn_ref[...], wu_ref[...]), 0.0)).astype(BF16)
    acc_ref[...] += _dot(h, wd_ref[...])

    @pl.when(c == pl.num_programs(0) - 1)
    def _():
        o_ref[...] = _rmsnorm(acc_ref[...], gf_ref[...]) if final else acc_ref[...]


def _out_mlp_body(x_ref, a1_ref, a2_ref, wo_ref, g_ref, wu_ref, wd_ref, gf_ref, o_ref, *slab, final):
    half = a1_ref.shape[1]
    mix = _dot(a1_ref[...], wo_ref[0:half, :]) + _dot(a2_ref[...], wo_ref[half:2 * half, :])
    x1 = x_ref[...] + mix
    xn = _rmsnorm(x1, g_ref[...]).astype(BF16)
    acc = x1
    for c in range(wu_ref.shape[1] // FF_CHUNK):
        h = _dot(xn, wu_ref[:, c * FF_CHUNK:(c + 1) * FF_CHUNK])
        h = jnp.square(jnp.maximum(h, 0.0)).astype(BF16)
        acc = acc + _dot(h, wd_ref[c * FF_CHUNK:(c + 1) * FF_CHUNK, :])
    if final:
        acc = _rmsnorm(acc, gf_ref[...])
    if slab:
        _deinterleave_rows(acc, slab[0], o_ref)
    else:
        o_ref[...] = acc


def _out_mlp(x2d, a1, a2, wo, g, wu, wd, gf, *, final, batch_major_out=None):
    rows, d = x2d.shape
    half = d // 2
    consts = (wo, g, wu, wd, gf)
    tm, grid, blk = _row_tiles(rows)
    out_spec, out_shape, scratch = blk(d), jax.ShapeDtypeStruct((rows, d), F32), []
    if batch_major_out is not None:
        nb, s = batch_major_out
        out_spec = pl.BlockSpec((nb, tm // nb, d), lambda i: (0, i, 0))
        out_shape = jax.ShapeDtypeStruct((nb, s, d), F32)
        scratch = [pltpu.VMEM((d // LANES, tm, LANES), F32)]
    if _resident_array(wu).dtype == F32:
        (w_up, layer), (w_down, _) = wu, wd
        d_ff = w_up.shape[2]
        assert d == FF_CHUNK and d_ff % FF_CHUNK == 0
        any_spec = pl.BlockSpec(memory_space=pl.ANY)
        row_consts = (wo, g, gf)
        return pl.pallas_call(
            functools.partial(_out_mlp_cast_body, final=final, layer=layer),
            grid=grid,
            in_specs=[blk(d), blk(half, 0), blk(half, a1.shape[1] // half - 1)] + [_resident(p) for p in row_consts] + [any_spec, any_spec],
            out_specs=(out_spec, any_spec, any_spec),
            out_shape=(out_shape, jax.ShapeDtypeStruct((d, d_ff), BF16), jax.ShapeDtypeStruct((d_ff, d), BF16)),
            scratch_shapes=[pltpu.VMEM((d, d_ff), BF16), pltpu.VMEM((d_ff, d), BF16), pltpu.VMEM((2, FF_CHUNK, FF_CHUNK), F32),
                            pltpu.SemaphoreType.DMA((2,)), pltpu.SemaphoreType.DMA((2,))] + scratch,
            compiler_params=_params(1),
            name="out_mlp_cast",
        )(x2d, a1, a2, *[_resident_array(p) for p in row_consts], w_up, w_down)
    if grid == (1,) and batch_major_out is None and not isinstance(wu, tuple) and wu.shape[1] > FF_CHUNK:
        d_ff = wu.shape[1]
        once = lambda shape: pl.BlockSpec(shape, lambda c: (0,) * len(shape))
        return pl.pallas_call(
            functools.partial(_out_mlp_stream_body, final=final),
            grid=(d_ff // FF_CHUNK,),
            in_specs=[once((tm, d)), once((tm, half)), pl.BlockSpec((tm, half), lambda c: (0, a1.shape[1] // half - 1)),
                      _resident(wo), _resident(g), pl.BlockSpec((d, FF_CHUNK), lambda c: (0, c)),
                      pl.BlockSpec((FF_CHUNK, d), lambda c: (c, 0)), _resident(gf)],
            out_specs=once((tm, d)),
            out_shape=out_shape,
            scratch_shapes=[pltpu.VMEM((tm, d), BF16), pltpu.VMEM((tm, d), F32)],
            compiler_params=_params(1),
            name="out_mlp_stream",
        )(x2d, a1, a2, _resident_array(wo), _resident_array(g), wu, wd, _resident_array(gf))
    return pl.pallas_call(
        functools.partial(_out_mlp_body, final=final),
        grid=grid,
        in_specs=[blk(d), blk(half, 0), blk(half, a1.shape[1] // half - 1)] + [_resident(p) for p in consts],
        out_specs=out_spec,
        out_shape=out_shape,
        scratch_shapes=scratch,
        compiler_params=_params(1),
        name="out_mlp",
    )(x2d, a1, a2, *[_resident_array(p) for p in consts])


def _time_grid(total_rows, n_bg, tblock, pack_groups=False):
    steps = total_rows // (n_bg * SUBLANES)
    tb = min(tblock, steps)
    n_t = steps // tb
    groups = 1
    if pack_groups and n_t == 1:
        groups = max(1, min(n_bg, ROW_TILE // (steps * SUBLANES)))
        while n_bg % groups:
            groups -= 1
    blk = lambda width: pl.BlockSpec((groups * tb * SUBLANES, width), lambda bg, t: (bg * n_t + t, 0))
    per_group = lambda rows, width: pl.BlockSpec((groups * rows, width), lambda bg, t: (bg, 0))
    return tb, groups, (n_bg // groups, n_t), blk, per_group


def _gla_body(q_ref, k_ref, la_ref, v_ref, gt_ref, gh_ref, s0_ref, *rest, c, nh, dk, dv, transposed):
    o_ref, sfin_ref, st_ref = rest[-3:]
    tb = pl.program_id(1)
    rows = SUBLANES * c
    n_chunks = q_ref.shape[0] // rows
    dk_shift = int(math.log2(dk))
    heads_per_tile = LANES // dk

    @pl.when(tb == 0)
    def _():
        st_ref[...] = s0_ref[...]

    def iota(shape, axis):
        return lax.broadcasted_iota(jnp.int32, shape, axis)

    causal = ((iota((rows, rows), 0) & 7) == (iota((rows, rows), 1) & 7)) & (iota((rows, rows), 1) <= iota((rows, rows), 0))
    emask = (iota((rows, SUBLANES * dk), 0) & 7) == lax.shift_right_logical(iota((rows, SUBLANES * dk), 1), dk_shift)
    dmask = iota((SUBLANES, SUBLANES * dk), 0) == lax.shift_right_logical(iota((SUBLANES, SUBLANES * dk), 1), dk_shift)
    lo_rows = iota((rows, LANES), 1) < dk
    lo_8 = iota((SUBLANES, LANES), 1) < dk

    def expand(tile, h, lo=lo_rows, mask=emask):
        rolled = pltpu.roll(tile, dk, axis=1)
        d = jnp.where(lo, tile, rolled) if h % heads_per_tile == 0 else jnp.where(lo, rolled, tile)
        return jnp.where(mask, jnp.concatenate([d] * (SUBLANES * dk // LANES), axis=1), 0.0)

    def chunk(n, carry):
        r0 = n * rows
        q = _rows(q_ref, r0, rows)
        k = _rows(k_ref, r0, rows)
        la = _rows(la_ref, r0, rows)
        v = _rows(v_ref, r0, rows)
        gt = _rows(gt_ref, r0, rows)
        acc = jnp.zeros((SUBLANES, nh * dk), F32)
        pieces = []
        for t in range(c):
            acc = acc + la[t * SUBLANES:(t + 1) * SUBLANES, :]
            pieces.append(acc)
        cum = jnp.concatenate(pieces, axis=0)
        last = jnp.concatenate([acc] * c, axis=0)
        q_t = q * jnp.exp(cum) * (dk ** -0.5)
        k_t = k * jnp.exp(-cum)
        k_end = k * jnp.exp(last - cum)
        dec = jnp.exp(acc)
        pad = jnp.zeros((LANES - SUBLANES, LANES), F32)
        outs = []
        for h in range(nh):
            tile = (h // heads_per_tile) * LANES
            lane0 = (h % heads_per_tile) * dk
            sel = lo_rows if lane0 == 0 else jnp.logical_not(lo_rows)
            q_tile = q_t[:, tile:tile + LANES]
            scores = _dot_nt(jnp.where(sel, q_tile, 0.0).astype(BF16), k_t[:, tile:tile + LANES].astype(BF16))
            scores = jnp.where(causal, scores, 0.0).astype(BF16)
            v_h = v[:, h * dv:(h + 1) * dv]
            q_exp = expand(q_tile, h).astype(BF16)
            k_exp = expand(k_end[:, tile:tile + LANES], h).astype(BF16)
            if transposed:
                st = st_ref[h]
                o = _dot(scores, v_h) + _dot_nt(q_exp, st.astype(BF16))
                dec_row = jnp.sum(expand(dec[:, tile:tile + LANES], h, lo_8, dmask), axis=0, keepdims=True)
                st_ref[h] = st * dec_row + _dot_tn(v_h, k_exp)
            else:
                st = jnp.concatenate([st_ref[b, h] for b in range(SUBLANES)], axis=0)
                o = _dot(scores, v_h) + _dot(q_exp, st.astype(BF16))
                upd = _dot_tn(k_exp, v_h)
                dec_t = jnp.concatenate([dec[:, tile:tile + LANES], pad], axis=0).T
                for b in range(SUBLANES):
                    dec_b = jnp.broadcast_to(dec_t[lane0:lane0 + dk, b:b + 1], (dk, dv))
                    st_ref[b, h] = st[b * dk:(b + 1) * dk, :] * dec_b + upd[b * dk:(b + 1) * dk, :]
            o = _rmsnorm(o, gh_ref[:, h * dv:(h + 1) * dv])
            g_h = gt[:, h * dv:(h + 1) * dv]
            outs.append(o * (g_h * _sigmoid(g_h)))
        o_ref[pl.ds(pl.multiple_of(r0, rows), rows), :] = jnp.concatenate(outs, axis=1).astype(BF16)
        return carry

    lax.fori_loop(0, n_chunks, chunk, 0, unroll=min(n_chunks, 8))

    @pl.when(tb == pl.num_programs(1) - 1)
    def _():
        sfin_ref[...] = st_ref[...]


def _gla_state_in(s):
    b, h, dk, dv = s.shape
    return jnp.transpose(s.reshape(b // SUBLANES, SUBLANES, h, dk, dv), (0, 2, 4, 1, 3)).reshape(b // SUBLANES * h, dv, SUBLANES * dk)


def _gla_state_out(st, nh, dk):
    dv = st.shape[1]
    n_bg = st.shape[0] // nh
    return jnp.transpose(st.reshape(n_bg, nh, dv, SUBLANES, dk), (0, 3, 1, 4, 2)).reshape(n_bg * SUBLANES, nh, dk, dv)


def _gla(q, k, la, v, gt, gh, states, layer, n_layers, acc, nb, dims, *, c, tblock):
    nh, dk, dv = dims
    n_bg = nb // SUBLANES
    tb, _, grid, blk, _ = _time_grid(q.shape[0], n_bg, tblock)
    transposed = q.shape[0] // n_bg > SUBLANES * c
    body = functools.partial(_gla_body, c=c, nh=nh, dk=dk, dv=dv, transposed=transposed)
    row_specs = [blk(nh * dk), blk(nh * dk), blk(nh * dk), blk(nh * dv), blk(nh * dv), _const_spec(gh.shape)]
    og_shape = jax.ShapeDtypeStruct((q.shape[0], nh * dv), BF16)
    if transposed:
        st_block = (nh, dv, SUBLANES * dk)
        st_spec = pl.BlockSpec(st_block, lambda bg, t: (bg, 0, 0))
        s_in = jnp.zeros((n_bg * nh, dv, SUBLANES * dk), F32) if states is None else _gla_state_in(states[layer].astype(F32))
        og, s_fin = pl.pallas_call(
            body, grid=grid, in_specs=row_specs + [st_spec], out_specs=(blk(nh * dv), st_spec),
            out_shape=(og_shape, jax.ShapeDtypeStruct(s_in.shape, F32)), scratch_shapes=[pltpu.VMEM(st_block, F32)],
            compiler_params=_params(2), name="gla",
        )(q, k, la, v, gt, gh, s_in)
        s_new = _gla_state_out(s_fin, nh, dk)[None]
        return og, (jnp.pad(s_new, ((0, n_layers - 1),) + ((0, 0),) * 4) if acc is None else lax.dynamic_update_slice(acc, s_new, (layer, 0, 0, 0, 0)))
    st_block = (SUBLANES, nh, dk, dv)
    stacked_spec = pl.BlockSpec((None,) + st_block, lambda bg, t: (layer, bg, 0, 0, 0))
    if states is None:
        s_in, in_spec = jnp.zeros((nb, nh, dk, dv), F32), pl.BlockSpec(st_block, lambda bg, t: (bg, 0, 0, 0))
    else:
        s_in, in_spec = states.astype(F32), stacked_spec
    if acc is None:
        acc = jnp.zeros((n_layers, nb, nh, dk, dv), F32)
    return pl.pallas_call(
        body, grid=grid, in_specs=row_specs + [in_spec, pl.BlockSpec(memory_space=pl.ANY)], out_specs=(blk(nh * dv), stacked_spec),
        out_shape=(og_shape, jax.ShapeDtypeStruct(acc.shape, F32)), scratch_shapes=[pltpu.VMEM(st_block, F32)],
        input_output_aliases={7: 1}, compiler_params=_params(2), name="gla",
    )(q, k, la, v, gt, gh, s_in, acc)


def _s5_disc_body(a_re_ref, a_im_ref, log_dt_ref, b_re_ref, b_im_ref, ab_re_ref, ab_im_ref, bb_re_ref, bb_im_ref):
    a_re = a_re_ref[...]
    a_im = a_im_ref[...]
    dt = jnp.exp(log_dt_ref[...])
    mag = jnp.exp(dt * a_re)
    ab_re = mag * jnp.cos(dt * a_im)
    ab_im = mag * jnp.sin(dt * a_im)
    den = a_re * a_re + a_im * a_im
    coef_re = ((ab_re - 1.0) * a_re + ab_im * a_im) / den
    coef_im = (ab_im * a_re - (ab_re - 1.0) * a_im) / den
    b_re = b_re_ref[...]
    b_im = b_im_ref[...]
    ab_re_ref[...] = ab_re
    ab_im_ref[...] = ab_im
    bb_re_ref[...] = coef_re * b_re - coef_im * b_im
    bb_im_ref[...] = coef_re * b_im + coef_im * b_re


def _s5_discretise(a_re, a_im, log_dt, b_re, b_im):
    g, p, ch = b_re.shape
    rep = lambda t: jnp.repeat(t, ch, axis=0)
    flat = lambda t: jnp.transpose(t, (0, 2, 1)).reshape(g * ch, p)
    args = (rep(a_re), rep(a_im), rep(jnp.broadcast_to(log_dt[:, None], (g, p))), flat(b_re), flat(b_im))
    spec = _const_spec((g * ch, p))
    shape = jax.ShapeDtypeStruct((g * ch, p), F32)
    ab_re, ab_im, bb_re, bb_im = pl.pallas_call(
        _s5_disc_body, grid=(1,), in_specs=[spec] * 5, out_specs=(spec,) * 4, out_shape=(shape,) * 4,
        compiler_params=_params(1), name="s5_discretise",
    )(*args)
    return ab_re[::ch], ab_im[::ch], bb_re.reshape(g, ch, p), bb_im.reshape(g, ch, p)


def _block_diag_in(w_gcp):
    g, ch, p = w_gcp.shape
    gpt = LANES // ch
    w = w_gcp.reshape(g // gpt, gpt, ch, p)
    eye = jnp.eye(gpt, dtype=w.dtype)
    return jnp.einsum("ab,jacp->jacbp", eye, w).reshape(g // gpt, gpt * ch, gpt * p)


def _block_diag_out(w_gcp):
    g, ch, p = w_gcp.shape
    gpt = LANES // ch
    w = w_gcp.reshape(g // gpt, gpt, ch, p)
    eye = jnp.eye(gpt, dtype=w.dtype)
    return jnp.einsum("ab,jacp->japbc", eye, w).reshape(g // gpt, gpt * p, gpt * ch)


def _s5_body(u_ref, ab_re_ref, ab_im_ref, wb_re_ref, wb_im_ref, wc_ref, d_ref, wglu_ref, bglu_ref, h0_re_ref, h0_im_ref,
             z_ref, hf_re_ref, hf_im_ref, h_re_ref, h_im_ref, bu_re_ref, bu_im_ref, y_ref):
    step = pl.program_id(1)
    rows = u_ref.shape[0]
    groups = h_re_ref.shape[0] // SUBLANES
    tb = rows // SUBLANES // groups
    n_tiles = wb_re_ref.shape[0]
    sw = wb_re_ref.shape[2]

    @pl.when(step == 0)
    def _():
        h_re_ref[...] = h0_re_ref[...]
        h_im_ref[...] = h0_im_ref[...]

    for j in range(n_tiles):
        u_j = u_ref[:, j * LANES:(j + 1) * LANES]
        u_b = u_j.astype(BF16)
        bu_re_ref[j] = _dot(u_b, wb_re_ref[j])
        bu_im_ref[j] = _dot(u_b, wb_im_ref[j])
        a_re = jnp.broadcast_to(ab_re_ref[:, j * sw:(j + 1) * sw], (SUBLANES, sw))
        a_im = jnp.broadcast_to(ab_im_ref[:, j * sw:(j + 1) * sw], (SUBLANES, sw))
        for grp in range(groups):
            state_rows = slice(grp * SUBLANES, (grp + 1) * SUBLANES)
            h_re = h_re_ref[state_rows, j * sw:(j + 1) * sw]
            h_im = h_im_ref[state_rows, j * sw:(j + 1) * sw]
            for t in range(grp * tb, (grp + 1) * tb):
                sl = slice(t * SUBLANES, (t + 1) * SUBLANES)
                h_re, h_im = (a_re * h_re - a_im * h_im + bu_re_ref[j, sl, :], a_re * h_im + a_im * h_re + bu_im_ref[j, sl, :])
                bu_re_ref[j, sl, :] = h_re
                bu_im_ref[j, sl, :] = h_im
            h_re_ref[state_rows, j * sw:(j + 1) * sw] = h_re
            h_im_ref[state_rows, j * sw:(j + 1) * sw] = h_im
        h_cat = jnp.concatenate([bu_re_ref[j].astype(BF16), bu_im_ref[j].astype(BF16)], axis=1)
        y_ref[:, j * LANES:(j + 1) * LANES] = _dot(h_cat, wc_ref[j]) + d_ref[:, j * LANES:(j + 1) * LANES] * u_j

    z = _gelu(y_ref[...])
    gate = _sigmoid(_dot(z.astype(BF16), wglu_ref[...]) + bglu_ref[...])
    z_ref[...] = (z * gate).astype(BF16)

    @pl.when(step == pl.num_programs(1) - 1)
    def _():
        hf_re_ref[...] = h_re_ref[...]
        hf_im_ref[...] = h_im_ref[...]


def _s5(u, ab_re, ab_im, wb_re, wb_im, wc, d_skip, wglu, bglu, h0_re, h0_im, *, tblock):
    total, width = u.shape
    n_state = ab_re.shape[1]
    n_bg = h0_re.shape[0] // SUBLANES
    sw = wb_re.shape[2]
    tb, groups, grid, blk, per_group = _time_grid(total, n_bg, tblock, pack_groups=True)
    rows = groups * tb * SUBLANES
    st_spec = per_group(SUBLANES, n_state)
    consts = (ab_re, ab_im, wb_re, wb_im, wc, d_skip, wglu, bglu)
    st_shape = jax.ShapeDtypeStruct((n_bg * SUBLANES, n_state), F32)
    return pl.pallas_call(
        _s5_body,
        grid=grid,
        in_specs=[blk(width)] + [_resident(p) for p in consts] + [st_spec, st_spec],
        out_specs=(blk(width), st_spec, st_spec),
        out_shape=(jax.ShapeDtypeStruct((total, width), BF16), st_shape, st_shape),
        scratch_shapes=[pltpu.VMEM((groups * SUBLANES, n_state), F32), pltpu.VMEM((groups * SUBLANES, n_state), F32),
                        pltpu.VMEM((wb_re.shape[0], rows, sw), F32), pltpu.VMEM((wb_re.shape[0], rows, sw), F32),
                        pltpu.VMEM((rows, width), F32)],
        compiler_params=_params(2),
        name="s5",
    )(u, *[_resident_array(p) for p in consts], h0_re, h0_im)


def _lru_block(n, xbuf_ref, gate_ref, params, a_ref, b_ref, h, rows):
    cw_ref, cb_ref, wr_ref, br_ref, wi_ref, bi_ref, lam_ref = params
    bw = wr_ref.shape[1]
    cols = slice(n * bw, (n + 1) * bw)
    a_blk, b_blk = a_ref.at[n % 2], b_ref.at[n % 2]
    taps = cw_ref.shape[0]
    groups = h.shape[0] // SUBLANES
    seg = rows // groups
    pitch = seg + (taps - 1) * SUBLANES
    pieces = []
    for grp in range(groups):
        xc = cb_ref[:, cols] + xbuf_ref[grp * pitch:grp * pitch + seg, cols] * cw_ref[0:1, cols]
        for j in range(1, taps):
            xc = xc + xbuf_ref[grp * pitch + j * SUBLANES:grp * pitch + j * SUBLANES + seg, cols] * cw_ref[j:j + 1, cols]
        pieces.append(xc)
    xc = jnp.concatenate(pieces, axis=0) if groups > 1 else pieces[0]
    xb = xc.astype(BF16)
    t_i = jnp.tanh(_dot(xb, wi_ref[n]) + bi_ref[:, cols])
    t_r = jnp.tanh(_dot(xb, wr_ref[n]) + br_ref[:, cols])
    half_rate = (-0.5 * LRU_C) * _softplus(-lam_ref[:, cols])
    log_a = t_r * half_rate + half_rate
    a_blk[...] = jnp.exp(log_a)
    th = jnp.tanh(log_a)
    w = th / (th - 1.0)
    root = jnp.where(w > 0.0, w * lax.rsqrt(w), 0.0)
    scaled = xc * (0.5 * math.sqrt(2.0))
    b_blk[...] = root * (t_i * scaled + scaled)
    finals = []
    for grp in range(groups):
        h_g = h[grp * SUBLANES:(grp + 1) * SUBLANES, :]
        for t in range(grp * seg // SUBLANES, (grp + 1) * seg // SUBLANES):
            sl = slice(t * SUBLANES, (t + 1) * SUBLANES)
            h_g = a_blk[sl, :] * h_g + b_blk[sl, :]
            b_blk[sl, :] = h_g
        finals.append(h_g)
    h = jnp.concatenate(finals, axis=0) if groups > 1 else finals[0]
    return (_gelu(gate_ref[:, cols]) * b_blk[...]).astype(BF16), h


def _lru_body(gate_ref, xin_ref, conv0_ref, h0_ref, cw_ref, cb_ref, wr_ref, br_ref, wi_ref, bi_ref, lam_ref,
              o_ref, hf_ref, convf_ref, xbuf_ref, a_ref, b_ref, h_ref):
    step = pl.program_id(1)
    rows, width = xin_ref.shape
    tail = (cw_ref.shape[0] - 1) * SUBLANES
    n_blocks = wr_ref.shape[0]
    bw = wr_ref.shape[1]
    groups = h_ref.shape[0] // SUBLANES
    seg = rows // groups
    pitch = seg + tail

    @pl.when(step == 0)
    def _():
        for grp in range(groups):
            xbuf_ref[grp * pitch:grp * pitch + tail, :] = conv0_ref[grp * tail:(grp + 1) * tail, :]
        h_ref[...] = h0_ref[...]

    for grp in range(groups):
        xbuf_ref[grp * pitch + tail:(grp + 1) * pitch, :] = xin_ref[grp * seg:(grp + 1) * seg, :]
    params = (cw_ref, cb_ref, wr_ref, br_ref, wi_ref, bi_ref, lam_ref)
    for n in range(n_blocks):
        cols = slice(n * bw, (n + 1) * bw)
        o_ref[:, cols], h_ref[:, cols] = _lru_block(n, xbuf_ref, gate_ref, params, a_ref, b_ref, h_ref[:, cols], rows)
    for grp in range(groups):
        xbuf_ref[grp * pitch:grp * pitch + tail, :] = xbuf_ref[grp * pitch + seg:(grp + 1) * pitch, :]

    @pl.when(step == pl.num_programs(1) - 1)
    def _():
        hf_ref[...] = h_ref[...]
        for grp in range(groups):
            convf_ref[grp * tail:(grp + 1) * tail, :] = xbuf_ref[grp * pitch:grp * pitch + tail, :]


def _lru(gate, xin, conv0, h0, cw, cb, wr, br, wi, bi, lam, *, tblock):
    total, width = xin.shape
    n_bg = h0.shape[0] // SUBLANES
    tail = (cw.shape[0] - 1) * SUBLANES
    bw = _resident_shape(wr)[1]
    tb, groups, grid, blk, per_group = _time_grid(total, n_bg, tblock, pack_groups=True)
    rows = groups * tb * SUBLANES
    h_spec = per_group(SUBLANES, width)
    c_spec = per_group(tail, width)
    consts = (cw, cb, wr, br, wi, bi, lam)
    return pl.pallas_call(
        _lru_body,
        grid=grid,
        in_specs=[blk(width), blk(width), c_spec, h_spec] + [_resident(p) for p in consts],
        out_specs=(blk(width), h_spec, c_spec),
        out_shape=(jax.ShapeDtypeStruct((total, width), BF16), jax.ShapeDtypeStruct((n_bg * SUBLANES, width), F32),
                   jax.ShapeDtypeStruct((n_bg * tail, width), F32)),
        scratch_shapes=[pltpu.VMEM((rows + groups * tail, width), F32), pltpu.VMEM((2, rows, bw), F32),
                        pltpu.VMEM((2, rows, bw), F32), pltpu.VMEM((groups * SUBLANES, width), F32)],
        compiler_params=_params(2),
        name="rg_lru",
    )(gate, xin, conv0, h0, *[_resident_array(p) for p in consts])


def _row(t):
    return t.reshape(1, -1).astype(F32)


def _to_rows(x):
    b, s, d = x.shape
    return jnp.transpose(x.reshape(b // SUBLANES, SUBLANES, s, d), (0, 2, 1, 3)).reshape(b * s, d)


def _from_rows(y, b, s):
    d = y.shape[-1]
    return jnp.transpose(y.reshape(b // SUBLANES, s, SUBLANES, d), (0, 2, 1, 3)).reshape(b, s, d)


def _even_layer(x, nb, gla_states, gla_acc, e, n_even, s_re, s_im, prm, mlp, *, final, batch_major_out):
    (g_norm, w_in, w_u, w_a, wa2, ba2, g_head, ab_re, ab_im, wb_re, wb_im, wc, d_skip, wglu, bglu, w_out, dims) = prm
    nh, dk, dv = dims
    q, k, v, gt, u, la, *reordered = _inproj_even(x, g_norm, w_in, w_u, w_a, wa2, ba2, kw=nh * dk, vw=nh * dv)
    x2d = reordered[0] if reordered else x
    steps = x2d.shape[0] // nb
    og, gla_acc = _gla(q, k, la, v, gt, g_head, gla_states, e, n_even, gla_acc, nb, dims, c=math.gcd(steps, GLA_CHUNK), tblock=128)
    n_state = ab_re.shape[1]
    h0_re = jnp.zeros((nb, n_state), F32) if s_re is None else s_re.astype(F32).reshape(nb, n_state)
    h0_im = jnp.zeros((nb, n_state), F32) if s_im is None else s_im.astype(F32).reshape(nb, n_state)
    zz, hf_re, hf_im = _s5(u, ab_re, ab_im, wb_re, wb_im, wc, d_skip, wglu, bglu, h0_re, h0_im, tblock=128)
    g_mlp, w_up, w_down, g_fin = mlp
    y = _out_mlp(x2d, og, zz, w_out, g_mlp, w_up, w_down, g_fin, final=final, batch_major_out=batch_major_out)
    return y, gla_acc, hf_re, hf_im


def _odd_layer(x2d, nb, s_lru, s_conv, prm, mlp, *, final, batch_major_out):
    (g_norm, w_in, cw, cb, wr, br, wi, bi, lam, w_out) = prm
    n_bg = nb // SUBLANES
    taps = cw.shape[0]
    width = _resident_shape(w_in)[1] // 2
    gate, xin = _inproj_odd(x2d, g_norm, w_in)
    h0 = jnp.zeros((nb, width), F32) if s_lru is None else s_lru.astype(F32)
    if s_conv is None:
        conv0 = jnp.zeros((n_bg * (taps - 1) * SUBLANES, width), F32)
    else:
        conv0 = jnp.transpose(s_conv.astype(F32).reshape(n_bg, SUBLANES, taps - 1, width), (0, 2, 1, 3)).reshape(-1, width)
    gh, hf, convf = _lru(gate, xin, conv0, h0, cw, cb, wr, br, wi, bi, lam, tblock=128)
    g_mlp, w_up, w_down, g_fin = mlp
    y = _out_mlp(x2d, gh, gh, w_out, g_mlp, w_up, w_down, g_fin, final=final, batch_major_out=batch_major_out)
    conv_new = jnp.transpose(convf.reshape(n_bg, taps - 1, SUBLANES, width), (0, 2, 1, 3)).reshape(nb, taps - 1, width)
    return y, hf, conv_new


def kernel(x_prompt, x_sample, state_gla, state_s5_re, state_s5_im, state_lru, state_conv, norm_mix_even, w_in_even, gla_w_a2, gla_b_a2, gla_norm, s5_a_re, s5_a_im, s5_log_dt, s5_b_re, s5_b_im, s5_c_re, s5_c_im, s5_d, s5_w_glu, s5_b_glu, w_out_even, norm_mix_odd, w_in_odd, conv_w, conv_b, lru_w_r, lru_b_r, lru_w_i, lru_b_i, lru_lam, w_out_odd, norm_mlp, w_up, w_down, norm_final):
    depth = norm_mlp.shape[0]
    nh, dk, dv = state_gla.shape[2:]
    rank = gla_w_a2.shape[1]
    n_groups, n_p, ch = s5_b_re.shape[1:]
    kw, vw = nh * dk, nh * dv
    assert rank <= LANES and LANES % ch == 0 and LANES % dk == 0 and dv == LANES
    bp, sp, _ = x_prompt.shape
    bs, ss, _ = x_sample.shape
    n_even = state_gla.shape[0]

    in_kernel_order = bp == SUBLANES and sp % (INPROJ_TILE // SUBLANES) == 0 and depth > 1
    yp = x_prompt.astype(F32) if in_kernel_order else _to_rows(x_prompt.astype(F32))
    ys = _to_rows(x_sample.astype(F32))
    g_fin = _row(norm_final)
    o4 = 2 * kw + 2 * vw
    o5 = o4 + rank
    w_in_main = w_in_even[:, :, :o4].astype(BF16)
    w_in_u = w_in_even[:, :, o5:].astype(BF16)
    w_in_a = jnp.pad(w_in_even[:, :, o4:o5], ((0, 0), (0, 0), (0, LANES - rank))).astype(BF16)
    w_a2 = jnp.pad(gla_w_a2, ((0, 0), (0, LANES - rank), (0, 0))).astype(BF16)
    w_up_f, w_down_f = w_up.astype(F32), w_down.astype(F32)
    w_out_even_b, w_out_odd_b, w_in_odd_b = w_out_even.astype(BF16), w_out_odd.astype(BF16), w_in_odd.astype(BF16)
    w_glu_b, w_r_b, w_i_b = s5_w_glu.astype(BF16), (0.5 * lru_w_r).astype(BF16), (0.5 * lru_w_i).astype(BF16)
    keys = ("re", "im", "lru", "conv")
    outs_p = {key: [] for key in keys}
    outs_s = {key: [] for key in keys}
    gla_p = gla_s = None

    def record(outs, sr, si):
        outs["re"].append(sr.reshape(-1, n_groups, n_p))
        outs["im"].append(si.reshape(-1, n_groups, n_p))

    for layer in range(depth):
        final = layer == depth - 1
        bm_out = (bp, sp) if final and in_kernel_order else None
        mlp = (_row(norm_mlp[layer]), (w_up_f, layer), (w_down_f, layer), g_fin)
        if layer % 2 == 0:
            e = layer // 2
            ab_re, ab_im, bb_re, bb_im = _s5_discretise(s5_a_re[e].astype(F32), s5_a_im[e].astype(F32), s5_log_dt[e].astype(F32),
                                                        s5_b_re[e].astype(F32), s5_b_im[e].astype(F32))
            wc = jnp.concatenate([_block_diag_out(s5_c_re[e].astype(F32)), -_block_diag_out(s5_c_im[e].astype(F32))], axis=1)
            prm = (_row(norm_mix_even[e]), (w_in_main, e), (w_in_u, e), (w_in_a, e), (w_a2, e), _row(gla_b_a2[e]), _row(gla_norm[e]),
                   ab_re.reshape(1, -1), ab_im.reshape(1, -1), _block_diag_in(bb_re).astype(BF16), _block_diag_in(bb_im).astype(BF16),
                   wc.astype(BF16), _row(s5_d[e]), (w_glu_b, e), _row(s5_b_glu[e]), (w_out_even_b, e), (nh, dk, dv))
            (yp, w_up_l, w_down_l), gla_p, sr, si = _even_layer(yp, bp, None, gla_p, e, n_even, None, None, prm, mlp, final=final,
                                                                batch_major_out=bm_out)
            record(outs_p, sr, si)
            mlp = (mlp[0], w_up_l, w_down_l, g_fin)
            ys, gla_s, sr, si = _even_layer(ys, bs, state_gla, gla_s, e, n_even, state_s5_re[e], state_s5_im[e], prm, mlp, final=final,
                                            batch_major_out=None)
            record(outs_s, sr, si)
        else:
            o = layer // 2
            prm = (_row(norm_mix_odd[o]), (w_in_odd_b, o), conv_w[o].astype(F32), _row(conv_b[o]),
                   (w_r_b, o), _row(0.5 * lru_b_r[o]), (w_i_b, o), _row(0.5 * lru_b_i[o]), _row(lru_lam[o]), (w_out_odd_b, o))
            (yp, w_up_l, w_down_l), sl, sc = _odd_layer(yp, bp, None, None, prm, mlp, final=final, batch_major_out=bm_out)
            outs_p["lru"].append(sl)
            outs_p["conv"].append(sc)
            mlp = (mlp[0], w_up_l, w_down_l, g_fin)
            ys, sl, sc = _odd_layer(ys, bs, state_lru[o], state_conv[o], prm, mlp, final=final, batch_major_out=None)
            outs_s["lru"].append(sl)
            outs_s["conv"].append(sc)

    stack = lambda outs: tuple(jnp.stack(outs[key]) for key in keys)
    return (yp if in_kernel_order else _from_rows(yp, bp, sp), _from_rows(ys, bs, ss), gla_p) + stack(outs_p) + (gla_s,) + stack(outs_s)
```

```python
import functools
import math

import jax
import jax.numpy as jnp
from jax import lax
from jax.experimental import pallas as pl
from jax.experimental.pallas import tpu as pltpu

F32 = jnp.float32
BF16 = jnp.bfloat16

NORM_EPS = 1e-6
GLA_TAU = 16.0
GLA_CHUNK = 16
LRU_C = 8.0

SUBLANES = 8
LANES = 128
ROW_TILE = 512
INPROJ_TILE = 1024
FF_CHUNK = 1024
VMEM_LIMIT = 48 * 1024 * 1024


def _params(n_axes):
    return pltpu.CompilerParams(dimension_semantics=("arbitrary",) * n_axes, vmem_limit_bytes=VMEM_LIMIT)


def _const_spec(shape):
    zeros = (0,) * len(shape)
    return pl.BlockSpec(shape, lambda *_: zeros, pipeline_mode=pl.Buffered(1))


def _resident(p):
    if not isinstance(p, tuple):
        return _const_spec(p.shape)
    stacked, layer = p
    index = (layer,) + (0,) * (stacked.ndim - 1)
    return pl.BlockSpec((None,) + stacked.shape[1:], lambda *_: index, pipeline_mode=pl.Buffered(1))


def _resident_array(p):
    return p[0] if isinstance(p, tuple) else p


def _resident_shape(p):
    return p[0].shape[1:] if isinstance(p, tuple) else p.shape


def _dot(a, b):
    return jnp.dot(a, b, preferred_element_type=F32)


def _dot_nt(a, b):
    return lax.dot_general(a, b, (((1,), (1,)), ((), ())), preferred_element_type=F32)


def _dot_tn(a, b):
    return lax.dot_general(a, b, (((0,), (0,)), ((), ())), preferred_element_type=F32)


def _rmsnorm(x, g):
    return x * lax.rsqrt(jnp.mean(x * x, axis=-1, keepdims=True) + NORM_EPS) * g


def _log_sigmoid(z):
    return jnp.minimum(z, 0.0) - jnp.log1p(jnp.exp(-jnp.abs(z)))


def _softplus(z):
    return jnp.maximum(z, 0.0) + jnp.log1p(jnp.exp(-jnp.abs(z)))


_GELU_C1 = math.sqrt(2.0 / math.pi)
_GELU_C2 = _GELU_C1 * 0.044715


def _gelu(x):
    half = 0.5 * x
    return half + half * jnp.tanh(x * (_GELU_C1 + _GELU_C2 * (x * x)))


def _sigmoid(x):
    return 0.5 * jnp.tanh(0.5 * x) + 0.5


def _rows(ref, start, size):
    return ref[pl.ds(pl.multiple_of(start, size), size), :]


def _interleave_rows(x_ref, slab_ref):
    nb, ts, d = x_ref.shape
    for b in range(nb):
        for l in range(d // LANES):
            slab_ref[l, pl.ds(b, ts, stride=nb), :] = x_ref[b, :, l * LANES:(l + 1) * LANES]
    return jnp.concatenate([slab_ref[l] for l in range(d // LANES)], axis=1)


def _deinterleave_rows(y, slab_ref, o_ref):
    nb, ts, d = o_ref.shape
    for l in range(d // LANES):
        slab_ref[l] = y[:, l * LANES:(l + 1) * LANES]
    for b in range(nb):
        for l in range(d // LANES):
            o_ref[b, :, l * LANES:(l + 1) * LANES] = slab_ref[l, pl.ds(b, ts, stride=nb), :]


def _row_tiles(rows, tile=ROW_TILE):
    tm = tile if rows % tile == 0 else min(ROW_TILE, rows)
    return tm, (rows // tm,), lambda width, col=0: pl.BlockSpec((tm, width), lambda i: (i, col))


def _inproj_even_body(x_ref, g_ref, w_ref, wu_ref, wa_ref, wa2_ref, ba2_ref, q_ref, k_ref, v_ref, gt_ref, u_ref, la_ref, *rest,
                      kw, vw):
    if rest:
        xt_ref, slab_ref = rest
        x = _interleave_rows(x_ref, slab_ref)
        xt_ref[...] = x
    else:
        x = x_ref[...]
    xn = _rmsnorm(x, g_ref[...]).astype(BF16)
    o1, o2 = kw, 2 * kw
    o3 = o2 + vw
    o4 = o3 + vw
    a_lr = _dot(xn, wa_ref[...]).astype(BF16)
    z = _dot(a_lr, wa2_ref[...]) + ba2_ref[...]
    la_ref[...] = _log_sigmoid(z) * (1.0 / GLA_TAU)
    q_ref[...] = _dot(xn, w_ref[:, 0:o1])
    k_ref[...] = _dot(xn, w_ref[:, o1:o2])
    v_ref[...] = _dot(xn, w_ref[:, o2:o3]).astype(BF16)
    gt_ref[...] = _dot(xn, w_ref[:, o3:o4])
    u_ref[...] = _dot(xn, wu_ref[...])


def _inproj_even(x, g, w, wu, wa, wa2, ba2, *, kw, vw):
    batch_major = x.ndim == 3
    d = x.shape[-1]
    rows = x.size // d
    sw = _resident_shape(wu)[1]
    widths = [kw, kw, vw, vw, sw, kw]
    dtypes = [F32, F32, BF16, F32, F32, F32]
    consts = (g, w, wu, wa, wa2, ba2)
    tm, grid, blk = _row_tiles(rows, INPROJ_TILE)
    x_spec, scratch = blk(d), []
    if batch_major:
        nb = x.shape[0]
        x_spec = pl.BlockSpec((nb, tm // nb, d), lambda i: (0, i, 0))
        widths.append(d)
        dtypes.append(F32)
        scratch = [pltpu.VMEM((d // LANES, tm, LANES), F32)]
    return pl.pallas_call(
        functools.partial(_inproj_even_body, kw=kw, vw=vw),
        grid=grid,
        in_specs=[x_spec] + [_resident(p) for p in consts],
        out_specs=tuple(blk(wd) for wd in widths),
        out_shape=tuple(jax.ShapeDtypeStruct((rows, wd), dt) for wd, dt in zip(widths, dtypes)),
        scratch_shapes=scratch,
        compiler_params=_params(1),
        name="inproj_even",
    )(x, *[_resident_array(p) for p in consts])


def _inproj_odd_body(x_ref, g_ref, w_ref, gate_ref, xin_ref, *, width):
    xn = _rmsnorm(x_ref[...], g_ref[...]).astype(BF16)
    gate_ref[...] = _dot(xn, w_ref[:, 0:width])
    xin_ref[...] = _dot(xn, w_ref[:, width:2 * width])


def _inproj_odd(x2d, g, w):
    rows, d = x2d.shape
    width = _resident_shape(w)[1] // 2
    _, grid, blk = _row_tiles(rows, INPROJ_TILE)
    shape = jax.ShapeDtypeStruct((rows, width), F32)
    return pl.pallas_call(
        functools.partial(_inproj_odd_body, width=width),
        grid=grid,
        in_specs=[blk(d), _resident(g), _resident(w)],
        out_specs=(blk(width), blk(width)),
        out_shape=(shape, shape),
        compiler_params=_params(1),
        name="inproj_odd",
    )(x2d, _resident_array(g), _resident_array(w))


def _out_mlp_cast_body(x_ref, a1_ref, a2_ref, wo_ref, g_ref, gf_ref, wu_hbm, wd_hbm, o_ref, wu_out, wd_out,
                       wu_bf, wd_bf, stage, sem_in, sem_out, *slab, final, layer):
    i = pl.program_id(0)
    d, d_ff = wu_bf.shape
    n_up = d_ff // FF_CHUNK
    n_chunks = 2 * n_up

    def fetch(k):
        if k < n_up:
            src = wu_hbm.at[layer, :, pl.ds(k * FF_CHUNK, FF_CHUNK)]
        else:
            src = wd_hbm.at[layer, pl.ds((k - n_up) * FF_CHUNK, FF_CHUNK), :]
        return pltpu.make_async_copy(src, stage.at[k % 2], sem_in.at[k % 2])

    def publish():
        return (pltpu.make_async_copy(wu_bf, wu_out, sem_out.at[0]), pltpu.make_async_copy(wd_bf, wd_out, sem_out.at[1]))

    @pl.when(i == 0)
    def _():
        fetch(0).start()
        for k in range(n_chunks):
            if k + 1 < n_chunks:
                fetch(k + 1).start()
            fetch(k).wait()
            w = stage[k % 2].astype(BF16)
            if k < n_up:
                wu_bf[:, k * FF_CHUNK:(k + 1) * FF_CHUNK] = w
            else:
                wd_bf[(k - n_up) * FF_CHUNK:(k - n_up + 1) * FF_CHUNK, :] = w
        for copy in publish():
            copy.start()

    _out_mlp_body(x_ref, a1_ref, a2_ref, wo_ref, g_ref, wu_bf, wd_bf, gf_ref, o_ref, *slab, final=final)

    @pl.when(i == pl.num_programs(0) - 1)
    def _():
        for copy in publish():
            copy.wait()


def _out_mlp_stream_body(x_ref, a1_ref, a2_ref, wo_ref, g_ref, wu_ref, wd_ref, gf_ref, o_ref, xn_ref, acc_ref, *, final):
    c = pl.program_id(0)
    half = a1_ref.shape[1]

    @pl.when(c == 0)
    def _():
        mix = _dot(a1_ref[...], wo_ref[0:half, :]) + _dot(a2_ref[...], wo_ref[half:2 * half, :])
        x1 = x_ref[...] + mix
        xn_ref[...] = _rmsnorm(x1, g_ref[...]).astype(BF16)
        acc_ref[...] = x1

    h = jnp.square(jnp.maximum(_dot(xn_ref[...], wu_ref[...]), 0.0)).astype(BF16)
    acc_ref[...] += _dot(h, wd_ref[...])

    @pl.when(c == pl.num_programs(0) - 1)
    def _():
        o_ref[...] = _rmsnorm(acc_ref[...], gf_ref[...]) if final else acc_ref[...]


def _out_mlp_body(x_ref, a1_ref, a2_ref, wo_ref, g_ref, wu_ref, wd_ref, gf_ref, o_ref, *slab, final):
    half = a1_ref.shape[1]
    mix = _dot(a1_ref[...], wo_ref[0:half, :]) + _dot(a2_ref[...], wo_ref[half:2 * half, :])
    x1 = x_ref[...] + mix
    xn = _rmsnorm(x1, g_ref[...]).astype(BF16)
    acc = x1
    for c in range(wu_ref.shape[1] // FF_CHUNK):
        h = _dot(xn, wu_ref[:, c * FF_CHUNK:(c + 1) * FF_CHUNK])
        h = jnp.square(jnp.maximum(h, 0.0)).astype(BF16)
        acc = acc + _dot(h, wd_ref[c * FF_CHUNK:(c + 1) * FF_CHUNK, :])
    if final:
        acc = _rmsnorm(acc, gf_ref[...])
    if slab:
        _deinterleave_rows(acc, slab[0], o_ref)
    else:
        o_ref[...] = acc


def _out_mlp(x2d, a1, a2, wo, g, wu, wd, gf, *, final, batch_major_out=None):
    rows, d = x2d.shape
    half = d // 2
    consts = (wo, g, wu, wd, gf)
    tm, grid, blk = _row_tiles(rows)
    out_spec, out_shape, scratch = blk(d), jax.ShapeDtypeStruct((rows, d), F32), []
    if batch_major_out is not None:
        nb, s = batch_major_out
        out_spec = pl.BlockSpec((nb, tm // nb, d), lambda i: (0, i, 0))
        out_shape = jax.ShapeDtypeStruct((nb, s, d), F32)
        scratch = [pltpu.VMEM((d // LANES, tm, LANES), F32)]
    if _resident_array(wu).dtype == F32:
        (w_up, layer), (w_down, _) = wu, wd
        d_ff = w_up.shape[2]
        assert d == FF_CHUNK and d_ff % FF_CHUNK == 0
        any_spec = pl.BlockSpec(memory_space=pl.ANY)
        row_consts = (wo, g, gf)
        return pl.pallas_call(
            functools.partial(_out_mlp_cast_body, final=final, layer=layer),
            grid=grid,
            in_specs=[blk(d), blk(half, 0), blk(half, a1.shape[1] // half - 1)] + [_resident(p) for p in row_consts] + [any_spec, any_spec],
            out_specs=(out_spec, any_spec, any_spec),
            out_shape=(out_shape, jax.ShapeDtypeStruct((d, d_ff), BF16), jax.ShapeDtypeStruct((d_ff, d), BF16)),
            scratch_shapes=[pltpu.VMEM((d, d_ff), BF16), pltpu.VMEM((d_ff, d), BF16), pltpu.VMEM((2, FF_CHUNK, FF_CHUNK), F32),
                            pltpu.SemaphoreType.DMA((2,)), pltpu.SemaphoreType.DMA((2,))] + scratch,
            compiler_params=_params(1),
            name="out_mlp_cast",
        )(x2d, a1, a2, *[_resident_array(p) for p in row_consts], w_up, w_down)
    if grid == (1,) and batch_major_out is None and not isinstance(wu, tuple) and wu.shape[1] > FF_CHUNK:
        d_ff = wu.shape[1]
        once = lambda shape: pl.BlockSpec(shape, lambda c: (0,) * len(shape))
        return pl.pallas_call(
            functools.partial(_out_mlp_stream_body, final=final),
            grid=(d_ff // FF_CHUNK,),
            in_specs=[once((tm, d)), once((tm, half)), pl.BlockSpec((tm, half), lambda c: (0, a1.shape[1] // half - 1)),
                      _resident(wo), _resident(g), pl.BlockSpec((d, FF_CHUNK), lambda c: (0, c)),
                      pl.BlockSpec((FF_CHUNK, d), lambda c: (c, 0)), _resident(gf)],
            out_specs=once((tm, d)),
            out_shape=out_shape,
            scratch_shapes=[pltpu.VMEM((tm, d), BF16), pltpu.VMEM((tm, d), F32)],
            compiler_params=_params(1),
            name="out_mlp_stream",
        )(x2d, a1, a2, _resident_array(wo), _resident_array(g), wu, wd, _resident_array(gf))
    return pl.pallas_call(
        functools.partial(_out_mlp_body, final=final),
        grid=grid,
        in_specs=[blk(d), blk(half, 0), blk(half, a1.shape[1] // half - 1)] + [_resident(p) for p in consts],
        out_specs=out_spec,
        out_shape=out_shape,
        scratch_shapes=scratch,
        compiler_params=_params(1),
        name="out_mlp",
    )(x2d, a1, a2, *[_resident_array(p) for p in consts])


def _time_grid(total_rows, n_bg, tblock, pack_groups=False):
    steps = total_rows // (n_bg * SUBLANES)
    tb = min(tblock, steps)
    n_t = steps // tb
    groups = 1
    if pack_groups and n_t == 1:
        groups = max(1, min(n_bg, ROW_TILE // (steps * SUBLANES)))
        while n_bg % groups:
            groups -= 1
    blk = lambda width: pl.BlockSpec((groups * tb * SUBLANES, width), lambda bg, t: (bg * n_t + t, 0))
    per_group = lambda rows, width: pl.BlockSpec((groups * rows, width), lambda bg, t: (bg, 0))
    return tb, groups, (n_bg // groups, n_t), blk, per_group


def _gla_body(q_ref, k_ref, la_ref, v_ref, gt_ref, gh_ref, s0_ref, *rest, c, nh, dk, dv, transposed):
    o_ref, sfin_ref, st_ref = rest[-3:]
    tb = pl.program_id(1)
    rows = SUBLANES * c
    n_chunks = q_ref.shape[0] // rows
    dk_shift = int(math.log2(dk))
    heads_per_tile = LANES // dk

    @pl.when(tb == 0)
    def _():
        st_ref[...] = s0_ref[...]

    def iota(shape, axis):
        return lax.broadcasted_iota(jnp.int32, shape, axis)

    causal = ((iota((rows, rows), 0) & 7) == (iota((rows, rows), 1) & 7)) & (iota((rows, rows), 1) <= iota((rows, rows), 0))
    emask = (iota((rows, SUBLANES * dk), 0) & 7) == lax.shift_right_logical(iota((rows, SUBLANES * dk), 1), dk_shift)
    dmask = iota((SUBLANES, SUBLANES * dk), 0) == lax.shift_right_logical(iota((SUBLANES, SUBLANES * dk), 1), dk_shift)
    lo_rows = iota((rows, LANES), 1) < dk
    lo_8 = iota((SUBLANES, LANES), 1) < dk

    def expand(tile, h, lo=lo_rows, mask=emask):
        rolled = pltpu.roll(tile, dk, axis=1)
        d = jnp.where(lo, tile, rolled) if h % heads_per_tile == 0 else jnp.where(lo, rolled, tile)
        return jnp.where(mask, jnp.concatenate([d] * (SUBLANES * dk // LANES), axis=1), 0.0)

    def chunk(n, carry):
        r0 = n * rows
        q = _rows(q_ref, r0, rows)
        k = _rows(k_ref, r0, rows)
        la = _rows(la_ref, r0, rows)
        v = _rows(v_ref, r0, rows)
        gt = _rows(gt_ref, r0, rows)
        acc = jnp.zeros((SUBLANES, nh * dk), F32)
        pieces = []
        for t in range(c):
            acc = acc + la[t * SUBLANES:(t + 1) * SUBLANES, :]
            pieces.append(acc)
        cum = jnp.concatenate(pieces, axis=0)
        last = jnp.concatenate([acc] * c, axis=0)
        q_t = q * jnp.exp(cum) * (dk ** -0.5)
        k_t = k * jnp.exp(-cum)
        k_end = k * jnp.exp(last - cum)
        dec = jnp.exp(acc)
        pad = jnp.zeros((LANES - SUBLANES, LANES), F32)
        outs = []
        for h in range(nh):
            tile = (h // heads_per_tile) * LANES
            lane0 = (h % heads_per_tile) * dk
            sel = lo_rows if lane0 == 0 else jnp.logical_not(lo_rows)
            q_tile = q_t[:, tile:tile + LANES]
            scores = _dot_nt(jnp.where(sel, q_tile, 0.0).astype(BF16), k_t[:, tile:tile + LANES].astype(BF16))
            scores = jnp.where(causal, scores, 0.0).astype(BF16)
            v_h = v[:, h * dv:(h + 1) * dv]
            q_exp = expand(q_tile, h).astype(BF16)
            k_exp = expand(k_end[:, tile:tile + LANES], h).astype(BF16)
            if transposed:
                st = st_ref[h]
                o = _dot(scores, v_h) + _dot_nt(q_exp, st.astype(BF16))
                dec_row = jnp.sum(expand(dec[:, tile:tile + LANES], h, lo_8, dmask), axis=0, keepdims=True)
                st_ref[h] = st * dec_row + _dot_tn(v_h, k_exp)
            else:
                st = jnp.concatenate([st_ref[b, h] for b in range(SUBLANES)], axis=0)
                o = _dot(scores, v_h) + _dot(q_exp, st.astype(BF16))
                upd = _dot_tn(k_exp, v_h)
                dec_t = jnp.concatenate([dec[:, tile:tile + LANES], pad], axis=0).T
                for b in range(SUBLANES):
                    dec_b = jnp.broadcast_to(dec_t[lane0:lane0 + dk, b:b + 1], (dk, dv))
                    st_ref[b, h] = st[b * dk:(b + 1) * dk, :] * dec_b + upd[b * dk:(b + 1) * dk, :]
            o = _rmsnorm(o, gh_ref[:, h * dv:(h + 1) * dv])
            g_h = gt[:, h * dv:(h + 1) * dv]
            outs.append(o * (g_h * _sigmoid(g_h)))
        o_ref[pl.ds(pl.multiple_of(r0, rows), rows), :] = jnp.concatenate(outs, axis=1).astype(BF16)
        return carry

    lax.fori_loop(0, n_chunks, chunk, 0, unroll=min(n_chunks, 8))

    @pl.when(tb == pl.num_programs(1) - 1)
    def _():
        sfin_ref[...] = st_ref[...]


def _gla_state_in(s):
    b, h, dk, dv = s.shape
    return jnp.transpose(s.reshape(b // SUBLANES, SUBLANES, h, dk, dv), (0, 2, 4, 1, 3)).reshape(b // SUBLANES * h, dv, SUBLANES * dk)


def _gla_state_out(st, nh, dk):
    dv = st.shape[1]
    n_bg = st.shape[0] // nh
    return jnp.transpose(st.reshape(n_bg, nh, dv, SUBLANES, dk), (0, 3, 1, 4, 2)).reshape(n_bg * SUBLANES, nh, dk, dv)


def _gla(q, k, la, v, gt, gh, states, layer, n_layers, acc, nb, dims, *, c, tblock):
    nh, dk, dv = dims
    n_bg = nb // SUBLANES
    tb, _, grid, blk, _ = _time_grid(q.shape[0], n_bg, tblock)
    transposed = q.shape[0] // n_bg > SUBLANES * c
    body = functools.partial(_gla_body, c=c, nh=nh, dk=dk, dv=dv, transposed=transposed)
    row_specs = [blk(nh * dk), blk(nh * dk), blk(nh * dk), blk(nh * dv), blk(nh * dv), _const_spec(gh.shape)]
    og_shape = jax.ShapeDtypeStruct((q.shape[0], nh * dv), BF16)
    if transposed:
        st_block = (nh, dv, SUBLANES * dk)
        st_spec = pl.BlockSpec(st_block, lambda bg, t: (bg, 0, 0))
        s_in = jnp.zeros((n_bg * nh, dv, SUBLANES * dk), F32) if states is None else _gla_state_in(states[layer].astype(F32))
        og, s_fin = pl.pallas_call(
            body, grid=grid, in_specs=row_specs + [st_spec], out_specs=(blk(nh * dv), st_spec),
            out_shape=(og_shape, jax.ShapeDtypeStruct(s_in.shape, F32)), scratch_shapes=[pltpu.VMEM(st_block, F32)],
            compiler_params=_params(2), name="gla",
        )(q, k, la, v, gt, gh, s_in)
        s_new = _gla_state_out(s_fin, nh, dk)[None]
        return og, (jnp.pad(s_new, ((0, n_layers - 1),) + ((0, 0),) * 4) if acc is None else lax.dynamic_update_slice(acc, s_new, (layer, 0, 0, 0, 0)))
    st_block = (SUBLANES, nh, dk, dv)
    stacked_spec = pl.BlockSpec((None,) + st_block, lambda bg, t: (layer, bg, 0, 0, 0))
    if states is None:
        s_in, in_spec = jnp.zeros((nb, nh, dk, dv), F32), pl.BlockSpec(st_block, lambda bg, t: (bg, 0, 0, 0))
    else:
        s_in, in_spec = states.astype(F32), stacked_spec
    if acc is None:
        acc = jnp.zeros((n_layers, nb, nh, dk, dv), F32)
    return pl.pallas_call(
        body, grid=grid, in_specs=row_specs + [in_spec, pl.BlockSpec(memory_space=pl.ANY)], out_specs=(blk(nh * dv), stacked_spec),
        out_shape=(og_shape, jax.ShapeDtypeStruct(acc.shape, F32)), scratch_shapes=[pltpu.VMEM(st_block, F32)],
        input_output_aliases={7: 1}, compiler_params=_params(2), name="gla",
    )(q, k, la, v, gt, gh, s_in, acc)


def _s5_disc_body(a_re_ref, a_im_ref, log_dt_ref, b_re_ref, b_im_ref, ab_re_ref, ab_im_ref, bb_re_ref, bb_im_ref):
    a_re = a_re_ref[...]
    a_im = a_im_ref[...]
    dt = jnp.exp(log_dt_ref[...])
    mag = jnp.exp(dt * a_re)
    ab_re = mag * jnp.cos(dt * a_im)
    ab_im = mag * jnp.sin(dt * a_im)
    den = a_re * a_re + a_im * a_im
    coef_re = ((ab_re - 1.0) * a_re + ab_im * a_im) / den
    coef_im = (ab_im * a_re - (ab_re - 1.0) * a_im) / den
    b_re = b_re_ref[...]
    b_im = b_im_ref[...]
    ab_re_ref[...] = ab_re
    ab_im_ref[...] = ab_im
    bb_re_ref[...] = coef_re * b_re - coef_im * b_im
    bb_im_ref[...] = coef_re * b_im + coef_im * b_re


def _s5_discretise(a_re, a_im, log_dt, b_re, b_im):
    g, p, ch = b_re.shape
    rep = lambda t: jnp.repeat(t, ch, axis=0)
    flat = lambda t: jnp.transpose(t, (0, 2, 1)).reshape(g * ch, p)
    args = (rep(a_re), rep(a_im), rep(jnp.broadcast_to(log_dt[:, None], (g, p))), flat(b_re), flat(b_im))
    spec = _const_spec((g * ch, p))
    shape = jax.ShapeDtypeStruct((g * ch, p), F32)
    ab_re, ab_im, bb_re, bb_im = pl.pallas_call(
        _s5_disc_body, grid=(1,), in_specs=[spec] * 5, out_specs=(spec,) * 4, out_shape=(shape,) * 4,
        compiler_params=_params(1), name="s5_discretise",
    )(*args)
    return ab_re[::ch], ab_im[::ch], bb_re.reshape(g, ch, p), bb_im.reshape(g, ch, p)


def _block_diag_in(w_gcp):
    g, ch, p = w_gcp.shape
    gpt = LANES // ch
    w = w_gcp.reshape(g // gpt, gpt, ch, p)
    eye = jnp.eye(gpt, dtype=w.dtype)
    return jnp.einsum("ab,jacp->jacbp", eye, w).reshape(g // gpt, gpt * ch, gpt * p)


def _block_diag_out(w_gcp):
    g, ch, p = w_gcp.shape
    gpt = LANES // ch
    w = w_gcp.reshape(g // gpt, gpt, ch, p)
    eye = jnp.eye(gpt, dtype=w.dtype)
    return jnp.einsum("ab,jacp->japbc", eye, w).reshape(g // gpt, gpt * p, gpt * ch)


def _s5_body(u_ref, ab_re_ref, ab_im_ref, wb_re_ref, wb_im_ref, wc_ref, d_ref, wglu_ref, bglu_ref, h0_re_ref, h0_im_ref,
             z_ref, hf_re_ref, hf_im_ref, h_re_ref, h_im_ref, bu_re_ref, bu_im_ref, y_ref):
    step = pl.program_id(1)
    rows = u_ref.shape[0]
    groups = h_re_ref.shape[0] // SUBLANES
    tb = rows // SUBLANES // groups
    n_tiles = wb_re_ref.shape[0]
    sw = wb_re_ref.shape[2]

    @pl.when(step == 0)
    def _():
        h_re_ref[...] = h0_re_ref[...]
        h_im_ref[...] = h0_im_ref[...]

    def input_map(j):
        u_b = u_ref[:, j * LANES:(j + 1) * LANES].astype(BF16)
        bu_re_ref[j] = _dot(u_b, wb_re_ref[j])
        bu_im_ref[j] = _dot(u_b, wb_im_ref[j])

    def output_map(j):
        u_j = u_ref[:, j * LANES:(j + 1) * LANES]
        h_cat = jnp.concatenate([bu_re_ref[j].astype(BF16), bu_im_ref[j].astype(BF16)], axis=1)
        y_ref[:, j * LANES:(j + 1) * LANES] = _dot(h_cat, wc_ref[j]) + d_ref[:, j * LANES:(j + 1) * LANES] * u_j

    input_map(0)
    for j in range(n_tiles):
        if j + 1 < n_tiles:
            input_map(j + 1)
        a_re = jnp.broadcast_to(ab_re_ref[:, j * sw:(j + 1) * sw], (SUBLANES, sw))
        a_im = jnp.broadcast_to(ab_im_ref[:, j * sw:(j + 1) * sw], (SUBLANES, sw))
        for grp in range(groups):
            state_rows = slice(grp * SUBLANES, (grp + 1) * SUBLANES)
            h_re = h_re_ref[state_rows, j * sw:(j + 1) * sw]
            h_im = h_im_ref[state_rows, j * sw:(j + 1) * sw]
            for t in range(grp * tb, (grp + 1) * tb):
                sl = slice(t * SUBLANES, (t + 1) * SUBLANES)
                h_re, h_im = (a_re * h_re - a_im * h_im + bu_re_ref[j, sl, :], a_re * h_im + a_im * h_re + bu_im_ref[j, sl, :])
                bu_re_ref[j, sl, :] = h_re
                bu_im_ref[j, sl, :] = h_im
            h_re_ref[state_rows, j * sw:(j + 1) * sw] = h_re
            h_im_ref[state_rows, j * sw:(j + 1) * sw] = h_im
        output_map(j)

    z = _gelu(y_ref[...])
    gate = _sigmoid(_dot(z.astype(BF16), wglu_ref[...]) + bglu_ref[...])
    z_ref[...] = (z * gate).astype(BF16)

    @pl.when(step == pl.num_programs(1) - 1)
    def _():
        hf_re_ref[...] = h_re_ref[...]
        hf_im_ref[...] = h_im_ref[...]


def _s5(u, ab_re, ab_im, wb_re, wb_im, wc, d_skip, wglu, bglu, h0_re, h0_im, *, tblock):
    total, width = u.shape
    n_state = ab_re.shape[1]
    n_bg = h0_re.shape[0] // SUBLANES
    sw = wb_re.shape[2]
    tb, groups, grid, blk, per_group = _time_grid(total, n_bg, tblock, pack_groups=True)
    rows = groups * tb * SUBLANES
    st_spec = per_group(SUBLANES, n_state)
    consts = (ab_re, ab_im, wb_re, wb_im, wc, d_skip, wglu, bglu)
    st_shape = jax.ShapeDtypeStruct((n_bg * SUBLANES, n_state), F32)
    return pl.pallas_call(
        _s5_body,
        grid=grid,
        in_specs=[blk(width)] + [_resident(p) for p in consts] + [st_spec, st_spec],
        out_specs=(blk(width), st_spec, st_spec),
        out_shape=(jax.ShapeDtypeStruct((total, width), BF16), st_shape, st_shape),
        scratch_shapes=[pltpu.VMEM((groups * SUBLANES, n_state), F32), pltpu.VMEM((groups * SUBLANES, n_state), F32),
                        pltpu.VMEM((wb_re.shape[0], rows, sw), F32), pltpu.VMEM((wb_re.shape[0], rows, sw), F32),
                        pltpu.VMEM((rows, width), F32)],
        compiler_params=_params(2),
        name="s5",
    )(u, *[_resident_array(p) for p in consts], h0_re, h0_im)


def _lru_block(n, xbuf_ref, gate_ref, params, a_ref, b_ref, h, rows):
    cw_ref, cb_ref, wr_ref, br_ref, wi_ref, bi_ref, lam_ref = params
    bw = wr_ref.shape[1]
    cols = slice(n * bw, (n + 1) * bw)
    a_blk, b_blk = a_ref.at[n % 2], b_ref.at[n % 2]
    taps = cw_ref.shape[0]
    groups = h.shape[0] // SUBLANES
    seg = rows // groups
    pitch = seg + (taps - 1) * SUBLANES
    pieces = []
    for grp in range(groups):
        xc = cb_ref[:, cols] + xbuf_ref[grp * pitch:grp * pitch + seg, cols] * cw_ref[0:1, cols]
        for j in range(1, taps):
            xc = xc + xbuf_ref[grp * pitch + j * SUBLANES:grp * pitch + j * SUBLANES + seg, cols] * cw_ref[j:j + 1, cols]
        pieces.append(xc)
    xc = jnp.concatenate(pieces, axis=0) if groups > 1 else pieces[0]
    xb = xc.astype(BF16)
    t_i = jnp.tanh(_dot(xb, wi_ref[n]) + bi_ref[:, cols])
    t_r = jnp.tanh(_dot(xb, wr_ref[n]) + br_ref[:, cols])
    half_rate = (-0.5 * LRU_C) * _softplus(-lam_ref[:, cols])
    log_a = t_r * half_rate + half_rate
    a_blk[...] = jnp.exp(log_a)
    th = jnp.tanh(log_a)
    w = th / (th - 1.0)
    root = jnp.where(w > 0.0, w * lax.rsqrt(w), 0.0)
    scaled = xc * (0.5 * math.sqrt(2.0))
    b_blk[...] = root * (t_i * scaled + scaled)
    finals = []
    for grp in range(groups):
        h_g = h[grp * SUBLANES:(grp + 1) * SUBLANES, :]
        for t in range(grp * seg // SUBLANES, (grp + 1) * seg // SUBLANES):
            sl = slice(t * SUBLANES, (t + 1) * SUBLANES)
            h_g = a_blk[sl, :] * h_g + b_blk[sl, :]
            b_blk[sl, :] = h_g
        finals.append(h_g)
    h = jnp.concatenate(finals, axis=0) if groups > 1 else finals[0]
    return (_gelu(gate_ref[:, cols]) * b_blk[...]).astype(BF16), h


def _lru_body(gate_ref, xin_ref, conv0_ref, h0_ref, cw_ref, cb_ref, wr_ref, br_ref, wi_ref, bi_ref, lam_ref,
              o_ref, hf_ref, convf_ref, xbuf_ref, a_ref, b_ref, h_ref):
    step = pl.program_id(1)
    rows, width = xin_ref.shape
    tail = (cw_ref.shape[0] - 1) * SUBLANES
    n_blocks = wr_ref.shape[0]
    bw = wr_ref.shape[1]
    groups = h_ref.shape[0] // SUBLANES
    seg = rows // groups
    pitch = seg + tail

    @pl.when(step == 0)
    def _():
        for grp in range(groups):
            xbuf_ref[grp * pitch:grp * pitch + tail, :] = conv0_ref[grp * tail:(grp + 1) * tail, :]
        h_ref[...] = h0_ref[...]

    for grp in range(groups):
        xbuf_ref[grp * pitch + tail:(grp + 1) * pitch, :] = xin_ref[grp * seg:(grp + 1) * seg, :]
    params = (cw_ref, cb_ref, wr_ref, br_ref, wi_ref, bi_ref, lam_ref)
    for n in range(n_blocks):
        cols = slice(n * bw, (n + 1) * bw)
        o_ref[:, cols], h_ref[:, cols] = _lru_block(n, xbuf_ref, gate_ref, params, a_ref, b_ref, h_ref[:, cols], rows)
    for grp in range(groups):
        xbuf_ref[grp * pitch:grp * pitch + tail, :] = xbuf_ref[grp * pitch + seg:(grp + 1) * pitch, :]

    @pl.when(step == pl.num_programs(1) - 1)
    def _():
        hf_ref[...] = h_ref[...]
        for grp in range(groups):
            convf_ref[grp * tail:(grp + 1) * tail, :] = xbuf_ref[grp * pitch:grp * pitch + tail, :]


def _lru(gate, xin, conv0, h0, cw, cb, wr, br, wi, bi, lam, *, tblock):
    total, width = xin.shape
    n_bg = h0.shape[0] // SUBLANES
    tail = (cw.shape[0] - 1) * SUBLANES
    bw = _resident_shape(wr)[1]
    tb, groups, grid, blk, per_group = _time_grid(total, n_bg, tblock, pack_groups=True)
    rows = groups * tb * SUBLANES
    h_spec = per_group(SUBLANES, width)
    c_spec = per_group(tail, width)
    consts = (cw, cb, wr, br, wi, bi, lam)
    return pl.pallas_call(
        _lru_body,
        grid=grid,
        in_specs=[blk(width), blk(width), c_spec, h_spec] + [_resident(p) for p in consts],
        out_specs=(blk(width), h_spec, c_spec),
        out_shape=(jax.ShapeDtypeStruct((total, width), BF16), jax.ShapeDtypeStruct((n_bg * SUBLANES, width), F32),
                   jax.ShapeDtypeStruct((n_bg * tail, width), F32)),
        scratch_shapes=[pltpu.VMEM((rows + groups * tail, width), F32), pltpu.VMEM((2, rows, bw), F32),
                        pltpu.VMEM((2, rows, bw), F32), pltpu.VMEM((groups * SUBLANES, width), F32)],
        compiler_params=_params(2),
        name="rg_lru",
    )(gate, xin, conv0, h0, *[_resident_array(p) for p in consts])


def _row(t):
    return t.reshape(1, -1).astype(F32)


def _to_rows(x):
    b, s, d = x.shape
    return jnp.transpose(x.reshape(b // SUBLANES, SUBLANES, s, d), (0, 2, 1, 3)).reshape(b * s, d)


def _from_rows(y, b, s):
    d = y.shape[-1]
    return jnp.transpose(y.reshape(b // SUBLANES, s, SUBLANES, d), (0, 2, 1, 3)).reshape(b, s, d)


def _even_layer(x, nb, gla_states, gla_acc, e, n_even, s_re, s_im, prm, mlp, *, final, batch_major_out):
    (g_norm, w_in, w_u, w_a, wa2, ba2, g_head, ab_re, ab_im, wb_re, wb_im, wc, d_skip, wglu, bglu, w_out, dims) = prm
    nh, dk, dv = dims
    q, k, v, gt, u, la, *reordered = _inproj_even(x, g_norm, w_in, w_u, w_a, wa2, ba2, kw=nh * dk, vw=nh * dv)
    x2d = reordered[0] if reordered else x
    steps = x2d.shape[0] // nb
    og, gla_acc = _gla(q, k, la, v, gt, g_head, gla_states, e, n_even, gla_acc, nb, dims, c=math.gcd(steps, GLA_CHUNK), tblock=128)
    n_state = ab_re.shape[1]
    h0_re = jnp.zeros((nb, n_state), F32) if s_re is None else s_re.astype(F32).reshape(nb, n_state)
    h0_im = jnp.zeros((nb, n_state), F32) if s_im is None else s_im.astype(F32).reshape(nb, n_state)
    zz, hf_re, hf_im = _s5(u, ab_re, ab_im, wb_re, wb_im, wc, d_skip, wglu, bglu, h0_re, h0_im, tblock=128)
    g_mlp, w_up, w_down, g_fin = mlp
    y = _out_mlp(x2d, og, zz, w_out, g_mlp, w_up, w_down, g_fin, final=final, batch_major_out=batch_major_out)
    return y, gla_acc, hf_re, hf_im


def _odd_layer(x2d, nb, s_lru, s_conv, prm, mlp, *, final, batch_major_out):
    (g_norm, w_in, cw, cb, wr, br, wi, bi, lam, w_out) = prm
    n_bg = nb // SUBLANES
    taps = cw.shape[0]
    width = _resident_shape(w_in)[1] // 2
    gate, xin = _inproj_odd(x2d, g_norm, w_in)
    h0 = jnp.zeros((nb, width), F32) if s_lru is None else s_lru.astype(F32)
    if s_conv is None:
        conv0 = jnp.zeros((n_bg * (taps - 1) * SUBLANES, width), F32)
    else:
        conv0 = jnp.transpose(s_conv.astype(F32).reshape(n_bg, SUBLANES, taps - 1, width), (0, 2, 1, 3)).reshape(-1, width)
    gh, hf, convf = _lru(gate, xin, conv0, h0, cw, cb, wr, br, wi, bi, lam, tblock=128)
    g_mlp, w_up, w_down, g_fin = mlp
    y = _out_mlp(x2d, gh, gh, w_out, g_mlp, w_up, w_down, g_fin, final=final, batch_major_out=batch_major_out)
    conv_new = jnp.transpose(convf.reshape(n_bg, taps - 1, SUBLANES, width), (0, 2, 1, 3)).reshape(nb, taps - 1, width)
    return y, hf, conv_new


def kernel(x_prompt, x_sample, state_gla, state_s5_re, state_s5_im, state_lru, state_conv, norm_mix_even, w_in_even, gla_w_a2, gla_b_a2, gla_norm, s5_a_re, s5_a_im, s5_log_dt, s5_b_re, s5_b_im, s5_c_re, s5_c_im, s5_d, s5_w_glu, s5_b_glu, w_out_even, norm_mix_odd, w_in_odd, conv_w, conv_b, lru_w_r, lru_b_r, lru_w_i, lru_b_i, lru_lam, w_out_odd, norm_mlp, w_up, w_down, norm_final):
    depth = norm_mlp.shape[0]
    nh, dk, dv = state_gla.shape[2:]
    rank = gla_w_a2.shape[1]
    n_groups, n_p, ch = s5_b_re.shape[1:]
    kw, vw = nh * dk, nh * dv
    assert rank <= LANES and LANES % ch == 0 and LANES % dk == 0 and dv == LANES
    bp, sp, _ = x_prompt.shape
    bs, ss, _ = x_sample.shape
    n_even = state_gla.shape[0]

    in_kernel_order = bp == SUBLANES and sp % (INPROJ_TILE // SUBLANES) == 0 and depth > 1
    yp = x_prompt.astype(F32) if in_kernel_order else _to_rows(x_prompt.astype(F32))
    ys = _to_rows(x_sample.astype(F32))
    g_fin = _row(norm_final)
    o4 = 2 * kw + 2 * vw
    o5 = o4 + rank
    w_in_main = w_in_even[:, :, :o4].astype(BF16)
    w_in_u = w_in_even[:, :, o5:].astype(BF16)
    w_in_a = jnp.pad(w_in_even[:, :, o4:o5], ((0, 0), (0, 0), (0, LANES - rank))).astype(BF16)
    w_a2 = jnp.pad(gla_w_a2, ((0, 0), (0, LANES - rank), (0, 0))).astype(BF16)
    w_up_f, w_down_f = w_up.astype(F32), w_down.astype(F32)
    w_out_even_b, w_out_odd_b, w_in_odd_b = w_out_even.astype(BF16), w_out_odd.astype(BF16), w_in_odd.astype(BF16)
    w_glu_b, w_r_b, w_i_b = s5_w_glu.astype(BF16), (0.5 * lru_w_r).astype(BF16), (0.5 * lru_w_i).astype(BF16)
    keys = ("re", "im", "lru", "conv")
    outs_p = {key: [] for key in keys}
    outs_s = {key: [] for key in keys}
    gla_p = gla_s = None

    def record(outs, sr, si):
        outs["re"].append(sr.reshape(-1, n_groups, n_p))
        outs["im"].append(si.reshape(-1, n_groups, n_p))

    for layer in range(depth):
        final = layer == depth - 1
        bm_out = (bp, sp) if final and in_kernel_order else None
        mlp = (_row(norm_mlp[layer]), (w_up_f, layer), (w_down_f, layer), g_fin)
        if layer % 2 == 0:
            e = layer // 2
            ab_re, ab_im, bb_re, bb_im = _s5_discretise(s5_a_re[e].astype(F32), s5_a_im[e].astype(F32), s5_log_dt[e].astype(F32),
                                                        s5_b_re[e].astype(F32), s5_b_im[e].astype(F32))
            wc = jnp.concatenate([_block_diag_out(s5_c_re[e].astype(F32)), -_block_diag_out(s5_c_im[e].astype(F32))], axis=1)
            prm = (_row(norm_mix_even[e]), (w_in_main, e), (w_in_u, e), (w_in_a, e), (w_a2, e), _row(gla_b_a2[e]), _row(gla_norm[e]),
                   ab_re.reshape(1, -1), ab_im.reshape(1, -1), _block_diag_in(bb_re).astype(BF16), _block_diag_in(bb_im).astype(BF16),
                   wc.astype(BF16), _row(s5_d[e]), (w_glu_b, e), _row(s5_b_glu[e]), (w_out_even_b, e), (nh, dk, dv))
            (yp, w_up_l, w_down_l), gla_p, sr, si = _even_layer(yp, bp, None, gla_p, e, n_even, None, None, prm, mlp, final=final,
                                                                batch_major_out=bm_out)
            record(outs_p, sr, si)
            mlp = (mlp[0], w_up_l, w_down_l, g_fin)
            ys, gla_s, sr, si = _even_layer(ys, bs, state_gla, gla_s, e, n_even, state_s5_re[e], state_s5_im[e], prm, mlp, final=final,
                                            batch_major_out=None)
            record(outs_s, sr, si)
        else:
            o = layer // 2
            prm = (_row(norm_mix_odd[o]), (w_in_odd_b, o), conv_w[o].astype(F32), _row(conv_b[o]),
                   (w_r_b, o), _row(0.5 * lru_b_r[o]), (w_i_b, o), _row(0.5 * lru_b_i[o]), _row(lru_lam[o]), (w_out_odd_b, o))
            (yp, w_up_l, w_down_l), sl, sc = _odd_layer(yp, bp, None, None, prm, mlp, final=final, batch_major_out=bm_out)
            outs_p["lru"].append(sl)
            outs_p["conv"].append(sc)
            mlp = (mlp[0], w_up_l, w_down_l, g_fin)
            ys, sl, sc = _odd_layer(ys, bs, state_lru[o], state_conv[o], prm, mlp, final=final, batch_major_out=None)
            outs_s["lru"].append(sl)
            outs_s["conv"].append(sc)

    stack = lambda outs: tuple(jnp.stack(outs[key]) for key in keys)
    return (yp if in_kernel_order else _from_rows(yp, bp, sp), _from_rows(ys, bs, ss), gla_p) + stack(outs_p) + (gla_s,) + stack(outs_s)
```

```python
import functools
import math

import jax
import jax.numpy as jnp
from jax import lax
from jax.experimental import pallas as pl
from jax.experimental.pallas import tpu as pltpu

F32 = jnp.float32
BF16 = jnp.bfloat16

NORM_EPS = 1e-6
GLA_TAU = 16.0
GLA_CHUNK = 16
LRU_C = 8.0

SUBLANES = 8
LANES = 128
ROW_TILE = 512
INPROJ_TILE = 1024
FF_CHUNK = 1024
VMEM_LIMIT = 48 * 1024 * 1024


def _params(n_axes):
    return pltpu.CompilerParams(dimension_semantics=("arbitrary",) * n_axes, vmem_limit_bytes=VMEM_LIMIT)


def _const_spec(shape):
    zeros = (0,) * len(shape)
    return pl.BlockSpec(shape, lambda *_: zeros, pipeline_mode=pl.Buffered(1))


def _resident(p):
    if not isinstance(p, tuple):
        return _const_spec(p.shape)
    stacked, layer = p
    index = (layer,) + (0,) * (stacked.ndim - 1)
    return pl.BlockSpec((None,) + stacked.shape[1:], lambda *_: index, pipeline_mode=pl.Buffered(1))


def _resident_array(p):
    return p[0] if isinstance(p, tuple) else p


def _resident_shape(p):
    return p[0].shape[1:] if isinstance(p, tuple) else p.shape


def _dot(a, b):
    return jnp.dot(a, b, preferred_element_type=F32)


def _dot_nt(a, b):
    return lax.dot_general(a, b, (((1,), (1,)), ((), ())), preferred_element_type=F32)


def _dot_tn(a, b):
    return lax.dot_general(a, b, (((0,), (0,)), ((), ())), preferred_element_type=F32)


def _rmsnorm(x, g):
    return x * lax.rsqrt(jnp.mean(x * x, axis=-1, keepdims=True) + NORM_EPS) * g


def _log_sigmoid(z):
    return jnp.minimum(z, 0.0) - jnp.log1p(jnp.exp(-jnp.abs(z)))


def _softplus(z):
    return jnp.maximum(z, 0.0) + jnp.log1p(jnp.exp(-jnp.abs(z)))


_GELU_C1 = math.sqrt(2.0 / math.pi)
_GELU_C2 = _GELU_C1 * 0.044715


def _gelu(x):
    half = 0.5 * x
    return half + half * jnp.tanh(x * (_GELU_C1 + _GELU_C2 * (x * x)))


def _sigmoid(x):
    return 0.5 * jnp.tanh(0.5 * x) + 0.5


def _rows(ref, start, size):
    return ref[pl.ds(pl.multiple_of(start, size), size), :]


def _interleave_rows(x_ref, slab_ref):
    nb, ts, d = x_ref.shape
    for b in range(nb):
        for l in range(d // LANES):
            slab_ref[l, pl.ds(b, ts, stride=nb), :] = x_ref[b, :, l * LANES:(l + 1) * LANES]
    return jnp.concatenate([slab_ref[l] for l in range(d // LANES)], axis=1)


def _deinterleave_rows(y, slab_ref, o_ref):
    nb, ts, d = o_ref.shape
    for l in range(d // LANES):
        slab_ref[l] = y[:, l * LANES:(l + 1) * LANES]
    for b in range(nb):
        for l in range(d // LANES):
            o_ref[b, :, l * LANES:(l + 1) * LANES] = slab_ref[l, pl.ds(b, ts, stride=nb), :]


def _row_tiles(rows, tile=ROW_TILE):
    tm = tile if rows % tile == 0 else min(ROW_TILE, rows)
    return tm, (rows // tm,), lambda width, col=0: pl.BlockSpec((tm, width), lambda i: (i, col))


def _inproj_even_body(x_ref, g_ref, w_ref, wu_ref, wa_ref, wa2_ref, ba2_ref, q_ref, k_ref, v_ref, gt_ref, u_ref, la_ref, *rest,
                      kw, vw):
    if rest:
        xt_ref, slab_ref = rest
        x = _interleave_rows(x_ref, slab_ref)
        xt_ref[...] = x
    else:
        x = x_ref[...]
    xn = _rmsnorm(x, g_ref[...]).astype(BF16)
    o1, o2 = kw, 2 * kw
    o3 = o2 + vw
    o4 = o3 + vw
    a_lr = _dot(xn, wa_ref[...]).astype(BF16)
    z = _dot(a_lr, wa2_ref[...]) + ba2_ref[...]
    la_ref[...] = _log_sigmoid(z) * (1.0 / GLA_TAU)
    q_ref[...] = _dot(xn, w_ref[:, 0:o1])
    k_ref[...] = _dot(xn, w_ref[:, o1:o2])
    v_ref[...] = _dot(xn, w_ref[:, o2:o3]).astype(BF16)
    gt_ref[...] = _dot(xn, w_ref[:, o3:o4])
    u_ref[...] = _dot(xn, wu_ref[...])


def _inproj_even(x, g, w, wu, wa, wa2, ba2, *, kw, vw):
    batch_major = x.ndim == 3
    d = x.shape[-1]
    rows = x.size // d
    sw = _resident_shape(wu)[1]
    widths = [kw, kw, vw, vw, sw, kw]
    dtypes = [F32, F32, BF16, F32, F32, F32]
    consts = (g, w, wu, wa, wa2, ba2)
    tm, grid, blk = _row_tiles(rows, INPROJ_TILE)
    x_spec, scratch = blk(d), []
    if batch_major:
        nb = x.shape[0]
        x_spec = pl.BlockSpec((nb, tm // nb, d), lambda i: (0, i, 0))
        widths.append(d)
        dtypes.append(F32)
        scratch = [pltpu.VMEM((d // LANES, tm, LANES), F32)]
    return pl.pallas_call(
        functools.partial(_inproj_even_body, kw=kw, vw=vw),
        grid=grid,
        in_specs=[x_spec] + [_resident(p) for p in consts],
        out_specs=tuple(blk(wd) for wd in widths),
        out_shape=tuple(jax.ShapeDtypeStruct((rows, wd), dt) for wd, dt in zip(widths, dtypes)),
        scratch_shapes=scratch,
        compiler_params=_params(1),
        name="inproj_even",
    )(x, *[_resident_array(p) for p in consts])


def _inproj_odd_body(x_ref, g_ref, w_ref, gate_ref, xin_ref, *, width):
    xn = _rmsnorm(x_ref[...], g_ref[...]).astype(BF16)
    gate_ref[...] = _dot(xn, w_ref[:, 0:width])
    xin_ref[...] = _dot(xn, w_ref[:, width:2 * width])


def _inproj_odd(x2d, g, w):
    rows, d = x2d.shape
    width = _resident_shape(w)[1] // 2
    _, grid, blk = _row_tiles(rows, INPROJ_TILE)
    shape = jax.ShapeDtypeStruct((rows, width), F32)
    return pl.pallas_call(
        functools.partial(_inproj_odd_body, width=width),
        grid=grid,
        in_specs=[blk(d), _resident(g), _resident(w)],
        out_specs=(blk(width), blk(width)),
        out_shape=(shape, shape),
        compiler_params=_params(1),
        name="inproj_odd",
    )(x2d, _resident_array(g), _resident_array(w))


def _out_mlp_cast_body(x_ref, a1_ref, a2_ref, wo_ref, g_ref, gf_ref, wu_hbm, wd_hbm, o_ref, wu_out, wd_out,
                       wu_bf, wd_bf, stage, sem_in, sem_out, *slab, final, layer):
    i = pl.program_id(0)
    d, d_ff = wu_bf.shape
    n_up = d_ff // FF_CHUNK
    n_chunks = 2 * n_up

    def fetch(k):
        c = k // 2
        if k % 2 == 0:
            src = wu_hbm.at[layer, :, pl.ds(c * FF_CHUNK, FF_CHUNK)]
        else:
            src = wd_hbm.at[layer, pl.ds(c * FF_CHUNK, FF_CHUNK), :]
        return pltpu.make_async_copy(src, stage.at[k % 2], sem_in.at[k % 2])

    def publish():
        return (pltpu.make_async_copy(wu_bf, wu_out, sem_out.at[0]), pltpu.make_async_copy(wd_bf, wd_out, sem_out.at[1]))

    def ready(k):
        fetch(k).wait()
        w = stage[k % 2].astype(BF16)
        c = k // 2
        if k % 2 == 0:
            wu_bf[:, c * FF_CHUNK:(c + 1) * FF_CHUNK] = w
        else:
            wd_bf[c * FF_CHUNK:(c + 1) * FF_CHUNK, :] = w
        if k + 2 < n_chunks:
            fetch(k + 2).start()

    @pl.when(i == 0)
    def _():
        fetch(0).start()
        fetch(1).start()
        _out_mlp_body(x_ref, a1_ref, a2_ref, wo_ref, g_ref, wu_bf, wd_bf, gf_ref, o_ref, *slab, final=final, before_chunk=ready)
        for copy in publish():
            copy.start()

    @pl.when(i > 0)
    def _():
        _out_mlp_body(x_ref, a1_ref, a2_ref, wo_ref, g_ref, wu_bf, wd_bf, gf_ref, o_ref, *slab, final=final)

    @pl.when(i == pl.num_programs(0) - 1)
    def _():
        for copy in publish():
            copy.wait()


def _out_mlp_stream_body(x_ref, a1_ref, a2_ref, wo_ref, g_ref, wu_ref, wd_ref, gf_ref, o_ref, xn_ref, acc_ref, *, final):
    c = pl.program_id(0)
    half = a1_ref.shape[1]

    @pl.when(c == 0)
    def _():
        mix = _dot(a1_ref[...], wo_ref[0:half, :]) + _dot(a2_ref[...], wo_ref[half:2 * half, :])
        x1 = x_ref[...] + mix
        xn_ref[...] = _rmsnorm(x1, g_ref[...]).astype(BF16)
        acc_ref[...] = x1

    h = jnp.square(jnp.maximum(_dot(xn_ref[...], wu_ref[...]), 0.0)).astype(BF16)
    acc_ref[...] += _dot(h, wd_ref[...])

    @pl.when(c == pl.num_programs(0) - 1)
    def _():
        o_ref[...] = _rmsnorm(acc_ref[...], gf_ref[...]) if final else acc_ref[...]


def _out_mlp_body(x_ref, a1_ref, a2_ref, wo_ref, g_ref, wu_ref, wd_ref, gf_ref, o_ref, *slab, final, before_chunk=None):
    half = a1_ref.shape[1]
    mix = _dot(a1_ref[...], wo_ref[0:half, :]) + _dot(a2_ref[...], wo_ref[half:2 * half, :])
    x1 = x_ref[...] + mix
    xn = _rmsnorm(x1, g_ref[...]).astype(BF16)
    acc = x1
    for c in range(wu_ref.shape[1] // FF_CHUNK):
        if before_chunk is not None:
            before_chunk(2 * c)
        h = _dot(xn, wu_ref[:, c * FF_CHUNK:(c + 1) * FF_CHUNK])
        h = jnp.square(jnp.maximum(h, 0.0)).astype(BF16)
        if before_chunk is not None:
            before_chunk(2 * c + 1)
        acc = acc + _dot(h, wd_ref[c * FF_CHUNK:(c + 1) * FF_CHUNK, :])
    if final:
        acc = _rmsnorm(acc, gf_ref[...])
    if slab:
        _deinterleave_rows(acc, slab[0], o_ref)
    else:
        o_ref[...] = acc


def _out_mlp(x2d, a1, a2, wo, g, wu, wd, gf, *, final, batch_major_out=None):
    rows, d = x2d.shape
    half = d // 2
    consts = (wo, g, wu, wd, gf)
    tm, grid, blk = _row_tiles(rows)
    out_spec, out_shape, scratch = blk(d), jax.ShapeDtypeStruct((rows, d), F32), []
    if batch_major_out is not None:
        nb, s = batch_major_out
        out_spec = pl.BlockSpec((nb, tm // nb, d), lambda i: (0, i, 0))
        out_shape = jax.ShapeDtypeStruct((nb, s, d), F32)
        scratch = [pltpu.VMEM((d // LANES, tm, LANES), F32)]
    if _resident_array(wu).dtype == F32:
        (w_up, layer), (w_down, _) = wu, wd
        d_ff = w_up.shape[2]
        assert d == FF_CHUNK and d_ff % FF_CHUNK == 0
        any_spec = pl.BlockSpec(memory_space=pl.ANY)
        row_consts = (wo, g, gf)
        return pl.pallas_call(
            functools.partial(_out_mlp_cast_body, final=final, layer=layer),
            grid=grid,
            in_specs=[blk(d), blk(half, 0), blk(half, a1.shape[1] // half - 1)] + [_resident(p) for p in row_consts] + [any_spec, any_spec],
            out_specs=(out_spec, any_spec, any_spec),
            out_shape=(out_shape, jax.ShapeDtypeStruct((d, d_ff), BF16), jax.ShapeDtypeStruct((d_ff, d), BF16)),
            scratch_shapes=[pltpu.VMEM((d, d_ff), BF16), pltpu.VMEM((d_ff, d), BF16), pltpu.VMEM((2, FF_CHUNK, FF_CHUNK), F32),
                            pltpu.SemaphoreType.DMA((2,)), pltpu.SemaphoreType.DMA((2,))] + scratch,
            compiler_params=_params(1),
            name="out_mlp_cast",
        )(x2d, a1, a2, *[_resident_array(p) for p in row_consts], w_up, w_down)
    if grid == (1,) and batch_major_out is None and not isinstance(wu, tuple) and wu.shape[1] > FF_CHUNK:
        d_ff = wu.shape[1]
        once = lambda shape: pl.BlockSpec(shape, lambda c: (0,) * len(shape))
        return pl.pallas_call(
            functools.partial(_out_mlp_stream_body, final=final),
            grid=(d_ff // FF_CHUNK,),
            in_specs=[once((tm, d)), once((tm, half)), pl.BlockSpec((tm, half), lambda c: (0, a1.shape[1] // half - 1)),
                      _resident(wo), _resident(g), pl.BlockSpec((d, FF_CHUNK), lambda c: (0, c)),
                      pl.BlockSpec((FF_CHUNK, d), lambda c: (c, 0)), _resident(gf)],
            out_specs=once((tm, d)),
            out_shape=out_shape,
            scratch_shapes=[pltpu.VMEM((tm, d), BF16), pltpu.VMEM((tm, d), F32)],
            compiler_params=_params(1),
            name="out_mlp_stream",
        )(x2d, a1, a2, _resident_array(wo), _resident_array(g), wu, wd, _resident_array(gf))
    return pl.pallas_call(
        functools.partial(_out_mlp_body, final=final),
        grid=grid,
        in_specs=[blk(d), blk(half, 0), blk(half, a1.shape[1] // half - 1)] + [_resident(p) for p in consts],
        out_specs=out_spec,
        out_shape=out_shape,
        scratch_shapes=scratch,
        compiler_params=_params(1),
        name="out_mlp",
    )(x2d, a1, a2, *[_resident_array(p) for p in consts])


def _time_grid(total_rows, n_bg, tblock, pack_groups=False):
    steps = total_rows // (n_bg * SUBLANES)
    tb = min(tblock, steps)
    n_t = steps // tb
    groups = 1
    if pack_groups and n_t == 1:
        groups = max(1, min(n_bg, ROW_TILE // (steps * SUBLANES)))
        while n_bg % groups:
            groups -= 1
    blk = lambda width: pl.BlockSpec((groups * tb * SUBLANES, width), lambda bg, t: (bg * n_t + t, 0))
    per_group = lambda rows, width: pl.BlockSpec((groups * rows, width), lambda bg, t: (bg, 0))
    return tb, groups, (n_bg // groups, n_t), blk, per_group


def _gla_body(q_ref, k_ref, la_ref, v_ref, gt_ref, gh_ref, s0_ref, *rest, c, nh, dk, dv, transposed):
    o_ref, sfin_ref, st_ref = rest[-3:]
    tb = pl.program_id(1)
    rows = SUBLANES * c
    n_chunks = q_ref.shape[0] // rows
    dk_shift = int(math.log2(dk))
    heads_per_tile = LANES // dk

    @pl.when(tb == 0)
    def _():
        st_ref[...] = s0_ref[...]

    def iota(shape, axis):
        return lax.broadcasted_iota(jnp.int32, shape, axis)

    causal = ((iota((rows, rows), 0) & 7) == (iota((rows, rows), 1) & 7)) & (iota((rows, rows), 1) <= iota((rows, rows), 0))
    emask = (iota((rows, SUBLANES * dk), 0) & 7) == lax.shift_right_logical(iota((rows, SUBLANES * dk), 1), dk_shift)
    dmask = iota((SUBLANES, SUBLANES * dk), 0) == lax.shift_right_logical(iota((SUBLANES, SUBLANES * dk), 1), dk_shift)
    lo_rows = iota((rows, LANES), 1) < dk
    lo_8 = iota((SUBLANES, LANES), 1) < dk

    def expand(tile, h, lo=lo_rows, mask=emask):
        rolled = pltpu.roll(tile, dk, axis=1)
        d = jnp.where(lo, tile, rolled) if h % heads_per_tile == 0 else jnp.where(lo, rolled, tile)
        return jnp.where(mask, jnp.concatenate([d] * (SUBLANES * dk // LANES), axis=1), 0.0)

    def prepare(n):
        sl = slice(n * rows, (n + 1) * rows)
        q, k, la, v, gt = q_ref[sl, :], k_ref[sl, :], la_ref[sl, :], v_ref[sl, :], gt_ref[sl, :]
        acc = jnp.zeros((SUBLANES, nh * dk), F32)
        pieces = []
        for t in range(c):
            acc = acc + la[t * SUBLANES:(t + 1) * SUBLANES, :]
            pieces.append(acc)
        cum = jnp.concatenate(pieces, axis=0)
        last = jnp.concatenate([acc] * c, axis=0)
        q_t = q * jnp.exp(cum) * (dk ** -0.5)
        k_t = k * jnp.exp(-cum)
        k_end = k * jnp.exp(last - cum)
        dec = jnp.exp(acc)
        return sl, q_t, k_t, k_end, dec, v, gt

    def advance(chunk):
        sl, q_t, k_t, k_end, dec, v, gt = chunk
        pad = jnp.zeros((LANES - SUBLANES, LANES), F32)
        outs = []
        for h in range(nh):
            tile = (h // heads_per_tile) * LANES
            lane0 = (h % heads_per_tile) * dk
            sel = lo_rows if lane0 == 0 else jnp.logical_not(lo_rows)
            q_tile = q_t[:, tile:tile + LANES]
            scores = _dot_nt(jnp.where(sel, q_tile, 0.0).astype(BF16), k_t[:, tile:tile + LANES].astype(BF16))
            scores = jnp.where(causal, scores, 0.0).astype(BF16)
            v_h = v[:, h * dv:(h + 1) * dv]
            q_exp = expand(q_tile, h).astype(BF16)
            k_exp = expand(k_end[:, tile:tile + LANES], h).astype(BF16)
            if transposed:
                st = st_ref[h]
                o = _dot(scores, v_h) + _dot_nt(q_exp, st.astype(BF16))
                dec_row = jnp.sum(expand(dec[:, tile:tile + LANES], h, lo_8, dmask), axis=0, keepdims=True)
                st_ref[h] = st * dec_row + _dot_tn(v_h, k_exp)
            else:
                st = jnp.concatenate([st_ref[b, h] for b in range(SUBLANES)], axis=0)
                o = _dot(scores, v_h) + _dot(q_exp, st.astype(BF16))
                upd = _dot_tn(k_exp, v_h)
                dec_t = jnp.concatenate([dec[:, tile:tile + LANES], pad], axis=0).T
                for b in range(SUBLANES):
                    dec_b = jnp.broadcast_to(dec_t[lane0:lane0 + dk, b:b + 1], (dk, dv))
                    st_ref[b, h] = st[b * dk:(b + 1) * dk, :] * dec_b + upd[b * dk:(b + 1) * dk, :]
            o = _rmsnorm(o, gh_ref[:, h * dv:(h + 1) * dv])
            g_h = gt[:, h * dv:(h + 1) * dv]
            outs.append(o * (g_h * _sigmoid(g_h)))
        o_ref[sl, :] = jnp.concatenate(outs, axis=1).astype(BF16)

    chunk = prepare(0)
    for n in range(n_chunks):
        following = prepare(n + 1) if n + 1 < n_chunks else None
        advance(chunk)
        chunk = following

    @pl.when(tb == pl.num_programs(1) - 1)
    def _():
        sfin_ref[...] = st_ref[...]


def _gla_state_in(s):
    b, h, dk, dv = s.shape
    return jnp.transpose(s.reshape(b // SUBLANES, SUBLANES, h, dk, dv), (0, 2, 4, 1, 3)).reshape(b // SUBLANES * h, dv, SUBLANES * dk)


def _gla_state_out(st, nh, dk):
    dv = st.shape[1]
    n_bg = st.shape[0] // nh
    return jnp.transpose(st.reshape(n_bg, nh, dv, SUBLANES, dk), (0, 3, 1, 4, 2)).reshape(n_bg * SUBLANES, nh, dk, dv)


def _gla(q, k, la, v, gt, gh, states, layer, n_layers, acc, nb, dims, *, c, tblock):
    nh, dk, dv = dims
    n_bg = nb // SUBLANES
    tb, _, grid, blk, _ = _time_grid(q.shape[0], n_bg, tblock)
    transposed = q.shape[0] // n_bg > SUBLANES * c
    body = functools.partial(_gla_body, c=c, nh=nh, dk=dk, dv=dv, transposed=transposed)
    row_specs = [blk(nh * dk), blk(nh * dk), blk(nh * dk), blk(nh * dv), blk(nh * dv), _const_spec(gh.shape)]
    og_shape = jax.ShapeDtypeStruct((q.shape[0], nh * dv), BF16)
    if transposed:
        st_block = (nh, dv, SUBLANES * dk)
        st_spec = pl.BlockSpec(st_block, lambda bg, t: (bg, 0, 0))
        s_in = jnp.zeros((n_bg * nh, dv, SUBLANES * dk), F32) if states is None else _gla_state_in(states[layer].astype(F32))
        og, s_fin = pl.pallas_call(
            body, grid=grid, in_specs=row_specs + [st_spec], out_specs=(blk(nh * dv), st_spec),
            out_shape=(og_shape, jax.ShapeDtypeStruct(s_in.shape, F32)), scratch_shapes=[pltpu.VMEM(st_block, F32)],
            compiler_params=_params(2), name="gla",
        )(q, k, la, v, gt, gh, s_in)
        s_new = _gla_state_out(s_fin, nh, dk)[None]
        return og, (jnp.pad(s_new, ((0, n_layers - 1),) + ((0, 0),) * 4) if acc is None else lax.dynamic_update_slice(acc, s_new, (layer, 0, 0, 0, 0)))
    st_block = (SUBLANES, nh, dk, dv)
    stacked_spec = pl.BlockSpec((None,) + st_block, lambda bg, t: (layer, bg, 0, 0, 0))
    if states is None:
        s_in, in_spec = jnp.zeros((nb, nh, dk, dv), F32), pl.BlockSpec(st_block, lambda bg, t: (bg, 0, 0, 0))
    else:
        s_in, in_spec = states.astype(F32), stacked_spec
    if acc is None:
        acc = jnp.zeros((n_layers, nb, nh, dk, dv), F32)
    return pl.pallas_call(
        body, grid=grid, in_specs=row_specs + [in_spec, pl.BlockSpec(memory_space=pl.ANY)], out_specs=(blk(nh * dv), stacked_spec),
        out_shape=(og_shape, jax.ShapeDtypeStruct(acc.shape, F32)), scratch_shapes=[pltpu.VMEM(st_block, F32)],
        input_output_aliases={7: 1}, compiler_params=_params(2), name="gla",
    )(q, k, la, v, gt, gh, s_in, acc)


def _s5_disc_body(a_re_ref, a_im_ref, log_dt_ref, b_re_ref, b_im_ref, ab_re_ref, ab_im_ref, bb_re_ref, bb_im_ref):
    a_re = a_re_ref[...]
    a_im = a_im_ref[...]
    dt = jnp.exp(log_dt_ref[...])
    mag = jnp.exp(dt * a_re)
    ab_re = mag * jnp.cos(dt * a_im)
    ab_im = mag * jnp.sin(dt * a_im)
    den = a_re * a_re + a_im * a_im
    coef_re = ((ab_re - 1.0) * a_re + ab_im * a_im) / den
    coef_im = (ab_im * a_re - (ab_re - 1.0) * a_im) / den
    b_re = b_re_ref[...]
    b_im = b_im_ref[...]
    ab_re_ref[...] = ab_re
    ab_im_ref[...] = ab_im
    bb_re_ref[...] = coef_re * b_re - coef_im * b_im
    bb_im_ref[...] = coef_re * b_im + coef_im * b_re


def _s5_discretise(a_re, a_im, log_dt, b_re, b_im):
    g, p, ch = b_re.shape
    rep = lambda t: jnp.repeat(t, ch, axis=0)
    flat = lambda t: jnp.transpose(t, (0, 2, 1)).reshape(g * ch, p)
    args = (rep(a_re), rep(a_im), rep(jnp.broadcast_to(log_dt[:, None], (g, p))), flat(b_re), flat(b_im))
    spec = _const_spec((g * ch, p))
    shape = jax.ShapeDtypeStruct((g * ch, p), F32)
    ab_re, ab_im, bb_re, bb_im = pl.pallas_call(
        _s5_disc_body, grid=(1,), in_specs=[spec] * 5, out_specs=(spec,) * 4, out_shape=(shape,) * 4,
        compiler_params=_params(1), name="s5_discretise",
    )(*args)
    return ab_re[::ch], ab_im[::ch], bb_re.reshape(g, ch, p), bb_im.reshape(g, ch, p)


def _block_diag_in(w_gcp):
    g, ch, p = w_gcp.shape
    gpt = LANES // ch
    w = w_gcp.reshape(g // gpt, gpt, ch, p)
    eye = jnp.eye(gpt, dtype=w.dtype)
    return jnp.einsum("ab,jacp->jacbp", eye, w).reshape(g // gpt, gpt * ch, gpt * p)


def _block_diag_out(w_gcp):
    g, ch, p = w_gcp.shape
    gpt = LANES // ch
    w = w_gcp.reshape(g // gpt, gpt, ch, p)
    eye = jnp.eye(gpt, dtype=w.dtype)
    return jnp.einsum("ab,jacp->japbc", eye, w).reshape(g // gpt, gpt * p, gpt * ch)


def _s5_body(u_ref, ab_re_ref, ab_im_ref, wb_re_ref, wb_im_ref, wc_ref, d_ref, wglu_ref, bglu_ref, h0_re_ref, h0_im_ref,
             z_ref, hf_re_ref, hf_im_ref, h_re_ref, h_im_ref, bu_re_ref, bu_im_ref, y_ref):
    step = pl.program_id(1)
    rows = u_ref.shape[0]
    groups = h_re_ref.shape[0] // SUBLANES
    tb = rows // SUBLANES // groups
    n_tiles = wb_re_ref.shape[0]
    sw = wb_re_ref.shape[2]

    @pl.when(step == 0)
    def _():
        h_re_ref[...] = h0_re_ref[...]
        h_im_ref[...] = h0_im_ref[...]

    def input_map(j):
        u_b = u_ref[:, j * LANES:(j + 1) * LANES].astype(BF16)
        bu_re_ref[j] = _dot(u_b, wb_re_ref[j])
        bu_im_ref[j] = _dot(u_b, wb_im_ref[j])

    def output_map(j):
        u_j = u_ref[:, j * LANES:(j + 1) * LANES]
        h_cat = jnp.concatenate([bu_re_ref[j].astype(BF16), bu_im_ref[j].astype(BF16)], axis=1)
        y_ref[:, j * LANES:(j + 1) * LANES] = _dot(h_cat, wc_ref[j]) + d_ref[:, j * LANES:(j + 1) * LANES] * u_j

    input_map(0)
    for j in range(n_tiles):
        if j + 1 < n_tiles:
            input_map(j + 1)
        a_re = jnp.broadcast_to(ab_re_ref[:, j * sw:(j + 1) * sw], (SUBLANES, sw))
        a_im = jnp.broadcast_to(ab_im_ref[:, j * sw:(j + 1) * sw], (SUBLANES, sw))
        for grp in range(groups):
            state_rows = slice(grp * SUBLANES, (grp + 1) * SUBLANES)
            h_re = h_re_ref[state_rows, j * sw:(j + 1) * sw]
            h_im = h_im_ref[state_rows, j * sw:(j + 1) * sw]
            for t in range(grp * tb, (grp + 1) * tb):
                sl = slice(t * SUBLANES, (t + 1) * SUBLANES)
                h_re, h_im = (a_re * h_re - a_im * h_im + bu_re_ref[j, sl, :], a_re * h_im + a_im * h_re + bu_im_ref[j, sl, :])
                bu_re_ref[j, sl, :] = h_re
                bu_im_ref[j, sl, :] = h_im
            h_re_ref[state_rows, j * sw:(j + 1) * sw] = h_re
            h_im_ref[state_rows, j * sw:(j + 1) * sw] = h_im
        output_map(j)

    z = _gelu(y_ref[...])
    gate = _sigmoid(_dot(z.astype(BF16), wglu_ref[...]) + bglu_ref[...])
    z_ref[...] = (z * gate).astype(BF16)

    @pl.when(step == pl.num_programs(1) - 1)
    def _():
        hf_re_ref[...] = h_re_ref[...]
        hf_im_ref[...] = h_im_ref[...]


def _s5(u, ab_re, ab_im, wb_re, wb_im, wc, d_skip, wglu, bglu, h0_re, h0_im, *, tblock):
    total, width = u.shape
    n_state = ab_re.shape[1]
    n_bg = h0_re.shape[0] // SUBLANES
    sw = wb_re.shape[2]
    tb, groups, grid, blk, per_group = _time_grid(total, n_bg, tblock, pack_groups=True)
    rows = groups * tb * SUBLANES
    st_spec = per_group(SUBLANES, n_state)
    consts = (ab_re, ab_im, wb_re, wb_im, wc, d_skip, wglu, bglu)
    st_shape = jax.ShapeDtypeStruct((n_bg * SUBLANES, n_state), F32)
    return pl.pallas_call(
        _s5_body,
        grid=grid,
        in_specs=[blk(width)] + [_resident(p) for p in consts] + [st_spec, st_spec],
        out_specs=(blk(width), st_spec, st_spec),
        out_shape=(jax.ShapeDtypeStruct((total, width), BF16), st_shape, st_shape),
        scratch_shapes=[pltpu.VMEM((groups * SUBLANES, n_state), F32), pltpu.VMEM((groups * SUBLANES, n_state), F32),
                        pltpu.VMEM((wb_re.shape[0], rows, sw), F32), pltpu.VMEM((wb_re.shape[0], rows, sw), F32),
                        pltpu.VMEM((rows, width), F32)],
        compiler_params=_params(2),
        name="s5",
    )(u, *[_resident_array(p) for p in consts], h0_re, h0_im)


def _lru_block(n, xbuf_ref, gate_ref, params, a_ref, b_ref, h, rows):
    cw_ref, cb_ref, wr_ref, br_ref, wi_ref, bi_ref, lam_ref = params
    bw = wr_ref.shape[1]
    cols = slice(n * bw, (n + 1) * bw)
    a_blk, b_blk = a_ref.at[n % 2], b_ref.at[n % 2]
    taps = cw_ref.shape[0]
    groups = h.shape[0] // SUBLANES
    seg = rows // groups
    pitch = seg + (taps - 1) * SUBLANES
    pieces = []
    for grp in range(groups):
        xc = cb_ref[:, cols] + xbuf_ref[grp * pitch:grp * pitch + seg, cols] * cw_ref[0:1, cols]
        for j in range(1, taps):
            xc = xc + xbuf_ref[grp * pitch + j * SUBLANES:grp * pitch + j * SUBLANES + seg, cols] * cw_ref[j:j + 1, cols]
        pieces.append(xc)
    xc = jnp.concatenate(pieces, axis=0) if groups > 1 else pieces[0]
    xb = xc.astype(BF16)
    t_i = jnp.tanh(_dot(xb, wi_ref[n]) + bi_ref[:, cols])
    t_r = jnp.tanh(_dot(xb, wr_ref[n]) + br_ref[:, cols])
    half_rate = (-0.5 * LRU_C) * _softplus(-lam_ref[:, cols])
    log_a = t_r * half_rate + half_rate
    a_blk[...] = jnp.exp(log_a)
    th = jnp.tanh(log_a)
    w = th / (th - 1.0)
    root = jnp.where(w > 0.0, w * lax.rsqrt(w), 0.0)
    scaled = xc * (0.5 * math.sqrt(2.0))
    b_blk[...] = root * (t_i * scaled + scaled)
    finals = []
    for grp in range(groups):
        h_g = h[grp * SUBLANES:(grp + 1) * SUBLANES, :]
        for t in range(grp * seg // SUBLANES, (grp + 1) * seg // SUBLANES):
            sl = slice(t * SUBLANES, (t + 1) * SUBLANES)
            h_g = a_blk[sl, :] * h_g + b_blk[sl, :]
            b_blk[sl, :] = h_g
        finals.append(h_g)
    h = jnp.concatenate(finals, axis=0) if groups > 1 else finals[0]
    return (_gelu(gate_ref[:, cols]) * b_blk[...]).astype(BF16), h


def _lru_body(gate_ref, xin_ref, conv0_ref, h0_ref, cw_ref, cb_ref, wr_ref, br_ref, wi_ref, bi_ref, lam_ref,
              o_ref, hf_ref, convf_ref, xbuf_ref, a_ref, b_ref, h_ref):
    step = pl.program_id(1)
    rows, width = xin_ref.shape
    tail = (cw_ref.shape[0] - 1) * SUBLANES
    n_blocks = wr_ref.shape[0]
    bw = wr_ref.shape[1]
    groups = h_ref.shape[0] // SUBLANES
    seg = rows // groups
    pitch = seg + tail

    @pl.when(step == 0)
    def _():
        for grp in range(groups):
            xbuf_ref[grp * pitch:grp * pitch + tail, :] = conv0_ref[grp * tail:(grp + 1) * tail, :]
        h_ref[...] = h0_ref[...]

    for grp in range(groups):
        xbuf_ref[grp * pitch + tail:(grp + 1) * pitch, :] = xin_ref[grp * seg:(grp + 1) * seg, :]
    params = (cw_ref, cb_ref, wr_ref, br_ref, wi_ref, bi_ref, lam_ref)
    for n in range(n_blocks):
        cols = slice(n * bw, (n + 1) * bw)
        o_ref[:, cols], h_ref[:, cols] = _lru_block(n, xbuf_ref, gate_ref, params, a_ref, b_ref, h_ref[:, cols], rows)
    for grp in range(groups):
        xbuf_ref[grp * pitch:grp * pitch + tail, :] = xbuf_ref[grp * pitch + seg:(grp + 1) * pitch, :]

    @pl.when(step == pl.num_programs(1) - 1)
    def _():
        hf_ref[...] = h_ref[...]
        for grp in range(groups):
            convf_ref[grp * tail:(grp + 1) * tail, :] = xbuf_ref[grp * pitch:grp * pitch + tail, :]


def _lru(gate, xin, conv0, h0, cw, cb, wr, br, wi, bi, lam, *, tblock):
    total, width = xin.shape
    n_bg = h0.shape[0] // SUBLANES
    tail = (cw.shape[0] - 1) * SUBLANES
    bw = _resident_shape(wr)[1]
    tb, groups, grid, blk, per_group = _time_grid(total, n_bg, tblock, pack_groups=True)
    rows = groups * tb * SUBLANES
    h_spec = per_group(SUBLANES, width)
    c_spec = per_group(tail, width)
    consts = (cw, cb, wr, br, wi, bi, lam)
    return pl.pallas_call(
        _lru_body,
        grid=grid,
        in_specs=[blk(width), blk(width), c_spec, h_spec] + [_resident(p) for p in consts],
        out_specs=(blk(width), h_spec, c_spec),
        out_shape=(jax.ShapeDtypeStruct((total, width), BF16), jax.ShapeDtypeStruct((n_bg * SUBLANES, width), F32),
                   jax.ShapeDtypeStruct((n_bg * tail, width), F32)),
        scratch_shapes=[pltpu.VMEM((rows + groups * tail, width), F32), pltpu.VMEM((2, rows, bw), F32),
                        pltpu.VMEM((2, rows, bw), F32), pltpu.VMEM((groups * SUBLANES, width), F32)],
        compiler_params=_params(2),
        name="rg_lru",
    )(gate, xin, conv0, h0, *[_resident_array(p) for p in consts])


def _row(t):
    return t.reshape(1, -1).astype(F32)


def _to_rows(x):
    b, s, d = x.shape
    return jnp.transpose(x.reshape(b // SUBLANES, SUBLANES, s, d), (0, 2, 1, 3)).reshape(b * s, d)


def _from_rows(y, b, s):
    d = y.shape[-1]
    return jnp.transpose(y.reshape(b // SUBLANES, s, SUBLANES, d), (0, 2, 1, 3)).reshape(b, s, d)


def _even_layer(x, nb, gla_states, gla_acc, e, n_even, s_re, s_im, prm, mlp, *, final, batch_major_out):
    (g_norm, w_in, w_u, w_a, wa2, ba2, g_head, ab_re, ab_im, wb_re, wb_im, wc, d_skip, wglu, bglu, w_out, dims) = prm
    nh, dk, dv = dims
    q, k, v, gt, u, la, *reordered = _inproj_even(x, g_norm, w_in, w_u, w_a, wa2, ba2, kw=nh * dk, vw=nh * dv)
    x2d = reordered[0] if reordered else x
    steps = x2d.shape[0] // nb
    og, gla_acc = _gla(q, k, la, v, gt, g_head, gla_states, e, n_even, gla_acc, nb, dims, c=math.gcd(steps, GLA_CHUNK), tblock=128)
    n_state = ab_re.shape[1]
    h0_re = jnp.zeros((nb, n_state), F32) if s_re is None else s_re.astype(F32).reshape(nb, n_state)
    h0_im = jnp.zeros((nb, n_state), F32) if s_im is None else s_im.astype(F32).reshape(nb, n_state)
    zz, hf_re, hf_im = _s5(u, ab_re, ab_im, wb_re, wb_im, wc, d_skip, wglu, bglu, h0_re, h0_im, tblock=128)
    g_mlp, w_up, w_down, g_fin = mlp
    y = _out_mlp(x2d, og, zz, w_out, g_mlp, w_up, w_down, g_fin, final=final, batch_major_out=batch_major_out)
    return y, gla_acc, hf_re, hf_im


def _odd_layer(x2d, nb, s_lru, s_conv, prm, mlp, *, final, batch_major_out):
    (g_norm, w_in, cw, cb, wr, br, wi, bi, lam, w_out) = prm
    n_bg = nb // SUBLANES
    taps = cw.shape[0]
    width = _resident_shape(w_in)[1] // 2
    gate, xin = _inproj_odd(x2d, g_norm, w_in)
    h0 = jnp.zeros((nb, width), F32) if s_lru is None else s_lru.astype(F32)
    if s_conv is None:
        conv0 = jnp.zeros((n_bg * (taps - 1) * SUBLANES, width), F32)
    else:
        conv0 = jnp.transpose(s_conv.astype(F32).reshape(n_bg, SUBLANES, taps - 1, width), (0, 2, 1, 3)).reshape(-1, width)
    gh, hf, convf = _lru(gate, xin, conv0, h0, cw, cb, wr, br, wi, bi, lam, tblock=128)
    g_mlp, w_up, w_down, g_fin = mlp
    y = _out_mlp(x2d, gh, gh, w_out, g_mlp, w_up, w_down, g_fin, final=final, batch_major_out=batch_major_out)
    conv_new = jnp.transpose(convf.reshape(n_bg, taps - 1, SUBLANES, width), (0, 2, 1, 3)).reshape(nb, taps - 1, width)
    return y, hf, conv_new


def kernel(x_prompt, x_sample, state_gla, state_s5_re, state_s5_im, state_lru, state_conv, norm_mix_even, w_in_even, gla_w_a2, gla_b_a2, gla_norm, s5_a_re, s5_a_im, s5_log_dt, s5_b_re, s5_b_im, s5_c_re, s5_c_im, s5_d, s5_w_glu, s5_b_glu, w_out_even, norm_mix_odd, w_in_odd, conv_w, conv_b, lru_w_r, lru_b_r, lru_w_i, lru_b_i, lru_lam, w_out_odd, norm_mlp, w_up, w_down, norm_final):
    depth = norm_mlp.shape[0]
    nh, dk, dv = state_gla.shape[2:]
    rank = gla_w_a2.shape[1]
    n_groups, n_p, ch = s5_b_re.shape[1:]
    kw, vw = nh * dk, nh * dv
    assert rank <= LANES and LANES % ch == 0 and LANES % dk == 0 and dv == LANES
    bp, sp, _ = x_prompt.shape
    bs, ss, _ = x_sample.shape
    n_even = state_gla.shape[0]

    in_kernel_order = bp == SUBLANES and sp % (INPROJ_TILE // SUBLANES) == 0 and depth > 1
    yp = x_prompt.astype(F32) if in_kernel_order else _to_rows(x_prompt.astype(F32))
    ys = _to_rows(x_sample.astype(F32))
    g_fin = _row(norm_final)
    o4 = 2 * kw + 2 * vw
    o5 = o4 + rank
    w_in_main = w_in_even[:, :, :o4].astype(BF16)
    w_in_u = w_in_even[:, :, o5:].astype(BF16)
    w_in_a = jnp.pad(w_in_even[:, :, o4:o5], ((0, 0), (0, 0), (0, LANES - rank))).astype(BF16)
    w_a2 = jnp.pad(gla_w_a2, ((0, 0), (0, LANES - rank), (0, 0))).astype(BF16)
    w_up_f, w_down_f = w_up.astype(F32), w_down.astype(F32)
    w_out_even_b, w_out_odd_b, w_in_odd_b = w_out_even.astype(BF16), w_out_odd.astype(BF16), w_in_odd.astype(BF16)
    w_glu_b, w_r_b, w_i_b = s5_w_glu.astype(BF16), (0.5 * lru_w_r).astype(BF16), (0.5 * lru_w_i).astype(BF16)
    keys = ("re", "im", "lru", "conv")
    outs_p = {key: [] for key in keys}
    outs_s = {key: [] for key in keys}
    gla_p = gla_s = None

    def record(outs, sr, si):
        outs["re"].append(sr.reshape(-1, n_groups, n_p))
        outs["im"].append(si.reshape(-1, n_groups, n_p))

    for layer in range(depth):
        final = layer == depth - 1
        bm_out = (bp, sp) if final and in_kernel_order else None
        mlp = (_row(norm_mlp[layer]), (w_up_f, layer), (w_down_f, layer), g_fin)
        if layer % 2 == 0:
            e = layer // 2
            ab_re, ab_im, bb_re, bb_im = _s5_discretise(s5_a_re[e].astype(F32), s5_a_im[e].astype(F32), s5_log_dt[e].astype(F32),
                                                        s5_b_re[e].astype(F32), s5_b_im[e].astype(F32))
            wc = jnp.concatenate([_block_diag_out(s5_c_re[e].astype(F32)), -_block_diag_out(s5_c_im[e].astype(F32))], axis=1)
            prm = (_row(norm_mix_even[e]), (w_in_main, e), (w_in_u, e), (w_in_a, e), (w_a2, e), _row(gla_b_a2[e]), _row(gla_norm[e]),
                   ab_re.reshape(1, -1), ab_im.reshape(1, -1), _block_diag_in(bb_re).astype(BF16), _block_diag_in(bb_im).astype(BF16),
                   wc.astype(BF16), _row(s5_d[e]), (w_glu_b, e), _row(s5_b_glu[e]), (w_out_even_b, e), (nh, dk, dv))
            (yp, w_up_l, w_down_l), gla_p, sr, si = _even_layer(yp, bp, None, gla_p, e, n_even, None, None, prm, mlp, final=final,
                                                                batch_major_out=bm_out)
            record(outs_p, sr, si)
            mlp = (mlp[0], w_up_l, w_down_l, g_fin)
            ys, gla_s, sr, si = _even_layer(ys, bs, state_gla, gla_s, e, n_even, state_s5_re[e], state_s5_im[e], prm, mlp, final=final,
                                            batch_major_out=None)
            record(outs_s, sr, si)
        else:
            o = layer // 2
            prm = (_row(norm_mix_odd[o]), (w_in_odd_b, o), conv_w[o].astype(F32), _row(conv_b[o]),
                   (w_r_b, o), _row(0.5 * lru_b_r[o]), (w_i_b, o), _row(0.5 * lru_b_i[o]), _row(lru_lam[o]), (w_out_odd_b, o))
            (yp, w_up_l, w_down_l), sl, sc = _odd_layer(yp, bp, None, None, prm, mlp, final=final, batch_major_out=bm_out)
            outs_p["lru"].append(sl)
            outs_p["conv"].append(sc)
            mlp = (mlp[0], w_up_l, w_down_l, g_fin)
            ys, sl, sc = _odd_layer(ys, bs, state_lru[o], state_conv[o], prm, mlp, final=final, batch_major_out=None)
            outs_s["lru"].append(sl)
            outs_s["conv"].append(sc)

    stack = lambda outs: tuple(jnp.stack(outs[key]) for key in keys)
    return (yp if in_kernel_order else _from_rows(yp, bp, sp), _from_rows(ys, bs, ss), gla_p) + stack(outs_p) + (gla_s,) + stack(outs_s)
```

```python
import functools
import math

import jax
import jax.numpy as jnp
from jax import lax
from jax.experimental import pallas as pl
from jax.experimental.pallas import tpu as pltpu

F32 = jnp.float32
BF16 = jnp.bfloat16

NORM_EPS = 1e-6
GLA_TAU = 16.0
GLA_CHUNK = 16
LRU_C = 8.0

SUBLANES = 8
LANES = 128
ROW_TILE = 512
INPROJ_TILE = 1024
FF_CHUNK = 1024
VMEM_LIMIT = 48 * 1024 * 1024


def _params(n_axes):
    return pltpu.CompilerParams(dimension_semantics=("arbitrary",) * n_axes, vmem_limit_bytes=VMEM_LIMIT)


def _const_spec(shape):
    zeros = (0,) * len(shape)
    return pl.BlockSpec(shape, lambda *_: zeros, pipeline_mode=pl.Buffered(1))


def _resident(p):
    if not isinstance(p, tuple):
        return _const_spec(p.shape)
    stacked, layer = p
    index = (layer,) + (0,) * (stacked.ndim - 1)
    return pl.BlockSpec((None,) + stacked.shape[1:], lambda *_: index, pipeline_mode=pl.Buffered(1))


def _resident_array(p):
    return p[0] if isinstance(p, tuple) else p


def _resident_shape(p):
    return p[0].shape[1:] if isinstance(p, tuple) else p.shape


def _dot(a, b):
    return jnp.dot(a, b, preferred_element_type=F32)


def _dot_nt(a, b):
    return lax.dot_general(a, b, (((1,), (1,)), ((), ())), preferred_element_type=F32)


def _dot_tn(a, b):
    return lax.dot_general(a, b, (((0,), (0,)), ((), ())), preferred_element_type=F32)


def _rmsnorm(x, g):
    return x * lax.rsqrt(jnp.mean(x * x, axis=-1, keepdims=True) + NORM_EPS) * g


def _log_sigmoid(z):
    return jnp.minimum(z, 0.0) - jnp.log1p(jnp.exp(-jnp.abs(z)))


def _softplus(z):
    return jnp.maximum(z, 0.0) + jnp.log1p(jnp.exp(-jnp.abs(z)))


_GELU_C1 = math.sqrt(2.0 / math.pi)
_GELU_C2 = _GELU_C1 * 0.044715


def _gelu(x):
    half = 0.5 * x
    return half + half * jnp.tanh(x * (_GELU_C1 + _GELU_C2 * (x * x)))


def _sigmoid(x):
    return 0.5 * jnp.tanh(0.5 * x) + 0.5


def _rows(ref, start, size):
    return ref[pl.ds(pl.multiple_of(start, size), size), :]


def _interleave_rows(x_ref, slab_ref):
    nb, ts, d = x_ref.shape
    for b in range(nb):
        for l in range(d // LANES):
            slab_ref[l, pl.ds(b, ts, stride=nb), :] = x_ref[b, :, l * LANES:(l + 1) * LANES]
    return jnp.concatenate([slab_ref[l] for l in range(d // LANES)], axis=1)


def _deinterleave_rows(y, slab_ref, o_ref):
    nb, ts, d = o_ref.shape
    for l in range(d // LANES):
        slab_ref[l] = y[:, l * LANES:(l + 1) * LANES]
    for b in range(nb):
        for l in range(d // LANES):
            o_ref[b, :, l * LANES:(l + 1) * LANES] = slab_ref[l, pl.ds(b, ts, stride=nb), :]


def _row_tiles(rows, tile=ROW_TILE):
    tm = tile if rows % tile == 0 else min(ROW_TILE, rows)
    return tm, (rows // tm,), lambda width, col=0: pl.BlockSpec((tm, width), lambda i: (i, col))


def _inproj_even_body(x_ref, g_ref, w_ref, wu_ref, wa_ref, wa2_ref, ba2_ref, q_ref, k_ref, v_ref, gt_ref, u_ref, la_ref, *rest,
                      kw, vw):
    x = _interleave_rows(x_ref, rest[0]) if rest else x_ref[...]
    xn = _rmsnorm(x, g_ref[...]).astype(BF16)
    o1, o2 = kw, 2 * kw
    o3 = o2 + vw
    o4 = o3 + vw
    a_lr = _dot(xn, wa_ref[...]).astype(BF16)
    z = _dot(a_lr, wa2_ref[...]) + ba2_ref[...]
    la_ref[...] = _log_sigmoid(z) * (1.0 / GLA_TAU)
    q_ref[...] = _dot(xn, w_ref[:, 0:o1])
    k_ref[...] = _dot(xn, w_ref[:, o1:o2])
    v_ref[...] = _dot(xn, w_ref[:, o2:o3]).astype(BF16)
    gt_ref[...] = _dot(xn, w_ref[:, o3:o4])
    u_ref[...] = _dot(xn, wu_ref[...])


def _inproj_even(x, g, w, wu, wa, wa2, ba2, *, kw, vw):
    batch_major = x.ndim == 3
    d = x.shape[-1]
    rows = x.size // d
    sw = _resident_shape(wu)[1]
    widths = [kw, kw, vw, vw, sw, kw]
    dtypes = [F32, F32, BF16, F32, F32, F32]
    consts = (g, w, wu, wa, wa2, ba2)
    tm, grid, blk = _row_tiles(rows, INPROJ_TILE)
    x_spec, scratch = blk(d), []
    if batch_major:
        nb = x.shape[0]
        x_spec = pl.BlockSpec((nb, tm // nb, d), lambda i: (0, i, 0))
        scratch = [pltpu.VMEM((d // LANES, tm, LANES), F32)]
    return pl.pallas_call(
        functools.partial(_inproj_even_body, kw=kw, vw=vw),
        grid=grid,
        in_specs=[x_spec] + [_resident(p) for p in consts],
        out_specs=tuple(blk(wd) for wd in widths),
        out_shape=tuple(jax.ShapeDtypeStruct((rows, wd), dt) for wd, dt in zip(widths, dtypes)),
        scratch_shapes=scratch,
        compiler_params=_params(1),
        name="inproj_even",
    )(x, *[_resident_array(p) for p in consts])


def _inproj_odd_body(x_ref, g_ref, w_ref, gate_ref, xin_ref, *, width):
    xn = _rmsnorm(x_ref[...], g_ref[...]).astype(BF16)
    gate_ref[...] = _dot(xn, w_ref[:, 0:width])
    xin_ref[...] = _dot(xn, w_ref[:, width:2 * width])


def _inproj_odd(x2d, g, w):
    rows, d = x2d.shape
    width = _resident_shape(w)[1] // 2
    _, grid, blk = _row_tiles(rows, INPROJ_TILE)
    shape = jax.ShapeDtypeStruct((rows, width), F32)
    return pl.pallas_call(
        functools.partial(_inproj_odd_body, width=width),
        grid=grid,
        in_specs=[blk(d), _resident(g), _resident(w)],
        out_specs=(blk(width), blk(width)),
        out_shape=(shape, shape),
        compiler_params=_params(1),
        name="inproj_odd",
    )(x2d, _resident_array(g), _resident_array(w))


def _out_mlp_cast_body(x_ref, a1_ref, a2_ref, wo_ref, g_ref, gf_ref, wu_hbm, wd_hbm, o_ref, wu_out, wd_out,
                       wu_bf, wd_bf, stage, sem_in, sem_out, *slab, final, layer, batch_major_in):
    i = pl.program_id(0)
    d, d_ff = wu_bf.shape
    n_up = d_ff // FF_CHUNK
    n_chunks = 2 * n_up
    x_slab, slab = (slab[0], slab[1:]) if batch_major_in else (None, slab)

    def fetch(k):
        c = k // 2
        if k % 2 == 0:
            src = wu_hbm.at[layer, :, pl.ds(c * FF_CHUNK, FF_CHUNK)]
        else:
            src = wd_hbm.at[layer, pl.ds(c * FF_CHUNK, FF_CHUNK), :]
        return pltpu.make_async_copy(src, stage.at[k % 2], sem_in.at[k % 2])

    def publish():
        return (pltpu.make_async_copy(wu_bf, wu_out, sem_out.at[0]), pltpu.make_async_copy(wd_bf, wd_out, sem_out.at[1]))

    def ready(k):
        fetch(k).wait()
        w = stage[k % 2].astype(BF16)
        c = k // 2
        if k % 2 == 0:
            wu_bf[:, c * FF_CHUNK:(c + 1) * FF_CHUNK] = w
        else:
            wd_bf[c * FF_CHUNK:(c + 1) * FF_CHUNK, :] = w
        if k + 2 < n_chunks:
            fetch(k + 2).start()

    @pl.when(i == 0)
    def _():
        fetch(0).start()
        fetch(1).start()
        _out_mlp_body(x_ref, a1_ref, a2_ref, wo_ref, g_ref, wu_bf, wd_bf, gf_ref, o_ref, *slab, final=final, before_chunk=ready,
                      x_slab=x_slab)
        for copy in publish():
            copy.start()

    @pl.when(i > 0)
    def _():
        _out_mlp_body(x_ref, a1_ref, a2_ref, wo_ref, g_ref, wu_bf, wd_bf, gf_ref, o_ref, *slab, final=final, x_slab=x_slab)

    @pl.when(i == pl.num_programs(0) - 1)
    def _():
        for copy in publish():
            copy.wait()


def _out_mlp_stream_body(x_ref, a1_ref, a2_ref, wo_ref, g_ref, wu_ref, wd_ref, gf_ref, o_ref, xn_ref, acc_ref, *, final):
    c = pl.program_id(0)
    half = a1_ref.shape[1]

    @pl.when(c == 0)
    def _():
        mix = _dot(a1_ref[...], wo_ref[0:half, :]) + _dot(a2_ref[...], wo_ref[half:2 * half, :])
        x1 = x_ref[...] + mix
        xn_ref[...] = _rmsnorm(x1, g_ref[...]).astype(BF16)
        acc_ref[...] = x1

    h = jnp.square(jnp.maximum(_dot(xn_ref[...], wu_ref[...]), 0.0)).astype(BF16)
    acc_ref[...] += _dot(h, wd_ref[...])

    @pl.when(c == pl.num_programs(0) - 1)
    def _():
        o_ref[...] = _rmsnorm(acc_ref[...], gf_ref[...]) if final else acc_ref[...]


def _out_mlp_body(x_ref, a1_ref, a2_ref, wo_ref, g_ref, wu_ref, wd_ref, gf_ref, o_ref, *slab, final, before_chunk=None,
                  x_slab=None):
    half = a1_ref.shape[1]
    mix = _dot(a1_ref[...], wo_ref[0:half, :]) + _dot(a2_ref[...], wo_ref[half:2 * half, :])
    x1 = (x_ref[...] if x_slab is None else _interleave_rows(x_ref, x_slab)) + mix
    xn = _rmsnorm(x1, g_ref[...]).astype(BF16)
    acc = x1
    for c in range(wu_ref.shape[1] // FF_CHUNK):
        if before_chunk is not None:
            before_chunk(2 * c)
        h = _dot(xn, wu_ref[:, c * FF_CHUNK:(c + 1) * FF_CHUNK])
        h = jnp.square(jnp.maximum(h, 0.0)).astype(BF16)
        if before_chunk is not None:
            before_chunk(2 * c + 1)
        acc = acc + _dot(h, wd_ref[c * FF_CHUNK:(c + 1) * FF_CHUNK, :])
    if final:
        acc = _rmsnorm(acc, gf_ref[...])
    if slab:
        _deinterleave_rows(acc, slab[0], o_ref)
    else:
        o_ref[...] = acc


def _out_mlp(x2d, a1, a2, wo, g, wu, wd, gf, *, final, batch_major_out=None):
    batch_major_in = x2d.ndim == 3
    d = x2d.shape[-1]
    rows = x2d.size // d
    half = d // 2
    consts = (wo, g, wu, wd, gf)
    tm, grid, blk = _row_tiles(rows)
    out_spec, out_shape, scratch = blk(d), jax.ShapeDtypeStruct((rows, d), F32), []
    if batch_major_out is not None:
        nb, s = batch_major_out
        out_spec = pl.BlockSpec((nb, tm // nb, d), lambda i: (0, i, 0))
        out_shape = jax.ShapeDtypeStruct((nb, s, d), F32)
        scratch = [pltpu.VMEM((d // LANES, tm, LANES), F32)]
    if _resident_array(wu).dtype == F32:
        (w_up, layer), (w_down, _) = wu, wd
        d_ff = w_up.shape[2]
        assert d == FF_CHUNK and d_ff % FF_CHUNK == 0
        any_spec = pl.BlockSpec(memory_space=pl.ANY)
        row_consts = (wo, g, gf)
        x_spec = blk(d)
        if batch_major_in:
            x_spec = pl.BlockSpec((x2d.shape[0], tm // x2d.shape[0], d), lambda i: (0, i, 0))
            scratch = [pltpu.VMEM((d // LANES, tm, LANES), F32)] + scratch
        return pl.pallas_call(
            functools.partial(_out_mlp_cast_body, final=final, layer=layer, batch_major_in=batch_major_in),
            grid=grid,
            in_specs=[x_spec, blk(half, 0), blk(half, a1.shape[1] // half - 1)] + [_resident(p) for p in row_consts] + [any_spec, any_spec],
            out_specs=(out_spec, any_spec, any_spec),
            out_shape=(out_shape, jax.ShapeDtypeStruct((d, d_ff), BF16), jax.ShapeDtypeStruct((d_ff, d), BF16)),
            scratch_shapes=[pltpu.VMEM((d, d_ff), BF16), pltpu.VMEM((d_ff, d), BF16), pltpu.VMEM((2, FF_CHUNK, FF_CHUNK), F32),
                            pltpu.SemaphoreType.DMA((2,)), pltpu.SemaphoreType.DMA((2,))] + scratch,
            compiler_params=_params(1),
            name="out_mlp_cast",
        )(x2d, a1, a2, *[_resident_array(p) for p in row_consts], w_up, w_down)
    assert not batch_major_in
    if grid == (1,) and batch_major_out is None and not isinstance(wu, tuple) and wu.shape[1] > FF_CHUNK:
        d_ff = wu.shape[1]
        once = lambda shape: pl.BlockSpec(shape, lambda c: (0,) * len(shape))
        return pl.pallas_call(
            functools.partial(_out_mlp_stream_body, final=final),
            grid=(d_ff // FF_CHUNK,),
            in_specs=[once((tm, d)), once((tm, half)), pl.BlockSpec((tm, half), lambda c: (0, a1.shape[1] // half - 1)),
                      _resident(wo), _resident(g), pl.BlockSpec((d, FF_CHUNK), lambda c: (0, c)),
                      pl.BlockSpec((FF_CHUNK, d), lambda c: (c, 0)), _resident(gf)],
            out_specs=once((tm, d)),
            out_shape=out_shape,
            scratch_shapes=[pltpu.VMEM((tm, d), BF16), pltpu.VMEM((tm, d), F32)],
            compiler_params=_params(1),
            name="out_mlp_stream",
        )(x2d, a1, a2, _resident_array(wo), _resident_array(g), wu, wd, _resident_array(gf))
    return pl.pallas_call(
        functools.partial(_out_mlp_body, final=final),
        grid=grid,
        in_specs=[blk(d), blk(half, 0), blk(half, a1.shape[1] // half - 1)] + [_resident(p) for p in consts],
        out_specs=out_spec,
        out_shape=out_shape,
        scratch_shapes=scratch,
        compiler_params=_params(1),
        name="out_mlp",
    )(x2d, a1, a2, *[_resident_array(p) for p in consts])


def _time_grid(total_rows, n_bg, tblock, pack_groups=False):
    steps = total_rows // (n_bg * SUBLANES)
    tb = min(tblock, steps)
    n_t = steps // tb
    groups = 1
    if pack_groups and n_t == 1:
        groups = max(1, min(n_bg, ROW_TILE // (steps * SUBLANES)))
        while n_bg % groups:
            groups -= 1
    blk = lambda width: pl.BlockSpec((groups * tb * SUBLANES, width), lambda bg, t: (bg * n_t + t, 0))
    per_group = lambda rows, width: pl.BlockSpec((groups * rows, width), lambda bg, t: (bg, 0))
    return tb, groups, (n_bg // groups, n_t), blk, per_group


def _gla_body(q_ref, k_ref, la_ref, v_ref, gt_ref, gh_ref, s0_ref, *rest, c, nh, dk, dv, transposed):
    o_ref, sfin_ref, st_ref = rest[-3:]
    tb = pl.program_id(1)
    rows = SUBLANES * c
    n_chunks = q_ref.shape[0] // rows
    dk_shift = int(math.log2(dk))
    heads_per_tile = LANES // dk

    @pl.when(tb == 0)
    def _():
        st_ref[...] = s0_ref[...]

    def iota(shape, axis):
        return lax.broadcasted_iota(jnp.int32, shape, axis)

    causal = ((iota((rows, rows), 0) & 7) == (iota((rows, rows), 1) & 7)) & (iota((rows, rows), 1) <= iota((rows, rows), 0))
    emask = (iota((rows, SUBLANES * dk), 0) & 7) == lax.shift_right_logical(iota((rows, SUBLANES * dk), 1), dk_shift)
    dmask = iota((SUBLANES, SUBLANES * dk), 0) == lax.shift_right_logical(iota((SUBLANES, SUBLANES * dk), 1), dk_shift)
    lo_rows = iota((rows, LANES), 1) < dk
    lo_8 = iota((SUBLANES, LANES), 1) < dk

    def expand(tile, h, lo=lo_rows, mask=emask):
        rolled = pltpu.roll(tile, dk, axis=1)
        d = jnp.where(lo, tile, rolled) if h % heads_per_tile == 0 else jnp.where(lo, rolled, tile)
        return jnp.where(mask, jnp.concatenate([d] * (SUBLANES * dk // LANES), axis=1), 0.0)

    def prepare(n):
        sl = slice(n * rows, (n + 1) * rows)
        q, k, la, v, gt = q_ref[sl, :], k_ref[sl, :], la_ref[sl, :], v_ref[sl, :], gt_ref[sl, :]
        acc = jnp.zeros((SUBLANES, nh * dk), F32)
        pieces = []
        for t in range(c):
            acc = acc + la[t * SUBLANES:(t + 1) * SUBLANES, :]
            pieces.append(acc)
        cum = jnp.concatenate(pieces, axis=0)
        last = jnp.concatenate([acc] * c, axis=0)
        q_t = q * jnp.exp(cum) * (dk ** -0.5)
        k_t = k * jnp.exp(-cum)
        k_end = k * jnp.exp(last - cum)
        dec = jnp.exp(acc)
        return sl, q_t, k_t, k_end, dec, v, gt

    def advance(chunk):
        sl, q_t, k_t, k_end, dec, v, gt = chunk
        pad = jnp.zeros((LANES - SUBLANES, LANES), F32)
        outs = []
        for h in range(nh):
            tile = (h // heads_per_tile) * LANES
            lane0 = (h % heads_per_tile) * dk
            sel = lo_rows if lane0 == 0 else jnp.logical_not(lo_rows)
            q_tile = q_t[:, tile:tile + LANES]
            scores = _dot_nt(jnp.where(sel, q_tile, 0.0).astype(BF16), k_t[:, tile:tile + LANES].astype(BF16))
            scores = jnp.where(causal, scores, 0.0).astype(BF16)
            v_h = v[:, h * dv:(h + 1) * dv]
            q_exp = expand(q_tile, h).astype(BF16)
            k_exp = expand(k_end[:, tile:tile + LANES], h).astype(BF16)
            if transposed:
                st = st_ref[h]
                o = _dot(scores, v_h) + _dot_nt(q_exp, st.astype(BF16))
                dec_row = jnp.sum(expand(dec[:, tile:tile + LANES], h, lo_8, dmask), axis=0, keepdims=True)
                st_ref[h] = st * dec_row + _dot_tn(v_h, k_exp)
            else:
                st = jnp.concatenate([st_ref[b, h] for b in range(SUBLANES)], axis=0)
                o = _dot(scores, v_h) + _dot(q_exp, st.astype(BF16))
                upd = _dot_tn(k_exp, v_h)
                dec_t = jnp.concatenate([dec[:, tile:tile + LANES], pad], axis=0).T
                for b in range(SUBLANES):
                    dec_b = jnp.broadcast_to(dec_t[lane0:lane0 + dk, b:b + 1], (dk, dv))
                    st_ref[b, h] = st[b * dk:(b + 1) * dk, :] * dec_b + upd[b * dk:(b + 1) * dk, :]
            o = _rmsnorm(o, gh_ref[:, h * dv:(h + 1) * dv])
            g_h = gt[:, h * dv:(h + 1) * dv]
            outs.append(o * (g_h * _sigmoid(g_h)))
        o_ref[sl, :] = jnp.concatenate(outs, axis=1).astype(BF16)

    chunk = prepare(0)
    for n in range(n_chunks):
        following = prepare(n + 1) if n + 1 < n_chunks else None
        advance(chunk)
        chunk = following

    @pl.when(tb == pl.num_programs(1) - 1)
    def _():
        sfin_ref[...] = st_ref[...]


def _gla_state_in(s):
    b, h, dk, dv = s.shape
    return jnp.transpose(s.reshape(b // SUBLANES, SUBLANES, h, dk, dv), (0, 2, 4, 1, 3)).reshape(b // SUBLANES * h, dv, SUBLANES * dk)


def _gla_state_out(st, nh, dk):
    dv = st.shape[1]
    n_bg = st.shape[0] // nh
    return jnp.transpose(st.reshape(n_bg, nh, dv, SUBLANES, dk), (0, 3, 1, 4, 2)).reshape(n_bg * SUBLANES, nh, dk, dv)


def _gla(q, k, la, v, gt, gh, states, layer, n_layers, acc, nb, dims, *, c, tblock):
    nh, dk, dv = dims
    n_bg = nb // SUBLANES
    tb, _, grid, blk, _ = _time_grid(q.shape[0], n_bg, tblock)
    transposed = q.shape[0] // n_bg > SUBLANES * c
    body = functools.partial(_gla_body, c=c, nh=nh, dk=dk, dv=dv, transposed=transposed)
    row_specs = [blk(nh * dk), blk(nh * dk), blk(nh * dk), blk(nh * dv), blk(nh * dv), _const_spec(gh.shape)]
    og_shape = jax.ShapeDtypeStruct((q.shape[0], nh * dv), BF16)
    if transposed:
        st_block = (nh, dv, SUBLANES * dk)
        st_spec = pl.BlockSpec(st_block, lambda bg, t: (bg, 0, 0))
        s_in = jnp.zeros((n_bg * nh, dv, SUBLANES * dk), F32) if states is None else _gla_state_in(states[layer].astype(F32))
        og, s_fin = pl.pallas_call(
            body, grid=grid, in_specs=row_specs + [st_spec], out_specs=(blk(nh * dv), st_spec),
            out_shape=(og_shape, jax.ShapeDtypeStruct(s_in.shape, F32)), scratch_shapes=[pltpu.VMEM(st_block, F32)],
            compiler_params=_params(2), name="gla",
        )(q, k, la, v, gt, gh, s_in)
        s_new = _gla_state_out(s_fin, nh, dk)[None]
        return og, (jnp.pad(s_new, ((0, n_layers - 1),) + ((0, 0),) * 4) if acc is None else lax.dynamic_update_slice(acc, s_new, (layer, 0, 0, 0, 0)))
    st_block = (SUBLANES, nh, dk, dv)
    stacked_spec = pl.BlockSpec((None,) + st_block, lambda bg, t: (layer, bg, 0, 0, 0))
    if states is None:
        s_in, in_spec = jnp.zeros((nb, nh, dk, dv), F32), pl.BlockSpec(st_block, lambda bg, t: (bg, 0, 0, 0))
    else:
        s_in, in_spec = states.astype(F32), stacked_spec
    if acc is None:
        acc = jnp.zeros((n_layers, nb, nh, dk, dv), F32)
    return pl.pallas_call(
        body, grid=grid, in_specs=row_specs + [in_spec, pl.BlockSpec(memory_space=pl.ANY)], out_specs=(blk(nh * dv), stacked_spec),
        out_shape=(og_shape, jax.ShapeDtypeStruct(acc.shape, F32)), scratch_shapes=[pltpu.VMEM(st_block, F32)],
        input_output_aliases={7: 1}, compiler_params=_params(2), name="gla",
    )(q, k, la, v, gt, gh, s_in, acc)


def _s5_disc_body(a_re_ref, a_im_ref, log_dt_ref, b_re_ref, b_im_ref, ab_re_ref, ab_im_ref, bb_re_ref, bb_im_ref):
    a_re = a_re_ref[...]
    a_im = a_im_ref[...]
    dt = jnp.exp(log_dt_ref[...])
    mag = jnp.exp(dt * a_re)
    ab_re = mag * jnp.cos(dt * a_im)
    ab_im = mag * jnp.sin(dt * a_im)
    den = a_re * a_re + a_im * a_im
    coef_re = ((ab_re - 1.0) * a_re + ab_im * a_im) / den
    coef_im = (ab_im * a_re - (ab_re - 1.0) * a_im) / den
    b_re = b_re_ref[...]
    b_im = b_im_ref[...]
    ab_re_ref[...] = ab_re
    ab_im_ref[...] = ab_im
    bb_re_ref[...] = coef_re * b_re - coef_im * b_im
    bb_im_ref[...] = coef_re * b_im + coef_im * b_re


def _s5_discretise(a_re, a_im, log_dt, b_re, b_im):
    g, p, ch = b_re.shape
    rep = lambda t: jnp.repeat(t, ch, axis=0)
    flat = lambda t: jnp.transpose(t, (0, 2, 1)).reshape(g * ch, p)
    args = (rep(a_re), rep(a_im), rep(jnp.broadcast_to(log_dt[:, None], (g, p))), flat(b_re), flat(b_im))
    spec = _const_spec((g * ch, p))
    shape = jax.ShapeDtypeStruct((g * ch, p), F32)
    ab_re, ab_im, bb_re, bb_im = pl.pallas_call(
        _s5_disc_body, grid=(1,), in_specs=[spec] * 5, out_specs=(spec,) * 4, out_shape=(shape,) * 4,
        compiler_params=_params(1), name="s5_discretise",
    )(*args)
    return ab_re[::ch], ab_im[::ch], bb_re.reshape(g, ch, p), bb_im.reshape(g, ch, p)


def _block_diag_in(w_gcp):
    g, ch, p = w_gcp.shape
    gpt = LANES // ch
    w = w_gcp.reshape(g // gpt, gpt, ch, p)
    eye = jnp.eye(gpt, dtype=w.dtype)
    return jnp.einsum("ab,jacp->jacbp", eye, w).reshape(g // gpt, gpt * ch, gpt * p)


def _block_diag_out(w_gcp):
    g, ch, p = w_gcp.shape
    gpt = LANES // ch
    w = w_gcp.reshape(g // gpt, gpt, ch, p)
    eye = jnp.eye(gpt, dtype=w.dtype)
    return jnp.einsum("ab,jacp->japbc", eye, w).reshape(g // gpt, gpt * p, gpt * ch)


def _s5_body(u_ref, ab_re_ref, ab_im_ref, wb_re_ref, wb_im_ref, wc_ref, d_ref, wglu_ref, bglu_ref, h0_re_ref, h0_im_ref,
             z_ref, hf_re_ref, hf_im_ref, h_re_ref, h_im_ref, bu_re_ref, bu_im_ref, y_ref):
    step = pl.program_id(1)
    rows = u_ref.shape[0]
    groups = h_re_ref.shape[0] // SUBLANES
    tb = rows // SUBLANES // groups
    n_tiles = wb_re_ref.shape[0]
    sw = wb_re_ref.shape[2]

    @pl.when(step == 0)
    def _():
        h_re_ref[...] = h0_re_ref[...]
        h_im_ref[...] = h0_im_ref[...]

    def input_map(j):
        u_b = u_ref[:, j * LANES:(j + 1) * LANES].astype(BF16)
        bu_re_ref[j] = _dot(u_b, wb_re_ref[j])
        bu_im_ref[j] = _dot(u_b, wb_im_ref[j])

    def output_map(j):
        u_j = u_ref[:, j * LANES:(j + 1) * LANES]
        h_cat = jnp.concatenate([bu_re_ref[j].astype(BF16), bu_im_ref[j].astype(BF16)], axis=1)
        y_ref[:, j * LANES:(j + 1) * LANES] = _dot(h_cat, wc_ref[j]) + d_ref[:, j * LANES:(j + 1) * LANES] * u_j

    input_map(0)
    for j in range(n_tiles):
        if j + 1 < n_tiles:
            input_map(j + 1)
        a_re = jnp.broadcast_to(ab_re_ref[:, j * sw:(j + 1) * sw], (SUBLANES, sw))
        a_im = jnp.broadcast_to(ab_im_ref[:, j * sw:(j + 1) * sw], (SUBLANES, sw))
        for grp in range(groups):
            state_rows = slice(grp * SUBLANES, (grp + 1) * SUBLANES)
            h_re = h_re_ref[state_rows, j * sw:(j + 1) * sw]
            h_im = h_im_ref[state_rows, j * sw:(j + 1) * sw]
            for t in range(grp * tb, (grp + 1) * tb):
                sl = slice(t * SUBLANES, (t + 1) * SUBLANES)
                h_re, h_im = (a_re * h_re - a_im * h_im + bu_re_ref[j, sl, :], a_re * h_im + a_im * h_re + bu_im_ref[j, sl, :])
                bu_re_ref[j, sl, :] = h_re
                bu_im_ref[j, sl, :] = h_im
            h_re_ref[state_rows, j * sw:(j + 1) * sw] = h_re
            h_im_ref[state_rows, j * sw:(j + 1) * sw] = h_im
        output_map(j)

    z = _gelu(y_ref[...])
    gate = _sigmoid(_dot(z.astype(BF16), wglu_ref[...]) + bglu_ref[...])
    z_ref[...] = (z * gate).astype(BF16)

    @pl.when(step == pl.num_programs(1) - 1)
    def _():
        hf_re_ref[...] = h_re_ref[...]
        hf_im_ref[...] = h_im_ref[...]


def _s5(u, ab_re, ab_im, wb_re, wb_im, wc, d_skip, wglu, bglu, h0_re, h0_im, *, tblock):
    total, width = u.shape
    n_state = ab_re.shape[1]
    n_bg = h0_re.shape[0] // SUBLANES
    sw = wb_re.shape[2]
    tb, groups, grid, blk, per_group = _time_grid(total, n_bg, tblock, pack_groups=True)
    rows = groups * tb * SUBLANES
    st_spec = per_group(SUBLANES, n_state)
    consts = (ab_re, ab_im, wb_re, wb_im, wc, d_skip, wglu, bglu)
    st_shape = jax.ShapeDtypeStruct((n_bg * SUBLANES, n_state), F32)
    return pl.pallas_call(
        _s5_body,
        grid=grid,
        in_specs=[blk(width)] + [_resident(p) for p in consts] + [st_spec, st_spec],
        out_specs=(blk(width), st_spec, st_spec),
        out_shape=(jax.ShapeDtypeStruct((total, width), BF16), st_shape, st_shape),
        scratch_shapes=[pltpu.VMEM((groups * SUBLANES, n_state), F32), pltpu.VMEM((groups * SUBLANES, n_state), F32),
                        pltpu.VMEM((wb_re.shape[0], rows, sw), F32), pltpu.VMEM((wb_re.shape[0], rows, sw), F32),
                        pltpu.VMEM((rows, width), F32)],
        compiler_params=_params(2),
        name="s5",
    )(u, *[_resident_array(p) for p in consts], h0_re, h0_im)


def _lru_block(n, xbuf_ref, gate_ref, params, a_ref, b_ref, h, rows):
    cw_ref, cb_ref, wr_ref, br_ref, wi_ref, bi_ref, lam_ref = params
    bw = wr_ref.shape[1]
    cols = slice(n * bw, (n + 1) * bw)
    a_blk, b_blk = a_ref.at[n % 2], b_ref.at[n % 2]
    taps = cw_ref.shape[0]
    groups = h.shape[0] // SUBLANES
    seg = rows // groups
    pitch = seg + (taps - 1) * SUBLANES
    pieces = []
    for grp in range(groups):
        xc = cb_ref[:, cols] + xbuf_ref[grp * pitch:grp * pitch + seg, cols] * cw_ref[0:1, cols]
        for j in range(1, taps):
            xc = xc + xbuf_ref[grp * pitch + j * SUBLANES:grp * pitch + j * SUBLANES + seg, cols] * cw_ref[j:j + 1, cols]
        pieces.append(xc)
    xc = jnp.concatenate(pieces, axis=0) if groups > 1 else pieces[0]
    xb = xc.astype(BF16)
    t_i = jnp.tanh(_dot(xb, wi_ref[n]) + bi_ref[:, cols])
    t_r = jnp.tanh(_dot(xb, wr_ref[n]) + br_ref[:, cols])
    half_rate = (-0.5 * LRU_C) * _softplus(-lam_ref[:, cols])
    log_a = t_r * half_rate + half_rate
    a_blk[...] = jnp.exp(log_a)
    th = jnp.tanh(log_a)
    w = th / (th - 1.0)
    root = jnp.where(w > 0.0, w * lax.rsqrt(w), 0.0)
    scaled = xc * (0.5 * math.sqrt(2.0))
    b_blk[...] = root * (t_i * scaled + scaled)
    finals = []
    for grp in range(groups):
        h_g = h[grp * SUBLANES:(grp + 1) * SUBLANES, :]
        for t in range(grp * seg // SUBLANES, (grp + 1) * seg // SUBLANES):
            sl = slice(t * SUBLANES, (t + 1) * SUBLANES)
            h_g = a_blk[sl, :] * h_g + b_blk[sl, :]
            b_blk[sl, :] = h_g
        finals.append(h_g)
    h = jnp.concatenate(finals, axis=0) if groups > 1 else finals[0]
    return (_gelu(gate_ref[:, cols]) * b_blk[...]).astype(BF16), h


def _lru_body(gate_ref, xin_ref, conv0_ref, h0_ref, cw_ref, cb_ref, wr_ref, br_ref, wi_ref, bi_ref, lam_ref,
              o_ref, hf_ref, convf_ref, xbuf_ref, a_ref, b_ref, h_ref):
    step = pl.program_id(1)
    rows, width = xin_ref.shape
    tail = (cw_ref.shape[0] - 1) * SUBLANES
    n_blocks = wr_ref.shape[0]
    bw = wr_ref.shape[1]
    groups = h_ref.shape[0] // SUBLANES
    seg = rows // groups
    pitch = seg + tail

    @pl.when(step == 0)
    def _():
        for grp in range(groups):
            xbuf_ref[grp * pitch:grp * pitch + tail, :] = conv0_ref[grp * tail:(grp + 1) * tail, :]
        h_ref[...] = h0_ref[...]

    for grp in range(groups):
        xbuf_ref[grp * pitch + tail:(grp + 1) * pitch, :] = xin_ref[grp * seg:(grp + 1) * seg, :]
    params = (cw_ref, cb_ref, wr_ref, br_ref, wi_ref, bi_ref, lam_ref)
    for n in range(n_blocks):
        cols = slice(n * bw, (n + 1) * bw)
        o_ref[:, cols], h_ref[:, cols] = _lru_block(n, xbuf_ref, gate_ref, params, a_ref, b_ref, h_ref[:, cols], rows)
    for grp in range(groups):
        xbuf_ref[grp * pitch:grp * pitch + tail, :] = xbuf_ref[grp * pitch + seg:(grp + 1) * pitch, :]

    @pl.when(step == pl.num_programs(1) - 1)
    def _():
        hf_ref[...] = h_ref[...]
        for grp in range(groups):
            convf_ref[grp * tail:(grp + 1) * tail, :] = xbuf_ref[grp * pitch:grp * pitch + tail, :]


def _lru(gate, xin, conv0, h0, cw, cb, wr, br, wi, bi, lam, *, tblock):
    total, width = xin.shape
    n_bg = h0.shape[0] // SUBLANES
    tail = (cw.shape[0] - 1) * SUBLANES
    bw = _resident_shape(wr)[1]
    tb, groups, grid, blk, per_group = _time_grid(total, n_bg, tblock, pack_groups=True)
    rows = groups * tb * SUBLANES
    h_spec = per_group(SUBLANES, width)
    c_spec = per_group(tail, width)
    consts = (cw, cb, wr, br, wi, bi, lam)
    return pl.pallas_call(
        _lru_body,
        grid=grid,
        in_specs=[blk(width), blk(width), c_spec, h_spec] + [_resident(p) for p in consts],
        out_specs=(blk(width), h_spec, c_spec),
        out_shape=(jax.ShapeDtypeStruct((total, width), BF16), jax.ShapeDtypeStruct((n_bg * SUBLANES, width), F32),
                   jax.ShapeDtypeStruct((n_bg * tail, width), F32)),
        scratch_shapes=[pltpu.VMEM((rows + groups * tail, width), F32), pltpu.VMEM((2, rows, bw), F32),
                        pltpu.VMEM((2, rows, bw), F32), pltpu.VMEM((groups * SUBLANES, width), F32)],
        compiler_params=_params(2),
        name="rg_lru",
    )(gate, xin, conv0, h0, *[_resident_array(p) for p in consts])


def _row(t):
    return t.reshape(1, -1).astype(F32)


def _to_rows(x):
    b, s, d = x.shape
    return jnp.transpose(x.reshape(b // SUBLANES, SUBLANES, s, d), (0, 2, 1, 3)).reshape(b * s, d)


def _from_rows(y, b, s):
    d = y.shape[-1]
    return jnp.transpose(y.reshape(b // SUBLANES, s, SUBLANES, d), (0, 2, 1, 3)).reshape(b, s, d)


def _even_layer(x, nb, gla_states, gla_acc, e, n_even, s_re, s_im, prm, mlp, *, final, batch_major_out):
    (g_norm, w_in, w_u, w_a, wa2, ba2, g_head, ab_re, ab_im, wb_re, wb_im, wc, d_skip, wglu, bglu, w_out, dims) = prm
    nh, dk, dv = dims
    q, k, v, gt, u, la = _inproj_even(x, g_norm, w_in, w_u, w_a, wa2, ba2, kw=nh * dk, vw=nh * dv)
    steps = q.shape[0] // nb
    og, gla_acc = _gla(q, k, la, v, gt, g_head, gla_states, e, n_even, gla_acc, nb, dims, c=math.gcd(steps, GLA_CHUNK), tblock=128)
    n_state = ab_re.shape[1]
    h0_re = jnp.zeros((nb, n_state), F32) if s_re is None else s_re.astype(F32).reshape(nb, n_state)
    h0_im = jnp.zeros((nb, n_state), F32) if s_im is None else s_im.astype(F32).reshape(nb, n_state)
    zz, hf_re, hf_im = _s5(u, ab_re, ab_im, wb_re, wb_im, wc, d_skip, wglu, bglu, h0_re, h0_im, tblock=128)
    g_mlp, w_up, w_down, g_fin = mlp
    y = _out_mlp(x, og, zz, w_out, g_mlp, w_up, w_down, g_fin, final=final, batch_major_out=batch_major_out)
    return y, gla_acc, hf_re, hf_im


def _odd_layer(x2d, nb, s_lru, s_conv, prm, mlp, *, final, batch_major_out):
    (g_norm, w_in, cw, cb, wr, br, wi, bi, lam, w_out) = prm
    n_bg = nb // SUBLANES
    taps = cw.shape[0]
    width = _resident_shape(w_in)[1] // 2
    gate, xin = _inproj_odd(x2d, g_norm, w_in)
    h0 = jnp.zeros((nb, width), F32) if s_lru is None else s_lru.astype(F32)
    if s_conv is None:
        conv0 = jnp.zeros((n_bg * (taps - 1) * SUBLANES, width), F32)
    else:
        conv0 = jnp.transpose(s_conv.astype(F32).reshape(n_bg, SUBLANES, taps - 1, width), (0, 2, 1, 3)).reshape(-1, width)
    gh, hf, convf = _lru(gate, xin, conv0, h0, cw, cb, wr, br, wi, bi, lam, tblock=128)
    g_mlp, w_up, w_down, g_fin = mlp
    y = _out_mlp(x2d, gh, gh, w_out, g_mlp, w_up, w_down, g_fin, final=final, batch_major_out=batch_major_out)
    conv_new = jnp.transpose(convf.reshape(n_bg, taps - 1, SUBLANES, width), (0, 2, 1, 3)).reshape(nb, taps - 1, width)
    return y, hf, conv_new


def kernel(x_prompt, x_sample, state_gla, state_s5_re, state_s5_im, state_lru, state_conv, norm_mix_even, w_in_even, gla_w_a2, gla_b_a2, gla_norm, s5_a_re, s5_a_im, s5_log_dt, s5_b_re, s5_b_im, s5_c_re, s5_c_im, s5_d, s5_w_glu, s5_b_glu, w_out_even, norm_mix_odd, w_in_odd, conv_w, conv_b, lru_w_r, lru_b_r, lru_w_i, lru_b_i, lru_lam, w_out_odd, norm_mlp, w_up, w_down, norm_final):
    depth = norm_mlp.shape[0]
    nh, dk, dv = state_gla.shape[2:]
    rank = gla_w_a2.shape[1]
    n_groups, n_p, ch = s5_b_re.shape[1:]
    kw, vw = nh * dk, nh * dv
    assert rank <= LANES and LANES % ch == 0 and LANES % dk == 0 and dv == LANES
    bp, sp, _ = x_prompt.shape
    bs, ss, _ = x_sample.shape
    n_even = state_gla.shape[0]

    in_kernel_order = bp == SUBLANES and sp % (INPROJ_TILE // SUBLANES) == 0 and depth > 1
    yp = x_prompt.astype(F32) if in_kernel_order else _to_rows(x_prompt.astype(F32))
    ys = _to_rows(x_sample.astype(F32))
    g_fin = _row(norm_final)
    o4 = 2 * kw + 2 * vw
    o5 = o4 + rank
    w_in_main = w_in_even[:, :, :o4].astype(BF16)
    w_in_u = w_in_even[:, :, o5:].astype(BF16)
    w_in_a = jnp.pad(w_in_even[:, :, o4:o5], ((0, 0), (0, 0), (0, LANES - rank))).astype(BF16)
    w_a2 = jnp.pad(gla_w_a2, ((0, 0), (0, LANES - rank), (0, 0))).astype(BF16)
    w_up_f, w_down_f = w_up.astype(F32), w_down.astype(F32)
    w_out_even_b, w_out_odd_b, w_in_odd_b = w_out_even.astype(BF16), w_out_odd.astype(BF16), w_in_odd.astype(BF16)
    w_glu_b, w_r_b, w_i_b = s5_w_glu.astype(BF16), (0.5 * lru_w_r).astype(BF16), (0.5 * lru_w_i).astype(BF16)
    keys = ("re", "im", "lru", "conv")
    outs_p = {key: [] for key in keys}
    outs_s = {key: [] for key in keys}
    gla_p = gla_s = None

    def record(outs, sr, si):
        outs["re"].append(sr.reshape(-1, n_groups, n_p))
        outs["im"].append(si.reshape(-1, n_groups, n_p))

    for layer in range(depth):
        final = layer == depth - 1
        bm_out = (bp, sp) if final and in_kernel_order else None
        mlp = (_row(norm_mlp[layer]), (w_up_f, layer), (w_down_f, layer), g_fin)
        if layer % 2 == 0:
            e = layer // 2
            ab_re, ab_im, bb_re, bb_im = _s5_discretise(s5_a_re[e].astype(F32), s5_a_im[e].astype(F32), s5_log_dt[e].astype(F32),
                                                        s5_b_re[e].astype(F32), s5_b_im[e].astype(F32))
            wc = jnp.concatenate([_block_diag_out(s5_c_re[e].astype(F32)), -_block_diag_out(s5_c_im[e].astype(F32))], axis=1)
            prm = (_row(norm_mix_even[e]), (w_in_main, e), (w_in_u, e), (w_in_a, e), (w_a2, e), _row(gla_b_a2[e]), _row(gla_norm[e]),
                   ab_re.reshape(1, -1), ab_im.reshape(1, -1), _block_diag_in(bb_re).astype(BF16), _block_diag_in(bb_im).astype(BF16),
                   wc.astype(BF16), _row(s5_d[e]), (w_glu_b, e), _row(s5_b_glu[e]), (w_out_even_b, e), (nh, dk, dv))
            (yp, w_up_l, w_down_l), gla_p, sr, si = _even_layer(yp, bp, None, gla_p, e, n_even, None, None, prm, mlp, final=final,
                                                                batch_major_out=bm_out)
            record(outs_p, sr, si)
            mlp = (mlp[0], w_up_l, w_down_l, g_fin)
            ys, gla_s, sr, si = _even_layer(ys, bs, state_gla, gla_s, e, n_even, state_s5_re[e], state_s5_im[e], prm, mlp, final=final,
                                            batch_major_out=None)
            record(outs_s, sr, si)
        else:
            o = layer // 2
            prm = (_row(norm_mix_odd[o]), (w_in_odd_b, o), conv_w[o].astype(F32), _row(conv_b[o]),
                   (w_r_b, o), _row(0.5 * lru_b_r[o]), (w_i_b, o), _row(0.5 * lru_b_i[o]), _row(lru_lam[o]), (w_out_odd_b, o))
            (yp, w_up_l, w_down_l), sl, sc = _odd_layer(yp, bp, None, None, prm, mlp, final=final, batch_major_out=bm_out)
            outs_p["lru"].append(sl)
            outs_p["conv"].append(sc)
            mlp = (mlp[0], w_up_l, w_down_l, g_fin)
            ys, sl, sc = _odd_layer(ys, bs, state_lru[o], state_conv[o], prm, mlp, final=final, batch_major_out=None)
            outs_s["lru"].append(sl)
            outs_s["conv"].append(sc)

    stack = lambda outs: tuple(jnp.stack(outs[key]) for key in keys)
    return (yp if in_kernel_order else _from_rows(yp, bp, sp), _from_rows(ys, bs, ss), gla_p) + stack(outs_p) + (gla_s,) + stack(outs_s)
```

```python
import functools
import math

import jax
import jax.numpy as jnp
from jax import lax
from jax.experimental import pallas as pl
from jax.experimental.pallas import tpu as pltpu

F32 = jnp.float32
BF16 = jnp.bfloat16

NORM_EPS = 1e-6
GLA_TAU = 16.0
GLA_CHUNK = 16
LRU_C = 8.0

SUBLANES = 8
LANES = 128
ROW_TILE = 512
INPROJ_TILE = 1024
FF_CHUNK = 1024
VMEM_LIMIT = 48 * 1024 * 1024


def _params(n_axes):
    return pltpu.CompilerParams(dimension_semantics=("arbitrary",) * n_axes, vmem_limit_bytes=VMEM_LIMIT)


def _const_spec(shape):
    zeros = (0,) * len(shape)
    return pl.BlockSpec(shape, lambda *_: zeros, pipeline_mode=pl.Buffered(1))


def _resident(p):
    if not isinstance(p, tuple):
        return _const_spec(p.shape)
    stacked, layer = p
    index = (layer,) + (0,) * (stacked.ndim - 1)
    return pl.BlockSpec((None,) + stacked.shape[1:], lambda *_: index, pipeline_mode=pl.Buffered(1))


def _resident_array(p):
    return p[0] if isinstance(p, tuple) else p


def _resident_shape(p):
    return p[0].shape[1:] if isinstance(p, tuple) else p.shape


def _dot(a, b):
    return jnp.dot(a, b, preferred_element_type=F32)


def _dot_nt(a, b):
    return lax.dot_general(a, b, (((1,), (1,)), ((), ())), preferred_element_type=F32)


def _dot_tn(a, b):
    return lax.dot_general(a, b, (((0,), (0,)), ((), ())), preferred_element_type=F32)


def _rmsnorm(x, g):
    return x * lax.rsqrt(jnp.mean(x * x, axis=-1, keepdims=True) + NORM_EPS) * g


def _log_sigmoid(z):
    return jnp.minimum(z, 0.0) - jnp.log1p(jnp.exp(-jnp.abs(z)))


def _softplus(z):
    return jnp.maximum(z, 0.0) + jnp.log1p(jnp.exp(-jnp.abs(z)))


_GELU_C1 = math.sqrt(2.0 / math.pi)
_GELU_C2 = _GELU_C1 * 0.044715


def _gelu(x):
    half = 0.5 * x
    return half + half * jnp.tanh(x * (_GELU_C1 + _GELU_C2 * (x * x)))


def _sigmoid(x):
    return 0.5 * jnp.tanh(0.5 * x) + 0.5


def _rows(ref, start, size):
    return ref[pl.ds(pl.multiple_of(start, size), size), :]


def _interleave_rows(x_ref, slab_ref):
    nb, ts, d = x_ref.shape
    for b in range(nb):
        for l in range(d // LANES):
            slab_ref[l, pl.ds(b, ts, stride=nb), :] = x_ref[b, :, l * LANES:(l + 1) * LANES]
    return jnp.concatenate([slab_ref[l] for l in range(d // LANES)], axis=1)


def _deinterleave_rows(y, slab_ref, o_ref):
    nb, ts, d = o_ref.shape
    for l in range(d // LANES):
        slab_ref[l] = y[:, l * LANES:(l + 1) * LANES]
    for b in range(nb):
        for l in range(d // LANES):
            o_ref[b, :, l * LANES:(l + 1) * LANES] = slab_ref[l, pl.ds(b, ts, stride=nb), :]


def _row_tiles(rows, tile=ROW_TILE):
    tm = tile if rows % tile == 0 else min(ROW_TILE, rows)
    return tm, (rows // tm,), lambda width, col=0: pl.BlockSpec((tm, width), lambda i: (i, col))


def _inproj_even_body(x_ref, g_ref, w_ref, wu_ref, wa_ref, wa2_ref, ba2_ref, q_ref, k_ref, v_ref, gt_ref, u_ref, la_ref, *rest,
                      kw, vw):
    x = _interleave_rows(x_ref, rest[0]) if rest else x_ref[...]
    xn = _rmsnorm(x, g_ref[...]).astype(BF16)
    o1, o2 = kw, 2 * kw
    o3 = o2 + vw
    o4 = o3 + vw
    a_lr = _dot(xn, wa_ref[...]).astype(BF16)
    z = _dot(a_lr, wa2_ref[...]) + ba2_ref[...]
    la_ref[...] = _log_sigmoid(z) * (1.0 / GLA_TAU)
    q_ref[...] = _dot(xn, w_ref[:, 0:o1])
    k_ref[...] = _dot(xn, w_ref[:, o1:o2])
    v_ref[...] = _dot(xn, w_ref[:, o2:o3]).astype(BF16)
    gt_ref[...] = _dot(xn, w_ref[:, o3:o4])
    u_ref[...] = _dot(xn, wu_ref[...])


def _inproj_even(x, g, w, wu, wa, wa2, ba2, *, kw, vw):
    batch_major = x.ndim == 3
    d = x.shape[-1]
    rows = x.size // d
    sw = _resident_shape(wu)[1]
    widths = [kw, kw, vw, vw, sw, kw]
    dtypes = [F32, F32, BF16, F32, F32, F32]
    consts = (g, w, wu, wa, wa2, ba2)
    tm, grid, blk = _row_tiles(rows, INPROJ_TILE)
    x_spec, scratch = blk(d), []
    if batch_major:
        nb = x.shape[0]
        x_spec = pl.BlockSpec((nb, tm // nb, d), lambda i: (0, i, 0))
        scratch = [pltpu.VMEM((d // LANES, tm, LANES), F32)]
    return pl.pallas_call(
        functools.partial(_inproj_even_body, kw=kw, vw=vw),
        grid=grid,
        in_specs=[x_spec] + [_resident(p) for p in consts],
        out_specs=tuple(blk(wd) for wd in widths),
        out_shape=tuple(jax.ShapeDtypeStruct((rows, wd), dt) for wd, dt in zip(widths, dtypes)),
        scratch_shapes=scratch,
        compiler_params=_params(1),
        name="inproj_even",
    )(x, *[_resident_array(p) for p in consts])


def _inproj_odd_body(x_ref, g_ref, w_ref, gate_ref, xin_ref, *, width):
    xn = _rmsnorm(x_ref[...], g_ref[...]).astype(BF16)
    gate_ref[...] = _dot(xn, w_ref[:, 0:width])
    xin_ref[...] = _dot(xn, w_ref[:, width:2 * width])


def _inproj_odd(x2d, g, w):
    rows, d = x2d.shape
    width = _resident_shape(w)[1] // 2
    _, grid, blk = _row_tiles(rows, INPROJ_TILE)
    shape = jax.ShapeDtypeStruct((rows, width), F32)
    return pl.pallas_call(
        functools.partial(_inproj_odd_body, width=width),
        grid=grid,
        in_specs=[blk(d), _resident(g), _resident(w)],
        out_specs=(blk(width), blk(width)),
        out_shape=(shape, shape),
        compiler_params=_params(1),
        name="inproj_odd",
    )(x2d, _resident_array(g), _resident_array(w))


def _out_mlp_cast_body(x_ref, a1_ref, a2_ref, wo_ref, g_ref, gf_ref, wu_hbm, wd_hbm, o_ref, wu_out, wd_out,
                       wu_bf, wd_bf, stage, sem_in, sem_out, *slab, final, layer, batch_major_in):
    i = pl.program_id(0)
    d, d_ff = wu_bf.shape
    n_up = d_ff // FF_CHUNK
    n_chunks = 2 * n_up
    x_slab, slab = (slab[0], slab[1:]) if batch_major_in else (None, slab)

    def fetch(k):
        c = k // 2
        if k % 2 == 0:
            src = wu_hbm.at[layer, :, pl.ds(c * FF_CHUNK, FF_CHUNK)]
        else:
            src = wd_hbm.at[layer, pl.ds(c * FF_CHUNK, FF_CHUNK), :]
        return pltpu.make_async_copy(src, stage.at[k % 2], sem_in.at[k % 2])

    def publish():
        return (pltpu.make_async_copy(wu_bf, wu_out, sem_out.at[0]), pltpu.make_async_copy(wd_bf, wd_out, sem_out.at[1]))

    def ready(k):
        fetch(k).wait()
        w = stage[k % 2].astype(BF16)
        c = k // 2
        if k % 2 == 0:
            wu_bf[:, c * FF_CHUNK:(c + 1) * FF_CHUNK] = w
        else:
            wd_bf[c * FF_CHUNK:(c + 1) * FF_CHUNK, :] = w
        if k + 2 < n_chunks:
            fetch(k + 2).start()

    @pl.when(i == 0)
    def _():
        fetch(0).start()
        fetch(1).start()
        _out_mlp_body(x_ref, a1_ref, a2_ref, wo_ref, g_ref, wu_bf, wd_bf, gf_ref, o_ref, *slab, final=final, before_chunk=ready,
                      x_slab=x_slab)
        for copy in publish():
            copy.start()

    @pl.when(i > 0)
    def _():
        _out_mlp_body(x_ref, a1_ref, a2_ref, wo_ref, g_ref, wu_bf, wd_bf, gf_ref, o_ref, *slab, final=final, x_slab=x_slab)

    @pl.when(i == pl.num_programs(0) - 1)
    def _():
        for copy in publish():
            copy.wait()


def _out_mlp_stream_body(x_ref, a1_ref, a2_ref, wo_ref, g_ref, wu_ref, wd_ref, gf_ref, o_ref, xn_ref, acc_ref, *, final):
    c = pl.program_id(0)
    half = a1_ref.shape[1]

    @pl.when(c == 0)
    def _():
        mix = _dot(a1_ref[...], wo_ref[0:half, :]) + _dot(a2_ref[...], wo_ref[half:2 * half, :])
        x1 = x_ref[...] + mix
        xn_ref[...] = _rmsnorm(x1, g_ref[...]).astype(BF16)
        acc_ref[...] = x1

    h = jnp.square(jnp.maximum(_dot(xn_ref[...], wu_ref[...]), 0.0)).astype(BF16)
    acc_ref[...] += _dot(h, wd_ref[...])

    @pl.when(c == pl.num_programs(0) - 1)
    def _():
        o_ref[...] = _rmsnorm(acc_ref[...], gf_ref[...]) if final else acc_ref[...]


def _out_mlp_body(x_ref, a1_ref, a2_ref, wo_ref, g_ref, wu_ref, wd_ref, gf_ref, o_ref, *slab, final, before_chunk=None,
                  x_slab=None):
    half = a1_ref.shape[1]
    mix = _dot(a1_ref[...], wo_ref[0:half, :]) + _dot(a2_ref[...], wo_ref[half:2 * half, :])
    x1 = (x_ref[...] if x_slab is None else _interleave_rows(x_ref, x_slab)) + mix
    xn = _rmsnorm(x1, g_ref[...]).astype(BF16)
    acc = x1
    for c in range(wu_ref.shape[1] // FF_CHUNK):
        if before_chunk is not None:
            before_chunk(2 * c)
        h = _dot(xn, wu_ref[:, c * FF_CHUNK:(c + 1) * FF_CHUNK])
        h = jnp.square(jnp.maximum(h, 0.0)).astype(BF16)
        if before_chunk is not None:
            before_chunk(2 * c + 1)
        acc = acc + _dot(h, wd_ref[c * FF_CHUNK:(c + 1) * FF_CHUNK, :])
    if final:
        acc = _rmsnorm(acc, gf_ref[...])
    if slab:
        _deinterleave_rows(acc, slab[0], o_ref)
    else:
        o_ref[...] = acc


def _out_mlp(x2d, a1, a2, wo, g, wu, wd, gf, *, final, batch_major_out=None):
    batch_major_in = x2d.ndim == 3
    d = x2d.shape[-1]
    rows = x2d.size // d
    half = d // 2
    consts = (wo, g, wu, wd, gf)
    tm, grid, blk = _row_tiles(rows)
    out_spec, out_shape, scratch = blk(d), jax.ShapeDtypeStruct((rows, d), F32), []
    if batch_major_out is not None:
        nb, s = batch_major_out
        out_spec = pl.BlockSpec((nb, tm // nb, d), lambda i: (0, i, 0))
        out_shape = jax.ShapeDtypeStruct((nb, s, d), F32)
        scratch = [pltpu.VMEM((d // LANES, tm, LANES), F32)]
    if _resident_array(wu).dtype == F32:
        (w_up, layer), (w_down, _) = wu, wd
        d_ff = w_up.shape[2]
        assert d == FF_CHUNK and d_ff % FF_CHUNK == 0
        any_spec = pl.BlockSpec(memory_space=pl.ANY)
        row_consts = (wo, g, gf)
        x_spec = blk(d)
        if batch_major_in:
            x_spec = pl.BlockSpec((x2d.shape[0], tm // x2d.shape[0], d), lambda i: (0, i, 0))
            scratch = [pltpu.VMEM((d // LANES, tm, LANES), F32)] + scratch
        return pl.pallas_call(
            functools.partial(_out_mlp_cast_body, final=final, layer=layer, batch_major_in=batch_major_in),
            grid=grid,
            in_specs=[x_spec, blk(half, 0), blk(half, a1.shape[1] // half - 1)] + [_resident(p) for p in row_consts] + [any_spec, any_spec],
            out_specs=(out_spec, any_spec, any_spec),
            out_shape=(out_shape, jax.ShapeDtypeStruct((d, d_ff), BF16), jax.ShapeDtypeStruct((d_ff, d), BF16)),
            scratch_shapes=[pltpu.VMEM((d, d_ff), BF16), pltpu.VMEM((d_ff, d), BF16), pltpu.VMEM((2, FF_CHUNK, FF_CHUNK), F32),
                            pltpu.SemaphoreType.DMA((2,)), pltpu.SemaphoreType.DMA((2,))] + scratch,
            compiler_params=_params(1),
            name="out_mlp_cast",
        )(x2d, a1, a2, *[_resident_array(p) for p in row_consts], w_up, w_down)
    assert not batch_major_in
    if grid == (1,) and batch_major_out is None and not isinstance(wu, tuple) and wu.shape[1] > FF_CHUNK:
        d_ff = wu.shape[1]
        once = lambda shape: pl.BlockSpec(shape, lambda c: (0,) * len(shape))
        return pl.pallas_call(
            functools.partial(_out_mlp_stream_body, final=final),
            grid=(d_ff // FF_CHUNK,),
            in_specs=[once((tm, d)), once((tm, half)), pl.BlockSpec((tm, half), lambda c: (0, a1.shape[1] // half - 1)),
                      _resident(wo), _resident(g), pl.BlockSpec((d, FF_CHUNK), lambda c: (0, c)),
                      pl.BlockSpec((FF_CHUNK, d), lambda c: (c, 0)), _resident(gf)],
            out_specs=once((tm, d)),
            out_shape=out_shape,
            scratch_shapes=[pltpu.VMEM((tm, d), BF16), pltpu.VMEM((tm, d), F32)],
            compiler_params=_params(1),
            name="out_mlp_stream",
        )(x2d, a1, a2, _resident_array(wo), _resident_array(g), wu, wd, _resident_array(gf))
    return pl.pallas_call(
        functools.partial(_out_mlp_body, final=final),
        grid=grid,
        in_specs=[blk(d), blk(half, 0), blk(half, a1.shape[1] // half - 1)] + [_resident(p) for p in consts],
        out_specs=out_spec,
        out_shape=out_shape,
        scratch_shapes=scratch,
        compiler_params=_params(1),
        name="out_mlp",
    )(x2d, a1, a2, *[_resident_array(p) for p in consts])


def _time_grid(total_rows, n_bg, tblock, pack_groups=False, max_groups=None):
    steps = total_rows // (n_bg * SUBLANES)
    tb = min(tblock, steps)
    n_t = steps // tb
    groups = 1
    if pack_groups and n_t == 1:
        groups = max(1, min(n_bg, ROW_TILE // (steps * SUBLANES), max_groups or n_bg))
        while n_bg % groups:
            groups -= 1
    blk = lambda width: pl.BlockSpec((groups * tb * SUBLANES, width), lambda bg, t: (bg * n_t + t, 0))
    per_group = lambda rows, width: pl.BlockSpec((groups * rows, width), lambda bg, t: (bg, 0))
    return tb, groups, (n_bg // groups, n_t), blk, per_group


def _gla_body(q_ref, k_ref, la_ref, v_ref, gt_ref, gh_ref, s0_ref, *rest, c, nh, dk, dv, transposed):
    o_ref, sfin_ref, st_ref = rest[-3:]
    tb = pl.program_id(1)
    rows = SUBLANES * c
    n_chunks = q_ref.shape[0] // rows
    dk_shift = int(math.log2(dk))
    heads_per_tile = LANES // dk

    @pl.when(tb == 0)
    def _():
        st_ref[...] = s0_ref[...]

    def iota(shape, axis):
        return lax.broadcasted_iota(jnp.int32, shape, axis)

    causal = ((iota((rows, rows), 0) & 7) == (iota((rows, rows), 1) & 7)) & (iota((rows, rows), 1) <= iota((rows, rows), 0))
    emask = (iota((rows, SUBLANES * dk), 0) & 7) == lax.shift_right_logical(iota((rows, SUBLANES * dk), 1), dk_shift)
    dmask = iota((SUBLANES, SUBLANES * dk), 0) == lax.shift_right_logical(iota((SUBLANES, SUBLANES * dk), 1), dk_shift)
    lo_rows = iota((rows, LANES), 1) < dk
    lo_8 = iota((SUBLANES, LANES), 1) < dk

    def expand(tile, h, lo=lo_rows, mask=emask):
        rolled = pltpu.roll(tile, dk, axis=1)
        d = jnp.where(lo, tile, rolled) if h % heads_per_tile == 0 else jnp.where(lo, rolled, tile)
        return jnp.where(mask, jnp.concatenate([d] * (SUBLANES * dk // LANES), axis=1), 0.0)

    def prepare(n):
        sl = slice(n * rows, (n + 1) * rows)
        q, k, la, v, gt = q_ref[sl, :], k_ref[sl, :], la_ref[sl, :], v_ref[sl, :], gt_ref[sl, :]
        acc = jnp.zeros((SUBLANES, nh * dk), F32)
        pieces = []
        for t in range(c):
            acc = acc + la[t * SUBLANES:(t + 1) * SUBLANES, :]
            pieces.append(acc)
        cum = jnp.concatenate(pieces, axis=0)
        last = jnp.concatenate([acc] * c, axis=0)
        q_t = q * jnp.exp(cum) * (dk ** -0.5)
        k_t = k * jnp.exp(-cum)
        k_end = k * jnp.exp(last - cum)
        dec = jnp.exp(acc)
        return n, sl, q_t, k_t, k_end, dec, v, gt

    def advance(chunk):
        n, sl, q_t, k_t, k_end, dec, v, gt = chunk
        base = 0 if transposed else n // (n_chunks // (st_ref.shape[0] // SUBLANES)) * SUBLANES
        pad = jnp.zeros((LANES - SUBLANES, LANES), F32)
        outs = []
        for h in range(nh):
            tile = (h // heads_per_tile) * LANES
            lane0 = (h % heads_per_tile) * dk
            sel = lo_rows if lane0 == 0 else jnp.logical_not(lo_rows)
            q_tile = q_t[:, tile:tile + LANES]
            scores = _dot_nt(jnp.where(sel, q_tile, 0.0).astype(BF16), k_t[:, tile:tile + LANES].astype(BF16))
            scores = jnp.where(causal, scores, 0.0).astype(BF16)
            v_h = v[:, h * dv:(h + 1) * dv]
            q_exp = expand(q_tile, h).astype(BF16)
            k_exp = expand(k_end[:, tile:tile + LANES], h).astype(BF16)
            if transposed:
                st = st_ref[h]
                o = _dot(scores, v_h) + _dot_nt(q_exp, st.astype(BF16))
                dec_row = jnp.sum(expand(dec[:, tile:tile + LANES], h, lo_8, dmask), axis=0, keepdims=True)
                st_ref[h] = st * dec_row + _dot_tn(v_h, k_exp)
            else:
                st = jnp.concatenate([st_ref[base + b, h] for b in range(SUBLANES)], axis=0)
                o = _dot(scores, v_h) + _dot(q_exp, st.astype(BF16))
                upd = _dot_tn(k_exp, v_h)
                dec_t = jnp.concatenate([dec[:, tile:tile + LANES], pad], axis=0).T
                for b in range(SUBLANES):
                    dec_b = jnp.broadcast_to(dec_t[lane0:lane0 + dk, b:b + 1], (dk, dv))
                    st_ref[base + b, h] = st[b * dk:(b + 1) * dk, :] * dec_b + upd[b * dk:(b + 1) * dk, :]
            o = _rmsnorm(o, gh_ref[:, h * dv:(h + 1) * dv])
            g_h = gt[:, h * dv:(h + 1) * dv]
            outs.append(o * (g_h * _sigmoid(g_h)))
        o_ref[sl, :] = jnp.concatenate(outs, axis=1).astype(BF16)

    chunk = prepare(0)
    for n in range(n_chunks):
        following = prepare(n + 1) if n + 1 < n_chunks else None
        advance(chunk)
        chunk = following

    @pl.when(tb == pl.num_programs(1) - 1)
    def _():
        sfin_ref[...] = st_ref[...]


def _gla_state_in(s):
    b, h, dk, dv = s.shape
    return jnp.transpose(s.reshape(b // SUBLANES, SUBLANES, h, dk, dv), (0, 2, 4, 1, 3)).reshape(b // SUBLANES * h, dv, SUBLANES * dk)


def _gla_state_out(st, nh, dk):
    dv = st.shape[1]
    n_bg = st.shape[0] // nh
    return jnp.transpose(st.reshape(n_bg, nh, dv, SUBLANES, dk), (0, 3, 1, 4, 2)).reshape(n_bg * SUBLANES, nh, dk, dv)


def _gla(q, k, la, v, gt, gh, states, layer, n_layers, acc, nb, dims, *, c, tblock):
    nh, dk, dv = dims
    n_bg = nb // SUBLANES
    transposed = q.shape[0] // n_bg > SUBLANES * c
    state_bytes = SUBLANES * nh * dk * dv * 4
    tb, groups, grid, blk, _ = _time_grid(q.shape[0], n_bg, tblock, pack_groups=not transposed,
                                          max_groups=max(1, VMEM_LIMIT // (12 * state_bytes)))
    body = functools.partial(_gla_body, c=c, nh=nh, dk=dk, dv=dv, transposed=transposed)
    row_specs = [blk(nh * dk), blk(nh * dk), blk(nh * dk), blk(nh * dv), blk(nh * dv), _const_spec(gh.shape)]
    og_shape = jax.ShapeDtypeStruct((q.shape[0], nh * dv), BF16)
    if transposed:
        st_block = (nh, dv, SUBLANES * dk)
        st_spec = pl.BlockSpec(st_block, lambda bg, t: (bg, 0, 0))
        s_in = jnp.zeros((n_bg * nh, dv, SUBLANES * dk), F32) if states is None else _gla_state_in(states[layer].astype(F32))
        og, s_fin = pl.pallas_call(
            body, grid=grid, in_specs=row_specs + [st_spec], out_specs=(blk(nh * dv), st_spec),
            out_shape=(og_shape, jax.ShapeDtypeStruct(s_in.shape, F32)), scratch_shapes=[pltpu.VMEM(st_block, F32)],
            compiler_params=_params(2), name="gla",
        )(q, k, la, v, gt, gh, s_in)
        s_new = _gla_state_out(s_fin, nh, dk)[None]
        return og, (jnp.pad(s_new, ((0, n_layers - 1),) + ((0, 0),) * 4) if acc is None else lax.dynamic_update_slice(acc, s_new, (layer, 0, 0, 0, 0)))
    st_block = (groups * SUBLANES, nh, dk, dv)
    stacked_spec = pl.BlockSpec((None,) + st_block, lambda bg, t: (layer, bg, 0, 0, 0))
    if states is None:
        s_in, in_spec = jnp.zeros((nb, nh, dk, dv), F32), pl.BlockSpec(st_block, lambda bg, t: (bg, 0, 0, 0))
    else:
        s_in, in_spec = states.astype(F32), stacked_spec
    if acc is None:
        acc = jnp.zeros((n_layers, nb, nh, dk, dv), F32)
    return pl.pallas_call(
        body, grid=grid, in_specs=row_specs + [in_spec, pl.BlockSpec(memory_space=pl.ANY)], out_specs=(blk(nh * dv), stacked_spec),
        out_shape=(og_shape, jax.ShapeDtypeStruct(acc.shape, F32)), scratch_shapes=[pltpu.VMEM(st_block, F32)],
        input_output_aliases={7: 1}, compiler_params=_params(2), name="gla",
    )(q, k, la, v, gt, gh, s_in, acc)


def _s5_disc_body(a_re_ref, a_im_ref, log_dt_ref, b_re_ref, b_im_ref, ab_re_ref, ab_im_ref, bb_re_ref, bb_im_ref):
    a_re = a_re_ref[...]
    a_im = a_im_ref[...]
    dt = jnp.exp(log_dt_ref[...])
    mag = jnp.exp(dt * a_re)
    ab_re = mag * jnp.cos(dt * a_im)
    ab_im = mag * jnp.sin(dt * a_im)
    den = a_re * a_re + a_im * a_im
    coef_re = ((ab_re - 1.0) * a_re + ab_im * a_im) / den
    coef_im = (ab_im * a_re - (ab_re - 1.0) * a_im) / den
    b_re = b_re_ref[...]
    b_im = b_im_ref[...]
    ab_re_ref[...] = ab_re
    ab_im_ref[...] = ab_im
    bb_re_ref[...] = coef_re * b_re - coef_im * b_im
    bb_im_ref[...] = coef_re * b_im + coef_im * b_re


def _s5_discretise(a_re, a_im, log_dt, b_re, b_im):
    g, p, ch = b_re.shape
    rep = lambda t: jnp.repeat(t, ch, axis=0)
    flat = lambda t: jnp.transpose(t, (0, 2, 1)).reshape(g * ch, p)
    args = (rep(a_re), rep(a_im), rep(jnp.broadcast_to(log_dt[:, None], (g, p))), flat(b_re), flat(b_im))
    spec = _const_spec((g * ch, p))
    shape = jax.ShapeDtypeStruct((g * ch, p), F32)
    ab_re, ab_im, bb_re, bb_im = pl.pallas_call(
        _s5_disc_body, grid=(1,), in_specs=[spec] * 5, out_specs=(spec,) * 4, out_shape=(shape,) * 4,
        compiler_params=_params(1), name="s5_discretise",
    )(*args)
    return ab_re[::ch], ab_im[::ch], bb_re.reshape(g, ch, p), bb_im.reshape(g, ch, p)


def _block_diag_in(w_gcp):
    g, ch, p = w_gcp.shape
    gpt = LANES // ch
    w = w_gcp.reshape(g // gpt, gpt, ch, p)
    eye = jnp.eye(gpt, dtype=w.dtype)
    return jnp.einsum("ab,jacp->jacbp", eye, w).reshape(g // gpt, gpt * ch, gpt * p)


def _block_diag_out(w_gcp):
    g, ch, p = w_gcp.shape
    gpt = LANES // ch
    w = w_gcp.reshape(g // gpt, gpt, ch, p)
    eye = jnp.eye(gpt, dtype=w.dtype)
    return jnp.einsum("ab,jacp->japbc", eye, w).reshape(g // gpt, gpt * p, gpt * ch)


def _s5_body(u_ref, ab_re_ref, ab_im_ref, wb_re_ref, wb_im_ref, wc_ref, d_ref, wglu_ref, bglu_ref, h0_re_ref, h0_im_ref,
             z_ref, hf_re_ref, hf_im_ref, h_re_ref, h_im_ref, bu_re_ref, bu_im_ref, y_ref):
    step = pl.program_id(1)
    rows = u_ref.shape[0]
    groups = h_re_ref.shape[0] // SUBLANES
    tb = rows // SUBLANES // groups
    n_tiles = wb_re_ref.shape[0]
    sw = wb_re_ref.shape[2]

    @pl.when(step == 0)
    def _():
        h_re_ref[...] = h0_re_ref[...]
        h_im_ref[...] = h0_im_ref[...]

    def input_map(j):
        u_b = u_ref[:, j * LANES:(j + 1) * LANES].astype(BF16)
        bu_re_ref[j] = _dot(u_b, wb_re_ref[j])
        bu_im_ref[j] = _dot(u_b, wb_im_ref[j])

    def output_map(j):
        u_j = u_ref[:, j * LANES:(j + 1) * LANES]
        h_cat = jnp.concatenate([bu_re_ref[j].astype(BF16), bu_im_ref[j].astype(BF16)], axis=1)
        y_ref[:, j * LANES:(j + 1) * LANES] = _dot(h_cat, wc_ref[j]) + d_ref[:, j * LANES:(j + 1) * LANES] * u_j

    input_map(0)
    for j in range(n_tiles):
        if j + 1 < n_tiles:
            input_map(j + 1)
        a_re = jnp.broadcast_to(ab_re_ref[:, j * sw:(j + 1) * sw], (SUBLANES, sw))
        a_im = jnp.broadcast_to(ab_im_ref[:, j * sw:(j + 1) * sw], (SUBLANES, sw))
        for grp in range(groups):
            state_rows = slice(grp * SUBLANES, (grp + 1) * SUBLANES)
            h_re = h_re_ref[state_rows, j * sw:(j + 1) * sw]
            h_im = h_im_ref[state_rows, j * sw:(j + 1) * sw]
            for t in range(grp * tb, (grp + 1) * tb):
                sl = slice(t * SUBLANES, (t + 1) * SUBLANES)
                h_re, h_im = (a_re * h_re - a_im * h_im + bu_re_ref[j, sl, :], a_re * h_im + a_im * h_re + bu_im_ref[j, sl, :])
                bu_re_ref[j, sl, :] = h_re
                bu_im_ref[j, sl, :] = h_im
            h_re_ref[state_rows, j * sw:(j + 1) * sw] = h_re
            h_im_ref[state_rows, j * sw:(j + 1) * sw] = h_im
        output_map(j)

    z = _gelu(y_ref[...])
    gate = _sigmoid(_dot(z.astype(BF16), wglu_ref[...]) + bglu_ref[...])
    z_ref[...] = (z * gate).astype(BF16)

    @pl.when(step == pl.num_programs(1) - 1)
    def _():
        hf_re_ref[...] = h_re_ref[...]
        hf_im_ref[...] = h_im_ref[...]


def _s5(u, ab_re, ab_im, wb_re, wb_im, wc, d_skip, wglu, bglu, h0_re, h0_im, *, tblock):
    total, width = u.shape
    n_state = ab_re.shape[1]
    n_bg = h0_re.shape[0] // SUBLANES
    sw = wb_re.shape[2]
    tb, groups, grid, blk, per_group = _time_grid(total, n_bg, tblock, pack_groups=True)
    rows = groups * tb * SUBLANES
    st_spec = per_group(SUBLANES, n_state)
    consts = (ab_re, ab_im, wb_re, wb_im, wc, d_skip, wglu, bglu)
    st_shape = jax.ShapeDtypeStruct((n_bg * SUBLANES, n_state), F32)
    return pl.pallas_call(
        _s5_body,
        grid=grid,
        in_specs=[blk(width)] + [_resident(p) for p in consts] + [st_spec, st_spec],
        out_specs=(blk(width), st_spec, st_spec),
        out_shape=(jax.ShapeDtypeStruct((total, width), BF16), st_shape, st_shape),
        scratch_shapes=[pltpu.VMEM((groups * SUBLANES, n_state), F32), pltpu.VMEM((groups * SUBLANES, n_state), F32),
                        pltpu.VMEM((wb_re.shape[0], rows, sw), F32), pltpu.VMEM((wb_re.shape[0], rows, sw), F32),
                        pltpu.VMEM((rows, width), F32)],
        compiler_params=_params(2),
        name="s5",
    )(u, *[_resident_array(p) for p in consts], h0_re, h0_im)


def _lru_block(n, xbuf_ref, gate_ref, params, a_ref, b_ref, h, rows):
    cw_ref, cb_ref, wr_ref, br_ref, wi_ref, bi_ref, lam_ref = params
    bw = wr_ref.shape[1]
    cols = slice(n * bw, (n + 1) * bw)
    a_blk, b_blk = a_ref.at[n % 2], b_ref.at[n % 2]
    taps = cw_ref.shape[0]
    groups = h.shape[0] // SUBLANES
    seg = rows // groups
    pitch = seg + (taps - 1) * SUBLANES
    pieces = []
    for grp in range(groups):
        xc = cb_ref[:, cols] + xbuf_ref[grp * pitch:grp * pitch + seg, cols] * cw_ref[0:1, cols]
        for j in range(1, taps):
            xc = xc + xbuf_ref[grp * pitch + j * SUBLANES:grp * pitch + j * SUBLANES + seg, cols] * cw_ref[j:j + 1, cols]
        pieces.append(xc)
    xc = jnp.concatenate(pieces, axis=0) if groups > 1 else pieces[0]
    xb = xc.astype(BF16)
    t_i = jnp.tanh(_dot(xb, wi_ref[n]) + bi_ref[:, cols])
    t_r = jnp.tanh(_dot(xb, wr_ref[n]) + br_ref[:, cols])
    half_rate = (-0.5 * LRU_C) * _softplus(-lam_ref[:, cols])
    log_a = t_r * half_rate + half_rate
    a_blk[...] = jnp.exp(log_a)
    th = jnp.tanh(log_a)
    w = th / (th - 1.0)
    root = jnp.where(w > 0.0, w * lax.rsqrt(w), 0.0)
    scaled = xc * (0.5 * math.sqrt(2.0))
    b_blk[...] = root * (t_i * scaled + scaled)
    finals = []
    for grp in range(groups):
        h_g = h[grp * SUBLANES:(grp + 1) * SUBLANES, :]
        for t in range(grp * seg // SUBLANES, (grp + 1) * seg // SUBLANES):
            sl = slice(t * SUBLANES, (t + 1) * SUBLANES)
            h_g = a_blk[sl, :] * h_g + b_blk[sl, :]
            b_blk[sl, :] = h_g
        finals.append(h_g)
    h = jnp.concatenate(finals, axis=0) if groups > 1 else finals[0]
    return (_gelu(gate_ref[:, cols]) * b_blk[...]).astype(BF16), h


def _lru_body(gate_ref, xin_ref, conv0_ref, h0_ref, cw_ref, cb_ref, wr_ref, br_ref, wi_ref, bi_ref, lam_ref,
              o_ref, hf_ref, convf_ref, xbuf_ref, a_ref, b_ref, h_ref):
    step = pl.program_id(1)
    rows, width = xin_ref.shape
    tail = (cw_ref.shape[0] - 1) * SUBLANES
    n_blocks = wr_ref.shape[0]
    bw = wr_ref.shape[1]
    groups = h_ref.shape[0] // SUBLANES
    seg = rows // groups
    pitch = seg + tail

    @pl.when(step == 0)
    def _():
        for grp in range(groups):
            xbuf_ref[grp * pitch:grp * pitch + tail, :] = conv0_ref[grp * tail:(grp + 1) * tail, :]
        h_ref[...] = h0_ref[...]

    for grp in range(groups):
        xbuf_ref[grp * pitch + tail:(grp + 1) * pitch, :] = xin_ref[grp * seg:(grp + 1) * seg, :]
    params = (cw_ref, cb_ref, wr_ref, br_ref, wi_ref, bi_ref, lam_ref)
    for n in range(n_blocks):
        cols = slice(n * bw, (n + 1) * bw)
        o_ref[:, cols], h_ref[:, cols] = _lru_block(n, xbuf_ref, gate_ref, params, a_ref, b_ref, h_ref[:, cols], rows)
    for grp in range(groups):
        xbuf_ref[grp * pitch:grp * pitch + tail, :] = xbuf_ref[grp * pitch + seg:(grp + 1) * pitch, :]

    @pl.when(step == pl.num_programs(1) - 1)
    def _():
        hf_ref[...] = h_ref[...]
        for grp in range(groups):
            convf_ref[grp * tail:(grp + 1) * tail, :] = xbuf_ref[grp * pitch:grp * pitch + tail, :]


def _lru(gate, xin, conv0, h0, cw, cb, wr, br, wi, bi, lam, *, tblock):
    total, width = xin.shape
    n_bg = h0.shape[0] // SUBLANES
    tail = (cw.shape[0] - 1) * SUBLANES
    bw = _resident_shape(wr)[1]
    tb, groups, grid, blk, per_group = _time_grid(total, n_bg, tblock, pack_groups=True)
    rows = groups * tb * SUBLANES
    h_spec = per_group(SUBLANES, width)
    c_spec = per_group(tail, width)
    consts = (cw, cb, wr, br, wi, bi, lam)
    return pl.pallas_call(
        _lru_body,
        grid=grid,
        in_specs=[blk(width), blk(width), c_spec, h_spec] + [_resident(p) for p in consts],
        out_specs=(blk(width), h_spec, c_spec),
        out_shape=(jax.ShapeDtypeStruct((total, width), BF16), jax.ShapeDtypeStruct((n_bg * SUBLANES, width), F32),
                   jax.ShapeDtypeStruct((n_bg * tail, width), F32)),
        scratch_shapes=[pltpu.VMEM((rows + groups * tail, width), F32), pltpu.VMEM((2, rows, bw), F32),
                        pltpu.VMEM((2, rows, bw), F32), pltpu.VMEM((groups * SUBLANES, width), F32)],
        compiler_params=_params(2),
        name="rg_lru",
    )(gate, xin, conv0, h0, *[_resident_array(p) for p in consts])


def _row(t):
    return t.reshape(1, -1).astype(F32)


def _to_rows(x):
    b, s, d = x.shape
    return jnp.transpose(x.reshape(b // SUBLANES, SUBLANES, s, d), (0, 2, 1, 3)).reshape(b * s, d)


def _from_rows(y, b, s):
    d = y.shape[-1]
    return jnp.transpose(y.reshape(b // SUBLANES, s, SUBLANES, d), (0, 2, 1, 3)).reshape(b, s, d)


def _even_layer(x, nb, gla_states, gla_acc, e, n_even, s_re, s_im, prm, mlp, *, final, batch_major_out):
    (g_norm, w_in, w_u, w_a, wa2, ba2, g_head, ab_re, ab_im, wb_re, wb_im, wc, d_skip, wglu, bglu, w_out, dims) = prm
    nh, dk, dv = dims
    q, k, v, gt, u, la = _inproj_even(x, g_norm, w_in, w_u, w_a, wa2, ba2, kw=nh * dk, vw=nh * dv)
    steps = q.shape[0] // nb
    og, gla_acc = _gla(q, k, la, v, gt, g_head, gla_states, e, n_even, gla_acc, nb, dims, c=math.gcd(steps, GLA_CHUNK), tblock=128)
    n_state = ab_re.shape[1]
    h0_re = jnp.zeros((nb, n_state), F32) if s_re is None else s_re.astype(F32).reshape(nb, n_state)
    h0_im = jnp.zeros((nb, n_state), F32) if s_im is None else s_im.astype(F32).reshape(nb, n_state)
    zz, hf_re, hf_im = _s5(u, ab_re, ab_im, wb_re, wb_im, wc, d_skip, wglu, bglu, h0_re, h0_im, tblock=128)
    g_mlp, w_up, w_down, g_fin = mlp
    y = _out_mlp(x, og, zz, w_out, g_mlp, w_up, w_down, g_fin, final=final, batch_major_out=batch_major_out)
    return y, gla_acc, hf_re, hf_im


def _odd_layer(x2d, nb, s_lru, s_conv, prm, mlp, *, final, batch_major_out):
    (g_norm, w_in, cw, cb, wr, br, wi, bi, lam, w_out) = prm
    n_bg = nb // SUBLANES
    taps = cw.shape[0]
    width = _resident_shape(w_in)[1] // 2
    gate, xin = _inproj_odd(x2d, g_norm, w_in)
    h0 = jnp.zeros((nb, width), F32) if s_lru is None else s_lru.astype(F32)
    if s_conv is None:
        conv0 = jnp.zeros((n_bg * (taps - 1) * SUBLANES, width), F32)
    else:
        conv0 = jnp.transpose(s_conv.astype(F32).reshape(n_bg, SUBLANES, taps - 1, width), (0, 2, 1, 3)).reshape(-1, width)
    gh, hf, convf = _lru(gate, xin, conv0, h0, cw, cb, wr, br, wi, bi, lam, tblock=128)
    g_mlp, w_up, w_down, g_fin = mlp
    y = _out_mlp(x2d, gh, gh, w_out, g_mlp, w_up, w_down, g_fin, final=final, batch_major_out=batch_major_out)
    conv_new = jnp.transpose(convf.reshape(n_bg, taps - 1, SUBLANES, width), (0, 2, 1, 3)).reshape(nb, taps - 1, width)
    return y, hf, conv_new


def kernel(x_prompt, x_sample, state_gla, state_s5_re, state_s5_im, state_lru, state_conv, norm_mix_even, w_in_even, gla_w_a2, gla_b_a2, gla_norm, s5_a_re, s5_a_im, s5_log_dt, s5_b_re, s5_b_im, s5_c_re, s5_c_im, s5_d, s5_w_glu, s5_b_glu, w_out_even, norm_mix_odd, w_in_odd, conv_w, conv_b, lru_w_r, lru_b_r, lru_w_i, lru_b_i, lru_lam, w_out_odd, norm_mlp, w_up, w_down, norm_final):
    depth = norm_mlp.shape[0]
    nh, dk, dv = state_gla.shape[2:]
    rank = gla_w_a2.shape[1]
    n_groups, n_p, ch = s5_b_re.shape[1:]
    kw, vw = nh * dk, nh * dv
    assert rank <= LANES and LANES % ch == 0 and LANES % dk == 0 and dv == LANES
    bp, sp, _ = x_prompt.shape
    bs, ss, _ = x_sample.shape
    n_even = state_gla.shape[0]

    in_kernel_order = bp == SUBLANES and sp % (INPROJ_TILE // SUBLANES) == 0 and depth > 1
    yp = x_prompt.astype(F32) if in_kernel_order else _to_rows(x_prompt.astype(F32))
    ys = _to_rows(x_sample.astype(F32))
    g_fin = _row(norm_final)
    o4 = 2 * kw + 2 * vw
    o5 = o4 + rank
    w_in_main = w_in_even[:, :, :o4].astype(BF16)
    w_in_u = w_in_even[:, :, o5:].astype(BF16)
    w_in_a = jnp.pad(w_in_even[:, :, o4:o5], ((0, 0), (0, 0), (0, LANES - rank))).astype(BF16)
    w_a2 = jnp.pad(gla_w_a2, ((0, 0), (0, LANES - rank), (0, 0))).astype(BF16)
    w_up_f, w_down_f = w_up.astype(F32), w_down.astype(F32)
    w_out_even_b, w_out_odd_b, w_in_odd_b = w_out_even.astype(BF16), w_out_odd.astype(BF16), w_in_odd.astype(BF16)
    w_glu_b, w_r_b, w_i_b = s5_w_glu.astype(BF16), (0.5 * lru_w_r).astype(BF16), (0.5 * lru_w_i).astype(BF16)
    keys = ("re", "im", "lru", "conv")
    outs_p = {key: [] for key in keys}
    outs_s = {key: [] for key in keys}
    gla_p = gla_s = None

    def record(outs, sr, si):
        outs["re"].append(sr.reshape(-1, n_groups, n_p))
        outs["im"].append(si.reshape(-1, n_groups, n_p))

    for layer in range(depth):
        final = layer == depth - 1
        bm_out = (bp, sp) if final and in_kernel_order else None
        mlp = (_row(norm_mlp[layer]), (w_up_f, layer), (w_down_f, layer), g_fin)
        if layer % 2 == 0:
            e = layer // 2
            ab_re, ab_im, bb_re, bb_im = _s5_discretise(s5_a_re[e].astype(F32), s5_a_im[e].astype(F32), s5_log_dt[e].astype(F32),
                                                        s5_b_re[e].astype(F32), s5_b_im[e].astype(F32))
            wc = jnp.concatenate([_block_diag_out(s5_c_re[e].astype(F32)), -_block_diag_out(s5_c_im[e].astype(F32))], axis=1)
            prm = (_row(norm_mix_even[e]), (w_in_main, e), (w_in_u, e), (w_in_a, e), (w_a2, e), _row(gla_b_a2[e]), _row(gla_norm[e]),
                   ab_re.reshape(1, -1), ab_im.reshape(1, -1), _block_diag_in(bb_re).astype(BF16), _block_diag_in(bb_im).astype(BF16),
                   wc.astype(BF16), _row(s5_d[e]), (w_glu_b, e), _row(s5_b_glu[e]), (w_out_even_b, e), (nh, dk, dv))
            (yp, w_up_l, w_down_l), gla_p, sr, si = _even_layer(yp, bp, None, gla_p, e, n_even, None, None, prm, mlp, final=final,
                                                                batch_major_out=bm_out)
            record(outs_p, sr, si)
            mlp = (mlp[0], w_up_l, w_down_l, g_fin)
            ys, gla_s, sr, si = _even_layer(ys, bs, state_gla, gla_s, e, n_even, state_s5_re[e], state_s5_im[e], prm, mlp, final=final,
                                            batch_major_out=None)
            record(outs_s, sr, si)
        else:
            o = layer // 2
            prm = (_row(norm_mix_odd[o]), (w_in_odd_b, o), conv_w[o].astype(F32), _row(conv_b[o]),
                   (w_r_b, o), _row(0.5 * lru_b_r[o]), (w_i_b, o), _row(0.5 * lru_b_i[o]), _row(lru_lam[o]), (w_out_odd_b, o))
            (yp, w_up_l, w_down_l), sl, sc = _odd_layer(yp, bp, None, None, prm, mlp, final=final, batch_major_out=bm_out)
            outs_p["lru"].append(sl)
            outs_p["conv"].append(sc)
            mlp = (mlp[0], w_up_l, w_down_l, g_fin)
            ys, sl, sc = _odd_layer(ys, bs, state_lru[o], state_conv[o], prm, mlp, final=final, batch_major_out=None)
            outs_s["lru"].append(sl)
            outs_s["conv"].append(sc)

    stack = lambda outs: tuple(jnp.stack(outs[key]) for key in keys)
    return (yp if in_kernel_order else _from_rows(yp, bp, sp), _from_rows(ys, bs, ss), gla_p) + stack(outs_p) + (gla_s,) + stack(outs_s)
```

```python
import functools
import math

import jax
import jax.numpy as jnp
from jax import lax
from jax.experimental import pallas as pl
from jax.experimental.pallas import tpu as pltpu

F32 = jnp.float32
BF16 = jnp.bfloat16

NORM_EPS = 1e-6
GLA_TAU = 16.0
GLA_CHUNK = 16
LRU_C = 8.0

SUBLANES = 8
LANES = 128
ROW_TILE = 512
INPROJ_TILE = 1024
FF_CHUNK = 1024
VMEM_LIMIT = 48 * 1024 * 1024
WEIGHT_DMA_PRIORITY = 1


def _params(n_axes):
    return pltpu.CompilerParams(dimension_semantics=("arbitrary",) * n_axes, vmem_limit_bytes=VMEM_LIMIT)


def _const_spec(shape):
    zeros = (0,) * len(shape)
    return pl.BlockSpec(shape, lambda *_: zeros, pipeline_mode=pl.Buffered(1))


def _resident(p):
    if not isinstance(p, tuple):
        return _const_spec(p.shape)
    stacked, layer = p
    index = (layer,) + (0,) * (stacked.ndim - 1)
    return pl.BlockSpec((None,) + stacked.shape[1:], lambda *_: index, pipeline_mode=pl.Buffered(1))


def _resident_array(p):
    return p[0] if isinstance(p, tuple) else p


def _resident_shape(p):
    return p[0].shape[1:] if isinstance(p, tuple) else p.shape


def _dot(a, b):
    return jnp.dot(a, b, preferred_element_type=F32)


def _dot_nt(a, b):
    return lax.dot_general(a, b, (((1,), (1,)), ((), ())), preferred_element_type=F32)


def _dot_tn(a, b):
    return lax.dot_general(a, b, (((0,), (0,)), ((), ())), preferred_element_type=F32)


def _rmsnorm(x, g):
    return x * lax.rsqrt(jnp.mean(x * x, axis=-1, keepdims=True) + NORM_EPS) * g


def _log_sigmoid(z):
    return jnp.minimum(z, 0.0) - jnp.log1p(jnp.exp(-jnp.abs(z)))


def _softplus(z):
    return jnp.maximum(z, 0.0) + jnp.log1p(jnp.exp(-jnp.abs(z)))


_GELU_C1 = math.sqrt(2.0 / math.pi)
_GELU_C2 = _GELU_C1 * 0.044715


def _gelu(x):
    half = 0.5 * x
    return half + half * jnp.tanh(x * (_GELU_C1 + _GELU_C2 * (x * x)))


def _sigmoid(x):
    return 0.5 * jnp.tanh(0.5 * x) + 0.5


def _rows(ref, start, size):
    return ref[pl.ds(pl.multiple_of(start, size), size), :]


def _interleave_rows(x_ref, slab_ref):
    nb, ts, d = x_ref.shape
    for b in range(nb):
        for l in range(d // LANES):
            slab_ref[l, pl.ds(b, ts, stride=nb), :] = x_ref[b, :, l * LANES:(l + 1) * LANES]
    return jnp.concatenate([slab_ref[l] for l in range(d // LANES)], axis=1)


def _deinterleave_rows(y, slab_ref, o_ref):
    nb, ts, d = o_ref.shape
    for l in range(d // LANES):
        slab_ref[l] = y[:, l * LANES:(l + 1) * LANES]
    for b in range(nb):
        for l in range(d // LANES):
            o_ref[b, :, l * LANES:(l + 1) * LANES] = slab_ref[l, pl.ds(b, ts, stride=nb), :]


def _row_tiles(rows, tile=ROW_TILE):
    tm = tile if rows % tile == 0 else min(ROW_TILE, rows)
    return tm, (rows // tm,), lambda width, col=0: pl.BlockSpec((tm, width), lambda i: (i, col))


def _inproj_even_body(x_ref, g_ref, w_ref, wu_ref, wa_ref, wa2_ref, ba2_ref, q_ref, k_ref, v_ref, gt_ref, u_ref, la_ref, *rest,
                      kw, vw):
    x = _interleave_rows(x_ref, rest[0]) if rest else x_ref[...]
    xn = _rmsnorm(x, g_ref[...]).astype(BF16)
    o1, o2 = kw, 2 * kw
    o3 = o2 + vw
    o4 = o3 + vw
    a_lr = _dot(xn, wa_ref[...]).astype(BF16)
    z = _dot(a_lr, wa2_ref[...]) + ba2_ref[...]
    la_ref[...] = _log_sigmoid(z) * (1.0 / GLA_TAU)
    q_ref[...] = _dot(xn, w_ref[:, 0:o1])
    k_ref[...] = _dot(xn, w_ref[:, o1:o2])
    v_ref[...] = _dot(xn, w_ref[:, o2:o3]).astype(BF16)
    gt_ref[...] = _dot(xn, w_ref[:, o3:o4])
    u_ref[...] = _dot(xn, wu_ref[...])


def _inproj_even(x, g, w, wu, wa, wa2, ba2, *, kw, vw):
    batch_major = x.ndim == 3
    d = x.shape[-1]
    rows = x.size // d
    sw = _resident_shape(wu)[1]
    widths = [kw, kw, vw, vw, sw, kw]
    dtypes = [F32, F32, BF16, F32, F32, F32]
    consts = (g, w, wu, wa, wa2, ba2)
    tm, grid, blk = _row_tiles(rows, INPROJ_TILE)
    x_spec, scratch = blk(d), []
    if batch_major:
        nb = x.shape[0]
        x_spec = pl.BlockSpec((nb, tm // nb, d), lambda i: (0, i, 0))
        scratch = [pltpu.VMEM((d // LANES, tm, LANES), F32)]
    return pl.pallas_call(
        functools.partial(_inproj_even_body, kw=kw, vw=vw),
        grid=grid,
        in_specs=[x_spec] + [_resident(p) for p in consts],
        out_specs=tuple(blk(wd) for wd in widths),
        out_shape=tuple(jax.ShapeDtypeStruct((rows, wd), dt) for wd, dt in zip(widths, dtypes)),
        scratch_shapes=scratch,
        compiler_params=_params(1),
        name="inproj_even",
    )(x, *[_resident_array(p) for p in consts])


def _inproj_odd_body(x_ref, g_ref, w_ref, gate_ref, xin_ref, *, width):
    xn = _rmsnorm(x_ref[...], g_ref[...]).astype(BF16)
    gate_ref[...] = _dot(xn, w_ref[:, 0:width])
    xin_ref[...] = _dot(xn, w_ref[:, width:2 * width])


def _inproj_odd(x2d, g, w):
    rows, d = x2d.shape
    width = _resident_shape(w)[1] // 2
    _, grid, blk = _row_tiles(rows, INPROJ_TILE)
    shape = jax.ShapeDtypeStruct((rows, width), F32)
    return pl.pallas_call(
        functools.partial(_inproj_odd_body, width=width),
        grid=grid,
        in_specs=[blk(d), _resident(g), _resident(w)],
        out_specs=(blk(width), blk(width)),
        out_shape=(shape, shape),
        compiler_params=_params(1),
        name="inproj_odd",
    )(x2d, _resident_array(g), _resident_array(w))


def _out_mlp_cast_body(x_ref, a1_ref, a2_ref, wo_ref, g_ref, gf_ref, wu_hbm, wd_hbm, o_ref, wu_out, wd_out,
                       wu_bf, wd_bf, stage, sem_in, sem_out, *slab, final, layer, batch_major_in):
    i = pl.program_id(0)
    d, d_ff = wu_bf.shape
    n_up = d_ff // FF_CHUNK
    n_chunks = 2 * n_up
    x_slab, slab = (slab[0], slab[1:]) if batch_major_in else (None, slab)

    def fetch(k):
        c = k // 2
        if k % 2 == 0:
            src = wu_hbm.at[layer, :, pl.ds(c * FF_CHUNK, FF_CHUNK)]
        else:
            src = wd_hbm.at[layer, pl.ds(c * FF_CHUNK, FF_CHUNK), :]
        return pltpu.make_async_copy(src, stage.at[k % 2], sem_in.at[k % 2])

    def publish():
        return (pltpu.make_async_copy(wu_bf, wu_out, sem_out.at[0]), pltpu.make_async_copy(wd_bf, wd_out, sem_out.at[1]))

    def ready(k):
        fetch(k).wait()
        w = stage[k % 2].astype(BF16)
        c = k // 2
        if k % 2 == 0:
            wu_bf[:, c * FF_CHUNK:(c + 1) * FF_CHUNK] = w
        else:
            wd_bf[c * FF_CHUNK:(c + 1) * FF_CHUNK, :] = w
        if k + 2 < n_chunks:
            fetch(k + 2).start(priority=WEIGHT_DMA_PRIORITY)

    @pl.when(i == 0)
    def _():
        fetch(0).start(priority=WEIGHT_DMA_PRIORITY)
        fetch(1).start(priority=WEIGHT_DMA_PRIORITY)
        _out_mlp_body(x_ref, a1_ref, a2_ref, wo_ref, g_ref, wu_bf, wd_bf, gf_ref, o_ref, *slab, final=final, before_chunk=ready,
                      x_slab=x_slab)
        for copy in publish():
            copy.start(priority=WEIGHT_DMA_PRIORITY)

    @pl.when(i > 0)
    def _():
        _out_mlp_body(x_ref, a1_ref, a2_ref, wo_ref, g_ref, wu_bf, wd_bf, gf_ref, o_ref, *slab, final=final, x_slab=x_slab)

    @pl.when(i == pl.num_programs(0) - 1)
    def _():
        for copy in publish():
            copy.wait()


def _out_mlp_stream_body(x_ref, a1_ref, a2_ref, wo_ref, g_ref, wu_ref, wd_ref, gf_ref, o_ref, xn_ref, acc_ref, *, final):
    c = pl.program_id(0)
    half = a1_ref.shape[1]

    @pl.when(c == 0)
    def _():
        mix = _dot(a1_ref[...], wo_ref[0:half, :]) + _dot(a2_ref[...], wo_ref[half:2 * half, :])
        x1 = x_ref[...] + mix
        xn_ref[...] = _rmsnorm(x1, g_ref[...]).astype(BF16)
        acc_ref[...] = x1

    h = jnp.square(jnp.maximum(_dot(xn_ref[...], wu_ref[...]), 0.0)).astype(BF16)
    acc_ref[...] += _dot(h, wd_ref[...])

    @pl.when(c == pl.num_programs(0) - 1)
    def _():
        o_ref[...] = _rmsnorm(acc_ref[...], gf_ref[...]) if final else acc_ref[...]


def _out_mlp_body(x_ref, a1_ref, a2_ref, wo_ref, g_ref, wu_ref, wd_ref, gf_ref, o_ref, *slab, final, before_chunk=None,
                  x_slab=None):
    half = a1_ref.shape[1]
    mix = _dot(a1_ref[...], wo_ref[0:half, :]) + _dot(a2_ref[...], wo_ref[half:2 * half, :])
    x1 = (x_ref[...] if x_slab is None else _interleave_rows(x_ref, x_slab)) + mix
    xn = _rmsnorm(x1, g_ref[...]).astype(BF16)
    acc = x1
    for c in range(wu_ref.shape[1] // FF_CHUNK):
        if before_chunk is not None:
            before_chunk(2 * c)
        h = _dot(xn, wu_ref[:, c * FF_CHUNK:(c + 1) * FF_CHUNK])
        h = jnp.square(jnp.maximum(h, 0.0)).astype(BF16)
        if before_chunk is not None:
            before_chunk(2 * c + 1)
        acc = acc + _dot(h, wd_ref[c * FF_CHUNK:(c + 1) * FF_CHUNK, :])
    if final:
        acc = _rmsnorm(acc, gf_ref[...])
    if slab:
        _deinterleave_rows(acc, slab[0], o_ref)
    else:
        o_ref[...] = acc


def _out_mlp(x2d, a1, a2, wo, g, wu, wd, gf, *, final, batch_major_out=None):
    batch_major_in = x2d.ndim == 3
    d = x2d.shape[-1]
    rows = x2d.size // d
    half = d // 2
    consts = (wo, g, wu, wd, gf)
    tm, grid, blk = _row_tiles(rows)
    out_spec, out_shape, scratch = blk(d), jax.ShapeDtypeStruct((rows, d), F32), []
    if batch_major_out is not None:
        nb, s = batch_major_out
        out_spec = pl.BlockSpec((nb, tm // nb, d), lambda i: (0, i, 0))
        out_shape = jax.ShapeDtypeStruct((nb, s, d), F32)
        scratch = [pltpu.VMEM((d // LANES, tm, LANES), F32)]
    if _resident_array(wu).dtype == F32:
        (w_up, layer), (w_down, _) = wu, wd
        d_ff = w_up.shape[2]
        assert d == FF_CHUNK and d_ff % FF_CHUNK == 0
        any_spec = pl.BlockSpec(memory_space=pl.ANY)
        row_consts = (wo, g, gf)
        x_spec = blk(d)
        if batch_major_in:
            x_spec = pl.BlockSpec((x2d.shape[0], tm // x2d.shape[0], d), lambda i: (0, i, 0))
            scratch = [pltpu.VMEM((d // LANES, tm, LANES), F32)] + scratch
        return pl.pallas_call(
            functools.partial(_out_mlp_cast_body, final=final, layer=layer, batch_major_in=batch_major_in),
            grid=grid,
            in_specs=[x_spec, blk(half, 0), blk(half, a1.shape[1] // half - 1)] + [_resident(p) for p in row_consts] + [any_spec, any_spec],
            out_specs=(out_spec, any_spec, any_spec),
            out_shape=(out_shape, jax.ShapeDtypeStruct((d, d_ff), BF16), jax.ShapeDtypeStruct((d_ff, d), BF16)),
            scratch_shapes=[pltpu.VMEM((d, d_ff), BF16), pltpu.VMEM((d_ff, d), BF16), pltpu.VMEM((2, FF_CHUNK, FF_CHUNK), F32),
                            pltpu.SemaphoreType.DMA((2,)), pltpu.SemaphoreType.DMA((2,))] + scratch,
            compiler_params=_params(1),
            name="out_mlp_cast",
        )(x2d, a1, a2, *[_resident_array(p) for p in row_consts], w_up, w_down)
    assert not batch_major_in
    if grid == (1,) and batch_major_out is None and not isinstance(wu, tuple) and wu.shape[1] > FF_CHUNK:
        d_ff = wu.shape[1]
        once = lambda shape: pl.BlockSpec(shape, lambda c: (0,) * len(shape))
        return pl.pallas_call(
            functools.partial(_out_mlp_stream_body, final=final),
            grid=(d_ff // FF_CHUNK,),
            in_specs=[once((tm, d)), once((tm, half)), pl.BlockSpec((tm, half), lambda c: (0, a1.shape[1] // half - 1)),
                      _resident(wo), _resident(g), pl.BlockSpec((d, FF_CHUNK), lambda c: (0, c)),
                      pl.BlockSpec((FF_CHUNK, d), lambda c: (c, 0)), _resident(gf)],
            out_specs=once((tm, d)),
            out_shape=out_shape,
            scratch_shapes=[pltpu.VMEM((tm, d), BF16), pltpu.VMEM((tm, d), F32)],
            compiler_params=_params(1),
            name="out_mlp_stream",
        )(x2d, a1, a2, _resident_array(wo), _resident_array(g), wu, wd, _resident_array(gf))
    return pl.pallas_call(
        functools.partial(_out_mlp_body, final=final),
        grid=grid,
        in_specs=[blk(d), blk(half, 0), blk(half, a1.shape[1] // half - 1)] + [_resident(p) for p in consts],
        out_specs=out_spec,
        out_shape=out_shape,
        scratch_shapes=scratch,
        compiler_params=_params(1),
        name="out_mlp",
    )(x2d, a1, a2, *[_resident_array(p) for p in consts])


def _time_grid(total_rows, n_bg, tblock, pack_groups=False, max_groups=None):
    steps = total_rows // (n_bg * SUBLANES)
    tb = min(tblock, steps)
    n_t = steps // tb
    groups = 1
    if pack_groups and n_t == 1:
        groups = max(1, min(n_bg, ROW_TILE // (steps * SUBLANES), max_groups or n_bg))
        while n_bg % groups:
            groups -= 1
    blk = lambda width: pl.BlockSpec((groups * tb * SUBLANES, width), lambda bg, t: (bg * n_t + t, 0))
    per_group = lambda rows, width: pl.BlockSpec((groups * rows, width), lambda bg, t: (bg, 0))
    return tb, groups, (n_bg // groups, n_t), blk, per_group


def _gla_body(q_ref, k_ref, la_ref, v_ref, gt_ref, gh_ref, s0_ref, *rest, c, nh, dk, dv, transposed):
    o_ref, sfin_ref, st_ref = rest[-3:]
    tb = pl.program_id(1)
    rows = SUBLANES * c
    n_chunks = q_ref.shape[0] // rows
    dk_shift = int(math.log2(dk))
    heads_per_tile = LANES // dk

    @pl.when(tb == 0)
    def _():
        st_ref[...] = s0_ref[...]

    def iota(shape, axis):
        return lax.broadcasted_iota(jnp.int32, shape, axis)

    causal = ((iota((rows, rows), 0) & 7) == (iota((rows, rows), 1) & 7)) & (iota((rows, rows), 1) <= iota((rows, rows), 0))
    emask = (iota((rows, SUBLANES * dk), 0) & 7) == lax.shift_right_logical(iota((rows, SUBLANES * dk), 1), dk_shift)
    dmask = iota((SUBLANES, SUBLANES * dk), 0) == lax.shift_right_logical(iota((SUBLANES, SUBLANES * dk), 1), dk_shift)
    lo_rows = iota((rows, LANES), 1) < dk
    lo_8 = iota((SUBLANES, LANES), 1) < dk

    def expand(tile, h, lo=lo_rows, mask=emask):
        rolled = pltpu.roll(tile, dk, axis=1)
        d = jnp.where(lo, tile, rolled) if h % heads_per_tile == 0 else jnp.where(lo, rolled, tile)
        return jnp.where(mask, jnp.concatenate([d] * (SUBLANES * dk // LANES), axis=1), 0.0)

    def prepare(n):
        sl = slice(n * rows, (n + 1) * rows)
        q, k, la, v, gt = q_ref[sl, :], k_ref[sl, :], la_ref[sl, :], v_ref[sl, :], gt_ref[sl, :]
        acc = jnp.zeros((SUBLANES, nh * dk), F32)
        pieces = []
        for t in range(c):
            acc = acc + la[t * SUBLANES:(t + 1) * SUBLANES, :]
            pieces.append(acc)
        cum = jnp.concatenate(pieces, axis=0)
        last = jnp.concatenate([acc] * c, axis=0)
        q_t = q * jnp.exp(cum) * (dk ** -0.5)
        k_t = k * jnp.exp(-cum)
        k_end = k * jnp.exp(last - cum)
        dec = jnp.exp(acc)
        return n, sl, q_t, k_t, k_end, dec, v, gt

    def advance(chunk):
        n, sl, q_t, k_t, k_end, dec, v, gt = chunk
        base = 0 if transposed else n // (n_chunks // (st_ref.shape[0] // SUBLANES)) * SUBLANES
        pad = jnp.zeros((LANES - SUBLANES, LANES), F32)
        outs = []
        for h in range(nh):
            tile = (h // heads_per_tile) * LANES
            lane0 = (h % heads_per_tile) * dk
            sel = lo_rows if lane0 == 0 else jnp.logical_not(lo_rows)
            q_tile = q_t[:, tile:tile + LANES]
            scores = _dot_nt(jnp.where(sel, q_tile, 0.0).astype(BF16), k_t[:, tile:tile + LANES].astype(BF16))
            scores = jnp.where(causal, scores, 0.0).astype(BF16)
            v_h = v[:, h * dv:(h + 1) * dv]
            q_exp = expand(q_tile, h).astype(BF16)
            k_exp = expand(k_end[:, tile:tile + LANES], h).astype(BF16)
            if transposed:
                st = st_ref[h]
                o = _dot(scores, v_h) + _dot_nt(q_exp, st.astype(BF16))
                dec_row = jnp.sum(expand(dec[:, tile:tile + LANES], h, lo_8, dmask), axis=0, keepdims=True)
                st_ref[h] = st * dec_row + _dot_tn(v_h, k_exp)
            else:
                st = jnp.concatenate([st_ref[base + b, h] for b in range(SUBLANES)], axis=0)
                o = _dot(scores, v_h) + _dot(q_exp, st.astype(BF16))
                upd = _dot_tn(k_exp, v_h)
                dec_t = jnp.concatenate([dec[:, tile:tile + LANES], pad], axis=0).T
                for b in range(SUBLANES):
                    dec_b = jnp.broadcast_to(dec_t[lane0:lane0 + dk, b:b + 1], (dk, dv))
                    st_ref[base + b, h] = st[b * dk:(b + 1) * dk, :] * dec_b + upd[b * dk:(b + 1) * dk, :]
            o = _rmsnorm(o, gh_ref[:, h * dv:(h + 1) * dv])
            g_h = gt[:, h * dv:(h + 1) * dv]
            outs.append(o * (g_h * _sigmoid(g_h)))
        o_ref[sl, :] = jnp.concatenate(outs, axis=1).astype(BF16)

    chunk = prepare(0)
    for n in range(n_chunks):
        following = prepare(n + 1) if n + 1 < n_chunks else None
        advance(chunk)
        chunk = following

    @pl.when(tb == pl.num_programs(1) - 1)
    def _():
        sfin_ref[...] = st_ref[...]


def _gla_state_in(s):
    b, h, dk, dv = s.shape
    return jnp.transpose(s.reshape(b // SUBLANES, SUBLANES, h, dk, dv), (0, 2, 4, 1, 3)).reshape(b // SUBLANES * h, dv, SUBLANES * dk)


def _gla_state_out(st, nh, dk):
    dv = st.shape[1]
    n_bg = st.shape[0] // nh
    return jnp.transpose(st.reshape(n_bg, nh, dv, SUBLANES, dk), (0, 3, 1, 4, 2)).reshape(n_bg * SUBLANES, nh, dk, dv)


def _gla(q, k, la, v, gt, gh, states, layer, n_layers, acc, nb, dims, *, c, tblock):
    nh, dk, dv = dims
    n_bg = nb // SUBLANES
    transposed = q.shape[0] // n_bg > SUBLANES * c
    state_bytes = SUBLANES * nh * dk * dv * 4
    tb, groups, grid, blk, _ = _time_grid(q.shape[0], n_bg, tblock, pack_groups=not transposed,
                                          max_groups=max(1, VMEM_LIMIT // (12 * state_bytes)))
    body = functools.partial(_gla_body, c=c, nh=nh, dk=dk, dv=dv, transposed=transposed)
    row_specs = [blk(nh * dk), blk(nh * dk), blk(nh * dk), blk(nh * dv), blk(nh * dv), _const_spec(gh.shape)]
    og_shape = jax.ShapeDtypeStruct((q.shape[0], nh * dv), BF16)
    if transposed:
        st_block = (nh, dv, SUBLANES * dk)
        st_spec = pl.BlockSpec(st_block, lambda bg, t: (bg, 0, 0))
        s_in = jnp.zeros((n_bg * nh, dv, SUBLANES * dk), F32) if states is None else _gla_state_in(states[layer].astype(F32))
        og, s_fin = pl.pallas_call(
            body, grid=grid, in_specs=row_specs + [st_spec], out_specs=(blk(nh * dv), st_spec),
            out_shape=(og_shape, jax.ShapeDtypeStruct(s_in.shape, F32)), scratch_shapes=[pltpu.VMEM(st_block, F32)],
            compiler_params=_params(2), name="gla",
        )(q, k, la, v, gt, gh, s_in)
        s_new = _gla_state_out(s_fin, nh, dk)[None]
        return og, (jnp.pad(s_new, ((0, n_layers - 1),) + ((0, 0),) * 4) if acc is None else lax.dynamic_update_slice(acc, s_new, (layer, 0, 0, 0, 0)))
    st_block = (groups * SUBLANES, nh, dk, dv)
    stacked_spec = pl.BlockSpec((None,) + st_block, lambda bg, t: (layer, bg, 0, 0, 0))
    if states is None:
        s_in, in_spec = jnp.zeros((nb, nh, dk, dv), F32), pl.BlockSpec(st_block, lambda bg, t: (bg, 0, 0, 0))
    else:
        s_in, in_spec = states.astype(F32), stacked_spec
    if acc is None:
        acc = jnp.zeros((n_layers, nb, nh, dk, dv), F32)
    return pl.pallas_call(
        body, grid=grid, in_specs=row_specs + [in_spec, pl.BlockSpec(memory_space=pl.ANY)], out_specs=(blk(nh * dv), stacked_spec),
        out_shape=(og_shape, jax.ShapeDtypeStruct(acc.shape, F32)), scratch_shapes=[pltpu.VMEM(st_block, F32)],
        input_output_aliases={7: 1}, compiler_params=_params(2), name="gla",
    )(q, k, la, v, gt, gh, s_in, acc)


def _s5_disc_body(a_re_ref, a_im_ref, log_dt_ref, b_re_ref, b_im_ref, ab_re_ref, ab_im_ref, bb_re_ref, bb_im_ref):
    a_re = a_re_ref[...]
    a_im = a_im_ref[...]
    dt = jnp.exp(log_dt_ref[...])
    mag = jnp.exp(dt * a_re)
    ab_re = mag * jnp.cos(dt * a_im)
    ab_im = mag * jnp.sin(dt * a_im)
    den = a_re * a_re + a_im * a_im
    coef_re = ((ab_re - 1.0) * a_re + ab_im * a_im) / den
    coef_im = (ab_im * a_re - (ab_re - 1.0) * a_im) / den
    b_re = b_re_ref[...]
    b_im = b_im_ref[...]
    ab_re_ref[...] = ab_re
    ab_im_ref[...] = ab_im
    bb_re_ref[...] = coef_re * b_re - coef_im * b_im
    bb_im_ref[...] = coef_re * b_im + coef_im * b_re


def _s5_discretise(a_re, a_im, log_dt, b_re, b_im):
    g, p, ch = b_re.shape
    rep = lambda t: jnp.repeat(t, ch, axis=0)
    flat = lambda t: jnp.transpose(t, (0, 2, 1)).reshape(g * ch, p)
    args = (rep(a_re), rep(a_im), rep(jnp.broadcast_to(log_dt[:, None], (g, p))), flat(b_re), flat(b_im))
    spec = _const_spec((g * ch, p))
    shape = jax.ShapeDtypeStruct((g * ch, p), F32)
    ab_re, ab_im, bb_re, bb_im = pl.pallas_call(
        _s5_disc_body, grid=(1,), in_specs=[spec] * 5, out_specs=(spec,) * 4, out_shape=(shape,) * 4,
        compiler_params=_params(1), name="s5_discretise",
    )(*args)
    return ab_re[::ch], ab_im[::ch], bb_re.reshape(g, ch, p), bb_im.reshape(g, ch, p)


def _block_diag_in(w_gcp):
    g, ch, p = w_gcp.shape
    gpt = LANES // ch
    w = w_gcp.reshape(g // gpt, gpt, ch, p)
    eye = jnp.eye(gpt, dtype=w.dtype)
    return jnp.einsum("ab,jacp->jacbp", eye, w).reshape(g // gpt, gpt * ch, gpt * p)


def _block_diag_out(w_gcp):
    g, ch, p = w_gcp.shape
    gpt = LANES // ch
    w = w_gcp.reshape(g // gpt, gpt, ch, p)
    eye = jnp.eye(gpt, dtype=w.dtype)
    return jnp.einsum("ab,jacp->japbc", eye, w).reshape(g // gpt, gpt * p, gpt * ch)


def _s5_body(u_ref, ab_re_ref, ab_im_ref, wb_re_ref, wb_im_ref, wc_ref, d_ref, wglu_ref, bglu_ref, h0_re_ref, h0_im_ref,
             z_ref, hf_re_ref, hf_im_ref, h_re_ref, h_im_ref, bu_re_ref, bu_im_ref, y_ref):
    step = pl.program_id(1)
    rows = u_ref.shape[0]
    groups = h_re_ref.shape[0] // SUBLANES
    tb = rows // SUBLANES // groups
    n_tiles = wb_re_ref.shape[0]
    sw = wb_re_ref.shape[2]

    @pl.when(step == 0)
    def _():
        h_re_ref[...] = h0_re_ref[...]
        h_im_ref[...] = h0_im_ref[...]

    def input_map(j):
        u_b = u_ref[:, j * LANES:(j + 1) * LANES].astype(BF16)
        bu_re_ref[j] = _dot(u_b, wb_re_ref[j])
        bu_im_ref[j] = _dot(u_b, wb_im_ref[j])

    def output_map(j):
        u_j = u_ref[:, j * LANES:(j + 1) * LANES]
        h_cat = jnp.concatenate([bu_re_ref[j].astype(BF16), bu_im_ref[j].astype(BF16)], axis=1)
        y_ref[:, j * LANES:(j + 1) * LANES] = _dot(h_cat, wc_ref[j]) + d_ref[:, j * LANES:(j + 1) * LANES] * u_j

    input_map(0)
    for j in range(n_tiles):
        if j + 1 < n_tiles:
            input_map(j + 1)
        a_re = jnp.broadcast_to(ab_re_ref[:, j * sw:(j + 1) * sw], (SUBLANES, sw))
        a_im = jnp.broadcast_to(ab_im_ref[:, j * sw:(j + 1) * sw], (SUBLANES, sw))
        for grp in range(groups):
            state_rows = slice(grp * SUBLANES, (grp + 1) * SUBLANES)
            h_re = h_re_ref[state_rows, j * sw:(j + 1) * sw]
            h_im = h_im_ref[state_rows, j * sw:(j + 1) * sw]
            for t in range(grp * tb, (grp + 1) * tb):
                sl = slice(t * SUBLANES, (t + 1) * SUBLANES)
                h_re, h_im = (a_re * h_re - a_im * h_im + bu_re_ref[j, sl, :], a_re * h_im + a_im * h_re + bu_im_ref[j, sl, :])
                bu_re_ref[j, sl, :] = h_re
                bu_im_ref[j, sl, :] = h_im
            h_re_ref[state_rows, j * sw:(j + 1) * sw] = h_re
            h_im_ref[state_rows, j * sw:(j + 1) * sw] = h_im
        output_map(j)

    z = _gelu(y_ref[...])
    gate = _sigmoid(_dot(z.astype(BF16), wglu_ref[...]) + bglu_ref[...])
    z_ref[...] = (z * gate).astype(BF16)

    @pl.when(step == pl.num_programs(1) - 1)
    def _():
        hf_re_ref[...] = h_re_ref[...]
        hf_im_ref[...] = h_im_ref[...]


def _s5(u, ab_re, ab_im, wb_re, wb_im, wc, d_skip, wglu, bglu, h0_re, h0_im, *, tblock):
    total, width = u.shape
    n_state = ab_re.shape[1]
    n_bg = h0_re.shape[0] // SUBLANES
    sw = wb_re.shape[2]
    tb, groups, grid, blk, per_group = _time_grid(total, n_bg, tblock, pack_groups=True)
    rows = groups * tb * SUBLANES
    st_spec = per_group(SUBLANES, n_state)
    consts = (ab_re, ab_im, wb_re, wb_im, wc, d_skip, wglu, bglu)
    st_shape = jax.ShapeDtypeStruct((n_bg * SUBLANES, n_state), F32)
    return pl.pallas_call(
        _s5_body,
        grid=grid,
        in_specs=[blk(width)] + [_resident(p) for p in consts] + [st_spec, st_spec],
        out_specs=(blk(width), st_spec, st_spec),
        out_shape=(jax.ShapeDtypeStruct((total, width), BF16), st_shape, st_shape),
        scratch_shapes=[pltpu.VMEM((groups * SUBLANES, n_state), F32), pltpu.VMEM((groups * SUBLANES, n_state), F32),
                        pltpu.VMEM((wb_re.shape[0], rows, sw), F32), pltpu.VMEM((wb_re.shape[0], rows, sw), F32),
                        pltpu.VMEM((rows, width), F32)],
        compiler_params=_params(2),
        name="s5",
    )(u, *[_resident_array(p) for p in consts], h0_re, h0_im)


def _lru_block(n, xbuf_ref, gate_ref, params, a_ref, b_ref, h, rows):
    cw_ref, cb_ref, wr_ref, br_ref, wi_ref, bi_ref, lam_ref = params
    bw = wr_ref.shape[1]
    cols = slice(n * bw, (n + 1) * bw)
    a_blk, b_blk = a_ref.at[n % 2], b_ref.at[n % 2]
    taps = cw_ref.shape[0]
    groups = h.shape[0] // SUBLANES
    seg = rows // groups
    pitch = seg + (taps - 1) * SUBLANES
    pieces = []
    for grp in range(groups):
        xc = cb_ref[:, cols] + xbuf_ref[grp * pitch:grp * pitch + seg, cols] * cw_ref[0:1, cols]
        for j in range(1, taps):
            xc = xc + xbuf_ref[grp * pitch + j * SUBLANES:grp * pitch + j * SUBLANES + seg, cols] * cw_ref[j:j + 1, cols]
        pieces.append(xc)
    xc = jnp.concatenate(pieces, axis=0) if groups > 1 else pieces[0]
    xb = xc.astype(BF16)
    t_i = jnp.tanh(_dot(xb, wi_ref[n]) + bi_ref[:, cols])
    t_r = jnp.tanh(_dot(xb, wr_ref[n]) + br_ref[:, cols])
    half_rate = (-0.5 * LRU_C) * _softplus(-lam_ref[:, cols])
    log_a = t_r * half_rate + half_rate
    a_blk[...] = jnp.exp(log_a)
    th = jnp.tanh(log_a)
    w = th / (th - 1.0)
    root = jnp.where(w > 0.0, w * lax.rsqrt(w), 0.0)
    scaled = xc * (0.5 * math.sqrt(2.0))
    b_blk[...] = root * (t_i * scaled + scaled)
    finals = []
    for grp in range(groups):
        h_g = h[grp * SUBLANES:(grp + 1) * SUBLANES, :]
        for t in range(grp * seg // SUBLANES, (grp + 1) * seg // SUBLANES):
            sl = slice(t * SUBLANES, (t + 1) * SUBLANES)
            h_g = a_blk[sl, :] * h_g + b_blk[sl, :]
            b_blk[sl, :] = h_g
        finals.append(h_g)
    h = jnp.concatenate(finals, axis=0) if groups > 1 else finals[0]
    return (_gelu(gate_ref[:, cols]) * b_blk[...]).astype(BF16), h


def _lru_body(gate_ref, xin_ref, conv0_ref, h0_ref, cw_ref, cb_ref, wr_ref, br_ref, wi_ref, bi_ref, lam_ref,
              o_ref, hf_ref, convf_ref, xbuf_ref, a_ref, b_ref, h_ref):
    step = pl.program_id(1)
    rows, width = xin_ref.shape
    tail = (cw_ref.shape[0] - 1) * SUBLANES
    n_blocks = wr_ref.shape[0]
    bw = wr_ref.shape[1]
    groups = h_ref.shape[0] // SUBLANES
    seg = rows // groups
    pitch = seg + tail

    @pl.when(step == 0)
    def _():
        for grp in range(groups):
            xbuf_ref[grp * pitch:grp * pitch + tail, :] = conv0_ref[grp * tail:(grp + 1) * tail, :]
        h_ref[...] = h0_ref[...]

    for grp in range(groups):
        xbuf_ref[grp * pitch + tail:(grp + 1) * pitch, :] = xin_ref[grp * seg:(grp + 1) * seg, :]
    params = (cw_ref, cb_ref, wr_ref, br_ref, wi_ref, bi_ref, lam_ref)
    for n in range(n_blocks):
        cols = slice(n * bw, (n + 1) * bw)
        o_ref[:, cols], h_ref[:, cols] = _lru_block(n, xbuf_ref, gate_ref, params, a_ref, b_ref, h_ref[:, cols], rows)
    for grp in range(groups):
        xbuf_ref[grp * pitch:grp * pitch + tail, :] = xbuf_ref[grp * pitch + seg:(grp + 1) * pitch, :]

    @pl.when(step == pl.num_programs(1) - 1)
    def _():
        hf_ref[...] = h_ref[...]
        for grp in range(groups):
            convf_ref[grp * tail:(grp + 1) * tail, :] = xbuf_ref[grp * pitch:grp * pitch + tail, :]


def _lru(gate, xin, conv0, h0, cw, cb, wr, br, wi, bi, lam, *, tblock):
    total, width = xin.shape
    n_bg = h0.shape[0] // SUBLANES
    tail = (cw.shape[0] - 1) * SUBLANES
    bw = _resident_shape(wr)[1]
    tb, groups, grid, blk, per_group = _time_grid(total, n_bg, tblock, pack_groups=True)
    rows = groups * tb * SUBLANES
    h_spec = per_group(SUBLANES, width)
    c_spec = per_group(tail, width)
    consts = (cw, cb, wr, br, wi, bi, lam)
    return pl.pallas_call(
        _lru_body,
        grid=grid,
        in_specs=[blk(width), blk(width), c_spec, h_spec] + [_resident(p) for p in consts],
        out_specs=(blk(width), h_spec, c_spec),
        out_shape=(jax.ShapeDtypeStruct((total, width), BF16), jax.ShapeDtypeStruct((n_bg * SUBLANES, width), F32),
                   jax.ShapeDtypeStruct((n_bg * tail, width), F32)),
        scratch_shapes=[pltpu.VMEM((rows + groups * tail, width), F32), pltpu.VMEM((2, rows, bw), F32),
                        pltpu.VMEM((2, rows, bw), F32), pltpu.VMEM((groups * SUBLANES, width), F32)],
        compiler_params=_params(2),
        name="rg_lru",
    )(gate, xin, conv0, h0, *[_resident_array(p) for p in consts])


def _row(t):
    return t.reshape(1, -1).astype(F32)


def _to_rows(x):
    b, s, d = x.shape
    return jnp.transpose(x.reshape(b // SUBLANES, SUBLANES, s, d), (0, 2, 1, 3)).reshape(b * s, d)


def _from_rows(y, b, s):
    d = y.shape[-1]
    return jnp.transpose(y.reshape(b // SUBLANES, s, SUBLANES, d), (0, 2, 1, 3)).reshape(b, s, d)


def _even_layer(x, nb, gla_states, gla_acc, e, n_even, s_re, s_im, prm, mlp, *, final, batch_major_out):
    (g_norm, w_in, w_u, w_a, wa2, ba2, g_head, ab_re, ab_im, wb_re, wb_im, wc, d_skip, wglu, bglu, w_out, dims) = prm
    nh, dk, dv = dims
    q, k, v, gt, u, la = _inproj_even(x, g_norm, w_in, w_u, w_a, wa2, ba2, kw=nh * dk, vw=nh * dv)
    steps = q.shape[0] // nb
    og, gla_acc = _gla(q, k, la, v, gt, g_head, gla_states, e, n_even, gla_acc, nb, dims, c=math.gcd(steps, GLA_CHUNK), tblock=128)
    n_state = ab_re.shape[1]
    h0_re = jnp.zeros((nb, n_state), F32) if s_re is None else s_re.astype(F32).reshape(nb, n_state)
    h0_im = jnp.zeros((nb, n_state), F32) if s_im is None else s_im.astype(F32).reshape(nb, n_state)
    zz, hf_re, hf_im = _s5(u, ab_re, ab_im, wb_re, wb_im, wc, d_skip, wglu, bglu, h0_re, h0_im, tblock=128)
    g_mlp, w_up, w_down, g_fin = mlp
    y = _out_mlp(x, og, zz, w_out, g_mlp, w_up, w_down, g_fin, final=final, batch_major_out=batch_major_out)
    return y, gla_acc, hf_re, hf_im


def _odd_layer(x2d, nb, s_lru, s_conv, prm, mlp, *, final, batch_major_out):
    (g_norm, w_in, cw, cb, wr, br, wi, bi, lam, w_out) = prm
    n_bg = nb // SUBLANES
    taps = cw.shape[0]
    width = _resident_shape(w_in)[1] // 2
    gate, xin = _inproj_odd(x2d, g_norm, w_in)
    h0 = jnp.zeros((nb, width), F32) if s_lru is None else s_lru.astype(F32)
    if s_conv is None:
        conv0 = jnp.zeros((n_bg * (taps - 1) * SUBLANES, width), F32)
    else:
        conv0 = jnp.transpose(s_conv.astype(F32).reshape(n_bg, SUBLANES, taps - 1, width), (0, 2, 1, 3)).reshape(-1, width)
    gh, hf, convf = _lru(gate, xin, conv0, h0, cw, cb, wr, br, wi, bi, lam, tblock=128)
    g_mlp, w_up, w_down, g_fin = mlp
    y = _out_mlp(x2d, gh, gh, w_out, g_mlp, w_up, w_down, g_fin, final=final, batch_major_out=batch_major_out)
    conv_new = jnp.transpose(convf.reshape(n_bg, taps - 1, SUBLANES, width), (0, 2, 1, 3)).reshape(nb, taps - 1, width)
    return y, hf, conv_new


def kernel(x_prompt, x_sample, state_gla, state_s5_re, state_s5_im, state_lru, state_conv, norm_mix_even, w_in_even, gla_w_a2, gla_b_a2, gla_norm, s5_a_re, s5_a_im, s5_log_dt, s5_b_re, s5_b_im, s5_c_re, s5_c_im, s5_d, s5_w_glu, s5_b_glu, w_out_even, norm_mix_odd, w_in_odd, conv_w, conv_b, lru_w_r, lru_b_r, lru_w_i, lru_b_i, lru_lam, w_out_odd, norm_mlp, w_up, w_down, norm_final):
    depth = norm_mlp.shape[0]
    nh, dk, dv = state_gla.shape[2:]
    rank = gla_w_a2.shape[1]
    n_groups, n_p, ch = s5_b_re.shape[1:]
    kw, vw = nh * dk, nh * dv
    assert rank <= LANES and LANES % ch == 0 and LANES % dk == 0 and dv == LANES
    bp, sp, _ = x_prompt.shape
    bs, ss, _ = x_sample.shape
    n_even = state_gla.shape[0]

    in_kernel_order = bp == SUBLANES and sp % (INPROJ_TILE // SUBLANES) == 0 and depth > 1
    yp = x_prompt.astype(F32) if in_kernel_order else _to_rows(x_prompt.astype(F32))
    ys = _to_rows(x_sample.astype(F32))
    g_fin = _row(norm_final)
    o4 = 2 * kw + 2 * vw
    o5 = o4 + rank
    w_in_main = w_in_even[:, :, :o4].astype(BF16)
    w_in_u = w_in_even[:, :, o5:].astype(BF16)
    w_in_a = jnp.pad(w_in_even[:, :, o4:o5], ((0, 0), (0, 0), (0, LANES - rank))).astype(BF16)
    w_a2 = jnp.pad(gla_w_a2, ((0, 0), (0, LANES - rank), (0, 0))).astype(BF16)
    w_up_f, w_down_f = w_up.astype(F32), w_down.astype(F32)
    w_out_even_b, w_out_odd_b, w_in_odd_b = w_out_even.astype(BF16), w_out_odd.astype(BF16), w_in_odd.astype(BF16)
    w_glu_b, w_r_b, w_i_b = s5_w_glu.astype(BF16), (0.5 * lru_w_r).astype(BF16), (0.5 * lru_w_i).astype(BF16)
    keys = ("re", "im", "lru", "conv")
    outs_p = {key: [] for key in keys}
    outs_s = {key: [] for key in keys}
    gla_p = gla_s = None

    def record(outs, sr, si):
        outs["re"].append(sr.reshape(-1, n_groups, n_p))
        outs["im"].append(si.reshape(-1, n_groups, n_p))

    for layer in range(depth):
        final = layer == depth - 1
        bm_out = (bp, sp) if final and in_kernel_order else None
        mlp = (_row(norm_mlp[layer]), (w_up_f, layer), (w_down_f, layer), g_fin)
        if layer % 2 == 0:
            e = layer // 2
            ab_re, ab_im, bb_re, bb_im = _s5_discretise(s5_a_re[e].astype(F32), s5_a_im[e].astype(F32), s5_log_dt[e].astype(F32),
                                                        s5_b_re[e].astype(F32), s5_b_im[e].astype(F32))
            wc = jnp.concatenate([_block_diag_out(s5_c_re[e].astype(F32)), -_block_diag_out(s5_c_im[e].astype(F32))], axis=1)
            prm = (_row(norm_mix_even[e]), (w_in_main, e), (w_in_u, e), (w_in_a, e), (w_a2, e), _row(gla_b_a2[e]), _row(gla_norm[e]),
                   ab_re.reshape(1, -1), ab_im.reshape(1, -1), _block_diag_in(bb_re).astype(BF16), _block_diag_in(bb_im).astype(BF16),
                   wc.astype(BF16), _row(s5_d[e]), (w_glu_b, e), _row(s5_b_glu[e]), (w_out_even_b, e), (nh, dk, dv))
            (yp, w_up_l, w_down_l), gla_p, sr, si = _even_layer(yp, bp, None, gla_p, e, n_even, None, None, prm, mlp, final=final,
                                                                batch_major_out=bm_out)
            record(outs_p, sr, si)
            mlp = (mlp[0], w_up_l, w_down_l, g_fin)
            ys, gla_s, sr, si = _even_layer(ys, bs, state_gla, gla_s, e, n_even, state_s5_re[e], state_s5_im[e], prm, mlp, final=final,
                                            batch_major_out=None)
            record(outs_s, sr, si)
        else:
            o = layer // 2
            prm = (_row(norm_mix_odd[o]), (w_in_odd_b, o), conv_w[o].astype(F32), _row(conv_b[o]),
                   (w_r_b, o), _row(0.5 * lru_b_r[o]), (w_i_b, o), _row(0.5 * lru_b_i[o]), _row(lru_lam[o]), (w_out_odd_b, o))
            (yp, w_up_l, w_down_l), sl, sc = _odd_layer(yp, bp, None, None, prm, mlp, final=final, batch_major_out=bm_out)
            outs_p["lru"].append(sl)
            outs_p["conv"].append(sc)
            mlp = (mlp[0], w_up_l, w_down_l, g_fin)
            ys, sl, sc = _odd_layer(ys, bs, state_lru[o], state_conv[o], prm, mlp, final=final, batch_major_out=None)
            outs_s["lru"].append(sl)
            outs_s["conv"].append(sc)

    stack = lambda outs: tuple(jnp.stack(outs[key]) for key in keys)
    return (yp if in_kernel_order else _from_rows(yp, bp, sp), _from_rows(ys, bs, ss), gla_p) + stack(outs_p) + (gla_s,) + stack(outs_s)
```
